```python
import math
import jax, jax.numpy as jnp
from jax import lax
import numpy as np

D_MODEL = 1024
BATCH = 8
SEQ = 2048
DEPTH = 2

GRID_W = 64
CTX_LEN = 256
EPS = 1e-6
A_HEADS = 4
A_HEAD_DIM = 64
A_WIDTH = 2 * A_HEADS * A_HEAD_DIM
ROPE_THETA = 10000.0
ATTN_Q_BLOCK = 128
B_HEADS = 4
B_KEY_DIM = 128
B_VAL_DIM = 128
B_KEY_TOT = B_HEADS * B_KEY_DIM
B_WIDTH = B_HEADS * B_VAL_DIM
HGRN_CHUNK = 64
C_GROUPS = 4
C_GROUP_DIM = 128
C_WIDTH = C_GROUPS * C_GROUP_DIM
C_CHUNK = 128
D_WIDTH = 512
CONV_W = 3

EVEN_SPLITS = (A_WIDTH, A_WIDTH, A_WIDTH, A_WIDTH, B_KEY_TOT, B_WIDTH, B_KEY_TOT, B_KEY_TOT, B_WIDTH)
ODD_SPLITS = (C_WIDTH, C_WIDTH, C_WIDTH, D_WIDTH, D_WIDTH, D_WIDTH, D_WIDTH)
EVEN_IN = sum(EVEN_SPLITS)
ODD_IN = sum(ODD_SPLITS)
N_EVEN = (DEPTH + 1) // 2
N_ODD = DEPTH // 2

kernel_name = "hybrid_diffattn_hgrn2_gmlp_shortconv_prefix"


def rms_norm(t, gain):
    tf = t.astype(jnp.float32)
    y = tf * lax.rsqrt(jnp.mean(tf * tf, axis=-1, keepdims=True) + EPS)
    return (y * gain.astype(jnp.float32)).astype(t.dtype)


def split_cols(p, sizes):
    return jnp.split(p, np.cumsum(sizes)[:-1].tolist(), axis=-1)


def adaln(cond, w, b):
    m = jnp.matmul(jax.nn.silu(cond), w) + b
    return jnp.split(m, 3, axis=-1)


def modulate(t, gain, shift, scale):
    return rms_norm(t, gain) * (1 + scale) + shift


def axial_rope(rows):
    row = jnp.repeat(jnp.arange(rows, dtype=jnp.float32), GRID_W)
    col = jnp.tile(jnp.arange(GRID_W, dtype=jnp.float32), rows)
    n_freq = A_HEAD_DIM // 4
    inv = ROPE_THETA ** (-jnp.arange(n_freq, dtype=jnp.float32) / n_freq)
    ang = jnp.concatenate([row[:, None] * inv, col[:, None] * inv], axis=-1)
    return jnp.cos(ang), jnp.sin(ang)


def apply_rope(t, cos, sin):
    t1, t2 = jnp.split(t, 2, axis=-1)
    cs = cos[None, :, None, :].astype(t.dtype)
    sn = sin[None, :, None, :].astype(t.dtype)
    return jnp.concatenate([t1 * cs - t2 * sn, t1 * sn + t2 * cs], axis=-1)


def attn_heads(t, gain, rope=None):
    bn, n, _ = t.shape
    t = rms_norm(t.reshape(bn, n, 2 * A_HEADS, A_HEAD_DIM), gain)
    if rope is not None:
        t = apply_rope(t, rope[0], rope[1])
    return t.reshape(bn, n, A_HEADS, 2, A_HEAD_DIM)


def diff_softmax_mix(q, k, v, lam):
    s = jnp.einsum('bqhmd,bkhmd->bhmqk', q, k, preferred_element_type=jnp.float32)
    p = jax.nn.softmax(s, axis=-1)
    w = p[:, :, 0] - lam * p[:, :, 1]
    return jnp.einsum('bhqk,bkhe->bqhe', w.astype(v.dtype), v)


def hgrn2_scan(q, k, v, log_f, s0):
    bn, n, h, _ = q.shape
    L = HGRN_CHUNK
    nc = n // L

    def to_chunks(t):
        return t.reshape(bn, nc, L, h, t.shape[-1]).transpose(1, 0, 3, 2, 4)

    mask = jnp.tril(jnp.ones((L, L), dtype=bool))[:, :, None]

    def step(S, inp):
        qb, kb, vb, gb = inp
        G = jnp.cumsum(gb, axis=2)
        diff = G[:, :, :, None, :] - G[:, :, None, :, :]
        decay = jnp.exp(jnp.where(mask, diff, -jnp.inf))
        A = jnp.einsum('bhtk,bhsk,bhtsk->bhts', qb, kb, decay)
        o = jnp.einsum('bhts,bhsv->bhtv', A, vb) + jnp.einsum('bhtk,bhkv->bhtv', qb * jnp.exp(G), S)
        G_last = G[:, :, -1:, :]
        S_new = jnp.exp(G_last[:, :, 0, :])[..., None] * S + jnp.einsum('bhsk,bhsv->bhkv', kb * jnp.exp(G_last - G), vb)
        return S_new, o

    S_fin, oc = lax.scan(step, s0, (to_chunks(q), to_chunks(k), to_chunks(v), to_chunks(log_f)))
    return oc.transpose(1, 0, 3, 2, 4).reshape(bn, n, h, v.shape[-1]), S_fin


def hgrn_gates(f_raw, lb):
    f = lb + (1.0 - lb) * jax.nn.sigmoid(f_raw.astype(jnp.float32))
    return 1.0 - f, jnp.log(f)


def hgrn2_bidir(q, i_val, f_fwd, f_bwd, lb, s0_fwd, s0_bwd):
    bn, n, _ = q.shape
    heads = lambda t, d: t.astype(jnp.float32).reshape(bn, n, B_HEADS, d)
    flip = lambda t: jnp.flip(t, axis=1)
    qh = heads(q, B_KEY_DIM)
    vh = heads(i_val, B_VAL_DIM)
    k_f, g_f = hgrn_gates(f_fwd, lb[0])
    k_b, g_b = hgrn_gates(f_bwd, lb[1])
    o_f, S_f = hgrn2_scan(qh, heads(k_f, B_KEY_DIM), vh, heads(g_f, B_KEY_DIM), s0_fwd)
    o_b, S_b = hgrn2_scan(flip(qh), flip(heads(k_b, B_KEY_DIM)), flip(vh), flip(heads(g_b, B_KEY_DIM)), s0_bwd)
    return o_f + flip(o_b), S_f, S_b


def short_conv(t, w):
    return lax.conv_general_dilated(t, w[:, None, :].astype(t.dtype), window_strides=(1,),
                                    padding=((CONV_W // 2, CONV_W // 2),),
                                    dimension_numbers=('NWC', 'WIO', 'NWC'),
                                    feature_group_count=t.shape[-1])


def even_layer(x, ctx, c, c_ctx, layer_idx, need_ctx_out, rope, norm_g, ada_w, ada_b,
               w_in, w_out, qk_gain, lam_p, subln_g, lb, hgrn_g):
    shift, scale, gate = adaln(c[:, None, :], ada_w, ada_b)
    shift_c, scale_c, gate_c = adaln(c_ctx, ada_w, ada_b)
    bn, n, _ = x.shape
    pl = split_cols(jnp.matmul(modulate(x, norm_g, shift, scale), w_in), EVEN_SPLITS)
    pc = split_cols(jnp.matmul(modulate(ctx, norm_g, shift_c, scale_c), w_in), EVEN_SPLITS)
    q_scale = A_HEAD_DIM ** -0.5

    lam_init = 0.8 - 0.6 * math.exp(-0.3 * layer_idx)
    lp = lam_p.astype(jnp.float32)
    lam = jnp.exp(jnp.sum(lp[0] * lp[1])) - jnp.exp(jnp.sum(lp[2] * lp[3])) + lam_init
    k_c = attn_heads(pc[1], qk_gain[1])
    v_c = pc[2].reshape(bn, ctx.shape[1], A_HEADS, 2 * A_HEAD_DIM)
    q_l = attn_heads(pl[0], qk_gain[0], rope) * q_scale
    k_l = attn_heads(pl[1], qk_gain[1], rope)
    v_l = pl[2].reshape(bn, n, A_HEADS, 2 * A_HEAD_DIM)
    k_all = jnp.concatenate([k_c, k_l], axis=1)
    v_all = jnp.concatenate([v_c, v_l], axis=1)
    qb = q_l.reshape(bn, n // ATTN_Q_BLOCK, ATTN_Q_BLOCK, A_HEADS, 2, A_HEAD_DIM).transpose(1, 0, 2, 3, 4, 5)
    o_al = lax.map(lambda qq: diff_softmax_mix(qq, k_all, v_all, lam), qb)
    o_al = o_al.transpose(1, 0, 2, 3, 4).reshape(bn, n, A_HEADS, 2 * A_HEAD_DIM)

    def attn_out(o):
        return (rms_norm(o, subln_g) * (1.0 - lam_init)).reshape(o.shape[0], o.shape[1], A_WIDTH)

    s0 = jnp.zeros((bn, B_HEADS, B_KEY_DIM, B_VAL_DIM), jnp.float32)
    o_bc, S_f, S_b = hgrn2_bidir(pc[4], pc[5], pc[6], pc[7], lb, s0, s0)
    o_bl, _, _ = hgrn2_bidir(pl[4], pl[5], pl[6], pl[7], lb, S_f, S_b)

    def hgrn_out(o):
        return rms_norm(o, hgrn_g).reshape(o.shape[0], o.shape[1], B_WIDTH).astype(x.dtype)

    y = jnp.concatenate([attn_out(o_al) * jax.nn.silu(pl[3]), hgrn_out(o_bl) * jax.nn.silu(pl[8])], axis=-1)
    x_new = x + gate * jnp.matmul(y, w_out)
    if need_ctx_out:
        q_c = attn_heads(pc[0], qk_gain[0]) * q_scale
        o_ac = diff_softmax_mix(q_c, k_c, v_c, lam)
        yc = jnp.concatenate([attn_out(o_ac) * jax.nn.silu(pc[3]), hgrn_out(o_bc) * jax.nn.silu(pc[8])], axis=-1)
        ctx = ctx + gate_c * jnp.matmul(yc, w_out)
    return x_new, ctx


def odd_mix(p, v_g, w_s, b_s, conv_w, w_out):
    u, v, g_c, b_gate, c_gate, h_d, g_d = p
    bn, n, _ = u.shape
    u = jax.nn.gelu(u, approximate=False)
    v = rms_norm(jax.nn.gelu(v, approximate=False), v_g)
    vc = v.reshape(bn, n // C_CHUNK, C_CHUNK, C_GROUPS, C_GROUP_DIM)
    s = jnp.einsum('gts,bcsgd->bctgd', w_s, vc) + b_s.T[:, :, None]
    o_c = u * s.reshape(bn, n, C_WIDTH) * jax.nn.silu(g_c)
    o_d = b_gate * short_conv(c_gate * h_d, conv_w) * jax.nn.silu(g_d)
    return jnp.matmul(jnp.concatenate([o_c, o_d], axis=-1), w_out)


def odd_layer(x, ctx, c, c_ctx, need_ctx_out, norm_g, ada_w, ada_b, w_in, w_out, v_g, w_s, b_s, conv_w):
    shift, scale, gate = adaln(c[:, None, :], ada_w, ada_b)
    p = split_cols(jnp.matmul(modulate(x, norm_g, shift, scale), w_in), ODD_SPLITS)
    x_new = x + gate * odd_mix(p, v_g, w_s, b_s, conv_w, w_out)
    if need_ctx_out:
        shift_c, scale_c, gate_c = adaln(c_ctx, ada_w, ada_b)
        pc = split_cols(jnp.matmul(modulate(ctx, norm_g, shift_c, scale_c), w_in), ODD_SPLITS)
        ctx = ctx + gate_c * odd_mix(pc, v_g, w_s, b_s, conv_w, w_out)
    return x_new, ctx


def setup_inputs(seed: int = 0) -> dict:
    key = jax.random.key(seed)
    ks = jax.random.split(key, 20)
    nrm = lambda k, shape, s: jax.random.normal(k, shape, jnp.float32) * s
    return {
        "x": nrm(ks[0], (BATCH, SEQ, D_MODEL), 1.0),
        "c": nrm(ks[1], (BATCH, D_MODEL), 1.0),
        "ctx": nrm(ks[2], (BATCH, CTX_LEN, D_MODEL), 1.0),
        "c_ctx": nrm(ks[3], (D_MODEL,), 1.0),
        "norm_gain": 1.0 + nrm(ks[4], (DEPTH, D_MODEL), 0.02),
        "ada_w": nrm(ks[5], (DEPTH, D_MODEL, 3 * D_MODEL), D_MODEL ** -0.5),
        "ada_b": nrm(ks[6], (DEPTH, 3 * D_MODEL), 0.02),
        "even_w_in": nrm(ks[7], (N_EVEN, D_MODEL, EVEN_IN), D_MODEL ** -0.5),
        "even_w_out": nrm(ks[8], (N_EVEN, A_WIDTH + B_WIDTH, D_MODEL), (A_WIDTH + B_WIDTH) ** -0.5),
        "attn_qk_gain": 1.0 + nrm(ks[9], (N_EVEN, 2, A_HEAD_DIM), 0.02),
        "attn_lambda": nrm(ks[10], (N_EVEN, 4, A_HEAD_DIM), 0.1),
        "attn_subln_gain": 1.0 + nrm(ks[11], (N_EVEN, 2 * A_HEAD_DIM), 0.02),
        "hgrn_lb_logits": nrm(ks[12], (2, N_EVEN + 1, B_KEY_TOT), 0.5),
        "hgrn_norm_gain": 1.0 + nrm(ks[13], (N_EVEN, B_VAL_DIM), 0.02),
        "odd_w_in": nrm(ks[14], (N_ODD, D_MODEL, ODD_IN), D_MODEL ** -0.5),
        "odd_w_out": nrm(ks[15], (N_ODD, C_WIDTH + D_WIDTH, D_MODEL), (C_WIDTH + D_WIDTH) ** -0.5),
        "gmlp_v_gain": 1.0 + nrm(ks[16], (N_ODD, C_WIDTH), 0.02),
        "gmlp_w_s": nrm(ks[17], (N_ODD, C_GROUPS, C_CHUNK, C_CHUNK), C_CHUNK ** -0.5),
        "gmlp_b_s": 1.0 + nrm(ks[18], (N_ODD, C_GROUPS, C_CHUNK), 0.02),
        "conv_w": nrm(ks[19], (N_ODD, CONV_W, D_WIDTH), CONV_W ** -0.5),
    }


def reference(x, c, ctx, c_ctx, norm_gain, ada_w, ada_b, even_w_in, even_w_out, attn_qk_gain,
              attn_lambda, attn_subln_gain, hgrn_lb_logits, hgrn_norm_gain, odd_w_in, odd_w_out,
              gmlp_v_gain, gmlp_w_s, gmlp_b_s, conv_w):
    n = x.shape[1]
    ROWS = n // GRID_W
    rope = axial_rope(ROWS)
    lb_all = jnp.cumsum(jax.nn.softmax(hgrn_lb_logits.astype(jnp.float32), axis=1), axis=1)
    for i in range(DEPTH):
        need_ctx_out = any(j % 2 == 0 for j in range(i + 1, DEPTH))
        if i % 2 == 0:
            e = i // 2
            x, ctx = even_layer(x, ctx, c, c_ctx, i, need_ctx_out, rope, norm_gain[i], ada_w[i], ada_b[i],
                                even_w_in[e], even_w_out[e], attn_qk_gain[e], attn_lambda[e],
                                attn_subln_gain[e], lb_all[:, e], hgrn_norm_gain[e])
        else:
            o = i // 2
            x, ctx = odd_layer(x, ctx, c, c_ctx, need_ctx_out, norm_gain[i], ada_w[i], ada_b[i],
                               odd_w_in[o], odd_w_out[o], gmlp_v_gain[o], gmlp_w_s[o], gmlp_b_s[o], conv_w[o])
    return x
```

```python
import functools
import math

import jax
import jax.numpy as jnp
from jax import lax
from jax.experimental import pallas as pl
from jax.experimental.pallas import tpu as pltpu

F32 = jnp.float32
BF16 = jnp.bfloat16

EPS = 1e-6
GRID_W = 64
ROPE_THETA = 10000.0
A_HEADS = 4
A_HEAD_DIM = 64
A_WIDTH = 2 * A_HEADS * A_HEAD_DIM
B_HEADS = 4
B_DIM = 128
B_WIDTH = B_HEADS * B_DIM
C_GROUPS = 4
C_CHUNK = 128
C_WIDTH = 512
D_WIDTH = 512
EVEN_IN = 4 * A_WIDTH + 5 * B_WIDTH
ODD_IN = 3 * C_WIDTH + 4 * D_WIDTH

LANES = 128
HGRN_CHUNK = 128
HGRN_DIAG = 8
EXP_CLAMP = 80.0
COND_ROWS = 16
VMEM_LIMIT = 56 * 1024 * 1024


def _cparams(*sem):
    return pltpu.CompilerParams(dimension_semantics=sem, vmem_limit_bytes=VMEM_LIMIT)


def _const_spec(shape):
    nd = len(shape)
    return pl.BlockSpec(shape, lambda *_: (0,) * nd, pipeline_mode=pl.Buffered(1))


def _silu(t):
    return t * jax.nn.sigmoid(t)


def _gelu(t):
    return 0.5 * t * (1.0 + lax.erf(t * (1.0 / math.sqrt(2.0))))


def _rms(t, gain):
    ms = jnp.mean(t * t, axis=-1, keepdims=True)
    return t * lax.rsqrt(ms + EPS) * gain


def _adaln_kernel(cond_ref, w_ref, b_ref, o_ref):
    a = _silu(cond_ref[...])
    o_ref[0] = jnp.dot(a, w_ref[0], preferred_element_type=F32) + b_ref[0]


def _adaln(cond, ada_w, ada_b):
    depth, d, n3 = ada_w.shape
    tn = 512
    return pl.pallas_call(
        _adaln_kernel,
        grid=(depth, n3 // tn),
        in_specs=[
            pl.BlockSpec((COND_ROWS, d), lambda l, j: (0, 0)),
            pl.BlockSpec((1, d, tn), lambda l, j: (l, 0, j)),
            pl.BlockSpec((1, 1, tn), lambda l, j: (l, 0, j)),
        ],
        out_specs=pl.BlockSpec((1, COND_ROWS, tn), lambda l, j: (l, 0, j)),
        out_shape=jax.ShapeDtypeStruct((depth, COND_ROWS, n3), F32),
        compiler_params=_cparams("arbitrary", "arbitrary"),
        name="adaln",
    )(cond, ada_w, ada_b.reshape(depth, 1, n3))


def _modulate(x, gain, mod, d):
    shift = mod[:, 0:d]
    scale = mod[:, d:2 * d]
    return _rms(x, gain) * (1.0 + scale) + shift


def _inproj_kernel(x_ref, mod_ref, g_ref, w_ref, o_ref):
    d = x_ref.shape[-1]
    xm = _modulate(x_ref[0], g_ref[...], mod_ref[0], d)
    o_ref[0] = jnp.dot(xm.astype(BF16), w_ref[...], preferred_element_type=F32).astype(o_ref.dtype)


def _inproj(x, mod, mod_row, gain, w, tb):
    b, n, d = x.shape
    nout = w.shape[1]
    return pl.pallas_call(
        _inproj_kernel,
        grid=(b, n // tb),
        in_specs=[
            pl.BlockSpec((1, tb, d), lambda i, j: (i, j, 0)),
            pl.BlockSpec((1, 1, 3 * d), lambda i, j: (mod_row(i), 0, 0)),
            _const_spec((1, d)),
            _const_spec((d, nout)),
        ],
        out_specs=pl.BlockSpec((1, tb, nout), lambda i, j: (i, j, 0)),
        out_shape=jax.ShapeDtypeStruct((b, n, nout), BF16),
        compiler_params=_cparams("arbitrary", "arbitrary"),
        name="inproj_even",
    )(x, mod, gain, w)


def _qk_norm(t, gain, bd):
    ms = jnp.dot((t * t).astype(BF16), bd, preferred_element_type=F32)
    return t * lax.rsqrt(ms + EPS) * gain


def _rope(t, cos, sin_signed):
    lane = lax.broadcasted_iota(jnp.int32, t.shape, 1)
    first = (lane % A_HEAD_DIM) < (A_HEAD_DIM // 2)
    partner = jnp.where(first,
                        pltpu.roll(t, LANES - A_HEAD_DIM // 2, 1),
                        pltpu.roll(t, A_HEAD_DIM // 2, 1))
    return t * cos + partner * sin_signed


def _attn_kernel(q_ref, k_ref, v_ref, g_ref, kc_ref, vc_ref, cosq_ref, sinq_ref,
                 cosk_ref, sink_ref, qkg_ref, bd_ref, subg_ref, lamp_ref,
                 o_ref, kn_scr, vn_scr, *, lam_init, n_ctx):
    bd = bd_ref[...]

    @pl.when(pl.program_id(1) == 0)
    def _():
        gk = qkg_ref[1:2, :]
        for h in range(A_HEADS):
            sl = slice(h * LANES, (h + 1) * LANES)
            kc = _qk_norm(kc_ref[0, :, sl].astype(F32), gk, bd)
            kn_scr[0:n_ctx, sl] = kc.astype(BF16)
            kl = _qk_norm(k_ref[0, :, sl].astype(F32), gk, bd)
            kl = _rope(kl, cosk_ref[...], sink_ref[...])
            kn_scr[n_ctx:, sl] = kl.astype(BF16)
        vn_scr[0:n_ctx, :] = vc_ref[0]
        vn_scr[n_ctx:, :] = v_ref[0]

    lp = lamp_ref[...]
    lam = (jnp.exp(jnp.sum(lp[0:1] * lp[1:2], axis=-1, keepdims=True))
           - jnp.exp(jnp.sum(lp[2:3] * lp[3:4], axis=-1, keepdims=True)) + lam_init)
    gq = qkg_ref[0:1, :]
    lane = lax.broadcasted_iota(jnp.int32, (1, LANES), 1)
    q_scale = A_HEAD_DIM ** -0.5
    for h in range(A_HEADS):
        sl = slice(h * LANES, (h + 1) * LANES)
        qn = _qk_norm(q_ref[0, :, sl].astype(F32), gq, bd)
        qn = _rope(qn, cosq_ref[...], sinq_ref[...]) * q_scale
        kn = kn_scr[:, sl]
        probs = []
        for m in range(2):
            in_map = (lane // A_HEAD_DIM) == m
            qm = jnp.where(in_map, qn, 0.0).astype(BF16)
            s = lax.dot_general(qm, kn, (((1,), (1,)), ((), ())), preferred_element_type=F32)
            p = jnp.exp(s - jnp.max(s, axis=-1, keepdims=True))
            probs.append((p, jnp.sum(p, axis=-1, keepdims=True)))
        (p0, l0), (p1, l1) = probs
        w = p0 * (1.0 / l0) - p1 * (lam / l1)
        o = jnp.dot(w.astype(BF16), vn_scr[:, sl], preferred_element_type=F32)
        on = _rms(o, subg_ref[...]) * (1.0 - lam_init)
        o_ref[0, :, sl] = (on * _silu(g_ref[0, :, sl].astype(F32))).astype(o_ref.dtype)


def _attention(pl_x, pl_c, cos, sin_signed, qk_gain, bd, subln_g, lam_p, lam_init, tq):
    b, n, _ = pl_x.shape
    n_ctx = pl_c.shape[1]
    w = A_WIDTH
    kern = functools.partial(_attn_kernel, lam_init=lam_init, n_ctx=n_ctx)
    return pl.pallas_call(
        kern,
        grid=(b, n // tq),
        in_specs=[
            pl.BlockSpec((1, tq, w), lambda i, j: (i, j, 0)),
            pl.BlockSpec((1, n, w), lambda i, j: (i, 0, 1)),
            pl.BlockSpec((1, n, w), lambda i, j: (i, 0, 2)),
            pl.BlockSpec((1, tq, w), lambda i, j: (i, j, 3)),
            pl.BlockSpec((1, n_ctx, w), lambda i, j: (i, 0, 1)),
            pl.BlockSpec((1, n_ctx, w), lambda i, j: (i, 0, 2)),
            pl.BlockSpec((tq, LANES), lambda i, j: (j, 0)),
            pl.BlockSpec((tq, LANES), lambda i, j: (j, 0)),
            _const_spec((n, LANES)),
            _const_spec((n, LANES)),
            _const_spec((2, LANES)),
            _const_spec((LANES, LANES)),
            _const_spec((1, LANES)),
            _const_spec((4, A_HEAD_DIM)),
        ],
        out_specs=pl.BlockSpec((1, tq, w), lambda i, j: (i, j, 0)),
        out_shape=jax.ShapeDtypeStruct((b, n, w), BF16),
        scratch_shapes=[pltpu.VMEM((n_ctx + n, w), BF16), pltpu.VMEM((n_ctx + n, w), BF16)],
        compiler_params=_cparams("arbitrary", "arbitrary"),
        name="diff_attn",
    )(pl_x, pl_x, pl_x, pl_x, pl_c, pl_c, cos, sin_signed, cos, sin_signed,
      qk_gain, bd, subln_g, lam_p)


def _split3(t):
    hi = t.astype(BF16)
    r1 = t - hi.astype(F32)
    mid = r1.astype(BF16)
    lo = (r1 - mid.astype(F32)).astype(BF16)
    return hi, mid, lo


def _block_ref(g, block, row):
    c, w = g.shape
    g3 = g.reshape(c // block, block, w)
    return jnp.broadcast_to(g3[:, row:row + 1, :], g3.shape).reshape(c, w)


def _hgrn_gates(f_raw, lb, reverse):
    c = f_raw.shape[0]
    f = lb + (1.0 - lb) * jax.nn.sigmoid(f_raw)
    row = lax.broadcasted_iota(jnp.int32, (c, c), 0)
    col = lax.broadcasted_iota(jnp.int32, (c, c), 1)
    tri = jnp.where((col >= row) if reverse else (col <= row), 1.0, 0.0).astype(BF16)
    cum = sum(jnp.dot(tri, part, preferred_element_type=F32) for part in _split3(jnp.log(f)))
    return 1.0 - f, cum


def _hgrn_state_update(st, kk, v, cum, reverse):
    c = cum.shape[0]
    edge = cum[0:1, :] if reverse else cum[c - 1:c, :]
    kg = (kk * jnp.exp(edge - cum)).astype(BF16)
    upd = lax.dot_general(v.astype(BF16), kg, (((0,), (0,)), ((), ())), preferred_element_type=F32)
    return st * jnp.exp(edge) + upd


def _hgrn_intra(q, kk, cum, reverse):
    c = q.shape[0]
    row = lax.broadcasted_iota(jnp.int32, (c, c), 0)
    col = lax.broadcasted_iota(jnp.int32, (c, c), 1)
    nt = (((1,), (1,)), ((), ()))

    ref = _block_ref(cum, HGRN_DIAG, HGRN_DIAG // 2)
    qd = (q * jnp.exp(jnp.minimum(cum - ref, EXP_CLAMP))).astype(BF16)
    kd = (kk * jnp.exp(jnp.minimum(ref - cum, EXP_CLAMP))).astype(BF16)
    same = (row // HGRN_DIAG) == (col // HGRN_DIAG)
    tri = (col >= row) if reverse else (col <= row)
    a = jnp.where(same & tri, lax.dot_general(qd, kd, nt, preferred_element_type=F32), 0.0)

    b = HGRN_DIAG
    while b < c:
        ref = _block_ref(cum, 2 * b, b if reverse else b - 1)
        ql = (q * jnp.exp(jnp.minimum(cum - ref, 0.0))).astype(BF16)
        kl = (kk * jnp.exp(jnp.minimum(ref - cum, 0.0))).astype(BF16)
        rb, cb = row // b, col // b
        if reverse:
            valid = (cb == rb + 1) & (rb % 2 == 0)
        else:
            valid = (rb == cb + 1) & (cb % 2 == 0)
        a = jnp.where(valid, lax.dot_general(ql, kl, nt, preferred_element_type=F32), a)
        b *= 2
    return a


def _hgrn_kernel(q_ref, i_ref, ff_ref, fb_ref, g_ref, ic_ref, ffc_ref, fbc_ref,
                 lbl_ref, ng_ref, o_ref, acc_scr):
    c = HGRN_CHUNK
    n = q_ref.shape[1]
    n_ctx = ic_ref.shape[1]
    nt = (((1,), (1,)), ((), ()))

    def lower_bound(direction):
        logits = [lbl_ref[direction, l] for l in range(lbl_ref.shape[1])]
        top = functools.reduce(jnp.maximum, logits)
        e = [jnp.exp(t - top) for t in logits]
        return e[0] / sum(e)

    def rows(i):
        return pl.ds(pl.multiple_of(i * c, c), c)

    def run(reverse):
        lb = lower_bound(1 if reverse else 0)
        f_ctx = fbc_ref if reverse else ffc_ref
        f_lat = fb_ref if reverse else ff_ref
        nc_ctx, nc_lat = n_ctx // c, n // c

        def ctx_step(j, st):
            i = (nc_ctx - 1 - j) if reverse else j
            kk, cum = _hgrn_gates(f_ctx[0, rows(i), :].astype(F32), lb, reverse)
            return _hgrn_state_update(st, kk, ic_ref[0, rows(i), :], cum, reverse)

        st0 = lax.fori_loop(0, nc_ctx, ctx_step, jnp.zeros((B_DIM, B_DIM), F32))

        def lat_step(j, st):
            i = (nc_lat - 1 - j) if reverse else j
            q = q_ref[0, rows(i), :].astype(F32)
            v = i_ref[0, rows(i), :]
            kk, cum = _hgrn_gates(f_lat[0, rows(i), :].astype(F32), lb, reverse)
            a = _hgrn_intra(q, kk, cum, reverse)
            o = jnp.dot(a.astype(BF16), v, preferred_element_type=F32)
            o += lax.dot_general((q * jnp.exp(cum)).astype(BF16), st.astype(BF16), nt,
                                 preferred_element_type=F32)
            if reverse:
                tot = acc_scr[rows(i), :] + o
                y = _rms(tot, ng_ref[...]) * _silu(g_ref[0, rows(i), :].astype(F32))
                o_ref[0, rows(i), :] = y.astype(o_ref.dtype)
            else:
                acc_scr[rows(i), :] = o
            return _hgrn_state_update(st, kk, v, cum, reverse)

        lax.fori_loop(0, nc_lat, lat_step, st0)

    run(False)
    run(True)


def _hgrn(pl_x, pl_c, lb_logits, norm_g):
    b, n, _ = pl_x.shape
    n_ctx = pl_c.shape[1]
    col0 = 4 * A_WIDTH // LANES

    def xs(rows_, off):
        return pl.BlockSpec((1, rows_, LANES), lambda i, h: (i, 0, col0 + off * B_HEADS + h))

    n_layers = lb_logits.shape[1]
    return pl.pallas_call(
        _hgrn_kernel,
        grid=(b, B_HEADS),
        in_specs=[
            xs(n, 0), xs(n, 1), xs(n, 2), xs(n, 3), xs(n, 4),
            xs(n_ctx, 1), xs(n_ctx, 2), xs(n_ctx, 3),
            pl.BlockSpec((2, n_layers, 1, LANES), lambda i, h: (0, 0, 0, h)),
            _const_spec((1, LANES)),
        ],
        out_specs=pl.BlockSpec((1, n, LANES), lambda i, h: (i, 0, h)),
        out_shape=jax.ShapeDtypeStruct((b, n, B_WIDTH), BF16),
        scratch_shapes=[pltpu.VMEM((n, LANES), F32)],
        compiler_params=_cparams("arbitrary", "arbitrary"),
        name="hgrn2",
    )(pl_x, pl_x, pl_x, pl_x, pl_x, pl_c, pl_c, pl_c,
      lb_logits.reshape(2, n_layers, 1, B_WIDTH), norm_g)


def _mid_kernel(ya_ref, yb_ref, x_ref, mod0_ref, mod1_ref, g1_ref, wo_ref, wi_ref, x1_ref, p1_ref):
    d = x_ref.shape[-1]
    half = ya_ref.shape[-1]
    upd = (jnp.dot(ya_ref[0], wo_ref[0:half, :], preferred_element_type=F32)
           + jnp.dot(yb_ref[0], wo_ref[half:, :], preferred_element_type=F32))
    x1 = x_ref[0] + mod0_ref[0][:, 2 * d:] * upd
    x1_ref[0] = x1
    xm = _modulate(x1, g1_ref[...], mod1_ref[0], d)
    p1_ref[0] = jnp.dot(xm.astype(BF16), wi_ref[...], preferred_element_type=F32).astype(p1_ref.dtype)


def _mid(ya, yb, x, mod0, mod1, gain1, w_out0, w_in1, tb):
    b, n, d = x.shape
    half = ya.shape[-1]
    nout = w_in1.shape[1]
    tok = lambda width: pl.BlockSpec((1, tb, width), lambda i, j: (i, j, 0))
    modspec = pl.BlockSpec((1, 1, 3 * d), lambda i, j: (i, 0, 0))
    return pl.pallas_call(
        _mid_kernel,
        grid=(b, n // tb),
        in_specs=[tok(half), tok(half), tok(d), modspec, modspec, _const_spec((1, d)),
                  _const_spec((2 * half, d)), _const_spec((d, nout))],
        out_specs=[tok(d), tok(nout)],
        out_shape=[jax.ShapeDtypeStruct((b, n, d), F32), jax.ShapeDtypeStruct((b, n, nout), BF16)],
        compiler_params=_cparams("arbitrary", "arbitrary"),
        name="outproj_even_inproj_odd",
    )(ya, yb, x, mod0, mod1, gain1, w_out0, w_in1)


def _odd_kernel(p_ref, cp_ref, hp_ref, cn_ref, hn_ref, x_ref, mod_ref, vg_ref, ws_ref, bs_ref,
                cw_ref, wo_ref, o_ref):
    d = x_ref.shape[-1]
    tb = p_ref.shape[1]
    j = pl.program_id(1)
    col = lambda k: p_ref[0, :, k * C_WIDTH:(k + 1) * C_WIDTH].astype(F32)

    u = _gelu(col(0))
    vn = _rms(_gelu(col(1)), vg_ref[...]).astype(BF16)
    chunks = []
    for ci in range(tb // C_CHUNK):
        r = slice(ci * C_CHUNK, (ci + 1) * C_CHUNK)
        groups = []
        for g in range(C_GROUPS):
            gl = slice(g * LANES, (g + 1) * LANES)
            groups.append(jnp.dot(ws_ref[g], vn[r, gl], preferred_element_type=F32) + bs_ref[g])
        chunks.append(jnp.concatenate(groups, axis=1))
    s = jnp.concatenate(chunks, axis=0)
    o_c = u * s * _silu(col(2))

    z = col(4) * col(5)
    last = cp_ref.shape[1] - 1
    z_prev_row = cp_ref[0, last:, :].astype(F32) * hp_ref[0, last:, :].astype(F32)
    z_next_row = cn_ref[0, 0:1, :].astype(F32) * hn_ref[0, 0:1, :].astype(F32)
    z_prev_row = jnp.where(j == 0, 0.0, z_prev_row)
    z_next_row = jnp.where(j == pl.num_programs(1) - 1, 0.0, z_next_row)
    rowi = lax.broadcasted_iota(jnp.int32, z.shape, 0)
    z_prev = jnp.where(rowi == 0, z_prev_row, pltpu.roll(z, 1, 0))
    z_next = jnp.where(rowi == tb - 1, z_next_row, pltpu.roll(z, tb - 1, 0))
    conv = cw_ref[0:1, :] * z_prev + cw_ref[1:2, :] * z + cw_ref[2:3, :] * z_next
    o_d = col(3) * conv * _silu(col(6))

    upd = (jnp.dot(o_c.astype(BF16), wo_ref[0:C_WIDTH, :], preferred_element_type=F32)
           + jnp.dot(o_d.astype(BF16), wo_ref[C_WIDTH:, :], preferred_element_type=F32))
    o_ref[0] = x_ref[0] + mod_ref[0][:, 2 * d:] * upd


def _odd(p1, x1, mod1, v_gain, w_s, b_s, conv_w, w_out1, tb):
    b, n, d = x1.shape
    halo = 8
    nb = tb // halo
    last_blk = n // halo - 1
    c_col, h_col = 4, 5
    prev = lambda colblk: pl.BlockSpec(
        (1, halo, D_WIDTH), lambda i, j: (i, jnp.maximum(j * nb - 1, 0), colblk))
    nxt = lambda colblk: pl.BlockSpec(
        (1, halo, D_WIDTH), lambda i, j: (i, jnp.minimum((j + 1) * nb, last_blk), colblk))
    return pl.pallas_call(
        _odd_kernel,
        grid=(b, n // tb),
        in_specs=[
            pl.BlockSpec((1, tb, ODD_IN), lambda i, j: (i, j, 0)),
            prev(c_col), prev(h_col), nxt(c_col), nxt(h_col),
            pl.BlockSpec((1, tb, d), lambda i, j: (i, j, 0)),
            pl.BlockSpec((1, 1, 3 * d), lambda i, j: (i, 0, 0)),
            _const_spec((1, C_WIDTH)),
            _const_spec((C_GROUPS, C_CHUNK, C_CHUNK)),
            _const_spec((C_GROUPS, C_CHUNK, LANES)),
            _const_spec((3, D_WIDTH)),
            _const_spec((C_WIDTH + D_WIDTH, d)),
        ],
        out_specs=pl.BlockSpec((1, tb, d), lambda i, j: (i, j, 0)),
        out_shape=jax.ShapeDtypeStruct((b, n, d), F32),
        compiler_params=_cparams("arbitrary", "arbitrary"),
        name="odd_mix_outproj",
    )(p1, p1, p1, p1, p1, x1, mod1, v_gain, w_s, b_s, conv_w, w_out1)


def _rope_tables(n):
    rows_ = n // GRID_W
    row = jnp.repeat(jnp.arange(rows_, dtype=F32), GRID_W)
    col = jnp.tile(jnp.arange(GRID_W, dtype=F32), rows_)
    n_freq = A_HEAD_DIM // 4
    inv = ROPE_THETA ** (-jnp.arange(n_freq, dtype=F32) / n_freq)
    ang = jnp.concatenate([row[:, None] * inv, col[:, None] * inv], axis=-1)
    cos, sin = jnp.cos(ang), jnp.sin(ang)
    reps = LANES // A_HEAD_DIM
    return (jnp.tile(jnp.concatenate([cos, cos], axis=-1), (1, reps)),
            jnp.tile(jnp.concatenate([-sin, sin], axis=-1), (1, reps)))


def kernel(x, c, ctx, c_ctx, norm_gain, ada_w, ada_b, even_w_in, even_w_out, attn_qk_gain,
           attn_lambda, attn_subln_gain, hgrn_lb_logits, hgrn_norm_gain, odd_w_in, odd_w_out,
           gmlp_v_gain, gmlp_w_s, gmlp_b_s, conv_w):
    b, n, d = x.shape
    assert b + 1 <= COND_ROWS and n % 512 == 0 and ctx.shape[1] % HGRN_CHUNK == 0
    assert norm_gain.shape[0] == 2, "two-layer block: one even layer then one odd layer"

    cond = jnp.zeros((COND_ROWS, d), F32).at[:b].set(c).at[b].set(c_ctx)
    mod = _adaln(cond, ada_w, ada_b)
    mod0 = mod[0].reshape(COND_ROWS, 1, 3 * d)
    mod1 = mod[1].reshape(COND_ROWS, 1, 3 * d)

    w_in0 = even_w_in[0].astype(BF16)
    gain0 = norm_gain[0].reshape(1, d)
    pl_x = _inproj(x, mod0, lambda i: i, gain0, w_in0, 512)
    pl_c = _inproj(ctx, mod0, lambda i: b, gain0, w_in0, ctx.shape[1])

    cos, sin_signed = _rope_tables(n)
    qk_gain = jnp.tile(attn_qk_gain[0], (1, LANES // A_HEAD_DIM))
    blk = jnp.arange(LANES) // A_HEAD_DIM
    bd = jnp.where(blk[:, None] == blk[None, :], 1.0 / A_HEAD_DIM, 0.0).astype(BF16)
    lam_init = 0.8 - 0.6 * math.exp(-0.3 * 0)
    ya = _attention(pl_x, pl_c, cos, sin_signed, qk_gain, bd,
                    attn_subln_gain[0].reshape(1, LANES), attn_lambda[0], lam_init, 256)
    yb = _hgrn(pl_x, pl_c, hgrn_lb_logits, hgrn_norm_gain[0].reshape(1, LANES))

    x1, p1 = _mid(ya, yb, x, mod0, mod1, norm_gain[1].reshape(1, d),
                  even_w_out[0].astype(BF16), odd_w_in[0].astype(BF16), 512)
    b_s = jnp.broadcast_to(gmlp_b_s[0][:, :, None], (C_GROUPS, C_CHUNK, LANES))
    return _odd(p1, x1, mod1, gmlp_v_gain[0].reshape(1, C_WIDTH), gmlp_w_s[0].astype(BF16),
                b_s, conv_w[0], odd_w_out[0].astype(BF16), 512)
```

```python
import functools
import math

import jax
import jax.numpy as jnp
import numpy as np
from jax import lax
from jax.experimental import pallas as pl
from jax.experimental.pallas import tpu as pltpu

F32 = jnp.float32
BF16 = jnp.bfloat16

EPS = 1e-6
GRID_W = 64
ROPE_THETA = 10000.0
A_HEADS = 4
A_HEAD_DIM = 64
A_WIDTH = 2 * A_HEADS * A_HEAD_DIM
B_HEADS = 4
B_DIM = 128
B_WIDTH = B_HEADS * B_DIM
C_GROUPS = 4
C_CHUNK = 128
C_WIDTH = 512
D_WIDTH = 512
EVEN_IN = 4 * A_WIDTH + 5 * B_WIDTH
ODD_IN = 3 * C_WIDTH + 4 * D_WIDTH

LANES = 128
HGRN_CHUNK = 128
HGRN_HEADS_PER_STEP = 4
HGRN_DIAG = 8
EXP_CLAMP = 80.0
COND_ROWS = 16
VMEM_LIMIT = 56 * 1024 * 1024


def _cparams(*sem):
    return pltpu.CompilerParams(dimension_semantics=sem, vmem_limit_bytes=VMEM_LIMIT)


def _const_spec(shape):
    nd = len(shape)
    return pl.BlockSpec(shape, lambda *_: (0,) * nd, pipeline_mode=pl.Buffered(1))


def _silu(t):
    return t * jax.nn.sigmoid(t)


def _gelu(t):
    return 0.5 * t * (1.0 + lax.erf(t * (1.0 / math.sqrt(2.0))))


def _rms(t, gain):
    ms = jnp.mean(t * t, axis=-1, keepdims=True)
    return t * lax.rsqrt(ms + EPS) * gain


def _adaln_kernel(cond_ref, w_ref, b_ref, o_ref):
    a = _silu(cond_ref[...])
    o_ref[0] = jnp.dot(a, w_ref[0], preferred_element_type=F32) + b_ref[0]


def _adaln(cond, ada_w, ada_b):
    depth, d, n3 = ada_w.shape
    tn = 512
    return pl.pallas_call(
        _adaln_kernel,
        grid=(depth, n3 // tn),
        in_specs=[
            pl.BlockSpec((COND_ROWS, d), lambda l, j: (0, 0)),
            pl.BlockSpec((1, d, tn), lambda l, j: (l, 0, j)),
            pl.BlockSpec((1, 1, tn), lambda l, j: (l, 0, j)),
        ],
        out_specs=pl.BlockSpec((1, COND_ROWS, tn), lambda l, j: (l, 0, j)),
        out_shape=jax.ShapeDtypeStruct((depth, COND_ROWS, n3), F32),
        compiler_params=_cparams("arbitrary", "arbitrary"),
        name="adaln",
    )(cond, ada_w, ada_b.reshape(depth, 1, n3))


def _modulate(x, gain, mod, d):
    shift = mod[:, 0:d]
    scale = mod[:, d:2 * d]
    return _rms(x, gain) * (1.0 + scale) + shift


def _inproj_kernel(x_ref, mod_ref, g_ref, w_ref, o_ref):
    d = x_ref.shape[-1]
    xm = _modulate(x_ref[0], g_ref[...], mod_ref[0], d)
    o_ref[0] = jnp.dot(xm.astype(BF16), w_ref[...], preferred_element_type=F32).astype(o_ref.dtype)


def _inproj(x, mod, mod_row, gain, w, tb):
    b, n, d = x.shape
    nout = w.shape[1]
    return pl.pallas_call(
        _inproj_kernel,
        grid=(b, n // tb),
        in_specs=[
            pl.BlockSpec((1, tb, d), lambda i, j: (i, j, 0)),
            pl.BlockSpec((1, 1, 3 * d), lambda i, j: (mod_row(i), 0, 0)),
            _const_spec((1, d)),
            _const_spec((d, nout)),
        ],
        out_specs=pl.BlockSpec((1, tb, nout), lambda i, j: (i, j, 0)),
        out_shape=jax.ShapeDtypeStruct((b, n, nout), BF16),
        compiler_params=_cparams("arbitrary", "arbitrary"),
        name="inproj_even",
    )(x, mod, gain, w)


def _qk_norm(t, gain, bd):
    ms = jnp.dot((t * t).astype(BF16), bd, preferred_element_type=F32)
    return t * lax.rsqrt(ms + EPS) * gain


def _rope(t, cos, sin_signed):
    lane = lax.broadcasted_iota(jnp.int32, t.shape, 1)
    first = (lane % A_HEAD_DIM) < (A_HEAD_DIM // 2)
    partner = jnp.where(first,
                        pltpu.roll(t, LANES - A_HEAD_DIM // 2, 1),
                        pltpu.roll(t, A_HEAD_DIM // 2, 1))
    return t * cos + partner * sin_signed


def _attn_kernel(q_ref, k_ref, v_ref, g_ref, kc_ref, vc_ref, cosq_ref, sinq_ref,
                 cosk_ref, sink_ref, qkg_ref, bd_ref, subg_ref, lamp_ref,
                 o_ref, kn_scr, vn_scr, *, lam_init, n_ctx):
    bd = bd_ref[...]

    @pl.when(pl.program_id(1) == 0)
    def _():
        gk = qkg_ref[1:2, :]
        for h in range(A_HEADS):
            sl = slice(h * LANES, (h + 1) * LANES)
            kc = _qk_norm(kc_ref[0, :, sl].astype(F32), gk, bd)
            kn_scr[0:n_ctx, sl] = kc.astype(BF16)
            kl = _qk_norm(k_ref[0, :, sl].astype(F32), gk, bd)
            kl = _rope(kl, cosk_ref[...], sink_ref[...])
            kn_scr[n_ctx:, sl] = kl.astype(BF16)
        vn_scr[0:n_ctx, :] = vc_ref[0]
        vn_scr[n_ctx:, :] = v_ref[0]

    lp = lamp_ref[...]
    lam = (jnp.exp(jnp.sum(lp[0:1] * lp[1:2], axis=-1, keepdims=True))
           - jnp.exp(jnp.sum(lp[2:3] * lp[3:4], axis=-1, keepdims=True)) + lam_init)
    gq = qkg_ref[0:1, :]
    lane = lax.broadcasted_iota(jnp.int32, (1, LANES), 1)
    q_scale = A_HEAD_DIM ** -0.5
    for h in range(A_HEADS):
        sl = slice(h * LANES, (h + 1) * LANES)
        qn = _qk_norm(q_ref[0, :, sl].astype(F32), gq, bd)
        qn = _rope(qn, cosq_ref[...], sinq_ref[...]) * q_scale
        kn = kn_scr[:, sl]
        probs = []
        for m in range(2):
            in_map = (lane // A_HEAD_DIM) == m
            qm = jnp.where(in_map, qn, 0.0).astype(BF16)
            s = lax.dot_general(qm, kn, (((1,), (1,)), ((), ())), preferred_element_type=F32)
            p = jnp.exp(s - jnp.max(s, axis=-1, keepdims=True))
            probs.append((p, jnp.sum(p, axis=-1, keepdims=True)))
        (p0, l0), (p1, l1) = probs
        w = p0 * (1.0 / l0) - p1 * (lam / l1)
        o = jnp.dot(w.astype(BF16), vn_scr[:, sl], preferred_element_type=F32)
        on = _rms(o, subg_ref[...]) * (1.0 - lam_init)
        o_ref[0, :, sl] = (on * _silu(g_ref[0, :, sl].astype(F32))).astype(o_ref.dtype)


def _attention(pl_x, pl_c, cos, sin_signed, qk_gain, bd, subln_g, lam_p, lam_init, tq):
    b, n, _ = pl_x.shape
    n_ctx = pl_c.shape[1]
    w = A_WIDTH
    kern = functools.partial(_attn_kernel, lam_init=lam_init, n_ctx=n_ctx)
    return pl.pallas_call(
        kern,
        grid=(b, n // tq),
        in_specs=[
            pl.BlockSpec((1, tq, w), lambda i, j: (i, j, 0)),
            pl.BlockSpec((1, n, w), lambda i, j: (i, 0, 1)),
            pl.BlockSpec((1, n, w), lambda i, j: (i, 0, 2)),
            pl.BlockSpec((1, tq, w), lambda i, j: (i, j, 3)),
            pl.BlockSpec((1, n_ctx, w), lambda i, j: (i, 0, 1)),
            pl.BlockSpec((1, n_ctx, w), lambda i, j: (i, 0, 2)),
            pl.BlockSpec((tq, LANES), lambda i, j: (j, 0)),
            pl.BlockSpec((tq, LANES), lambda i, j: (j, 0)),
            _const_spec((n, LANES)),
            _const_spec((n, LANES)),
            _const_spec((2, LANES)),
            _const_spec((LANES, LANES)),
            _const_spec((1, LANES)),
            _const_spec((4, A_HEAD_DIM)),
        ],
        out_specs=pl.BlockSpec((1, tq, w), lambda i, j: (i, j, 0)),
        out_shape=jax.ShapeDtypeStruct((b, n, w), BF16),
        scratch_shapes=[pltpu.VMEM((n_ctx + n, w), BF16), pltpu.VMEM((n_ctx + n, w), BF16)],
        compiler_params=_cparams("arbitrary", "arbitrary"),
        name="diff_attn",
    )(pl_x, pl_x, pl_x, pl_x, pl_c, pl_c, cos, sin_signed, cos, sin_signed,
      qk_gain, bd, subln_g, lam_p)


def _split3(t):
    hi = t.astype(BF16)
    r1 = t - hi.astype(F32)
    mid = r1.astype(BF16)
    lo = (r1 - mid.astype(F32)).astype(BF16)
    return hi, mid, lo


def _block_ref(g, block, row):
    c, w = g.shape
    g3 = g.reshape(c // block, block, w)
    return jnp.broadcast_to(g3[:, row:row + 1, :], g3.shape).reshape(c, w)


def _hgrn_tables(c):
    t = np.arange(c)[:, None]
    s = np.arange(c)[None, :]
    lvl = np.zeros((c, c), np.int32)
    lvl[(t // HGRN_DIAG == s // HGRN_DIAG) & (s <= t)] = 1
    b, k = HGRN_DIAG, 2
    while b < c:
        lvl[(t // b == s // b + 1) & ((s // b) % 2 == 0)] = k
        b, k = 2 * b, k + 1
    tri = (s <= t).astype(np.float32)
    return jnp.asarray(np.stack([tri, tri.T]), BF16), jnp.asarray(np.stack([lvl, lvl.T]))


def _hgrn_intra(q, kk, cum, lvl, reverse):
    c = q.shape[0]
    nt = (((1,), (1,)), ((), ()))

    ref = _block_ref(cum, HGRN_DIAG, HGRN_DIAG // 2)
    qd = (q * jnp.exp(jnp.minimum(cum - ref, EXP_CLAMP))).astype(BF16)
    kd = (kk * jnp.exp(jnp.minimum(ref - cum, EXP_CLAMP))).astype(BF16)
    a = jnp.where(lvl == 1, lax.dot_general(qd, kd, nt, preferred_element_type=F32), 0.0)

    b, k = HGRN_DIAG, 2
    while b < c:
        ref = _block_ref(cum, 2 * b, b if reverse else b - 1)
        ql = (q * jnp.exp(cum - ref)).astype(BF16)
        kl = (kk * jnp.exp(ref - cum)).astype(BF16)
        a = jnp.where(lvl == k, lax.dot_general(ql, kl, nt, preferred_element_type=F32), a)
        b, k = 2 * b, k + 1
    return a


def _hgrn_chunk(q, v, f_raw, lb, st, tri, lvl, reverse, want_out):
    c = f_raw.shape[0]
    nt = (((1,), (1,)), ((), ()))
    f = lb + (1.0 - lb) * jax.nn.sigmoid(f_raw)
    kk = 1.0 - f
    cum = sum(jnp.dot(tri, part, preferred_element_type=F32) for part in _split3(jnp.log(f)))
    edge = cum[0:1, :] if reverse else cum[c - 1:c, :]
    o = None
    if want_out:
        a = _hgrn_intra(q, kk, cum, lvl, reverse)
        o = jnp.dot(a.astype(BF16), v, preferred_element_type=F32)
        o += lax.dot_general((q * jnp.exp(cum)).astype(BF16), st.astype(BF16), nt,
                             preferred_element_type=F32)
    kg = (kk * jnp.exp(edge - cum)).astype(BF16)
    upd = lax.dot_general(v, kg, (((0,), (0,)), ((), ())), preferred_element_type=F32)
    return o, st * jnp.exp(edge) + upd


def _hgrn_kernel(q_ref, i_ref, ff_ref, fb_ref, g_ref, ic_ref, ffc_ref, fbc_ref,
                 lbl_ref, ng_ref, tri_ref, lvl_ref, o_ref, acc_scr):
    c = HGRN_CHUNK
    heads = q_ref.shape[2] // LANES
    nc_lat = q_ref.shape[1] // c
    nc_ctx = ic_ref.shape[1] // c
    f_lat = (ff_ref, fb_ref)
    f_ctx = (ffc_ref, fbc_ref)

    def lower_bound(direction, sl):
        logits = [lbl_ref[direction, l, :, sl] for l in range(lbl_ref.shape[1])]
        top = functools.reduce(jnp.maximum, logits)
        e = [jnp.exp(t - top) for t in logits]
        return e[0] / sum(e)

    lanes = [slice(h * LANES, (h + 1) * LANES) for h in range(heads)]
    lbs = [[lower_bound(d, sl) for sl in lanes] for d in (0, 1)]

    def rows(i):
        return pl.ds(pl.multiple_of(i * c, c), c)

    def ctx_step(j, sts):
        new = []
        for d in (0, 1):
            r = rows(nc_ctx - 1 - j if d else j)
            for h, sl in enumerate(lanes):
                _, st = _hgrn_chunk(None, ic_ref[0, r, sl], f_ctx[d][0, r, sl].astype(F32),
                                    lbs[d][h], sts[d * heads + h], tri_ref[d], None, bool(d), False)
                new.append(st)
        return tuple(new)

    def lat_step(j, sts, second_visit):
        new = []
        for d in (0, 1):
            r = rows(nc_lat - 1 - j if d else j)
            for h, sl in enumerate(lanes):
                o, st = _hgrn_chunk(q_ref[0, r, sl].astype(F32), i_ref[0, r, sl],
                                    f_lat[d][0, r, sl].astype(F32), lbs[d][h],
                                    sts[d * heads + h], tri_ref[d], lvl_ref[d], bool(d), True)
                new.append(st)
                if second_visit:
                    y = _rms(acc_scr[r, sl] + o, ng_ref[...]) * _silu(g_ref[0, r, sl].astype(F32))
                    o_ref[0, r, sl] = y.astype(o_ref.dtype)
                else:
                    acc_scr[r, sl] = o
        return tuple(new)

    sts = tuple(jnp.zeros((B_DIM, B_DIM), F32) for _ in range(2 * heads))
    sts = lax.fori_loop(0, nc_ctx, ctx_step, sts)
    sts = lax.fori_loop(0, nc_lat // 2, functools.partial(lat_step, second_visit=False), sts)
    lax.fori_loop(nc_lat // 2, nc_lat, functools.partial(lat_step, second_visit=True), sts)


def _hgrn(pl_x, pl_c, lb_logits, norm_g, heads_per_step):
    b, n, _ = pl_x.shape
    n_ctx = pl_c.shape[1]
    assert (n // HGRN_CHUNK) % 2 == 0 and B_HEADS % heads_per_step == 0
    w = heads_per_step * LANES
    steps = B_HEADS // heads_per_step
    col0 = 4 * A_WIDTH // w

    def xs(rows_, off):
        return pl.BlockSpec((1, rows_, w), lambda i, h: (i, 0, col0 + off * steps + h))

    n_layers = lb_logits.shape[1]
    tri, lvl = _hgrn_tables(HGRN_CHUNK)
    return pl.pallas_call(
        _hgrn_kernel,
        grid=(b, steps),
        in_specs=[
            xs(n, 0), xs(n, 1), xs(n, 2), xs(n, 3), xs(n, 4),
            xs(n_ctx, 1), xs(n_ctx, 2), xs(n_ctx, 3),
            pl.BlockSpec((2, n_layers, 1, w), lambda i, h: (0, 0, 0, h)),
            _const_spec((1, LANES)),
            _const_spec(tri.shape),
            _const_spec(lvl.shape),
        ],
        out_specs=pl.BlockSpec((1, n, w), lambda i, h: (i, 0, h)),
        out_shape=jax.ShapeDtypeStruct((b, n, B_WIDTH), BF16),
        scratch_shapes=[pltpu.VMEM((n, w), F32)],
        compiler_params=_cparams("arbitrary", "arbitrary"),
        name="hgrn2",
    )(pl_x, pl_x, pl_x, pl_x, pl_x, pl_c, pl_c, pl_c,
      lb_logits.reshape(2, n_layers, 1, B_WIDTH), norm_g, tri, lvl)


def _mid_kernel(ya_ref, yb_ref, x_ref, mod0_ref, mod1_ref, g1_ref, wo_ref, wi_ref, x1_ref, p1_ref):
    d = x_ref.shape[-1]
    half = ya_ref.shape[-1]
    upd = (jnp.dot(ya_ref[0], wo_ref[0:half, :], preferred_element_type=F32)
           + jnp.dot(yb_ref[0], wo_ref[half:, :], preferred_element_type=F32))
    x1 = x_ref[0] + mod0_ref[0][:, 2 * d:] * upd
    x1_ref[0] = x1
    xm = _modulate(x1, g1_ref[...], mod1_ref[0], d)
    p1_ref[0] = jnp.dot(xm.astype(BF16), wi_ref[...], preferred_element_type=F32).astype(p1_ref.dtype)


def _mid(ya, yb, x, mod0, mod1, gain1, w_out0, w_in1, tb):
    b, n, d = x.shape
    half = ya.shape[-1]
    nout = w_in1.shape[1]
    tok = lambda width: pl.BlockSpec((1, tb, width), lambda i, j: (i, j, 0))
    modspec = pl.BlockSpec((1, 1, 3 * d), lambda i, j: (i, 0, 0))
    return pl.pallas_call(
        _mid_kernel,
        grid=(b, n // tb),
        in_specs=[tok(half), tok(half), tok(d), modspec, modspec, _const_spec((1, d)),
                  _const_spec((2 * half, d)), _const_spec((d, nout))],
        out_specs=[tok(d), tok(nout)],
        out_shape=[jax.ShapeDtypeStruct((b, n, d), F32), jax.ShapeDtypeStruct((b, n, nout), BF16)],
        compiler_params=_cparams("arbitrary", "arbitrary"),
        name="outproj_even_inproj_odd",
    )(ya, yb, x, mod0, mod1, gain1, w_out0, w_in1)


def _odd_kernel(p_ref, cp_ref, hp_ref, cn_ref, hn_ref, x_ref, mod_ref, vg_ref, ws_ref, bs_ref,
                cw_ref, wo_ref, o_ref):
    d = x_ref.shape[-1]
    tb = p_ref.shape[1]
    j = pl.program_id(1)
    col = lambda k: p_ref[0, :, k * C_WIDTH:(k + 1) * C_WIDTH].astype(F32)

    u = _gelu(col(0))
    vn = _rms(_gelu(col(1)), vg_ref[...]).astype(BF16)
    chunks = []
    for ci in range(tb // C_CHUNK):
        r = slice(ci * C_CHUNK, (ci + 1) * C_CHUNK)
        groups = []
        for g in range(C_GROUPS):
            gl = slice(g * LANES, (g + 1) * LANES)
            groups.append(jnp.dot(ws_ref[g], vn[r, gl], preferred_element_type=F32) + bs_ref[g])
        chunks.append(jnp.concatenate(groups, axis=1))
    s = jnp.concatenate(chunks, axis=0)
    o_c = u * s * _silu(col(2))

    z = col(4) * col(5)
    last = cp_ref.shape[1] - 1
    z_prev_row = cp_ref[0, last:, :].astype(F32) * hp_ref[0, last:, :].astype(F32)
    z_next_row = cn_ref[0, 0:1, :].astype(F32) * hn_ref[0, 0:1, :].astype(F32)
    z_prev_row = jnp.where(j == 0, 0.0, z_prev_row)
    z_next_row = jnp.where(j == pl.num_programs(1) - 1, 0.0, z_next_row)
    rowi = lax.broadcasted_iota(jnp.int32, z.shape, 0)
    z_prev = jnp.where(rowi == 0, z_prev_row, pltpu.roll(z, 1, 0))
    z_next = jnp.where(rowi == tb - 1, z_next_row, pltpu.roll(z, tb - 1, 0))
    conv = cw_ref[0:1, :] * z_prev + cw_ref[1:2, :] * z + cw_ref[2:3, :] * z_next
    o_d = col(3) * conv * _silu(col(6))

    upd = (jnp.dot(o_c.astype(BF16), wo_ref[0:C_WIDTH, :], preferred_element_type=F32)
           + jnp.dot(o_d.astype(BF16), wo_ref[C_WIDTH:, :], preferred_element_type=F32))
    o_ref[0] = x_ref[0] + mod_ref[0][:, 2 * d:] * upd


def _odd(p1, x1, mod1, v_gain, w_s, b_s, conv_w, w_out1, tb):
    b, n, d = x1.shape
    halo = 8
    nb = tb // halo
    last_blk = n // halo - 1
    c_col, h_col = 4, 5
    prev = lambda colblk: pl.BlockSpec(
        (1, halo, D_WIDTH), lambda i, j: (i, jnp.maximum(j * nb - 1, 0), colblk))
    nxt = lambda colblk: pl.BlockSpec(
        (1, halo, D_WIDTH), lambda i, j: (i, jnp.minimum((j + 1) * nb, last_blk), colblk))
    return pl.pallas_call(
        _odd_kernel,
        grid=(b, n // tb),
        in_specs=[
            pl.BlockSpec((1, tb, ODD_IN), lambda i, j: (i, j, 0)),
            prev(c_col), prev(h_col), nxt(c_col), nxt(h_col),
            pl.BlockSpec((1, tb, d), lambda i, j: (i, j, 0)),
            pl.BlockSpec((1, 1, 3 * d), lambda i, j: (i, 0, 0)),
            _const_spec((1, C_WIDTH)),
            _const_spec((C_GROUPS, C_CHUNK, C_CHUNK)),
            _const_spec((C_GROUPS, C_CHUNK, LANES)),
            _const_spec((3, D_WIDTH)),
            _const_spec((C_WIDTH + D_WIDTH, d)),
        ],
        out_specs=pl.BlockSpec((1, tb, d), lambda i, j: (i, j, 0)),
        out_shape=jax.ShapeDtypeStruct((b, n, d), F32),
        compiler_params=_cparams("arbitrary", "arbitrary"),
        name="odd_mix_outproj",
    )(p1, p1, p1, p1, p1, x1, mod1, v_gain, w_s, b_s, conv_w, w_out1)


def _rope_tables(n):
    rows_ = n // GRID_W
    row = jnp.repeat(jnp.arange(rows_, dtype=F32), GRID_W)
    col = jnp.tile(jnp.arange(GRID_W, dtype=F32), rows_)
    n_freq = A_HEAD_DIM // 4
    inv = ROPE_THETA ** (-jnp.arange(n_freq, dtype=F32) / n_freq)
    ang = jnp.concatenate([row[:, None] * inv, col[:, None] * inv], axis=-1)
    cos, sin = jnp.cos(ang), jnp.sin(ang)
    reps = LANES // A_HEAD_DIM
    return (jnp.tile(jnp.concatenate([cos, cos], axis=-1), (1, reps)),
            jnp.tile(jnp.concatenate([-sin, sin], axis=-1), (1, reps)))


def kernel(x, c, ctx, c_ctx, norm_gain, ada_w, ada_b, even_w_in, even_w_out, attn_qk_gain,
           attn_lambda, attn_subln_gain, hgrn_lb_logits, hgrn_norm_gain, odd_w_in, odd_w_out,
           gmlp_v_gain, gmlp_w_s, gmlp_b_s, conv_w):
    b, n, d = x.shape
    assert b + 1 <= COND_ROWS and n % 512 == 0 and ctx.shape[1] % HGRN_CHUNK == 0
    assert norm_gain.shape[0] == 2, "two-layer block: one even layer then one odd layer"

    cond = jnp.zeros((COND_ROWS, d), F32).at[:b].set(c).at[b].set(c_ctx)
    mod = _adaln(cond, ada_w, ada_b)
    mod0 = mod[0].reshape(COND_ROWS, 1, 3 * d)
    mod1 = mod[1].reshape(COND_ROWS, 1, 3 * d)

    w_in0 = even_w_in[0].astype(BF16)
    gain0 = norm_gain[0].reshape(1, d)
    pl_x = _inproj(x, mod0, lambda i: i, gain0, w_in0, 512)
    pl_c = _inproj(ctx, mod0, lambda i: b, gain0, w_in0, ctx.shape[1])

    cos, sin_signed = _rope_tables(n)
    qk_gain = jnp.tile(attn_qk_gain[0], (1, LANES // A_HEAD_DIM))
    blk = jnp.arange(LANES) // A_HEAD_DIM
    bd = jnp.where(blk[:, None] == blk[None, :], 1.0 / A_HEAD_DIM, 0.0).astype(BF16)
    lam_init = 0.8 - 0.6 * math.exp(-0.3 * 0)
    ya = _attention(pl_x, pl_c, cos, sin_signed, qk_gain, bd,
                    attn_subln_gain[0].reshape(1, LANES), attn_lambda[0], lam_init, 256)
    yb = _hgrn(pl_x, pl_c, hgrn_lb_logits, hgrn_norm_gain[0].reshape(1, LANES), HGRN_HEADS_PER_STEP)

    x1, p1 = _mid(ya, yb, x, mod0, mod1, norm_gain[1].reshape(1, d),
                  even_w_out[0].astype(BF16), odd_w_in[0].astype(BF16), 512)
    b_s = jnp.broadcast_to(gmlp_b_s[0][:, :, None], (C_GROUPS, C_CHUNK, LANES))
    return _odd(p1, x1, mod1, gmlp_v_gain[0].reshape(1, C_WIDTH), gmlp_w_s[0].astype(BF16),
                b_s, conv_w[0], odd_w_out[0].astype(BF16), 512)
```

```python
import functools
import math

import jax
import jax.numpy as jnp
import numpy as np
from jax import lax
from jax.experimental import pallas as pl
from jax.experimental.pallas import tpu as pltpu

F32 = jnp.float32
BF16 = jnp.bfloat16

EPS = 1e-6
GRID_W = 64
ROPE_THETA = 10000.0
A_HEADS = 4
A_HEAD_DIM = 64
A_WIDTH = 2 * A_HEADS * A_HEAD_DIM
B_HEADS = 4
B_DIM = 128
B_WIDTH = B_HEADS * B_DIM
C_GROUPS = 4
C_CHUNK = 128
C_WIDTH = 512
D_WIDTH = 512
EVEN_IN = 4 * A_WIDTH + 5 * B_WIDTH
ODD_IN = 3 * C_WIDTH + 4 * D_WIDTH

LANES = 128
HGRN_CHUNK = 128
HGRN_HEADS_PER_STEP = 4
HGRN_DIAG = 8
SCORE_BOUND = 100.0
EXP_CLAMP = 80.0
COND_ROWS = 16
VMEM_LIMIT = 56 * 1024 * 1024


def _cparams(*sem):
    return pltpu.CompilerParams(dimension_semantics=sem, vmem_limit_bytes=VMEM_LIMIT)


def _const_spec(shape):
    nd = len(shape)
    return pl.BlockSpec(shape, lambda *_: (0,) * nd, pipeline_mode=pl.Buffered(1))


def _silu(t):
    return t * jax.nn.sigmoid(t)


def _gelu(t):
    return 0.5 * t * (1.0 + lax.erf(t * (1.0 / math.sqrt(2.0))))


def _rms(t, gain):
    ms = jnp.mean(t * t, axis=-1, keepdims=True)
    return t * lax.rsqrt(ms + EPS) * gain


def _adaln_kernel(cond_ref, w_ref, b_ref, o_ref):
    a = _silu(cond_ref[...])
    o_ref[0] = jnp.dot(a, w_ref[0], preferred_element_type=F32) + b_ref[0]


def _adaln(cond, ada_w, ada_b):
    depth, d, n3 = ada_w.shape
    tn = 512
    return pl.pallas_call(
        _adaln_kernel,
        grid=(depth, n3 // tn),
        in_specs=[
            pl.BlockSpec((COND_ROWS, d), lambda l, j: (0, 0)),
            pl.BlockSpec((1, d, tn), lambda l, j: (l, 0, j)),
            pl.BlockSpec((1, 1, tn), lambda l, j: (l, 0, j)),
        ],
        out_specs=pl.BlockSpec((1, COND_ROWS, tn), lambda l, j: (l, 0, j)),
        out_shape=jax.ShapeDtypeStruct((depth, COND_ROWS, n3), F32),
        compiler_params=_cparams("arbitrary", "arbitrary"),
        name="adaln",
    )(cond, ada_w, ada_b.reshape(depth, 1, n3))


def _modulate(x, gain, mod, d):
    shift = mod[:, 0:d]
    scale = mod[:, d:2 * d]
    return _rms(x, gain) * (1.0 + scale) + shift


def _inproj_kernel(x_ref, mod_ref, g_ref, w_ref, o_ref):
    d = x_ref.shape[-1]
    xm = _modulate(x_ref[0], g_ref[...], mod_ref[0], d)
    o_ref[0] = jnp.dot(xm.astype(BF16), w_ref[...], preferred_element_type=F32).astype(o_ref.dtype)


def _inproj(x, mod, mod_row, gain, w, tb):
    b, n, d = x.shape
    nout = w.shape[1]
    return pl.pallas_call(
        _inproj_kernel,
        grid=(b, n // tb),
        in_specs=[
            pl.BlockSpec((1, tb, d), lambda i, j: (i, j, 0)),
            pl.BlockSpec((1, 1, 3 * d), lambda i, j: (mod_row(i), 0, 0)),
            _const_spec((1, d)),
            _const_spec((d, nout)),
        ],
        out_specs=pl.BlockSpec((1, tb, nout), lambda i, j: (i, j, 0)),
        out_shape=jax.ShapeDtypeStruct((b, n, nout), BF16),
        compiler_params=_cparams("arbitrary", "arbitrary"),
        name="inproj_even",
    )(x, mod, gain, w)


def _qk_norm(t, gain, bd):
    ms = jnp.dot((t * t).astype(BF16), bd, preferred_element_type=F32)
    return t * lax.rsqrt(ms + EPS) * gain


def _rope(t, cos, sin_signed):
    lane = lax.broadcasted_iota(jnp.int32, t.shape, 1)
    first = (lane % A_HEAD_DIM) < (A_HEAD_DIM // 2)
    partner = jnp.where(first,
                        pltpu.roll(t, LANES - A_HEAD_DIM // 2, 1),
                        pltpu.roll(t, A_HEAD_DIM // 2, 1))
    return t * cos + partner * sin_signed


def _map_sq_norm(t, bd):
    return A_HEAD_DIM * jnp.dot((t * t).astype(BF16), bd, preferred_element_type=F32)


def _attn_kernel(q_ref, k_ref, v_ref, g_ref, kc_ref, vc_ref, cosq_ref, sinq_ref,
                 cosk_ref, sink_ref, qkg_ref, bd_ref, subg_ref, lamp_ref,
                 o_ref, kn_scr, vt_scr, ksq_scr, *, lam_init, n_ctx):
    bd = bd_ref[...]
    heads = [slice(h * LANES, (h + 1) * LANES) for h in range(A_HEADS)]

    @pl.when(pl.program_id(1) == 0)
    def _():
        gk = qkg_ref[1:2, :]
        for h, sl in enumerate(heads):
            kc = _qk_norm(kc_ref[0, :, sl].astype(F32), gk, bd)
            kn_scr[0:n_ctx, sl] = kc.astype(BF16)
            kl = _qk_norm(k_ref[0, :, sl].astype(F32), gk, bd)
            kl = _rope(kl, cosk_ref[...], sink_ref[...])
            kn_scr[n_ctx:, sl] = kl.astype(BF16)
            ksq_scr[h] = jnp.maximum(jnp.max(_map_sq_norm(kc, bd)), jnp.max(_map_sq_norm(kl, bd)))
            vt_scr[sl, 0:n_ctx] = vc_ref[0, :, sl].astype(F32).T.astype(BF16)
            vt_scr[sl, n_ctx:] = v_ref[0, :, sl].astype(F32).T.astype(BF16)

    lp = lamp_ref[...]
    lam = (jnp.exp(jnp.sum(lp[0:1] * lp[1:2], axis=-1, keepdims=True))
           - jnp.exp(jnp.sum(lp[2:3] * lp[3:4], axis=-1, keepdims=True)) + lam_init)
    gq = qkg_ref[0:1, :]
    lane = lax.broadcasted_iota(jnp.int32, (1, LANES), 1)
    q_scale = A_HEAD_DIM ** -0.5 * math.log2(math.e)
    qs = []
    bound_sq = jnp.float32(0.0)
    for h, sl in enumerate(heads):
        qn = _qk_norm(q_ref[0, :, sl].astype(F32), gq, bd)
        qn = _rope(qn, cosq_ref[...], sinq_ref[...]) * q_scale
        qs.append(qn)
        bound_sq = jnp.maximum(bound_sq, jnp.max(_map_sq_norm(qn, bd)) * ksq_scr[h])

    def run_heads(shift):
        for h, sl in enumerate(heads):
            kn = kn_scr[:, sl]
            probs = []
            for m in range(2):
                in_map = (lane // A_HEAD_DIM) == m
                qm = jnp.where(in_map, qs[h], 0.0).astype(BF16)
                s = lax.dot_general(kn, qm, (((1,), (1,)), ((), ())), preferred_element_type=F32)
                if shift:
                    s = s - jnp.max(s, axis=0, keepdims=True)
                p = jnp.exp2(s)
                probs.append((p, jnp.sum(p, axis=0, keepdims=True)))
            (p0, l0), (p1, l1) = probs
            w = p0 * (1.0 / l0) - p1 * (lam / l1)
            ot = jnp.dot(vt_scr[sl, :], w.astype(BF16), preferred_element_type=F32)
            ms = jnp.mean(ot * ot, axis=0, keepdims=True)
            on = (ot * lax.rsqrt(ms + EPS)).T * (subg_ref[...] * (1.0 - lam_init))
            o_ref[0, :, sl] = (on * _silu(g_ref[0, :, sl].astype(F32))).astype(o_ref.dtype)

    no_shift_ok = bound_sq <= SCORE_BOUND * SCORE_BOUND
    pl.when(no_shift_ok)(functools.partial(run_heads, False))
    pl.when(jnp.logical_not(no_shift_ok))(functools.partial(run_heads, True))


def _attention(pl_x, pl_c, cos, sin_signed, qk_gain, bd, subln_g, lam_p, lam_init, tq):
    b, n, _ = pl_x.shape
    n_ctx = pl_c.shape[1]
    w = A_WIDTH
    kern = functools.partial(_attn_kernel, lam_init=lam_init, n_ctx=n_ctx)
    return pl.pallas_call(
        kern,
        grid=(b, n // tq),
        in_specs=[
            pl.BlockSpec((1, tq, w), lambda i, j: (i, j, 0)),
            pl.BlockSpec((1, n, w), lambda i, j: (i, 0, 1)),
            pl.BlockSpec((1, n, w), lambda i, j: (i, 0, 2)),
            pl.BlockSpec((1, tq, w), lambda i, j: (i, j, 3)),
            pl.BlockSpec((1, n_ctx, w), lambda i, j: (i, 0, 1)),
            pl.BlockSpec((1, n_ctx, w), lambda i, j: (i, 0, 2)),
            pl.BlockSpec((tq, LANES), lambda i, j: (j, 0)),
            pl.BlockSpec((tq, LANES), lambda i, j: (j, 0)),
            _const_spec((n, LANES)),
            _const_spec((n, LANES)),
            _const_spec((2, LANES)),
            _const_spec((LANES, LANES)),
            _const_spec((1, LANES)),
            _const_spec((4, A_HEAD_DIM)),
        ],
        out_specs=pl.BlockSpec((1, tq, w), lambda i, j: (i, j, 0)),
        out_shape=jax.ShapeDtypeStruct((b, n, w), BF16),
        scratch_shapes=[pltpu.VMEM((n_ctx + n, w), BF16),
                        pltpu.VMEM((w, n_ctx + n), BF16),
                        pltpu.SMEM((A_HEADS,), F32)],
        compiler_params=_cparams("arbitrary", "arbitrary"),
        name="diff_attn",
    )(pl_x, pl_x, pl_x, pl_x, pl_c, pl_c, cos, sin_signed, cos, sin_signed,
      qk_gain, bd, subln_g, lam_p)


def _split3(t):
    hi = t.astype(BF16)
    r1 = t - hi.astype(F32)
    mid = r1.astype(BF16)
    lo = (r1 - mid.astype(F32)).astype(BF16)
    return hi, mid, lo


def _block_ref(g, block, row):
    c, w = g.shape
    g3 = g.reshape(c // block, block, w)
    return jnp.broadcast_to(g3[:, row:row + 1, :], g3.shape).reshape(c, w)


def _hgrn_tables(c):
    t = np.arange(c)[:, None]
    s = np.arange(c)[None, :]
    lvl = np.zeros((c, c), np.int32)
    lvl[(t // HGRN_DIAG == s // HGRN_DIAG) & (s <= t)] = 1
    b, k = HGRN_DIAG, 2
    while b < c:
        lvl[(t // b == s // b + 1) & ((s // b) % 2 == 0)] = k
        b, k = 2 * b, k + 1
    tri = (s <= t).astype(np.float32)
    return jnp.asarray(np.stack([tri, tri.T]), BF16), jnp.asarray(np.stack([lvl, lvl.T]))


def _hgrn_intra(q, kk, cum, lvl, reverse):
    c = q.shape[0]
    nt = (((1,), (1,)), ((), ()))

    ref = _block_ref(cum, HGRN_DIAG, HGRN_DIAG // 2)
    qd = (q * jnp.exp(jnp.minimum(cum - ref, EXP_CLAMP))).astype(BF16)
    kd = (kk * jnp.exp(jnp.minimum(ref - cum, EXP_CLAMP))).astype(BF16)
    a = jnp.where(lvl == 1, lax.dot_general(qd, kd, nt, preferred_element_type=F32), 0.0)

    b, k = HGRN_DIAG, 2
    while b < c:
        ref = _block_ref(cum, 2 * b, b if reverse else b - 1)
        ql = (q * jnp.exp(cum - ref)).astype(BF16)
        kl = (kk * jnp.exp(ref - cum)).astype(BF16)
        a = jnp.where(lvl == k, lax.dot_general(ql, kl, nt, preferred_element_type=F32), a)
        b, k = 2 * b, k + 1
    return a


def _hgrn_chunk(q, v, f_raw, lb, st, tri, lvl, reverse, want_out):
    c = f_raw.shape[0]
    nt = (((1,), (1,)), ((), ()))
    f = lb + (1.0 - lb) * jax.nn.sigmoid(f_raw)
    kk = 1.0 - f
    cum = sum(jnp.dot(tri, part, preferred_element_type=F32) for part in _split3(jnp.log(f)))
    edge = cum[0:1, :] if reverse else cum[c - 1:c, :]
    o = None
    if want_out:
        a = _hgrn_intra(q, kk, cum, lvl, reverse)
        o = jnp.dot(a.astype(BF16), v, preferred_element_type=F32)
        o += lax.dot_general((q * jnp.exp(cum)).astype(BF16), st.astype(BF16), nt,
                             preferred_element_type=F32)
    kg = (kk * jnp.exp(edge - cum)).astype(BF16)
    upd = lax.dot_general(v, kg, (((0,), (0,)), ((), ())), preferred_element_type=F32)
    return o, st * jnp.exp(edge) + upd


def _hgrn_kernel(q_ref, i_ref, ff_ref, fb_ref, g_ref, ic_ref, ffc_ref, fbc_ref,
                 lbl_ref, ng_ref, tri_ref, lvl_ref, o_ref, acc_scr):
    c = HGRN_CHUNK
    heads = q_ref.shape[2] // LANES
    nc_lat = q_ref.shape[1] // c
    nc_ctx = ic_ref.shape[1] // c
    f_lat = (ff_ref, fb_ref)
    f_ctx = (ffc_ref, fbc_ref)

    def lower_bound(direction, sl):
        logits = [lbl_ref[direction, l, :, sl] for l in range(lbl_ref.shape[1])]
        top = functools.reduce(jnp.maximum, logits)
        e = [jnp.exp(t - top) for t in logits]
        return e[0] / sum(e)

    lanes = [slice(h * LANES, (h + 1) * LANES) for h in range(heads)]
    lbs = [[lower_bound(d, sl) for sl in lanes] for d in (0, 1)]

    def rows(i):
        return pl.ds(pl.multiple_of(i * c, c), c)

    def ctx_step(j, sts):
        new = []
        for d in (0, 1):
            r = rows(nc_ctx - 1 - j if d else j)
            for h, sl in enumerate(lanes):
                _, st = _hgrn_chunk(None, ic_ref[0, r, sl], f_ctx[d][0, r, sl].astype(F32),
                                    lbs[d][h], sts[d * heads + h], tri_ref[d], None, bool(d), False)
                new.append(st)
        return tuple(new)

    def lat_step(j, sts, second_visit):
        new = []
        for d in (0, 1):
            r = rows(nc_lat - 1 - j if d else j)
            for h, sl in enumerate(lanes):
                o, st = _hgrn_chunk(q_ref[0, r, sl].astype(F32), i_ref[0, r, sl],
                                    f_lat[d][0, r, sl].astype(F32), lbs[d][h],
                                    sts[d * heads + h], tri_ref[d], lvl_ref[d], bool(d), True)
                new.append(st)
                if second_visit:
                    y = _rms(acc_scr[r, sl] + o, ng_ref[...]) * _silu(g_ref[0, r, sl].astype(F32))
                    o_ref[0, r, sl] = y.astype(o_ref.dtype)
                else:
                    acc_scr[r, sl] = o
        return tuple(new)

    sts = tuple(jnp.zeros((B_DIM, B_DIM), F32) for _ in range(2 * heads))
    sts = lax.fori_loop(0, nc_ctx, ctx_step, sts)
    sts = lax.fori_loop(0, nc_lat // 2, functools.partial(lat_step, second_visit=False), sts)
    lax.fori_loop(nc_lat // 2, nc_lat, functools.partial(lat_step, second_visit=True), sts)


def _hgrn(pl_x, pl_c, lb_logits, norm_g, heads_per_step):
    b, n, _ = pl_x.shape
    n_ctx = pl_c.shape[1]
    assert (n // HGRN_CHUNK) % 2 == 0 and B_HEADS % heads_per_step == 0
    w = heads_per_step * LANES
    steps = B_HEADS // heads_per_step
    col0 = 4 * A_WIDTH // w

    def xs(rows_, off):
        return pl.BlockSpec((1, rows_, w), lambda i, h: (i, 0, col0 + off * steps + h))

    n_layers = lb_logits.shape[1]
    tri, lvl = _hgrn_tables(HGRN_CHUNK)
    return pl.pallas_call(
        _hgrn_kernel,
        grid=(b, steps),
        in_specs=[
            xs(n, 0), xs(n, 1), xs(n, 2), xs(n, 3), xs(n, 4),
            xs(n_ctx, 1), xs(n_ctx, 2), xs(n_ctx, 3),
            pl.BlockSpec((2, n_layers, 1, w), lambda i, h: (0, 0, 0, h)),
            _const_spec((1, LANES)),
            _const_spec(tri.shape),
            _const_spec(lvl.shape),
        ],
        out_specs=pl.BlockSpec((1, n, w), lambda i, h: (i, 0, h)),
        out_shape=jax.ShapeDtypeStruct((b, n, B_WIDTH), BF16),
        scratch_shapes=[pltpu.VMEM((n, w), F32)],
        compiler_params=_cparams("arbitrary", "arbitrary"),
        name="hgrn2",
    )(pl_x, pl_x, pl_x, pl_x, pl_x, pl_c, pl_c, pl_c,
      lb_logits.reshape(2, n_layers, 1, B_WIDTH), norm_g, tri, lvl)


def _mid_kernel(ya_ref, yb_ref, x_ref, mod0_ref, mod1_ref, g1_ref, wo_ref, wi_ref, x1_ref, p1_ref):
    d = x_ref.shape[-1]
    half = ya_ref.shape[-1]
    upd = (jnp.dot(ya_ref[0], wo_ref[0:half, :], preferred_element_type=F32)
           + jnp.dot(yb_ref[0], wo_ref[half:, :], preferred_element_type=F32))
    x1 = x_ref[0] + mod0_ref[0][:, 2 * d:] * upd
    x1_ref[0] = x1
    xm = _modulate(x1, g1_ref[...], mod1_ref[0], d)
    p1_ref[0] = jnp.dot(xm.astype(BF16), wi_ref[...], preferred_element_type=F32).astype(p1_ref.dtype)


def _mid(ya, yb, x, mod0, mod1, gain1, w_out0, w_in1, tb):
    b, n, d = x.shape
    half = ya.shape[-1]
    nout = w_in1.shape[1]
    tok = lambda width: pl.BlockSpec((1, tb, width), lambda i, j: (i, j, 0))
    modspec = pl.BlockSpec((1, 1, 3 * d), lambda i, j: (i, 0, 0))
    return pl.pallas_call(
        _mid_kernel,
        grid=(b, n // tb),
        in_specs=[tok(half), tok(half), tok(d), modspec, modspec, _const_spec((1, d)),
                  _const_spec((2 * half, d)), _const_spec((d, nout))],
        out_specs=[tok(d), tok(nout)],
        out_shape=[jax.ShapeDtypeStruct((b, n, d), F32), jax.ShapeDtypeStruct((b, n, nout), BF16)],
        compiler_params=_cparams("arbitrary", "arbitrary"),
        name="outproj_even_inproj_odd",
    )(ya, yb, x, mod0, mod1, gain1, w_out0, w_in1)


def _odd_kernel(p_ref, cp_ref, hp_ref, cn_ref, hn_ref, x_ref, mod_ref, vg_ref, ws_ref, bs_ref,
                cw_ref, wo_ref, o_ref):
    d = x_ref.shape[-1]
    tb = p_ref.shape[1]
    j = pl.program_id(1)
    col = lambda k: p_ref[0, :, k * C_WIDTH:(k + 1) * C_WIDTH].astype(F32)

    u = _gelu(col(0))
    vn = _rms(_gelu(col(1)), vg_ref[...]).astype(BF16)
    chunks = []
    for ci in range(tb // C_CHUNK):
        r = slice(ci * C_CHUNK, (ci + 1) * C_CHUNK)
        groups = []
        for g in range(C_GROUPS):
            gl = slice(g * LANES, (g + 1) * LANES)
            groups.append(jnp.dot(ws_ref[g], vn[r, gl], preferred_element_type=F32) + bs_ref[g])
        chunks.append(jnp.concatenate(groups, axis=1))
    s = jnp.concatenate(chunks, axis=0)
    o_c = u * s * _silu(col(2))

    z = col(4) * col(5)
    last = cp_ref.shape[1] - 1
    z_prev_row = cp_ref[0, last:, :].astype(F32) * hp_ref[0, last:, :].astype(F32)
    z_next_row = cn_ref[0, 0:1, :].astype(F32) * hn_ref[0, 0:1, :].astype(F32)
    z_prev_row = jnp.where(j == 0, 0.0, z_prev_row)
    z_next_row = jnp.where(j == pl.num_programs(1) - 1, 0.0, z_next_row)
    rowi = lax.broadcasted_iota(jnp.int32, z.shape, 0)
    z_prev = jnp.where(rowi == 0, z_prev_row, pltpu.roll(z, 1, 0))
    z_next = jnp.where(rowi == tb - 1, z_next_row, pltpu.roll(z, tb - 1, 0))
    conv = cw_ref[0:1, :] * z_prev + cw_ref[1:2, :] * z + cw_ref[2:3, :] * z_next
    o_d = col(3) * conv * _silu(col(6))

    upd = (jnp.dot(o_c.astype(BF16), wo_ref[0:C_WIDTH, :], preferred_element_type=F32)
           + jnp.dot(o_d.astype(BF16), wo_ref[C_WIDTH:, :], preferred_element_type=F32))
    o_ref[0] = x_ref[0] + mod_ref[0][:, 2 * d:] * upd


def _odd(p1, x1, mod1, v_gain, w_s, b_s, conv_w, w_out1, tb):
    b, n, d = x1.shape
    halo = 8
    nb = tb // halo
    last_blk = n // halo - 1
    c_col, h_col = 4, 5
    prev = lambda colblk: pl.BlockSpec(
        (1, halo, D_WIDTH), lambda i, j: (i, jnp.maximum(j * nb - 1, 0), colblk))
    nxt = lambda colblk: pl.BlockSpec(
        (1, halo, D_WIDTH), lambda i, j: (i, jnp.minimum((j + 1) * nb, last_blk), colblk))
    return pl.pallas_call(
        _odd_kernel,
        grid=(b, n // tb),
        in_specs=[
            pl.BlockSpec((1, tb, ODD_IN), lambda i, j: (i, j, 0)),
            prev(c_col), prev(h_col), nxt(c_col), nxt(h_col),
            pl.BlockSpec((1, tb, d), lambda i, j: (i, j, 0)),
            pl.BlockSpec((1, 1, 3 * d), lambda i, j: (i, 0, 0)),
            _const_spec((1, C_WIDTH)),
            _const_spec((C_GROUPS, C_CHUNK, C_CHUNK)),
            _const_spec((C_GROUPS, C_CHUNK, LANES)),
            _const_spec((3, D_WIDTH)),
            _const_spec((C_WIDTH + D_WIDTH, d)),
        ],
        out_specs=pl.BlockSpec((1, tb, d), lambda i, j: (i, j, 0)),
        out_shape=jax.ShapeDtypeStruct((b, n, d), F32),
        compiler_params=_cparams("arbitrary", "arbitrary"),
        name="odd_mix_outproj",
    )(p1, p1, p1, p1, p1, x1, mod1, v_gain, w_s, b_s, conv_w, w_out1)


def _rope_tables(n):
    rows_ = n // GRID_W
    row = jnp.repeat(jnp.arange(rows_, dtype=F32), GRID_W)
    col = jnp.tile(jnp.arange(GRID_W, dtype=F32), rows_)
    n_freq = A_HEAD_DIM // 4
    inv = ROPE_THETA ** (-jnp.arange(n_freq, dtype=F32) / n_freq)
    ang = jnp.concatenate([row[:, None] * inv, col[:, None] * inv], axis=-1)
    cos, sin = jnp.cos(ang), jnp.sin(ang)
    reps = LANES // A_HEAD_DIM
    return (jnp.tile(jnp.concatenate([cos, cos], axis=-1), (1, reps)),
            jnp.tile(jnp.concatenate([-sin, sin], axis=-1), (1, reps)))


def kernel(x, c, ctx, c_ctx, norm_gain, ada_w, ada_b, even_w_in, even_w_out, attn_qk_gain,
           attn_lambda, attn_subln_gain, hgrn_lb_logits, hgrn_norm_gain, odd_w_in, odd_w_out,
           gmlp_v_gain, gmlp_w_s, gmlp_b_s, conv_w):
    b, n, d = x.shape
    assert b + 1 <= COND_ROWS and n % 512 == 0 and ctx.shape[1] % HGRN_CHUNK == 0
    assert norm_gain.shape[0] == 2, "two-layer block: one even layer then one odd layer"

    cond = jnp.zeros((COND_ROWS, d), F32).at[:b].set(c).at[b].set(c_ctx)
    mod = _adaln(cond, ada_w, ada_b)
    mod0 = mod[0].reshape(COND_ROWS, 1, 3 * d)
    mod1 = mod[1].reshape(COND_ROWS, 1, 3 * d)

    w_in0 = even_w_in[0].astype(BF16)
    gain0 = norm_gain[0].reshape(1, d)
    pl_x = _inproj(x, mod0, lambda i: i, gain0, w_in0, 512)
    pl_c = _inproj(ctx, mod0, lambda i: b, gain0, w_in0, ctx.shape[1])

    cos, sin_signed = _rope_tables(n)
    qk_gain = jnp.tile(attn_qk_gain[0], (1, LANES // A_HEAD_DIM))
    blk = jnp.arange(LANES) // A_HEAD_DIM
    bd = jnp.where(blk[:, None] == blk[None, :], 1.0 / A_HEAD_DIM, 0.0).astype(BF16)
    lam_init = 0.8 - 0.6 * math.exp(-0.3 * 0)
    ya = _attention(pl_x, pl_c, cos, sin_signed, qk_gain, bd,
                    attn_subln_gain[0].reshape(1, LANES), attn_lambda[0], lam_init, 256)
    yb = _hgrn(pl_x, pl_c, hgrn_lb_logits, hgrn_norm_gain[0].reshape(1, LANES), HGRN_HEADS_PER_STEP)

    x1, p1 = _mid(ya, yb, x, mod0, mod1, norm_gain[1].reshape(1, d),
                  even_w_out[0].astype(BF16), odd_w_in[0].astype(BF16), 512)
    b_s = jnp.broadcast_to(gmlp_b_s[0][:, :, None], (C_GROUPS, C_CHUNK, LANES))
    return _odd(p1, x1, mod1, gmlp_v_gain[0].reshape(1, C_WIDTH), gmlp_w_s[0].astype(BF16),
                b_s, conv_w[0], odd_w_out[0].astype(BF16), 512)
```

```python
import functools
import math

import jax
import jax.numpy as jnp
import numpy as np
from jax import lax
from jax.experimental import pallas as pl
from jax.experimental.pallas import tpu as pltpu

F32 = jnp.float32
BF16 = jnp.bfloat16

EPS = 1e-6
GRID_W = 64
ROPE_THETA = 10000.0
A_HEADS = 4
A_HEAD_DIM = 64
A_WIDTH = 2 * A_HEADS * A_HEAD_DIM
B_HEADS = 4
B_DIM = 128
B_WIDTH = B_HEADS * B_DIM
C_GROUPS = 4
C_CHUNK = 128
C_WIDTH = 512
D_WIDTH = 512
EVEN_IN = 4 * A_WIDTH + 5 * B_WIDTH
ODD_IN = 3 * C_WIDTH + 4 * D_WIDTH

LANES = 128
HGRN_CHUNK = 128
HGRN_HEADS_PER_STEP = 4
HGRN_DIAG = 8
SCORE_BOUND = 100.0
EXP_CLAMP = 80.0
COND_ROWS = 16
VMEM_LIMIT = 56 * 1024 * 1024


def _cparams(*sem):
    return pltpu.CompilerParams(dimension_semantics=sem, vmem_limit_bytes=VMEM_LIMIT)


def _const_spec(shape):
    nd = len(shape)
    return pl.BlockSpec(shape, lambda *_: (0,) * nd, pipeline_mode=pl.Buffered(1))


def _silu(t):
    return t * jax.nn.sigmoid(t)


def _gelu(t):
    return 0.5 * t * (1.0 + lax.erf(t * (1.0 / math.sqrt(2.0))))


def _rms(t, gain):
    ms = jnp.mean(t * t, axis=-1, keepdims=True)
    return t * lax.rsqrt(ms + EPS) * gain


def _adaln_kernel(cond_ref, w_ref, b_ref, o_ref):
    a = _silu(cond_ref[...])
    o_ref[0] = jnp.dot(a, w_ref[0], preferred_element_type=F32) + b_ref[0]


def _adaln(cond, ada_w, ada_b):
    depth, d, n3 = ada_w.shape
    tn = 512
    return pl.pallas_call(
        _adaln_kernel,
        grid=(depth, n3 // tn),
        in_specs=[
            pl.BlockSpec((COND_ROWS, d), lambda l, j: (0, 0)),
            pl.BlockSpec((1, d, tn), lambda l, j: (l, 0, j)),
            pl.BlockSpec((1, 1, tn), lambda l, j: (l, 0, j)),
        ],
        out_specs=pl.BlockSpec((1, COND_ROWS, tn), lambda l, j: (l, 0, j)),
        out_shape=jax.ShapeDtypeStruct((depth, COND_ROWS, n3), F32),
        compiler_params=_cparams("arbitrary", "arbitrary"),
        name="adaln",
    )(cond, ada_w, ada_b.reshape(depth, 1, n3))


def _modulate(x, gain, mod, d):
    shift = mod[:, 0:d]
    scale = mod[:, d:2 * d]
    return _rms(x, gain) * (1.0 + scale) + shift


def _inproj_kernel(x_ref, mod_ref, g_ref, w_ref, o_ref):
    d = x_ref.shape[-1]
    xm = _modulate(x_ref[0], g_ref[...], mod_ref[0], d)
    o_ref[0] = jnp.dot(xm.astype(BF16), w_ref[...], preferred_element_type=F32).astype(o_ref.dtype)


def _inproj(x, mod, mod_row, gain, w, tb):
    b, n, d = x.shape
    nout = w.shape[1]
    return pl.pallas_call(
        _inproj_kernel,
        grid=(b, n // tb),
        in_specs=[
            pl.BlockSpec((1, tb, d), lambda i, j: (i, j, 0)),
            pl.BlockSpec((1, 1, 3 * d), lambda i, j: (mod_row(i), 0, 0)),
            _const_spec((1, d)),
            _const_spec((d, nout)),
        ],
        out_specs=pl.BlockSpec((1, tb, nout), lambda i, j: (i, j, 0)),
        out_shape=jax.ShapeDtypeStruct((b, n, nout), BF16),
        compiler_params=_cparams("arbitrary", "arbitrary"),
        name="inproj_even",
    )(x, mod, gain, w)


def _qk_norm(t, gain, bd):
    ms = jnp.dot((t * t).astype(BF16), bd, preferred_element_type=F32)
    return t * lax.rsqrt(ms + EPS) * gain


def _rope(t, cos, sin_signed):
    lane = lax.broadcasted_iota(jnp.int32, t.shape, 1)
    first = (lane % A_HEAD_DIM) < (A_HEAD_DIM // 2)
    partner = jnp.where(first,
                        pltpu.roll(t, LANES - A_HEAD_DIM // 2, 1),
                        pltpu.roll(t, A_HEAD_DIM // 2, 1))
    return t * cos + partner * sin_signed


def _map_sq_norm(t, bd):
    return A_HEAD_DIM * jnp.dot((t * t).astype(BF16), bd, preferred_element_type=F32)


def _attn_kernel(q_ref, k_ref, v_ref, g_ref, kc_ref, vc_ref, cosq_ref, sinq_ref,
                 cosk_ref, sink_ref, qkg_ref, bd_ref, subg_ref, lamp_ref,
                 o_ref, kn_scr, vt_scr, ksq_scr, *, lam_init, n_ctx):
    bd = bd_ref[...]
    heads = [slice(h * LANES, (h + 1) * LANES) for h in range(A_HEADS)]

    @pl.when(pl.program_id(1) == 0)
    def _():
        gk = qkg_ref[1:2, :]
        for h, sl in enumerate(heads):
            kc = _qk_norm(kc_ref[0, :, sl].astype(F32), gk, bd)
            kn_scr[0:n_ctx, sl] = kc.astype(BF16)
            kl = _qk_norm(k_ref[0, :, sl].astype(F32), gk, bd)
            kl = _rope(kl, cosk_ref[...], sink_ref[...])
            kn_scr[n_ctx:, sl] = kl.astype(BF16)
            ksq_scr[h] = jnp.maximum(jnp.max(_map_sq_norm(kc, bd)), jnp.max(_map_sq_norm(kl, bd)))
            vt_scr[sl, 0:n_ctx] = vc_ref[0, :, sl].astype(F32).T.astype(BF16)
            vt_scr[sl, n_ctx:] = v_ref[0, :, sl].astype(F32).T.astype(BF16)

    lp = lamp_ref[...]
    lam = (jnp.exp(jnp.sum(lp[0:1] * lp[1:2], axis=-1, keepdims=True))
           - jnp.exp(jnp.sum(lp[2:3] * lp[3:4], axis=-1, keepdims=True)) + lam_init)
    gq = qkg_ref[0:1, :]
    lane = lax.broadcasted_iota(jnp.int32, (1, LANES), 1)
    q_scale = A_HEAD_DIM ** -0.5 * math.log2(math.e)
    qs = []
    bound_sq = jnp.float32(0.0)
    for h, sl in enumerate(heads):
        qn = _qk_norm(q_ref[0, :, sl].astype(F32), gq, bd)
        qn = _rope(qn, cosq_ref[...], sinq_ref[...]) * q_scale
        qs.append(qn)
        bound_sq = jnp.maximum(bound_sq, jnp.max(_map_sq_norm(qn, bd)) * ksq_scr[h])

    def scores(h):
        out = []
        for m in range(2):
            in_map = (lane // A_HEAD_DIM) == m
            qm = jnp.where(in_map, qs[h], 0.0).astype(BF16)
            out.append(lax.dot_general(kn_scr[:, heads[h]], qm, (((1,), (1,)), ((), ())),
                                       preferred_element_type=F32))
        return out

    def run_heads(shift):
        s_next = scores(0)
        for h, sl in enumerate(heads):
            s_both = s_next
            if h + 1 < A_HEADS:
                s_next = scores(h + 1)
            probs = []
            for s in s_both:
                if shift:
                    s = s - jnp.max(s, axis=0, keepdims=True)
                p = jnp.exp2(s)
                probs.append((p, jnp.sum(p, axis=0, keepdims=True)))
            (p0, l0), (p1, l1) = probs
            w = p0 * (1.0 / l0) - p1 * (lam / l1)
            ot = jnp.dot(vt_scr[sl, :], w.astype(BF16), preferred_element_type=F32)
            ms = jnp.mean(ot * ot, axis=0, keepdims=True)
            on = (ot * lax.rsqrt(ms + EPS)).T * (subg_ref[...] * (1.0 - lam_init))
            o_ref[0, :, sl] = (on * _silu(g_ref[0, :, sl].astype(F32))).astype(o_ref.dtype)

    no_shift_ok = bound_sq <= SCORE_BOUND * SCORE_BOUND
    pl.when(no_shift_ok)(functools.partial(run_heads, False))
    pl.when(jnp.logical_not(no_shift_ok))(functools.partial(run_heads, True))


def _attention(pl_x, pl_c, cos, sin_signed, qk_gain, bd, subln_g, lam_p, lam_init, tq):
    b, n, _ = pl_x.shape
    n_ctx = pl_c.shape[1]
    w = A_WIDTH
    kern = functools.partial(_attn_kernel, lam_init=lam_init, n_ctx=n_ctx)
    return pl.pallas_call(
        kern,
        grid=(b, n // tq),
        in_specs=[
            pl.BlockSpec((1, tq, w), lambda i, j: (i, j, 0)),
            pl.BlockSpec((1, n, w), lambda i, j: (i, 0, 1)),
            pl.BlockSpec((1, n, w), lambda i, j: (i, 0, 2)),
            pl.BlockSpec((1, tq, w), lambda i, j: (i, j, 3)),
            pl.BlockSpec((1, n_ctx, w), lambda i, j: (i, 0, 1)),
            pl.BlockSpec((1, n_ctx, w), lambda i, j: (i, 0, 2)),
            pl.BlockSpec((tq, LANES), lambda i, j: (j, 0)),
            pl.BlockSpec((tq, LANES), lambda i, j: (j, 0)),
            _const_spec((n, LANES)),
            _const_spec((n, LANES)),
            _const_spec((2, LANES)),
            _const_spec((LANES, LANES)),
            _const_spec((1, LANES)),
            _const_spec((4, A_HEAD_DIM)),
        ],
        out_specs=pl.BlockSpec((1, tq, w), lambda i, j: (i, j, 0)),
        out_shape=jax.ShapeDtypeStruct((b, n, w), BF16),
        scratch_shapes=[pltpu.VMEM((n_ctx + n, w), BF16),
                        pltpu.VMEM((w, n_ctx + n), BF16),
                        pltpu.SMEM((A_HEADS,), F32)],
        compiler_params=_cparams("arbitrary", "arbitrary"),
        name="diff_attn",
    )(pl_x, pl_x, pl_x, pl_x, pl_c, pl_c, cos, sin_signed, cos, sin_signed,
      qk_gain, bd, subln_g, lam_p)


def _split3(t):
    hi = t.astype(BF16)
    r1 = t - hi.astype(F32)
    mid = r1.astype(BF16)
    lo = (r1 - mid.astype(F32)).astype(BF16)
    return hi, mid, lo


def _block_ref(g, block, row):
    c, w = g.shape
    g3 = g.reshape(c // block, block, w)
    return jnp.broadcast_to(g3[:, row:row + 1, :], g3.shape).reshape(c, w)


def _hgrn_tables(c):
    t = np.arange(c)[:, None]
    s = np.arange(c)[None, :]
    lvl = np.zeros((c, c), np.int32)
    lvl[(t // HGRN_DIAG == s // HGRN_DIAG) & (s <= t)] = 1
    b, k = HGRN_DIAG, 2
    while b < c:
        lvl[(t // b == s // b + 1) & ((s // b) % 2 == 0)] = k
        b, k = 2 * b, k + 1
    tri = (s <= t).astype(np.float32)
    return jnp.asarray(np.stack([tri, tri.T]), BF16), jnp.asarray(np.stack([lvl, lvl.T]))


def _hgrn_chunks(chains, tri_ref, lvl_ref, want_out):
    nt = (((1,), (1,)), ((), ()))
    tn = (((0,), (0,)), ((), ()))
    n = len(chains)
    c = chains[0][2].shape[0]

    kk, parts = [], []
    for (_, _, f_raw, lb, _, _) in chains:
        f = lb + (1.0 - lb) * jax.nn.sigmoid(f_raw)
        kk.append(1.0 - f)
        parts.append(_split3(jnp.log(f)))
    cum = [sum(jnp.dot(tri_ref[ch[5]], p, preferred_element_type=F32) for p in parts[i])
           for i, ch in enumerate(chains)]
    edge = [cum[i][0:1, :] if ch[5] else cum[i][c - 1:c, :] for i, ch in enumerate(chains)]

    outs = [None] * n
    if want_out:
        a = []
        for i, (q, _, _, _, _, d) in enumerate(chains):
            ref = _block_ref(cum[i], HGRN_DIAG, HGRN_DIAG // 2)
            qd = (q * jnp.exp(jnp.minimum(cum[i] - ref, EXP_CLAMP))).astype(BF16)
            kd = (kk[i] * jnp.exp(jnp.minimum(ref - cum[i], EXP_CLAMP))).astype(BF16)
            a.append(jnp.where(lvl_ref[d] == 1,
                               lax.dot_general(qd, kd, nt, preferred_element_type=F32), 0.0))
        b, k = HGRN_DIAG, 2
        while b < c:
            for i, (q, _, _, _, _, d) in enumerate(chains):
                ref = _block_ref(cum[i], 2 * b, b if d else b - 1)
                ql = (q * jnp.exp(cum[i] - ref)).astype(BF16)
                kl = (kk[i] * jnp.exp(ref - cum[i])).astype(BF16)
                a[i] = jnp.where(lvl_ref[d] == k,
                                 lax.dot_general(ql, kl, nt, preferred_element_type=F32), a[i])
            b, k = 2 * b, k + 1
        for i, (q, v, _, _, st, _) in enumerate(chains):
            o = jnp.dot(a[i].astype(BF16), v, preferred_element_type=F32)
            outs[i] = o + lax.dot_general((q * jnp.exp(cum[i])).astype(BF16), st.astype(BF16), nt,
                                          preferred_element_type=F32)

    sts = []
    for i, (_, v, _, _, st, _) in enumerate(chains):
        kg = (kk[i] * jnp.exp(edge[i] - cum[i])).astype(BF16)
        upd = lax.dot_general(v, kg, tn, preferred_element_type=F32)
        sts.append(st * jnp.exp(edge[i]) + upd)
    return outs, sts


def _hgrn_kernel(q_ref, i_ref, ff_ref, fb_ref, g_ref, ic_ref, ffc_ref, fbc_ref,
                 lbl_ref, ng_ref, tri_ref, lvl_ref, o_ref, acc_scr):
    c = HGRN_CHUNK
    heads = q_ref.shape[2] // LANES
    nc_lat = q_ref.shape[1] // c
    nc_ctx = ic_ref.shape[1] // c
    f_lat = (ff_ref, fb_ref)
    f_ctx = (ffc_ref, fbc_ref)

    def lower_bound(direction, sl):
        logits = [lbl_ref[direction, l, :, sl] for l in range(lbl_ref.shape[1])]
        top = functools.reduce(jnp.maximum, logits)
        e = [jnp.exp(t - top) for t in logits]
        return e[0] / sum(e)

    lanes = [slice(h * LANES, (h + 1) * LANES) for h in range(heads)]
    lbs = [[lower_bound(d, sl) for sl in lanes] for d in (0, 1)]

    def rows(i):
        return pl.ds(pl.multiple_of(i * c, c), c)

    def ctx_step(j, sts):
        chains = []
        for d in (0, 1):
            r = rows(nc_ctx - 1 - j if d else j)
            for h, sl in enumerate(lanes):
                chains.append((None, ic_ref[0, r, sl], f_ctx[d][0, r, sl].astype(F32),
                               lbs[d][h], sts[d * heads + h], d))
        return tuple(_hgrn_chunks(chains, tri_ref, lvl_ref, False)[1])

    def lat_step(j, sts, second_visit):
        chains, where = [], []
        for d in (0, 1):
            r = rows(nc_lat - 1 - j if d else j)
            for h, sl in enumerate(lanes):
                chains.append((q_ref[0, r, sl].astype(F32), i_ref[0, r, sl],
                               f_lat[d][0, r, sl].astype(F32), lbs[d][h], sts[d * heads + h], d))
                where.append((r, sl))
        outs, new = _hgrn_chunks(chains, tri_ref, lvl_ref, True)
        for o, (r, sl) in zip(outs, where):
            if second_visit:
                y = _rms(acc_scr[r, sl] + o, ng_ref[...]) * _silu(g_ref[0, r, sl].astype(F32))
                o_ref[0, r, sl] = y.astype(o_ref.dtype)
            else:
                acc_scr[r, sl] = o
        return tuple(new)

    sts = tuple(jnp.zeros((B_DIM, B_DIM), F32) for _ in range(2 * heads))
    sts = lax.fori_loop(0, nc_ctx, ctx_step, sts)
    sts = lax.fori_loop(0, nc_lat // 2, functools.partial(lat_step, second_visit=False), sts)
    lax.fori_loop(nc_lat // 2, nc_lat, functools.partial(lat_step, second_visit=True), sts)


def _hgrn(pl_x, pl_c, lb_logits, norm_g, heads_per_step):
    b, n, _ = pl_x.shape
    n_ctx = pl_c.shape[1]
    assert (n // HGRN_CHUNK) % 2 == 0 and B_HEADS % heads_per_step == 0
    w = heads_per_step * LANES
    steps = B_HEADS // heads_per_step
    col0 = 4 * A_WIDTH // w

    def xs(rows_, off):
        return pl.BlockSpec((1, rows_, w), lambda i, h: (i, 0, col0 + off * steps + h))

    n_layers = lb_logits.shape[1]
    tri, lvl = _hgrn_tables(HGRN_CHUNK)
    return pl.pallas_call(
        _hgrn_kernel,
        grid=(b, steps),
        in_specs=[
            xs(n, 0), xs(n, 1), xs(n, 2), xs(n, 3), xs(n, 4),
            xs(n_ctx, 1), xs(n_ctx, 2), xs(n_ctx, 3),
            pl.BlockSpec((2, n_layers, 1, w), lambda i, h: (0, 0, 0, h)),
            _const_spec((1, LANES)),
            _const_spec(tri.shape),
            _const_spec(lvl.shape),
        ],
        out_specs=pl.BlockSpec((1, n, w), lambda i, h: (i, 0, h)),
        out_shape=jax.ShapeDtypeStruct((b, n, B_WIDTH), BF16),
        scratch_shapes=[pltpu.VMEM((n, w), F32)],
        compiler_params=_cparams("arbitrary", "arbitrary"),
        name="hgrn2",
    )(pl_x, pl_x, pl_x, pl_x, pl_x, pl_c, pl_c, pl_c,
      lb_logits.reshape(2, n_layers, 1, B_WIDTH), norm_g, tri, lvl)


def _mid_kernel(ya_ref, yb_ref, x_ref, mod0_ref, mod1_ref, g1_ref, wo_ref, wi_ref, x1_ref, p1_ref):
    d = x_ref.shape[-1]
    half = ya_ref.shape[-1]
    upd = (jnp.dot(ya_ref[0], wo_ref[0:half, :], preferred_element_type=F32)
           + jnp.dot(yb_ref[0], wo_ref[half:, :], preferred_element_type=F32))
    x1 = x_ref[0] + mod0_ref[0][:, 2 * d:] * upd
    x1_ref[0] = x1
    xm = _modulate(x1, g1_ref[...], mod1_ref[0], d)
    p1_ref[0] = jnp.dot(xm.astype(BF16), wi_ref[...], preferred_element_type=F32).astype(p1_ref.dtype)


def _mid(ya, yb, x, mod0, mod1, gain1, w_out0, w_in1, tb):
    b, n, d = x.shape
    half = ya.shape[-1]
    nout = w_in1.shape[1]
    tok = lambda width: pl.BlockSpec((1, tb, width), lambda i, j: (i, j, 0))
    modspec = pl.BlockSpec((1, 1, 3 * d), lambda i, j: (i, 0, 0))
    return pl.pallas_call(
        _mid_kernel,
        grid=(b, n // tb),
        in_specs=[tok(half), tok(half), tok(d), modspec, modspec, _const_spec((1, d)),
                  _const_spec((2 * half, d)), _const_spec((d, nout))],
        out_specs=[tok(d), tok(nout)],
        out_shape=[jax.ShapeDtypeStruct((b, n, d), F32), jax.ShapeDtypeStruct((b, n, nout), BF16)],
        compiler_params=_cparams("arbitrary", "arbitrary"),
        name="outproj_even_inproj_odd",
    )(ya, yb, x, mod0, mod1, gain1, w_out0, w_in1)


def _odd_kernel(p_ref, cp_ref, hp_ref, cn_ref, hn_ref, x_ref, mod_ref, vg_ref, ws_ref, bs_ref,
                cw_ref, wo_ref, o_ref):
    d = x_ref.shape[-1]
    tb = p_ref.shape[1]
    j = pl.program_id(1)
    col = lambda k: p_ref[0, :, k * C_WIDTH:(k + 1) * C_WIDTH].astype(F32)

    u = _gelu(col(0))
    vn = _rms(_gelu(col(1)), vg_ref[...]).astype(BF16)
    chunks = []
    for ci in range(tb // C_CHUNK):
        r = slice(ci * C_CHUNK, (ci + 1) * C_CHUNK)
        groups = []
        for g in range(C_GROUPS):
            gl = slice(g * LANES, (g + 1) * LANES)
            groups.append(jnp.dot(ws_ref[g], vn[r, gl], preferred_element_type=F32) + bs_ref[g])
        chunks.append(jnp.concatenate(groups, axis=1))
    s = jnp.concatenate(chunks, axis=0)
    o_c = u * s * _silu(col(2))

    z = col(4) * col(5)
    last = cp_ref.shape[1] - 1
    z_prev_row = cp_ref[0, last:, :].astype(F32) * hp_ref[0, last:, :].astype(F32)
    z_next_row = cn_ref[0, 0:1, :].astype(F32) * hn_ref[0, 0:1, :].astype(F32)
    z_prev_row = jnp.where(j == 0, 0.0, z_prev_row)
    z_next_row = jnp.where(j == pl.num_programs(1) - 1, 0.0, z_next_row)
    rowi = lax.broadcasted_iota(jnp.int32, z.shape, 0)
    z_prev = jnp.where(rowi == 0, z_prev_row, pltpu.roll(z, 1, 0))
    z_next = jnp.where(rowi == tb - 1, z_next_row, pltpu.roll(z, tb - 1, 0))
    conv = cw_ref[0:1, :] * z_prev + cw_ref[1:2, :] * z + cw_ref[2:3, :] * z_next
    o_d = col(3) * conv * _silu(col(6))

    upd = (jnp.dot(o_c.astype(BF16), wo_ref[0:C_WIDTH, :], preferred_element_type=F32)
           + jnp.dot(o_d.astype(BF16), wo_ref[C_WIDTH:, :], preferred_element_type=F32))
    o_ref[0] = x_ref[0] + mod_ref[0][:, 2 * d:] * upd


def _odd(p1, x1, mod1, v_gain, w_s, b_s, conv_w, w_out1, tb):
    b, n, d = x1.shape
    halo = 8
    nb = tb // halo
    last_blk = n // halo - 1
    c_col, h_col = 4, 5
    prev = lambda colblk: pl.BlockSpec(
        (1, halo, D_WIDTH), lambda i, j: (i, jnp.maximum(j * nb - 1, 0), colblk))
    nxt = lambda colblk: pl.BlockSpec(
        (1, halo, D_WIDTH), lambda i, j: (i, jnp.minimum((j + 1) * nb, last_blk), colblk))
    return pl.pallas_call(
        _odd_kernel,
        grid=(b, n // tb),
        in_specs=[
            pl.BlockSpec((1, tb, ODD_IN), lambda i, j: (i, j, 0)),
            prev(c_col), prev(h_col), nxt(c_col), nxt(h_col),
            pl.BlockSpec((1, tb, d), lambda i, j: (i, j, 0)),
            pl.BlockSpec((1, 1, 3 * d), lambda i, j: (i, 0, 0)),
            _const_spec((1, C_WIDTH)),
            _const_spec((C_GROUPS, C_CHUNK, C_CHUNK)),
            _const_spec((C_GROUPS, C_CHUNK, LANES)),
            _const_spec((3, D_WIDTH)),
            _const_spec((C_WIDTH + D_WIDTH, d)),
        ],
        out_specs=pl.BlockSpec((1, tb, d), lambda i, j: (i, j, 0)),
        out_shape=jax.ShapeDtypeStruct((b, n, d), F32),
        compiler_params=_cparams("arbitrary", "arbitrary"),
        name="odd_mix_outproj",
    )(p1, p1, p1, p1, p1, x1, mod1, v_gain, w_s, b_s, conv_w, w_out1)


def _rope_tables(n):
    rows_ = n // GRID_W
    row = jnp.repeat(jnp.arange(rows_, dtype=F32), GRID_W)
    col = jnp.tile(jnp.arange(GRID_W, dtype=F32), rows_)
    n_freq = A_HEAD_DIM // 4
    inv = ROPE_THETA ** (-jnp.arange(n_freq, dtype=F32) / n_freq)
    ang = jnp.concatenate([row[:, None] * inv, col[:, None] * inv], axis=-1)
    cos, sin = jnp.cos(ang), jnp.sin(ang)
    reps = LANES // A_HEAD_DIM
    return (jnp.tile(jnp.concatenate([cos, cos], axis=-1), (1, reps)),
            jnp.tile(jnp.concatenate([-sin, sin], axis=-1), (1, reps)))


def kernel(x, c, ctx, c_ctx, norm_gain, ada_w, ada_b, even_w_in, even_w_out, attn_qk_gain,
           attn_lambda, attn_subln_gain, hgrn_lb_logits, hgrn_norm_gain, odd_w_in, odd_w_out,
           gmlp_v_gain, gmlp_w_s, gmlp_b_s, conv_w):
    b, n, d = x.shape
    assert b + 1 <= COND_ROWS and n % 512 == 0 and ctx.shape[1] % HGRN_CHUNK == 0
    assert norm_gain.shape[0] == 2, "two-layer block: one even layer then one odd layer"

    cond = jnp.zeros((COND_ROWS, d), F32).at[:b].set(c).at[b].set(c_ctx)
    mod = _adaln(cond, ada_w, ada_b)
    mod0 = mod[0].reshape(COND_ROWS, 1, 3 * d)
    mod1 = mod[1].reshape(COND_ROWS, 1, 3 * d)

    w_in0 = even_w_in[0].astype(BF16)
    gain0 = norm_gain[0].reshape(1, d)
    pl_x = _inproj(x, mod0, lambda i: i, gain0, w_in0, 512)
    pl_c = _inproj(ctx, mod0, lambda i: b, gain0, w_in0, ctx.shape[1])

    cos, sin_signed = _rope_tables(n)
    qk_gain = jnp.tile(attn_qk_gain[0], (1, LANES // A_HEAD_DIM))
    blk = jnp.arange(LANES) // A_HEAD_DIM
    bd = jnp.where(blk[:, None] == blk[None, :], 1.0 / A_HEAD_DIM, 0.0).astype(BF16)
    lam_init = 0.8 - 0.6 * math.exp(-0.3 * 0)
    ya = _attention(pl_x, pl_c, cos, sin_signed, qk_gain, bd,
                    attn_subln_gain[0].reshape(1, LANES), attn_lambda[0], lam_init, 256)
    yb = _hgrn(pl_x, pl_c, hgrn_lb_logits, hgrn_norm_gain[0].reshape(1, LANES), HGRN_HEADS_PER_STEP)

    x1, p1 = _mid(ya, yb, x, mod0, mod1, norm_gain[1].reshape(1, d),
                  even_w_out[0].astype(BF16), odd_w_in[0].astype(BF16), 512)
    b_s = jnp.broadcast_to(gmlp_b_s[0][:, :, None], (C_GROUPS, C_CHUNK, LANES))
    return _odd(p1, x1, mod1, gmlp_v_gain[0].reshape(1, C_WIDTH), gmlp_w_s[0].astype(BF16),
                b_s, conv_w[0], odd_w_out[0].astype(BF16), 512)
```

```python
import functools
import math

import jax
import jax.numpy as jnp
import numpy as np
from jax import lax
from jax.experimental import pallas as pl
from jax.experimental.pallas import tpu as pltpu

F32 = jnp.float32
BF16 = jnp.bfloat16

EPS = 1e-6
GRID_W = 64
ROPE_THETA = 10000.0
A_HEADS = 4
A_HEAD_DIM = 64
A_WIDTH = 2 * A_HEADS * A_HEAD_DIM
B_HEADS = 4
B_DIM = 128
B_WIDTH = B_HEADS * B_DIM
C_GROUPS = 4
C_CHUNK = 128
C_WIDTH = 512
D_WIDTH = 512
EVEN_IN = 4 * A_WIDTH + 5 * B_WIDTH
ODD_IN = 3 * C_WIDTH + 4 * D_WIDTH

CTX_GROUPS = (1, 2, 5, 6, 7)
CTX_COL_K, CTX_COL_V, CTX_COL_I, CTX_COL_FF, CTX_COL_FB = range(5)

LANES = 128
HGRN_CHUNK = 128
HGRN_HEADS_PER_STEP = 4
HGRN_DIAG = 8
SCORE_BOUND = 100.0
EXP2_CLAMP = 115.0
LOG2E = math.log2(math.e)
TOKEN_BLOCK = 512
ATTN_Q_BLOCK = 256
SUBLANES = 8
COND_ROWS = 16
VMEM_LIMIT = 56 * 1024 * 1024


def _cparams(*sem):
    return pltpu.CompilerParams(dimension_semantics=sem, vmem_limit_bytes=VMEM_LIMIT)


def _const_spec(shape):
    nd = len(shape)
    return pl.BlockSpec(shape, lambda *_: (0,) * nd, pipeline_mode=pl.Buffered(1))


def _silu(t):
    return t * jax.nn.sigmoid(t)


def _gelu(t):
    return 0.5 * t * (1.0 + lax.erf(t * (1.0 / math.sqrt(2.0))))


def _rms(t, gain):
    ms = jnp.mean(t * t, axis=-1, keepdims=True)
    return t * lax.rsqrt(ms + EPS) * gain


def _adaln_kernel(cond_ref, w_ref, b_ref, o_ref):
    a = _silu(cond_ref[...])
    o_ref[0] = jnp.dot(a, w_ref[0], preferred_element_type=F32) + b_ref[0]


def _adaln(cond, ada_w, ada_b):
    depth, d, n3 = ada_w.shape
    tn = 512
    return pl.pallas_call(
        _adaln_kernel,
        grid=(depth, n3 // tn),
        in_specs=[
            pl.BlockSpec((COND_ROWS, d), lambda l, j: (0, 0)),
            pl.BlockSpec((1, d, tn), lambda l, j: (l, 0, j)),
            pl.BlockSpec((1, 1, tn), lambda l, j: (l, 0, j)),
        ],
        out_specs=pl.BlockSpec((1, COND_ROWS, tn), lambda l, j: (l, 0, j)),
        out_shape=jax.ShapeDtypeStruct((depth, COND_ROWS, n3), F32),
        compiler_params=_cparams("arbitrary", "arbitrary"),
        name="adaln",
    )(cond, ada_w, ada_b.reshape(depth, 1, n3))


def _modulate(x, gain, mod, d):
    shift = mod[:, 0:d]
    scale = mod[:, d:2 * d]
    return _rms(x, gain) * (1.0 + scale) + shift


def _inproj_kernel(x_ref, mod_ref, g_ref, w_ref, o_ref):
    d = x_ref.shape[-1]
    xm = _modulate(x_ref[0], g_ref[...], mod_ref[0], d)
    o_ref[0] = jnp.dot(xm.astype(BF16), w_ref[...], preferred_element_type=F32).astype(o_ref.dtype)


def _inproj(x, mod, mod_row, gain, w, tb):
    b, n, d = x.shape
    nout = w.shape[1]
    return pl.pallas_call(
        _inproj_kernel,
        grid=(b, n // tb),
        in_specs=[
            pl.BlockSpec((1, tb, d), lambda i, j: (i, j, 0)),
            pl.BlockSpec((1, 1, 3 * d), lambda i, j: (mod_row(i), 0, 0)),
            _const_spec((1, d)),
            _const_spec((d, nout)),
        ],
        out_specs=pl.BlockSpec((1, tb, nout), lambda i, j: (i, j, 0)),
        out_shape=jax.ShapeDtypeStruct((b, n, nout), BF16),
        compiler_params=_cparams("arbitrary", "arbitrary"),
        name="inproj_even",
    )(x, mod, gain, w)


def _qk_norm(t, gain, bd):
    ms = jnp.dot((t * t).astype(BF16), bd, preferred_element_type=F32)
    return t * lax.rsqrt(ms + EPS) * gain


def _rope(t, cos, sin_signed):
    lane = lax.broadcasted_iota(jnp.int32, t.shape, 1)
    first = (lane % A_HEAD_DIM) < (A_HEAD_DIM // 2)
    partner = jnp.where(first,
                        pltpu.roll(t, LANES - A_HEAD_DIM // 2, 1),
                        pltpu.roll(t, A_HEAD_DIM // 2, 1))
    return t * cos + partner * sin_signed


def _attn_kernel(q_ref, k_ref, v_ref, g_ref, kc_ref, vc_ref, cos_ref, sin_ref,
                 qkg_ref, bd_ref, subg_ref, lamp_ref, o_ref, qm_scr, kn_scr, vt_scr,
                 *, lam_init, n_ctx):
    bd = bd_ref[...]
    tq = o_ref.shape[1]
    heads = [slice(h * LANES, (h + 1) * LANES) for h in range(A_HEADS)]
    gq = qkg_ref[0:1, :]
    gk = qkg_ref[1:2, :]
    q_scale = A_HEAD_DIM ** -0.5 * LOG2E

    @pl.when(pl.program_id(1) == 0)
    def _():
        lane = lax.broadcasted_iota(jnp.int32, (1, LANES), 1)
        cos, sin_signed = cos_ref[...], sin_ref[...]
        for h, sl in enumerate(heads):
            qn = _rope(_qk_norm(q_ref[0, :, sl].astype(F32), gq, bd), cos, sin_signed) * q_scale
            for m in range(2):
                qm_scr[m, :, sl] = jnp.where((lane // A_HEAD_DIM) == m, qn, 0.0).astype(BF16)
            kc = _qk_norm(kc_ref[0, :, sl].astype(F32), gk, bd)
            kn_scr[0:n_ctx, sl] = kc.astype(BF16)
            kl = _rope(_qk_norm(k_ref[0, :, sl].astype(F32), gk, bd), cos, sin_signed)
            kn_scr[n_ctx:, sl] = kl.astype(BF16)
            vt_scr[sl, 0:n_ctx] = vc_ref[0, :, sl].astype(F32).T.astype(BF16)
            vt_scr[sl, n_ctx:] = v_ref[0, :, sl].astype(F32).T.astype(BF16)

    lp = lamp_ref[...]
    lam = (jnp.exp(jnp.sum(lp[0:1] * lp[1:2], axis=-1, keepdims=True))
           - jnp.exp(jnp.sum(lp[2:3] * lp[3:4], axis=-1, keepdims=True)) + lam_init)
    score_bound = (A_HEAD_DIM * q_scale) * jnp.max(jnp.abs(gq)) * jnp.max(jnp.abs(gk))
    q_rows = pl.ds(pl.multiple_of(pl.program_id(1) * tq, tq), tq)

    def scores(h):
        return [lax.dot_general(kn_scr[:, heads[h]], qm_scr[m, q_rows, heads[h]],
                                (((1,), (1,)), ((), ())), preferred_element_type=F32)
                for m in range(2)]

    def run_heads(shift):
        s_next = scores(0)
        for h, sl in enumerate(heads):
            s_both = s_next
            if h + 1 < A_HEADS:
                s_next = scores(h + 1)
            probs = []
            for s in s_both:
                if shift:
                    s = s - jnp.max(s, axis=0, keepdims=True)
                p = jnp.exp2(s)
                probs.append((p, jnp.sum(p, axis=0, keepdims=True)))
            (p0, l0), (p1, l1) = probs
            w = p0 * (1.0 / l0) - p1 * (lam / l1)
            ot = jnp.dot(vt_scr[sl, :], w.astype(BF16), preferred_element_type=F32)
            ms = jnp.mean(ot * ot, axis=0, keepdims=True)
            on = (ot * lax.rsqrt(ms + EPS)).T * (subg_ref[...] * (1.0 - lam_init))
            o_ref[0, :, sl] = (on * _silu(g_ref[0, :, sl].astype(F32))).astype(o_ref.dtype)

    no_shift_ok = score_bound <= SCORE_BOUND
    pl.when(no_shift_ok)(functools.partial(run_heads, False))
    pl.when(jnp.logical_not(no_shift_ok))(functools.partial(run_heads, True))


def _attention(pl_x, pl_c, cos, sin_signed, qk_gain, bd, subln_g, lam_p, lam_init, tq):
    b, n, _ = pl_x.shape
    n_ctx = pl_c.shape[1]
    w = A_WIDTH
    kern = functools.partial(_attn_kernel, lam_init=lam_init, n_ctx=n_ctx)
    return pl.pallas_call(
        kern,
        grid=(b, n // tq),
        in_specs=[
            pl.BlockSpec((1, n, w), lambda i, j: (i, 0, 0)),
            pl.BlockSpec((1, n, w), lambda i, j: (i, 0, 1)),
            pl.BlockSpec((1, n, w), lambda i, j: (i, 0, 2)),
            pl.BlockSpec((1, tq, w), lambda i, j: (i, j, 3)),
            pl.BlockSpec((1, n_ctx, w), lambda i, j: (i, 0, CTX_COL_K)),
            pl.BlockSpec((1, n_ctx, w), lambda i, j: (i, 0, CTX_COL_V)),
            _const_spec((n, LANES)),
            _const_spec((n, LANES)),
            _const_spec((2, LANES)),
            _const_spec((LANES, LANES)),
            _const_spec((1, LANES)),
            _const_spec((4, A_HEAD_DIM)),
        ],
        out_specs=pl.BlockSpec((1, tq, w), lambda i, j: (i, j, 0)),
        out_shape=jax.ShapeDtypeStruct((b, n, w), BF16),
        scratch_shapes=[pltpu.VMEM((2, n, w), BF16),
                        pltpu.VMEM((n_ctx + n, w), BF16),
                        pltpu.VMEM((w, n_ctx + n), BF16)],
        compiler_params=_cparams("arbitrary", "arbitrary"),
        name="diff_attn",
    )(pl_x, pl_x, pl_x, pl_x, pl_c, pl_c, cos, sin_signed, qk_gain, bd, subln_g, lam_p)


def _split_bf16(t):
    hi = t.astype(BF16)
    return hi, (t - hi.astype(F32)).astype(BF16)


def _block_ref(g, block, row):
    c, w = g.shape
    g3 = g.reshape(c // block, block, w)
    return jnp.broadcast_to(g3[:, row:row + 1, :], g3.shape).reshape(c, w)


def _hgrn_tables(c):
    t = np.arange(c)[:, None]
    s = np.arange(c)[None, :]
    lvl = np.zeros((c, c), np.int32)
    lvl[(t // HGRN_DIAG == s // HGRN_DIAG) & (s <= t)] = 1
    b, k = HGRN_DIAG, 2
    while b < c:
        lvl[(t // b == s // b + 1) & ((s // b) % 2 == 0)] = k
        b, k = 2 * b, k + 1
    tri = (s <= t).astype(np.float32)
    return jnp.asarray(np.stack([tri, tri.T]), BF16), jnp.asarray(np.stack([lvl, lvl.T]))


def _hgrn_chunks(chains, tri_ref, lvl_ref, want_out):
    nt = (((1,), (1,)), ((), ()))
    tn = (((0,), (0,)), ((), ()))
    n = len(chains)
    c = chains[0][2].shape[0]

    kk, parts = [], []
    for (_, _, f_raw, lb, _, _) in chains:
        f = lb + (1.0 - lb) * jax.nn.sigmoid(f_raw)
        kk.append(1.0 - f)
        parts.append(_split_bf16(jnp.log(f) * LOG2E))
    cum = [sum(jnp.dot(tri_ref[ch[5]], p, preferred_element_type=F32) for p in parts[i])
           for i, ch in enumerate(chains)]
    edge = [cum[i][0:1, :] if ch[5] else cum[i][c - 1:c, :] for i, ch in enumerate(chains)]

    outs = [None] * n
    if want_out:
        a = []
        for i, (q, _, _, _, _, d) in enumerate(chains):
            ref = _block_ref(cum[i], HGRN_DIAG, HGRN_DIAG // 2)
            qd = (q * jnp.exp2(jnp.minimum(cum[i] - ref, EXP2_CLAMP))).astype(BF16)
            kd = (kk[i] * jnp.exp2(jnp.minimum(ref - cum[i], EXP2_CLAMP))).astype(BF16)
            a.append(jnp.where(lvl_ref[d] == 1,
                               lax.dot_general(qd, kd, nt, preferred_element_type=F32), 0.0))
        b, k = HGRN_DIAG, 2
        while b < c:
            for i, (q, _, _, _, _, d) in enumerate(chains):
                ref = _block_ref(cum[i], 2 * b, b if d else b - 1)
                decay = jnp.exp2(cum[i] - ref)
                ql = (q * decay).astype(BF16)
                kl = (kk[i] * (1.0 / decay)).astype(BF16)
                a[i] = jnp.where(lvl_ref[d] == k,
                                 lax.dot_general(ql, kl, nt, preferred_element_type=F32), a[i])
            b, k = 2 * b, k + 1
        for i, (q, v, _, _, st, _) in enumerate(chains):
            o = jnp.dot(a[i].astype(BF16), v, preferred_element_type=F32)
            outs[i] = o + lax.dot_general((q * jnp.exp2(cum[i])).astype(BF16), st.astype(BF16), nt,
                                          preferred_element_type=F32)

    sts = []
    for i, (_, v, _, _, st, _) in enumerate(chains):
        kg = (kk[i] * jnp.exp2(edge[i] - cum[i])).astype(BF16)
        upd = lax.dot_general(v, kg, tn, preferred_element_type=F32)
        sts.append(st * jnp.exp2(edge[i]) + upd)
    return outs, sts


def _hgrn_kernel(q_ref, i_ref, ff_ref, fb_ref, g_ref, ic_ref, ffc_ref, fbc_ref,
                 lbl_ref, ng_ref, tri_ref, lvl_ref, o_ref, acc_scr):
    c = HGRN_CHUNK
    heads = q_ref.shape[2] // LANES
    nc_lat = q_ref.shape[1] // c
    nc_ctx = ic_ref.shape[1] // c
    f_lat = (ff_ref, fb_ref)
    f_ctx = (ffc_ref, fbc_ref)

    def lower_bound(direction, sl):
        logits = [lbl_ref[direction, l, :, sl] for l in range(lbl_ref.shape[1])]
        top = functools.reduce(jnp.maximum, logits)
        e = [jnp.exp(t - top) for t in logits]
        return e[0] / sum(e)

    lanes = [slice(h * LANES, (h + 1) * LANES) for h in range(heads)]
    lbs = [[lower_bound(d, sl) for sl in lanes] for d in (0, 1)]

    def rows(i):
        return pl.ds(pl.multiple_of(i * c, c), c)

    def ctx_step(j, sts):
        chains = []
        for d in (0, 1):
            r = rows(nc_ctx - 1 - j if d else j)
            for h, sl in enumerate(lanes):
                chains.append((None, ic_ref[0, r, sl], f_ctx[d][0, r, sl].astype(F32),
                               lbs[d][h], sts[d * heads + h], d))
        return tuple(_hgrn_chunks(chains, tri_ref, lvl_ref, False)[1])

    def lat_step(j, sts, second_visit):
        chains, where = [], []
        for d in (0, 1):
            r = rows(nc_lat - 1 - j if d else j)
            for h, sl in enumerate(lanes):
                chains.append((q_ref[0, r, sl].astype(F32), i_ref[0, r, sl],
                               f_lat[d][0, r, sl].astype(F32), lbs[d][h], sts[d * heads + h], d))
                where.append((r, sl))
        outs, new = _hgrn_chunks(chains, tri_ref, lvl_ref, True)
        for o, (r, sl) in zip(outs, where):
            if second_visit:
                y = _rms(acc_scr[r, sl] + o, ng_ref[...]) * _silu(g_ref[0, r, sl].astype(F32))
                o_ref[0, r, sl] = y.astype(o_ref.dtype)
            else:
                acc_scr[r, sl] = o
        return tuple(new)

    sts = tuple(jnp.zeros((B_DIM, B_DIM), F32) for _ in range(2 * heads))
    sts = lax.fori_loop(0, nc_ctx, ctx_step, sts)
    sts = lax.fori_loop(0, nc_lat // 2, functools.partial(lat_step, second_visit=False), sts)
    lax.fori_loop(nc_lat // 2, nc_lat, functools.partial(lat_step, second_visit=True), sts)


def _hgrn(pl_x, pl_c, lb_logits, norm_g, heads_per_step):
    b, n, _ = pl_x.shape
    n_ctx = pl_c.shape[1]
    assert (n // HGRN_CHUNK) % 2 == 0 and B_HEADS % heads_per_step == 0
    w = heads_per_step * LANES
    steps = B_HEADS // heads_per_step
    col0 = 4 * A_WIDTH // w

    def xs(rows_, off):
        return pl.BlockSpec((1, rows_, w), lambda i, h: (i, 0, col0 + off * steps + h))

    def cs(group):
        return pl.BlockSpec((1, n_ctx, w), lambda i, h: (i, 0, group * steps + h))

    n_layers = lb_logits.shape[1]
    tri, lvl = _hgrn_tables(HGRN_CHUNK)
    return pl.pallas_call(
        _hgrn_kernel,
        grid=(b, steps),
        in_specs=[
            xs(n, 0), xs(n, 1), xs(n, 2), xs(n, 3), xs(n, 4),
            cs(CTX_COL_I), cs(CTX_COL_FF), cs(CTX_COL_FB),
            pl.BlockSpec((2, n_layers, 1, w), lambda i, h: (0, 0, 0, h)),
            _const_spec((1, LANES)),
            _const_spec(tri.shape),
            _const_spec(lvl.shape),
        ],
        out_specs=pl.BlockSpec((1, n, w), lambda i, h: (i, 0, h)),
        out_shape=jax.ShapeDtypeStruct((b, n, B_WIDTH), BF16),
        scratch_shapes=[pltpu.VMEM((n, w), F32)],
        compiler_params=_cparams("arbitrary", "arbitrary"),
        name="hgrn2",
    )(pl_x, pl_x, pl_x, pl_x, pl_x, pl_c, pl_c, pl_c,
      lb_logits.reshape(2, n_layers, 1, B_WIDTH), norm_g, tri, lvl)


def _mid_kernel(ya_ref, yb_ref, x_ref, mod0_ref, mod1_ref, g1_ref, wo_ref, wi_ref, x1_ref, p1_ref):
    d = x_ref.shape[-1]
    half = ya_ref.shape[-1]
    upd = (jnp.dot(ya_ref[0], wo_ref[0:half, :], preferred_element_type=F32)
           + jnp.dot(yb_ref[0], wo_ref[half:, :], preferred_element_type=F32))
    x1 = x_ref[0] + mod0_ref[0][:, 2 * d:] * upd
    x1_ref[0] = x1
    xm = _modulate(x1, g1_ref[...], mod1_ref[0], d)
    p1_ref[0] = jnp.dot(xm.astype(BF16), wi_ref[...], preferred_element_type=F32).astype(p1_ref.dtype)


def _mid(ya, yb, x, mod0, mod1, gain1, w_out0, w_in1, tb):
    b, n, d = x.shape
    half = ya.shape[-1]
    nout = w_in1.shape[1]
    tok = lambda width: pl.BlockSpec((1, tb, width), lambda i, j: (i, j, 0))
    modspec = pl.BlockSpec((1, 1, 3 * d), lambda i, j: (i, 0, 0))
    return pl.pallas_call(
        _mid_kernel,
        grid=(b, n // tb),
        in_specs=[tok(half), tok(half), tok(d), modspec, modspec, _const_spec((1, d)),
                  _const_spec((2 * half, d)), _const_spec((d, nout))],
        out_specs=[tok(d), tok(nout)],
        out_shape=[jax.ShapeDtypeStruct((b, n, d), F32), jax.ShapeDtypeStruct((b, n, nout), BF16)],
        compiler_params=_cparams("arbitrary", "arbitrary"),
        name="outproj_even_inproj_odd",
    )(ya, yb, x, mod0, mod1, gain1, w_out0, w_in1)


def _odd_kernel(p_ref, cp_ref, hp_ref, cn_ref, hn_ref, x_ref, mod_ref, vg_ref, ws_ref, bs_ref,
                cw_ref, wo_ref, o_ref):
    d = x_ref.shape[-1]
    tb = p_ref.shape[1]
    j = pl.program_id(1)
    col = lambda k: p_ref[0, :, k * C_WIDTH:(k + 1) * C_WIDTH].astype(F32)

    u = _gelu(col(0))
    vn = _rms(_gelu(col(1)), vg_ref[...]).astype(BF16)
    chunks = []
    for ci in range(tb // C_CHUNK):
        r = slice(ci * C_CHUNK, (ci + 1) * C_CHUNK)
        groups = []
        for g in range(C_GROUPS):
            gl = slice(g * LANES, (g + 1) * LANES)
            groups.append(jnp.dot(ws_ref[g], vn[r, gl], preferred_element_type=F32) + bs_ref[g])
        chunks.append(jnp.concatenate(groups, axis=1))
    s = jnp.concatenate(chunks, axis=0)
    o_c = u * s * _silu(col(2))

    z = col(4) * col(5)
    last = cp_ref.shape[1] - 1
    z_prev_row = cp_ref[0, last:, :].astype(F32) * hp_ref[0, last:, :].astype(F32)
    z_next_row = cn_ref[0, 0:1, :].astype(F32) * hn_ref[0, 0:1, :].astype(F32)
    z_prev_row = jnp.where(j == 0, 0.0, z_prev_row)
    z_next_row = jnp.where(j == pl.num_programs(1) - 1, 0.0, z_next_row)
    rowi = lax.broadcasted_iota(jnp.int32, z.shape, 0)
    z_prev = jnp.where(rowi == 0, z_prev_row, pltpu.roll(z, 1, 0))
    z_next = jnp.where(rowi == tb - 1, z_next_row, pltpu.roll(z, tb - 1, 0))
    conv = cw_ref[0:1, :] * z_prev + cw_ref[1:2, :] * z + cw_ref[2:3, :] * z_next
    o_d = col(3) * conv * _silu(col(6))

    upd = (jnp.dot(o_c.astype(BF16), wo_ref[0:C_WIDTH, :], preferred_element_type=F32)
           + jnp.dot(o_d.astype(BF16), wo_ref[C_WIDTH:, :], preferred_element_type=F32))
    o_ref[0] = x_ref[0] + mod_ref[0][:, 2 * d:] * upd


def _odd(p1, x1, mod1, v_gain, w_s, b_s, conv_w, w_out1, tb):
    b, n, d = x1.shape
    halo = SUBLANES
    nb = tb // halo
    last_blk = n // halo - 1
    c_col, h_col = 4, 5
    prev = lambda colblk: pl.BlockSpec(
        (1, halo, D_WIDTH), lambda i, j: (i, jnp.maximum(j * nb - 1, 0), colblk))
    nxt = lambda colblk: pl.BlockSpec(
        (1, halo, D_WIDTH), lambda i, j: (i, jnp.minimum((j + 1) * nb, last_blk), colblk))
    return pl.pallas_call(
        _odd_kernel,
        grid=(b, n // tb),
        in_specs=[
            pl.BlockSpec((1, tb, ODD_IN), lambda i, j: (i, j, 0)),
            prev(c_col), prev(h_col), nxt(c_col), nxt(h_col),
            pl.BlockSpec((1, tb, d), lambda i, j: (i, j, 0)),
            pl.BlockSpec((1, 1, 3 * d), lambda i, j: (i, 0, 0)),
            _const_spec((1, C_WIDTH)),
            _const_spec((C_GROUPS, C_CHUNK, C_CHUNK)),
            _const_spec((C_GROUPS, C_CHUNK, LANES)),
            _const_spec((3, D_WIDTH)),
            _const_spec((C_WIDTH + D_WIDTH, d)),
        ],
        out_specs=pl.BlockSpec((1, tb, d), lambda i, j: (i, j, 0)),
        out_shape=jax.ShapeDtypeStruct((b, n, d), F32),
        compiler_params=_cparams("arbitrary", "arbitrary"),
        name="odd_mix_outproj",
    )(p1, p1, p1, p1, p1, x1, mod1, v_gain, w_s, b_s, conv_w, w_out1)


def _rope_tables(n):
    rows_ = n // GRID_W
    row = jnp.repeat(jnp.arange(rows_, dtype=F32), GRID_W)
    col = jnp.tile(jnp.arange(GRID_W, dtype=F32), rows_)
    n_freq = A_HEAD_DIM // 4
    inv = ROPE_THETA ** (-jnp.arange(n_freq, dtype=F32) / n_freq)
    ang = jnp.concatenate([row[:, None] * inv, col[:, None] * inv], axis=-1)
    cos, sin = jnp.cos(ang), jnp.sin(ang)
    reps = LANES // A_HEAD_DIM
    return (jnp.tile(jnp.concatenate([cos, cos], axis=-1), (1, reps)),
            jnp.tile(jnp.concatenate([-sin, sin], axis=-1), (1, reps)))


def kernel(x, c, ctx, c_ctx, norm_gain, ada_w, ada_b, even_w_in, even_w_out, attn_qk_gain,
           attn_lambda, attn_subln_gain, hgrn_lb_logits, hgrn_norm_gain, odd_w_in, odd_w_out,
           gmlp_v_gain, gmlp_w_s, gmlp_b_s, conv_w):
    b, n, d = x.shape
    assert b + 1 <= COND_ROWS and n % 512 == 0 and ctx.shape[1] % HGRN_CHUNK == 0
    assert norm_gain.shape[0] == 2, "two-layer block: one even layer then one odd layer"

    cond = jnp.zeros((COND_ROWS, d), F32).at[:b].set(c).at[b].set(c_ctx)
    mod = _adaln(cond, ada_w, ada_b)
    mod0 = mod[0].reshape(COND_ROWS, 1, 3 * d)
    mod1 = mod[1].reshape(COND_ROWS, 1, 3 * d)

    w_in0 = even_w_in[0].astype(BF16)
    gain0 = norm_gain[0].reshape(1, d)
    pl_x = _inproj(x, mod0, lambda i: i, gain0, w_in0, TOKEN_BLOCK)
    w_ctx = jnp.concatenate([w_in0[:, g * A_WIDTH:(g + 1) * A_WIDTH] for g in CTX_GROUPS], axis=1)
    pl_c = _inproj(ctx, mod0, lambda i: b, gain0, w_ctx, ctx.shape[1])

    cos, sin_signed = _rope_tables(n)
    qk_gain = jnp.tile(attn_qk_gain[0], (1, LANES // A_HEAD_DIM))
    blk = jnp.arange(LANES) // A_HEAD_DIM
    bd = jnp.where(blk[:, None] == blk[None, :], 1.0 / A_HEAD_DIM, 0.0).astype(BF16)
    lam_init = 0.8 - 0.6 * math.exp(-0.3 * 0)
    ya = _attention(pl_x, pl_c, cos, sin_signed, qk_gain, bd,
                    attn_subln_gain[0].reshape(1, LANES), attn_lambda[0], lam_init, ATTN_Q_BLOCK)
    yb = _hgrn(pl_x, pl_c, hgrn_lb_logits, hgrn_norm_gain[0].reshape(1, LANES), HGRN_HEADS_PER_STEP)

    x1, p1 = _mid(ya, yb, x, mod0, mod1, norm_gain[1].reshape(1, d),
                  even_w_out[0].astype(BF16), odd_w_in[0].astype(BF16), TOKEN_BLOCK)
    b_s = jnp.broadcast_to(gmlp_b_s[0][:, :, None], (C_GROUPS, C_CHUNK, LANES))
    return _odd(p1, x1, mod1, gmlp_v_gain[0].reshape(1, C_WIDTH), gmlp_w_s[0].astype(BF16),
                b_s, conv_w[0], odd_w_out[0].astype(BF16), TOKEN_BLOCK)
```

```python
import functools
import math

import jax
import jax.numpy as jnp
import numpy as np
from jax import lax
from jax.experimental import pallas as pl
from jax.experimental.pallas import tpu as pltpu

F32 = jnp.float32
BF16 = jnp.bfloat16

EPS = 1e-6
GRID_W = 64
ROPE_THETA = 10000.0
A_HEADS = 4
A_HEAD_DIM = 64
A_WIDTH = 2 * A_HEADS * A_HEAD_DIM
B_HEADS = 4
B_DIM = 128
B_WIDTH = B_HEADS * B_DIM
C_GROUPS = 4
C_CHUNK = 128
C_WIDTH = 512
D_WIDTH = 512
EVEN_IN = 4 * A_WIDTH + 5 * B_WIDTH
ODD_IN = 3 * C_WIDTH + 4 * D_WIDTH

CTX_GROUPS = (1, 2, 5, 6, 7)
CTX_COL_K, CTX_COL_V, CTX_COL_I, CTX_COL_FF, CTX_COL_FB = range(5)

LANES = 128
HGRN_CHUNK = 128
HGRN_HEADS_PER_STEP = 4
HGRN_DIAG = 8
SCORE_BOUND = 100.0
EXP2_CLAMP = 115.0
LOG2E = math.log2(math.e)
TOKEN_BLOCK = 512
ATTN_Q_BLOCK = 256
SUBLANES = 8
L1_SUB_BLOCKS = 2
COND_ROWS = 16
VMEM_LIMIT = 56 * 1024 * 1024


def _cparams(*sem):
    return pltpu.CompilerParams(dimension_semantics=sem, vmem_limit_bytes=VMEM_LIMIT)


def _const_spec(shape):
    nd = len(shape)
    return pl.BlockSpec(shape, lambda *_: (0,) * nd, pipeline_mode=pl.Buffered(1))


def _silu(t):
    return t * jax.nn.sigmoid(t)


def _gelu(t):
    return 0.5 * t * (1.0 + lax.erf(t * (1.0 / math.sqrt(2.0))))


def _rms(t, gain):
    ms = jnp.mean(t * t, axis=-1, keepdims=True)
    return t * lax.rsqrt(ms + EPS) * gain


def _adaln_kernel(cond_ref, w_ref, b_ref, o_ref):
    a = _silu(cond_ref[...])
    o_ref[0] = jnp.dot(a, w_ref[0], preferred_element_type=F32) + b_ref[0]


def _adaln(cond, ada_w, ada_b):
    depth, d, n3 = ada_w.shape
    tn = 512
    return pl.pallas_call(
        _adaln_kernel,
        grid=(depth, n3 // tn),
        in_specs=[
            pl.BlockSpec((COND_ROWS, d), lambda l, j: (0, 0)),
            pl.BlockSpec((1, d, tn), lambda l, j: (l, 0, j)),
            pl.BlockSpec((1, 1, tn), lambda l, j: (l, 0, j)),
        ],
        out_specs=pl.BlockSpec((1, COND_ROWS, tn), lambda l, j: (l, 0, j)),
        out_shape=jax.ShapeDtypeStruct((depth, COND_ROWS, n3), F32),
        compiler_params=_cparams("arbitrary", "arbitrary"),
        name="adaln",
    )(cond, ada_w, ada_b.reshape(depth, 1, n3))


def _modulate(x, gain, mod, d):
    shift = mod[:, 0:d]
    scale = mod[:, d:2 * d]
    return _rms(x, gain) * (1.0 + scale) + shift


def _inproj_kernel(x_ref, mod_ref, g_ref, w_ref, o_ref):
    d = x_ref.shape[-1]
    xm = _modulate(x_ref[0], g_ref[...], mod_ref[0], d)
    o_ref[0] = jnp.dot(xm.astype(BF16), w_ref[...], preferred_element_type=F32).astype(o_ref.dtype)


def _inproj(x, mod, mod_row, gain, w, tb):
    b, n, d = x.shape
    nout = w.shape[1]
    return pl.pallas_call(
        _inproj_kernel,
        grid=(b, n // tb),
        in_specs=[
            pl.BlockSpec((1, tb, d), lambda i, j: (i, j, 0)),
            pl.BlockSpec((1, 1, 3 * d), lambda i, j: (mod_row(i), 0, 0)),
            _const_spec((1, d)),
            _const_spec((d, nout)),
        ],
        out_specs=pl.BlockSpec((1, tb, nout), lambda i, j: (i, j, 0)),
        out_shape=jax.ShapeDtypeStruct((b, n, nout), BF16),
        compiler_params=_cparams("arbitrary", "arbitrary"),
        name="inproj_even",
    )(x, mod, gain, w)


def _qk_norm(t, gain, bd):
    ms = jnp.dot((t * t).astype(BF16), bd, preferred_element_type=F32)
    return t * lax.rsqrt(ms + EPS) * gain


def _rope(t, cos, sin_signed):
    lane = lax.broadcasted_iota(jnp.int32, t.shape, 1)
    first = (lane % A_HEAD_DIM) < (A_HEAD_DIM // 2)
    partner = jnp.where(first,
                        pltpu.roll(t, LANES - A_HEAD_DIM // 2, 1),
                        pltpu.roll(t, A_HEAD_DIM // 2, 1))
    return t * cos + partner * sin_signed


def _attn_kernel(q_ref, k_ref, v_ref, g_ref, kc_ref, vc_ref, cos_ref, sin_ref,
                 qkg_ref, bd_ref, subg_ref, lamp_ref, o_ref, qm_scr, kn_scr, vt_scr,
                 *, lam_init, n_ctx):
    bd = bd_ref[...]
    tq = o_ref.shape[1]
    heads = [slice(h * LANES, (h + 1) * LANES) for h in range(A_HEADS)]
    gq = qkg_ref[0:1, :]
    gk = qkg_ref[1:2, :]
    q_scale = A_HEAD_DIM ** -0.5 * LOG2E

    @pl.when(pl.program_id(1) == 0)
    def _():
        lane = lax.broadcasted_iota(jnp.int32, (1, LANES), 1)
        cos, sin_signed = cos_ref[...], sin_ref[...]
        for h, sl in enumerate(heads):
            qn = _rope(_qk_norm(q_ref[0, :, sl].astype(F32), gq, bd), cos, sin_signed) * q_scale
            for m in range(2):
                qm_scr[m, :, sl] = jnp.where((lane // A_HEAD_DIM) == m, qn, 0.0).astype(BF16)
            kc = _qk_norm(kc_ref[0, :, sl].astype(F32), gk, bd)
            kn_scr[0:n_ctx, sl] = kc.astype(BF16)
            kl = _rope(_qk_norm(k_ref[0, :, sl].astype(F32), gk, bd), cos, sin_signed)
            kn_scr[n_ctx:, sl] = kl.astype(BF16)
            vt_scr[sl, 0:n_ctx] = vc_ref[0, :, sl].astype(F32).T.astype(BF16)
            vt_scr[sl, n_ctx:] = v_ref[0, :, sl].astype(F32).T.astype(BF16)

    lp = lamp_ref[...]
    lam = (jnp.exp(jnp.sum(lp[0:1] * lp[1:2], axis=-1, keepdims=True))
           - jnp.exp(jnp.sum(lp[2:3] * lp[3:4], axis=-1, keepdims=True)) + lam_init)
    score_bound = (A_HEAD_DIM * q_scale) * jnp.max(jnp.abs(gq)) * jnp.max(jnp.abs(gk))
    q_rows = pl.ds(pl.multiple_of(pl.program_id(1) * tq, tq), tq)

    def scores(h):
        return [lax.dot_general(kn_scr[:, heads[h]], qm_scr[m, q_rows, heads[h]],
                                (((1,), (1,)), ((), ())), preferred_element_type=F32)
                for m in range(2)]

    def run_heads(shift):
        s_next = scores(0)
        for h, sl in enumerate(heads):
            s_both = s_next
            if h + 1 < A_HEADS:
                s_next = scores(h + 1)
            probs = []
            for s in s_both:
                if shift:
                    s = s - jnp.max(s, axis=0, keepdims=True)
                p = jnp.exp2(s)
                probs.append((p, jnp.sum(p, axis=0, keepdims=True)))
            (p0, l0), (p1, l1) = probs
            w = p0 * (1.0 / l0) - p1 * (lam / l1)
            ot = jnp.dot(vt_scr[sl, :], w.astype(BF16), preferred_element_type=F32)
            ms = jnp.mean(ot * ot, axis=0, keepdims=True)
            on = (ot * lax.rsqrt(ms + EPS)).T * (subg_ref[...] * (1.0 - lam_init))
            o_ref[0, :, sl] = (on * _silu(g_ref[0, :, sl].astype(F32))).astype(o_ref.dtype)

    no_shift_ok = score_bound <= SCORE_BOUND
    pl.when(no_shift_ok)(functools.partial(run_heads, False))
    pl.when(jnp.logical_not(no_shift_ok))(functools.partial(run_heads, True))


def _attention(pl_x, pl_c, cos, sin_signed, qk_gain, bd, subln_g, lam_p, lam_init, tq):
    b, n, _ = pl_x.shape
    n_ctx = pl_c.shape[1]
    w = A_WIDTH
    kern = functools.partial(_attn_kernel, lam_init=lam_init, n_ctx=n_ctx)
    return pl.pallas_call(
        kern,
        grid=(b, n // tq),
        in_specs=[
            pl.BlockSpec((1, n, w), lambda i, j: (i, 0, 0)),
            pl.BlockSpec((1, n, w), lambda i, j: (i, 0, 1)),
            pl.BlockSpec((1, n, w), lambda i, j: (i, 0, 2)),
            pl.BlockSpec((1, tq, w), lambda i, j: (i, j, 3)),
            pl.BlockSpec((1, n_ctx, w), lambda i, j: (i, 0, CTX_COL_K)),
            pl.BlockSpec((1, n_ctx, w), lambda i, j: (i, 0, CTX_COL_V)),
            _const_spec((n, LANES)),
            _const_spec((n, LANES)),
            _const_spec((2, LANES)),
            _const_spec((LANES, LANES)),
            _const_spec((1, LANES)),
            _const_spec((4, A_HEAD_DIM)),
        ],
        out_specs=pl.BlockSpec((1, tq, w), lambda i, j: (i, j, 0)),
        out_shape=jax.ShapeDtypeStruct((b, n, w), BF16),
        scratch_shapes=[pltpu.VMEM((2, n, w), BF16),
                        pltpu.VMEM((n_ctx + n, w), BF16),
                        pltpu.VMEM((w, n_ctx + n), BF16)],
        compiler_params=_cparams("arbitrary", "arbitrary"),
        name="diff_attn",
    )(pl_x, pl_x, pl_x, pl_x, pl_c, pl_c, cos, sin_signed, qk_gain, bd, subln_g, lam_p)


def _split_bf16(t):
    hi = t.astype(BF16)
    return hi, (t - hi.astype(F32)).astype(BF16)


def _block_ref(g, block, row):
    c, w = g.shape
    g3 = g.reshape(c // block, block, w)
    return jnp.broadcast_to(g3[:, row:row + 1, :], g3.shape).reshape(c, w)


def _hgrn_tables(c):
    t = np.arange(c)[:, None]
    s = np.arange(c)[None, :]
    lvl = np.zeros((c, c), np.int32)
    lvl[(t // HGRN_DIAG == s // HGRN_DIAG) & (s <= t)] = 1
    b, k = HGRN_DIAG, 2
    while b < c:
        lvl[(t // b == s // b + 1) & ((s // b) % 2 == 0)] = k
        b, k = 2 * b, k + 1
    tri = (s <= t).astype(np.float32)
    return jnp.asarray(np.stack([tri, tri.T]), BF16), jnp.asarray(np.stack([lvl, lvl.T]))


def _hgrn_chunks(chains, tri_ref, lvl_ref, want_out):
    nt = (((1,), (1,)), ((), ()))
    tn = (((0,), (0,)), ((), ()))
    n = len(chains)
    c = chains[0][2].shape[0]

    kk, parts = [], []
    for (_, _, f_raw, lb, _, _) in chains:
        f = lb + (1.0 - lb) * jax.nn.sigmoid(f_raw)
        kk.append(1.0 - f)
        parts.append(_split_bf16(jnp.log(f) * LOG2E))
    cum = [sum(jnp.dot(tri_ref[ch[5]], p, preferred_element_type=F32) for p in parts[i])
           for i, ch in enumerate(chains)]
    edge = [cum[i][0:1, :] if ch[5] else cum[i][c - 1:c, :] for i, ch in enumerate(chains)]

    outs = [None] * n
    if want_out:
        a = []
        for i, (q, _, _, _, _, d) in enumerate(chains):
            ref = _block_ref(cum[i], HGRN_DIAG, HGRN_DIAG // 2)
            qd = (q * jnp.exp2(jnp.minimum(cum[i] - ref, EXP2_CLAMP))).astype(BF16)
            kd = (kk[i] * jnp.exp2(jnp.minimum(ref - cum[i], EXP2_CLAMP))).astype(BF16)
            a.append(jnp.where(lvl_ref[d] == 1,
                               lax.dot_general(qd, kd, nt, preferred_element_type=F32), 0.0))
        b, k = HGRN_DIAG, 2
        while b < c:
            for i, (q, _, _, _, _, d) in enumerate(chains):
                ref = _block_ref(cum[i], 2 * b, b if d else b - 1)
                decay = jnp.exp2(cum[i] - ref)
                ql = (q * decay).astype(BF16)
                kl = (kk[i] * (1.0 / decay)).astype(BF16)
                a[i] = jnp.where(lvl_ref[d] == k,
                                 lax.dot_general(ql, kl, nt, preferred_element_type=F32), a[i])
            b, k = 2 * b, k + 1
        for i, (q, v, _, _, st, _) in enumerate(chains):
            o = jnp.dot(a[i].astype(BF16), v, preferred_element_type=F32)
            outs[i] = o + lax.dot_general((q * jnp.exp2(cum[i])).astype(BF16), st.astype(BF16), nt,
                                          preferred_element_type=F32)

    sts = []
    for i, (_, v, _, _, st, _) in enumerate(chains):
        kg = (kk[i] * jnp.exp2(edge[i] - cum[i])).astype(BF16)
        upd = lax.dot_general(v, kg, tn, preferred_element_type=F32)
        sts.append(st * jnp.exp2(edge[i]) + upd)
    return outs, sts


def _hgrn_kernel(q_ref, i_ref, ff_ref, fb_ref, g_ref, ic_ref, ffc_ref, fbc_ref,
                 lbl_ref, ng_ref, tri_ref, lvl_ref, o_ref, acc_scr):
    c = HGRN_CHUNK
    heads = q_ref.shape[2] // LANES
    nc_lat = q_ref.shape[1] // c
    nc_ctx = ic_ref.shape[1] // c
    f_lat = (ff_ref, fb_ref)
    f_ctx = (ffc_ref, fbc_ref)

    def lower_bound(direction, sl):
        logits = [lbl_ref[direction, l, :, sl] for l in range(lbl_ref.shape[1])]
        top = functools.reduce(jnp.maximum, logits)
        e = [jnp.exp(t - top) for t in logits]
        return e[0] / sum(e)

    lanes = [slice(h * LANES, (h + 1) * LANES) for h in range(heads)]
    lbs = [[lower_bound(d, sl) for sl in lanes] for d in (0, 1)]

    def rows(i):
        return pl.ds(pl.multiple_of(i * c, c), c)

    def ctx_step(j, sts):
        chains = []
        for d in (0, 1):
            r = rows(nc_ctx - 1 - j if d else j)
            for h, sl in enumerate(lanes):
                chains.append((None, ic_ref[0, r, sl], f_ctx[d][0, r, sl].astype(F32),
                               lbs[d][h], sts[d * heads + h], d))
        return tuple(_hgrn_chunks(chains, tri_ref, lvl_ref, False)[1])

    def lat_step(j, sts, second_visit):
        chains, where = [], []
        for d in (0, 1):
            r = rows(nc_lat - 1 - j if d else j)
            for h, sl in enumerate(lanes):
                chains.append((q_ref[0, r, sl].astype(F32), i_ref[0, r, sl],
                               f_lat[d][0, r, sl].astype(F32), lbs[d][h], sts[d * heads + h], d))
                where.append((r, sl))
        outs, new = _hgrn_chunks(chains, tri_ref, lvl_ref, True)
        for o, (r, sl) in zip(outs, where):
            if second_visit:
                y = _rms(acc_scr[r, sl] + o, ng_ref[...]) * _silu(g_ref[0, r, sl].astype(F32))
                o_ref[0, r, sl] = y.astype(o_ref.dtype)
            else:
                acc_scr[r, sl] = o
        return tuple(new)

    sts = tuple(jnp.zeros((B_DIM, B_DIM), F32) for _ in range(2 * heads))
    sts = lax.fori_loop(0, nc_ctx, ctx_step, sts)
    sts = lax.fori_loop(0, nc_lat // 2, functools.partial(lat_step, second_visit=False), sts)
    lax.fori_loop(nc_lat // 2, nc_lat, functools.partial(lat_step, second_visit=True), sts)


def _hgrn(pl_x, pl_c, lb_logits, norm_g, heads_per_step):
    b, n, _ = pl_x.shape
    n_ctx = pl_c.shape[1]
    assert (n // HGRN_CHUNK) % 2 == 0 and B_HEADS % heads_per_step == 0
    w = heads_per_step * LANES
    steps = B_HEADS // heads_per_step
    col0 = 4 * A_WIDTH // w

    def xs(rows_, off):
        return pl.BlockSpec((1, rows_, w), lambda i, h: (i, 0, col0 + off * steps + h))

    def cs(group):
        return pl.BlockSpec((1, n_ctx, w), lambda i, h: (i, 0, group * steps + h))

    n_layers = lb_logits.shape[1]
    tri, lvl = _hgrn_tables(HGRN_CHUNK)
    return pl.pallas_call(
        _hgrn_kernel,
        grid=(b, steps),
        in_specs=[
            xs(n, 0), xs(n, 1), xs(n, 2), xs(n, 3), xs(n, 4),
            cs(CTX_COL_I), cs(CTX_COL_FF), cs(CTX_COL_FB),
            pl.BlockSpec((2, n_layers, 1, w), lambda i, h: (0, 0, 0, h)),
            _const_spec((1, LANES)),
            _const_spec(tri.shape),
            _const_spec(lvl.shape),
        ],
        out_specs=pl.BlockSpec((1, n, w), lambda i, h: (i, 0, h)),
        out_shape=jax.ShapeDtypeStruct((b, n, B_WIDTH), BF16),
        scratch_shapes=[pltpu.VMEM((n, w), F32)],
        compiler_params=_cparams("arbitrary", "arbitrary"),
        name="hgrn2",
    )(pl_x, pl_x, pl_x, pl_x, pl_x, pl_c, pl_c, pl_c,
      lb_logits.reshape(2, n_layers, 1, B_WIDTH), norm_g, tri, lvl)


def _layer1_input(ya, yb, x, gate0, mod1, gain1, wo0_ref):
    d = x.shape[-1]
    half = ya.shape[-1]
    upd = (jnp.dot(ya, wo0_ref[0:half, :], preferred_element_type=F32)
           + jnp.dot(yb, wo0_ref[half:, :], preferred_element_type=F32))
    x1 = x + gate0 * upd
    return x1, _modulate(x1, gain1, mod1, d).astype(BF16)


def _edge_kernel(ya_ref, yb_ref, x_ref, mod0_ref, mod1_ref, g1_ref, wo0_ref, wi_ref, z_ref,
                 *, rows_per_batch):
    d = x_ref.shape[-1]
    parts = []
    for i in range(x_ref.shape[0] // rows_per_batch):
        r = slice(i * rows_per_batch, (i + 1) * rows_per_batch)
        _, xm = _layer1_input(ya_ref[r, :], yb_ref[r, :], x_ref[r, :], mod0_ref[i][:, 2 * d:],
                              mod1_ref[i], g1_ref[...], wo0_ref)
        p = jnp.dot(xm, wi_ref[...], preferred_element_type=F32)
        parts.append(p[:, :D_WIDTH] * p[:, D_WIDTH:])
    z_ref[...] = jnp.concatenate(parts, axis=0)


def _block_edges(t, tb):
    b, n, w = t.shape
    te = t.reshape(b, n // tb, tb, w)
    return jnp.concatenate([te[:, :, :SUBLANES], te[:, :, tb - SUBLANES:]], axis=2).reshape(-1, w)


def _edge_z(ya, yb, x, mod0, mod1, gain1, w_out0, w_in1, tb):
    b, n, d = x.shape
    rows = (n // tb) * 2 * SUBLANES
    cg_start = 3 * C_WIDTH + D_WIDTH
    assert cg_start % (2 * D_WIDTH) == 0
    cg_blk = cg_start // (2 * D_WIDTH)
    full = lambda arr: _const_spec(arr.shape)
    xe, yae, ybe = _block_edges(x, tb), _block_edges(ya, tb), _block_edges(yb, tb)
    return pl.pallas_call(
        functools.partial(_edge_kernel, rows_per_batch=rows),
        grid=(1,),
        in_specs=[full(yae), full(ybe), full(xe), full(mod0), full(mod1), _const_spec((1, d)),
                  full(w_out0),
                  pl.BlockSpec((d, 2 * D_WIDTH), lambda i: (0, cg_blk), pipeline_mode=pl.Buffered(1))],
        out_specs=pl.BlockSpec((b * rows, D_WIDTH), lambda i: (0, 0)),
        out_shape=jax.ShapeDtypeStruct((b * rows, D_WIDTH), F32),
        compiler_params=_cparams("arbitrary"),
        name="conv_edge_rows",
    )(yae, ybe, xe, mod0, mod1, gain1, w_out0, w_in1).reshape(b, rows, D_WIDTH)


def _layer1_kernel(ya_ref, yb_ref, x_ref, ze_ref, mod0_ref, mod1_ref, g1_ref, wo0_ref, wi_ref,
                   vg_ref, ws_ref, bs_ref, cw_ref, wo1_ref, o_ref):
    d = x_ref.shape[-1]
    tb = x_ref.shape[1]
    j = pl.program_id(1)
    last_j = pl.num_programs(1) - 1
    gate0 = mod0_ref[0][:, 2 * d:]
    gate1 = mod1_ref[0][:, 2 * d:]
    sub = tb // L1_SUB_BLOCKS
    subs = [slice(s * sub, (s + 1) * sub) for s in range(L1_SUB_BLOCKS)]
    col = lambda p, k: p[:, k * C_WIDTH:(k + 1) * C_WIDTH]
    n_gmlp = 3 * C_WIDTH

    x1s, xms = [], []
    for r in subs:
        x1, xm = _layer1_input(ya_ref[0, r, :], yb_ref[0, r, :], x_ref[0, r, :], gate0,
                               mod1_ref[0], g1_ref[...], wo0_ref)
        x1s.append(x1)
        xms.append(xm)
    pgs = [jnp.dot(xm, wi_ref[:, 0:n_gmlp], preferred_element_type=F32) for xm in xms]
    pcs = [jnp.dot(xm, wi_ref[:, n_gmlp:], preferred_element_type=F32) for xm in xms]

    upd_c = []
    for pg in pgs:
        u = _gelu(col(pg, 0))
        vn = _rms(_gelu(col(pg, 1)), vg_ref[...]).astype(BF16)
        chunks = []
        for ci in range(sub // C_CHUNK):
            cr = slice(ci * C_CHUNK, (ci + 1) * C_CHUNK)
            groups = []
            for g in range(C_GROUPS):
                gl = slice(g * LANES, (g + 1) * LANES)
                groups.append(jnp.dot(ws_ref[g], vn[cr, gl], preferred_element_type=F32) + bs_ref[g])
            chunks.append(jnp.concatenate(groups, axis=1))
        o_c = u * jnp.concatenate(chunks, axis=0) * _silu(col(pg, 2))
        upd_c.append(jnp.dot(o_c.astype(BF16), wo1_ref[0:C_WIDTH, :], preferred_element_type=F32))

    z = jnp.concatenate([col(pc, 1) * col(pc, 2) for pc in pcs], axis=0)
    grp = 2 * SUBLANES
    prev_grp = ze_ref[0, pl.ds(pl.multiple_of(jnp.maximum(j - 1, 0) * grp + SUBLANES, SUBLANES),
                               SUBLANES), :]
    next_grp = ze_ref[0, pl.ds(pl.multiple_of(jnp.minimum(j + 1, last_j) * grp, SUBLANES),
                               SUBLANES), :]
    z_prev_row = jnp.where(j == 0, 0.0, prev_grp[SUBLANES - 1:, :])
    z_next_row = jnp.where(j == last_j, 0.0, next_grp[0:1, :])
    rowi = lax.broadcasted_iota(jnp.int32, z.shape, 0)
    z_prev = jnp.where(rowi == 0, z_prev_row, pltpu.roll(z, 1, 0))
    z_next = jnp.where(rowi == tb - 1, z_next_row, pltpu.roll(z, tb - 1, 0))
    conv = cw_ref[0:1, :] * z_prev + cw_ref[1:2, :] * z + cw_ref[2:3, :] * z_next

    for r, x1, pc, uc in zip(subs, x1s, pcs, upd_c):
        o_d = col(pc, 0) * conv[r, :] * _silu(col(pc, 3))
        upd = uc + jnp.dot(o_d.astype(BF16), wo1_ref[C_WIDTH:, :], preferred_element_type=F32)
        o_ref[0, r, :] = x1 + gate1 * upd


def _layer1(ya, yb, x, ze, mod0, mod1, gain1, w_out0, w_in1, v_gain, w_s, b_s, conv_w, w_out1, tb):
    b, n, d = x.shape
    half = ya.shape[-1]
    tok = lambda width: pl.BlockSpec((1, tb, width), lambda i, j: (i, j, 0))
    modspec = pl.BlockSpec((1, 1, 3 * d), lambda i, j: (i, 0, 0))
    return pl.pallas_call(
        _layer1_kernel,
        grid=(b, n // tb),
        in_specs=[tok(half), tok(half), tok(d),
                  pl.BlockSpec((1,) + ze.shape[1:], lambda i, j: (i, 0, 0)),
                  modspec, modspec, _const_spec((1, d)),
                  _const_spec(w_out0.shape), _const_spec(w_in1.shape),
                  _const_spec((1, C_WIDTH)),
                  _const_spec((C_GROUPS, C_CHUNK, C_CHUNK)),
                  _const_spec((C_GROUPS, C_CHUNK, LANES)),
                  _const_spec((3, D_WIDTH)),
                  _const_spec(w_out1.shape)],
        out_specs=tok(d),
        out_shape=jax.ShapeDtypeStruct((b, n, d), F32),
        compiler_params=_cparams("arbitrary", "arbitrary"),
        name="outproj_even_layer_odd",
    )(ya, yb, x, ze, mod0, mod1, gain1, w_out0, w_in1, v_gain, w_s, b_s, conv_w, w_out1)


def _rope_tables(n):
    rows_ = n // GRID_W
    row = jnp.repeat(jnp.arange(rows_, dtype=F32), GRID_W)
    col = jnp.tile(jnp.arange(GRID_W, dtype=F32), rows_)
    n_freq = A_HEAD_DIM // 4
    inv = ROPE_THETA ** (-jnp.arange(n_freq, dtype=F32) / n_freq)
    ang = jnp.concatenate([row[:, None] * inv, col[:, None] * inv], axis=-1)
    cos, sin = jnp.cos(ang), jnp.sin(ang)
    reps = LANES // A_HEAD_DIM
    return (jnp.tile(jnp.concatenate([cos, cos], axis=-1), (1, reps)),
            jnp.tile(jnp.concatenate([-sin, sin], axis=-1), (1, reps)))


def kernel(x, c, ctx, c_ctx, norm_gain, ada_w, ada_b, even_w_in, even_w_out, attn_qk_gain,
           attn_lambda, attn_subln_gain, hgrn_lb_logits, hgrn_norm_gain, odd_w_in, odd_w_out,
           gmlp_v_gain, gmlp_w_s, gmlp_b_s, conv_w):
    b, n, d = x.shape
    assert b + 1 <= COND_ROWS and n % 512 == 0 and ctx.shape[1] % HGRN_CHUNK == 0
    assert norm_gain.shape[0] == 2, "two-layer block: one even layer then one odd layer"

    cond = jnp.zeros((COND_ROWS, d), F32).at[:b].set(c).at[b].set(c_ctx)
    mod = _adaln(cond, ada_w, ada_b)
    mod0 = mod[0].reshape(COND_ROWS, 1, 3 * d)
    mod1 = mod[1].reshape(COND_ROWS, 1, 3 * d)

    w_in0 = even_w_in[0].astype(BF16)
    gain0 = norm_gain[0].reshape(1, d)
    pl_x = _inproj(x, mod0, lambda i: i, gain0, w_in0, TOKEN_BLOCK)
    w_ctx = jnp.concatenate([w_in0[:, g * A_WIDTH:(g + 1) * A_WIDTH] for g in CTX_GROUPS], axis=1)
    pl_c = _inproj(ctx, mod0, lambda i: b, gain0, w_ctx, ctx.shape[1])

    cos, sin_signed = _rope_tables(n)
    qk_gain = jnp.tile(attn_qk_gain[0], (1, LANES // A_HEAD_DIM))
    blk = jnp.arange(LANES) // A_HEAD_DIM
    bd = jnp.where(blk[:, None] == blk[None, :], 1.0 / A_HEAD_DIM, 0.0).astype(BF16)
    lam_init = 0.8 - 0.6 * math.exp(-0.3 * 0)
    ya = _attention(pl_x, pl_c, cos, sin_signed, qk_gain, bd,
                    attn_subln_gain[0].reshape(1, LANES), attn_lambda[0], lam_init, ATTN_Q_BLOCK)
    yb = _hgrn(pl_x, pl_c, hgrn_lb_logits, hgrn_norm_gain[0].reshape(1, LANES), HGRN_HEADS_PER_STEP)

    gain1 = norm_gain[1].reshape(1, d)
    w_out0, w_in1 = even_w_out[0].astype(BF16), odd_w_in[0].astype(BF16)
    ze = _edge_z(ya, yb, x, mod0, mod1, gain1, w_out0, w_in1, TOKEN_BLOCK)
    b_s = jnp.broadcast_to(gmlp_b_s[0][:, :, None], (C_GROUPS, C_CHUNK, LANES))
    return _layer1(ya, yb, x, ze, mod0, mod1, gain1, w_out0, w_in1,
                   gmlp_v_gain[0].reshape(1, C_WIDTH), gmlp_w_s[0].astype(BF16), b_s, conv_w[0],
                   odd_w_out[0].astype(BF16), TOKEN_BLOCK)
```

```python
import functools
import math

import jax
import jax.numpy as jnp
import numpy as np
from jax import lax
from jax.experimental import pallas as pl
from jax.experimental.pallas import tpu as pltpu

F32 = jnp.float32
BF16 = jnp.bfloat16

EPS = 1e-6
GRID_W = 64
ROPE_THETA = 10000.0
A_HEADS = 4
A_HEAD_DIM = 64
A_WIDTH = 2 * A_HEADS * A_HEAD_DIM
B_HEADS = 4
B_DIM = 128
B_WIDTH = B_HEADS * B_DIM
C_GROUPS = 4
C_CHUNK = 128
C_WIDTH = 512
D_WIDTH = 512
EVEN_IN = 4 * A_WIDTH + 5 * B_WIDTH
ODD_IN = 3 * C_WIDTH + 4 * D_WIDTH

CTX_GROUPS = (1, 2, 5, 6, 7)
CTX_COL_K, CTX_COL_V, CTX_COL_I, CTX_COL_FF, CTX_COL_FB = range(5)

LANES = 128
HGRN_CHUNK = 128
HGRN_HEADS_PER_STEP = 4
HGRN_DIAG = 8
SCORE_BOUND = 100.0
EXP2_CLAMP = 115.0
LOG2E = math.log2(math.e)
TOKEN_BLOCK = 512
ATTN_Q_BLOCK = 256
SUBLANES = 8
L1_SUB_BLOCKS = 2
COND_ROWS = 16
ADALN_COL_BLOCK = 512
VMEM_LIMIT = 56 * 1024 * 1024


def _cparams(*sem):
    return pltpu.CompilerParams(dimension_semantics=sem, vmem_limit_bytes=VMEM_LIMIT)


def _const_spec(shape):
    nd = len(shape)
    return pl.BlockSpec(shape, lambda *_: (0,) * nd, pipeline_mode=pl.Buffered(1))


def _silu(t):
    return t * jax.nn.sigmoid(t)


def _gelu(t):
    return 0.5 * t * (1.0 + lax.erf(t * (1.0 / math.sqrt(2.0))))


def _rms(t, gain):
    ms = jnp.mean(t * t, axis=-1, keepdims=True)
    return t * lax.rsqrt(ms + EPS) * gain


def _adaln_kernel(cond_ref, w_ref, b_ref, *o_refs):
    a = _silu(cond_ref[...])
    for layer, o_ref in enumerate(o_refs):
        o_ref[...] = jnp.dot(a, w_ref[layer], preferred_element_type=F32) + b_ref[layer]


def _adaln(cond, ada_w, ada_b):
    depth, d, n3 = ada_w.shape
    tn = ADALN_COL_BLOCK
    out = pl.BlockSpec((COND_ROWS, tn), lambda j: (0, j))
    return pl.pallas_call(
        _adaln_kernel,
        grid=(n3 // tn,),
        in_specs=[
            pl.BlockSpec((COND_ROWS, d), lambda j: (0, 0)),
            pl.BlockSpec((depth, d, tn), lambda j: (0, 0, j)),
            pl.BlockSpec((depth, 1, tn), lambda j: (0, 0, j)),
        ],
        out_specs=[out] * depth,
        out_shape=[jax.ShapeDtypeStruct((COND_ROWS, n3), F32)] * depth,
        compiler_params=_cparams("arbitrary"),
        name="adaln",
    )(cond, ada_w, ada_b.reshape(depth, 1, n3))


def _modulate(x, gain, mod, d):
    shift = mod[:, 0:d]
    scale = mod[:, d:2 * d]
    return _rms(x, gain) * (1.0 + scale) + shift


def _inproj_kernel(x_ref, mod_ref, g_ref, w_ref, o_ref, *, mod_row):
    d = x_ref.shape[-1]
    row = pl.program_id(0) if mod_row is None else mod_row
    xm = _modulate(x_ref[0], g_ref[...], mod_ref[pl.ds(row, 1), :], d)
    o_ref[0] = jnp.dot(xm.astype(BF16), w_ref[...], preferred_element_type=F32).astype(o_ref.dtype)


def _inproj(x, mod, mod_row, gain, w, tb):
    b, n, d = x.shape
    nout = w.shape[1]
    return pl.pallas_call(
        functools.partial(_inproj_kernel, mod_row=mod_row),
        grid=(b, n // tb),
        in_specs=[
            pl.BlockSpec((1, tb, d), lambda i, j: (i, j, 0)),
            _const_spec(mod.shape),
            _const_spec((1, d)),
            _const_spec((d, nout)),
        ],
        out_specs=pl.BlockSpec((1, tb, nout), lambda i, j: (i, j, 0)),
        out_shape=jax.ShapeDtypeStruct((b, n, nout), BF16),
        compiler_params=_cparams("arbitrary", "arbitrary"),
        name="inproj_even",
    )(x, mod, gain, w)


def _qk_norm(t, gain, bd):
    ms = jnp.dot((t * t).astype(BF16), bd, preferred_element_type=F32)
    return t * lax.rsqrt(ms + EPS) * gain


def _rope(t, cos, sin_signed):
    lane = lax.broadcasted_iota(jnp.int32, t.shape, 1)
    first = (lane % A_HEAD_DIM) < (A_HEAD_DIM // 2)
    partner = jnp.where(first,
                        pltpu.roll(t, LANES - A_HEAD_DIM // 2, 1),
                        pltpu.roll(t, A_HEAD_DIM // 2, 1))
    return t * cos + partner * sin_signed


def _attn_kernel(q_ref, k_ref, v_ref, g_ref, kc_ref, vc_ref, cos_ref, sin_ref,
                 qkg_ref, bd_ref, subg_ref, lamp_ref, o_ref, qm_scr, kn_scr, vt_scr,
                 *, lam_init, n_ctx):
    bd = bd_ref[...]
    tq = o_ref.shape[1]
    heads = [slice(h * LANES, (h + 1) * LANES) for h in range(A_HEADS)]
    gq = qkg_ref[0:1, :]
    gk = qkg_ref[1:2, :]
    q_scale = A_HEAD_DIM ** -0.5 * LOG2E

    @pl.when(pl.program_id(1) == 0)
    def _():
        lane = lax.broadcasted_iota(jnp.int32, (1, LANES), 1)
        cos, sin_signed = cos_ref[...], sin_ref[...]
        for h, sl in enumerate(heads):
            qn = _rope(_qk_norm(q_ref[0, :, sl].astype(F32), gq, bd), cos, sin_signed) * q_scale
            for m in range(2):
                qm_scr[m, :, sl] = jnp.where((lane // A_HEAD_DIM) == m, qn, 0.0).astype(BF16)
            kc = _qk_norm(kc_ref[0, :, sl].astype(F32), gk, bd)
            kn_scr[0:n_ctx, sl] = kc.astype(BF16)
            kl = _rope(_qk_norm(k_ref[0, :, sl].astype(F32), gk, bd), cos, sin_signed)
            kn_scr[n_ctx:, sl] = kl.astype(BF16)
            vt_scr[sl, 0:n_ctx] = vc_ref[0, :, sl].astype(F32).T.astype(BF16)
            vt_scr[sl, n_ctx:] = v_ref[0, :, sl].astype(F32).T.astype(BF16)

    lp = lamp_ref[...]
    lam = (jnp.exp(jnp.sum(lp[0:1] * lp[1:2], axis=-1, keepdims=True))
           - jnp.exp(jnp.sum(lp[2:3] * lp[3:4], axis=-1, keepdims=True)) + lam_init)
    score_bound = (A_HEAD_DIM * q_scale) * jnp.max(jnp.abs(gq)) * jnp.max(jnp.abs(gk))
    q_rows = pl.ds(pl.multiple_of(pl.program_id(1) * tq, tq), tq)

    def scores(h):
        return [lax.dot_general(kn_scr[:, heads[h]], qm_scr[m, q_rows, heads[h]],
                                (((1,), (1,)), ((), ())), preferred_element_type=F32)
                for m in range(2)]

    def run_heads(shift):
        s_next = scores(0)
        for h, sl in enumerate(heads):
            s_both = s_next
            if h + 1 < A_HEADS:
                s_next = scores(h + 1)
            probs = []
            for s in s_both:
                if shift:
                    s = s - jnp.max(s, axis=0, keepdims=True)
                p = jnp.exp2(s)
                probs.append((p, jnp.sum(p, axis=0, keepdims=True)))
            (p0, l0), (p1, l1) = probs
            w = p0 * (1.0 / l0) - p1 * (lam / l1)
            ot = jnp.dot(vt_scr[sl, :], w.astype(BF16), preferred_element_type=F32)
            ms = jnp.mean(ot * ot, axis=0, keepdims=True)
            on = (ot * lax.rsqrt(ms + EPS)).T * (subg_ref[...] * (1.0 - lam_init))
            o_ref[0, :, sl] = (on * _silu(g_ref[0, :, sl].astype(F32))).astype(o_ref.dtype)

    no_shift_ok = score_bound <= SCORE_BOUND
    pl.when(no_shift_ok)(functools.partial(run_heads, False))
    pl.when(jnp.logical_not(no_shift_ok))(functools.partial(run_heads, True))


def _attention(pl_x, pl_c, cos, sin_signed, qk_gain, bd, subln_g, lam_p, lam_init, tq):
    b, n, _ = pl_x.shape
    n_ctx = pl_c.shape[1]
    w = A_WIDTH
    kern = functools.partial(_attn_kernel, lam_init=lam_init, n_ctx=n_ctx)
    return pl.pallas_call(
        kern,
        grid=(b, n // tq),
        in_specs=[
            pl.BlockSpec((1, n, w), lambda i, j: (i, 0, 0)),
            pl.BlockSpec((1, n, w), lambda i, j: (i, 0, 1)),
            pl.BlockSpec((1, n, w), lambda i, j: (i, 0, 2)),
            pl.BlockSpec((1, tq, w), lambda i, j: (i, j, 3)),
            pl.BlockSpec((1, n_ctx, w), lambda i, j: (i, 0, CTX_COL_K)),
            pl.BlockSpec((1, n_ctx, w), lambda i, j: (i, 0, CTX_COL_V)),
            _const_spec((n, LANES)),
            _const_spec((n, LANES)),
            _const_spec((2, LANES)),
            _const_spec((LANES, LANES)),
            _const_spec((1, LANES)),
            _const_spec((4, A_HEAD_DIM)),
        ],
        out_specs=pl.BlockSpec((1, tq, w), lambda i, j: (i, j, 0)),
        out_shape=jax.ShapeDtypeStruct((b, n, w), BF16),
        scratch_shapes=[pltpu.VMEM((2, n, w), BF16),
                        pltpu.VMEM((n_ctx + n, w), BF16),
                        pltpu.VMEM((w, n_ctx + n), BF16)],
        compiler_params=_cparams("arbitrary", "arbitrary"),
        name="diff_attn",
    )(pl_x, pl_x, pl_x, pl_x, pl_c, pl_c, cos, sin_signed, qk_gain, bd, subln_g, lam_p)


def _split_bf16(t):
    hi = t.astype(BF16)
    return hi, (t - hi.astype(F32)).astype(BF16)


def _block_ref(g, block, row):
    c, w = g.shape
    g3 = g.reshape(c // block, block, w)
    return jnp.broadcast_to(g3[:, row:row + 1, :], g3.shape).reshape(c, w)


def _hgrn_tables(c):
    t = np.arange(c)[:, None]
    s = np.arange(c)[None, :]
    lvl = np.zeros((c, c), np.int32)
    lvl[(t // HGRN_DIAG == s // HGRN_DIAG) & (s <= t)] = 1
    b, k = HGRN_DIAG, 2
    while b < c:
        lvl[(t // b == s // b + 1) & ((s // b) % 2 == 0)] = k
        b, k = 2 * b, k + 1
    tri = (s <= t).astype(np.float32)
    return jnp.asarray(np.stack([tri, tri.T]), BF16), jnp.asarray(np.stack([lvl, lvl.T]))


def _hgrn_chunks(chains, tri_ref, lvl_ref, want_out):
    nt = (((1,), (1,)), ((), ()))
    tn = (((0,), (0,)), ((), ()))
    n = len(chains)
    c = chains[0][2].shape[0]

    kk, parts = [], []
    for (_, _, f_raw, lb, _, _) in chains:
        f = lb + (1.0 - lb) * jax.nn.sigmoid(f_raw)
        kk.append(1.0 - f)
        parts.append(_split_bf16(jnp.log(f) * LOG2E))
    cum = [sum(jnp.dot(tri_ref[ch[5]], p, preferred_element_type=F32) for p in parts[i])
           for i, ch in enumerate(chains)]
    edge = [cum[i][0:1, :] if ch[5] else cum[i][c - 1:c, :] for i, ch in enumerate(chains)]

    outs = [None] * n
    if want_out:
        a = []
        for i, (q, _, _, _, _, d) in enumerate(chains):
            ref = _block_ref(cum[i], HGRN_DIAG, HGRN_DIAG // 2)
            qd = (q * jnp.exp2(jnp.minimum(cum[i] - ref, EXP2_CLAMP))).astype(BF16)
            kd = (kk[i] * jnp.exp2(jnp.minimum(ref - cum[i], EXP2_CLAMP))).astype(BF16)
            a.append(jnp.where(lvl_ref[d] == 1,
                               lax.dot_general(qd, kd, nt, preferred_element_type=F32), 0.0))
        b, k = HGRN_DIAG, 2
        while b < c:
            for i, (q, _, _, _, _, d) in enumerate(chains):
                ref = _block_ref(cum[i], 2 * b, b if d else b - 1)
                decay = jnp.exp2(cum[i] - ref)
                ql = (q * decay).astype(BF16)
                kl = (kk[i] * (1.0 / decay)).astype(BF16)
                a[i] = jnp.where(lvl_ref[d] == k,
                                 lax.dot_general(ql, kl, nt, preferred_element_type=F32), a[i])
            b, k = 2 * b, k + 1
        for i, (q, v, _, _, st, _) in enumerate(chains):
            o = jnp.dot(a[i].astype(BF16), v, preferred_element_type=F32)
            outs[i] = o + lax.dot_general((q * jnp.exp2(cum[i])).astype(BF16), st.astype(BF16), nt,
                                          preferred_element_type=F32)

    sts = []
    for i, (_, v, _, _, st, _) in enumerate(chains):
        kg = (kk[i] * jnp.exp2(edge[i] - cum[i])).astype(BF16)
        upd = lax.dot_general(v, kg, tn, preferred_element_type=F32)
        sts.append(st * jnp.exp2(edge[i]) + upd)
    return outs, sts


def _hgrn_kernel(q_ref, i_ref, ff_ref, fb_ref, g_ref, ic_ref, ffc_ref, fbc_ref,
                 lbl_ref, ng_ref, tri_ref, lvl_ref, o_ref, acc_scr):
    c = HGRN_CHUNK
    heads = q_ref.shape[2] // LANES
    nc_lat = q_ref.shape[1] // c
    nc_ctx = ic_ref.shape[1] // c
    f_lat = (ff_ref, fb_ref)
    f_ctx = (ffc_ref, fbc_ref)

    def lower_bound(direction, sl):
        logits = [lbl_ref[direction, l, :, sl] for l in range(lbl_ref.shape[1])]
        top = functools.reduce(jnp.maximum, logits)
        e = [jnp.exp(t - top) for t in logits]
        return e[0] / sum(e)

    lanes = [slice(h * LANES, (h + 1) * LANES) for h in range(heads)]
    lbs = [[lower_bound(d, sl) for sl in lanes] for d in (0, 1)]

    def rows(i):
        return pl.ds(pl.multiple_of(i * c, c), c)

    def ctx_step(j, sts):
        chains = []
        for d in (0, 1):
            r = rows(nc_ctx - 1 - j if d else j)
            for h, sl in enumerate(lanes):
                chains.append((None, ic_ref[0, r, sl], f_ctx[d][0, r, sl].astype(F32),
                               lbs[d][h], sts[d * heads + h], d))
        return tuple(_hgrn_chunks(chains, tri_ref, lvl_ref, False)[1])

    def lat_step(j, sts, second_visit):
        chains, where = [], []
        for d in (0, 1):
            r = rows(nc_lat - 1 - j if d else j)
            for h, sl in enumerate(lanes):
                chains.append((q_ref[0, r, sl].astype(F32), i_ref[0, r, sl],
                               f_lat[d][0, r, sl].astype(F32), lbs[d][h], sts[d * heads + h], d))
                where.append((r, sl))
        outs, new = _hgrn_chunks(chains, tri_ref, lvl_ref, True)
        for o, (r, sl) in zip(outs, where):
            if second_visit:
                y = _rms(acc_scr[r, sl] + o, ng_ref[...]) * _silu(g_ref[0, r, sl].astype(F32))
                o_ref[0, r, sl] = y.astype(o_ref.dtype)
            else:
                acc_scr[r, sl] = o
        return tuple(new)

    sts = tuple(jnp.zeros((B_DIM, B_DIM), F32) for _ in range(2 * heads))
    sts = lax.fori_loop(0, nc_ctx, ctx_step, sts)
    sts = lax.fori_loop(0, nc_lat // 2, functools.partial(lat_step, second_visit=False), sts,
                        unroll=2)
    lax.fori_loop(nc_lat // 2, nc_lat, functools.partial(lat_step, second_visit=True), sts,
                  unroll=2)


def _hgrn(pl_x, pl_c, lb_logits, norm_g, heads_per_step):
    b, n, _ = pl_x.shape
    n_ctx = pl_c.shape[1]
    assert (n // HGRN_CHUNK) % 2 == 0 and B_HEADS % heads_per_step == 0
    w = heads_per_step * LANES
    steps = B_HEADS // heads_per_step
    col0 = 4 * A_WIDTH // w

    def xs(rows_, off):
        return pl.BlockSpec((1, rows_, w), lambda i, h: (i, 0, col0 + off * steps + h))

    def cs(group):
        return pl.BlockSpec((1, n_ctx, w), lambda i, h: (i, 0, group * steps + h))

    n_layers = lb_logits.shape[1]
    tri, lvl = _hgrn_tables(HGRN_CHUNK)
    return pl.pallas_call(
        _hgrn_kernel,
        grid=(b, steps),
        in_specs=[
            xs(n, 0), xs(n, 1), xs(n, 2), xs(n, 3), xs(n, 4),
            cs(CTX_COL_I), cs(CTX_COL_FF), cs(CTX_COL_FB),
            pl.BlockSpec((2, n_layers, 1, w), lambda i, h: (0, 0, 0, h)),
            _const_spec((1, LANES)),
            _const_spec(tri.shape),
            _const_spec(lvl.shape),
        ],
        out_specs=pl.BlockSpec((1, n, w), lambda i, h: (i, 0, h)),
        out_shape=jax.ShapeDtypeStruct((b, n, B_WIDTH), BF16),
        scratch_shapes=[pltpu.VMEM((n, w), F32)],
        compiler_params=_cparams("arbitrary", "arbitrary"),
        name="hgrn2",
    )(pl_x, pl_x, pl_x, pl_x, pl_x, pl_c, pl_c, pl_c,
      lb_logits.reshape(2, n_layers, 1, B_WIDTH), norm_g, tri, lvl)


def _layer1_input(ya, yb, x, gate0, mod1, gain1, wo0_ref):
    d = x.shape[-1]
    half = ya.shape[-1]
    upd = (jnp.dot(ya, wo0_ref[0:half, :], preferred_element_type=F32)
           + jnp.dot(yb, wo0_ref[half:, :], preferred_element_type=F32))
    x1 = x + gate0 * upd
    return x1, _modulate(x1, gain1, mod1, d).astype(BF16)


def _edge_kernel(ya_ref, yb_ref, x_ref, mod0_ref, mod1_ref, g1_ref, wo0_ref, wi_ref, z_ref,
                 *, rows_per_batch):
    d = x_ref.shape[-1]

    def per_row(ref, lo, hi):
        return jnp.concatenate([jnp.broadcast_to(ref[i:i + 1, lo:hi], (rows_per_batch, hi - lo))
                                for i in range(x_ref.shape[0] // rows_per_batch)], axis=0)

    _, xm = _layer1_input(ya_ref[...], yb_ref[...], x_ref[...], per_row(mod0_ref, 2 * d, 3 * d),
                          per_row(mod1_ref, 0, 2 * d), g1_ref[...], wo0_ref)
    p = jnp.dot(xm, wi_ref[...], preferred_element_type=F32)
    z_ref[...] = p[:, :D_WIDTH] * p[:, D_WIDTH:]


def _block_edges(t, tb):
    b, n, w = t.shape
    te = t.reshape(b, n // tb, tb, w)
    return jnp.concatenate([te[:, :, :SUBLANES], te[:, :, tb - SUBLANES:]], axis=2).reshape(-1, w)


def _edge_z(ya, yb, x, mod0, mod1, gain1, w_out0, w_in1, tb):
    b, n, d = x.shape
    rows = (n // tb) * 2 * SUBLANES
    cg_start = 3 * C_WIDTH + D_WIDTH
    assert cg_start % (2 * D_WIDTH) == 0
    cg_blk = cg_start // (2 * D_WIDTH)
    full = lambda arr: _const_spec(arr.shape)
    xe, yae, ybe = _block_edges(x, tb), _block_edges(ya, tb), _block_edges(yb, tb)
    return pl.pallas_call(
        functools.partial(_edge_kernel, rows_per_batch=rows),
        grid=(1,),
        in_specs=[full(yae), full(ybe), full(xe), full(mod0), full(mod1), _const_spec((1, d)),
                  full(w_out0),
                  pl.BlockSpec((d, 2 * D_WIDTH), lambda i: (0, cg_blk), pipeline_mode=pl.Buffered(1))],
        out_specs=pl.BlockSpec((b * rows, D_WIDTH), lambda i: (0, 0)),
        out_shape=jax.ShapeDtypeStruct((b * rows, D_WIDTH), F32),
        compiler_params=_cparams("arbitrary"),
        name="conv_edge_rows",
    )(yae, ybe, xe, mod0, mod1, gain1, w_out0, w_in1).reshape(b, rows, D_WIDTH)


def _layer1_kernel(ya_ref, yb_ref, x_ref, ze_ref, mod0_ref, mod1_ref, g1_ref, wo0_ref, wi_ref,
                   vg_ref, ws_ref, bs_ref, cw_ref, wo1_ref, o_ref):
    d = x_ref.shape[-1]
    tb = x_ref.shape[1]
    j = pl.program_id(1)
    last_j = pl.num_programs(1) - 1
    mod1 = mod1_ref[pl.ds(pl.program_id(0), 1), :]
    gate0 = mod0_ref[pl.ds(pl.program_id(0), 1), 2 * d:]
    gate1 = mod1[:, 2 * d:]
    sub = tb // L1_SUB_BLOCKS
    subs = [slice(s * sub, (s + 1) * sub) for s in range(L1_SUB_BLOCKS)]
    col = lambda p, k: p[:, k * C_WIDTH:(k + 1) * C_WIDTH]
    n_gmlp = 3 * C_WIDTH

    x1s, xms = [], []
    for r in subs:
        x1, xm = _layer1_input(ya_ref[0, r, :], yb_ref[0, r, :], x_ref[0, r, :], gate0,
                               mod1, g1_ref[...], wo0_ref)
        x1s.append(x1)
        xms.append(xm)
    pgs = [jnp.dot(xm, wi_ref[:, 0:n_gmlp], preferred_element_type=F32) for xm in xms]
    pcs = [jnp.dot(xm, wi_ref[:, n_gmlp:], preferred_element_type=F32) for xm in xms]

    upd_c = []
    for pg in pgs:
        u = _gelu(col(pg, 0))
        vn = _rms(_gelu(col(pg, 1)), vg_ref[...]).astype(BF16)
        chunks = []
        for ci in range(sub // C_CHUNK):
            cr = slice(ci * C_CHUNK, (ci + 1) * C_CHUNK)
            groups = []
            for g in range(C_GROUPS):
                gl = slice(g * LANES, (g + 1) * LANES)
                groups.append(jnp.dot(ws_ref[g], vn[cr, gl], preferred_element_type=F32) + bs_ref[g])
            chunks.append(jnp.concatenate(groups, axis=1))
        o_c = u * jnp.concatenate(chunks, axis=0) * _silu(col(pg, 2))
        upd_c.append(jnp.dot(o_c.astype(BF16), wo1_ref[0:C_WIDTH, :], preferred_element_type=F32))

    z = jnp.concatenate([col(pc, 1) * col(pc, 2) for pc in pcs], axis=0)
    grp = 2 * SUBLANES
    prev_grp = ze_ref[0, pl.ds(pl.multiple_of(jnp.maximum(j - 1, 0) * grp + SUBLANES, SUBLANES),
                               SUBLANES), :]
    next_grp = ze_ref[0, pl.ds(pl.multiple_of(jnp.minimum(j + 1, last_j) * grp, SUBLANES),
                               SUBLANES), :]
    z_prev_row = jnp.where(j == 0, 0.0, prev_grp[SUBLANES - 1:, :])
    z_next_row = jnp.where(j == last_j, 0.0, next_grp[0:1, :])
    rowi = lax.broadcasted_iota(jnp.int32, z.shape, 0)
    z_prev = jnp.where(rowi == 0, z_prev_row, pltpu.roll(z, 1, 0))
    z_next = jnp.where(rowi == tb - 1, z_next_row, pltpu.roll(z, tb - 1, 0))
    conv = cw_ref[0:1, :] * z_prev + cw_ref[1:2, :] * z + cw_ref[2:3, :] * z_next

    for r, x1, pc, uc in zip(subs, x1s, pcs, upd_c):
        o_d = col(pc, 0) * conv[r, :] * _silu(col(pc, 3))
        upd = uc + jnp.dot(o_d.astype(BF16), wo1_ref[C_WIDTH:, :], preferred_element_type=F32)
        o_ref[0, r, :] = x1 + gate1 * upd


def _layer1(ya, yb, x, ze, mod0, mod1, gain1, w_out0, w_in1, v_gain, w_s, b_s, conv_w, w_out1, tb):
    b, n, d = x.shape
    half = ya.shape[-1]
    tok = lambda width: pl.BlockSpec((1, tb, width), lambda i, j: (i, j, 0))
    modspec = _const_spec(mod0.shape)
    return pl.pallas_call(
        _layer1_kernel,
        grid=(b, n // tb),
        in_specs=[tok(half), tok(half), tok(d),
                  pl.BlockSpec((1,) + ze.shape[1:], lambda i, j: (i, 0, 0)),
                  modspec, modspec, _const_spec((1, d)),
                  _const_spec(w_out0.shape), _const_spec(w_in1.shape),
                  _const_spec((1, C_WIDTH)),
                  _const_spec((C_GROUPS, C_CHUNK, C_CHUNK)),
                  _const_spec((C_GROUPS, C_CHUNK, LANES)),
                  _const_spec((3, D_WIDTH)),
                  _const_spec(w_out1.shape)],
        out_specs=tok(d),
        out_shape=jax.ShapeDtypeStruct((b, n, d), F32),
        compiler_params=_cparams("arbitrary", "arbitrary"),
        name="outproj_even_layer_odd",
    )(ya, yb, x, ze, mod0, mod1, gain1, w_out0, w_in1, v_gain, w_s, b_s, conv_w, w_out1)


def _rope_tables(n):
    rows_ = n // GRID_W
    row = np.repeat(np.arange(rows_, dtype=np.float64), GRID_W)
    col = np.tile(np.arange(GRID_W, dtype=np.float64), rows_)
    n_freq = A_HEAD_DIM // 4
    inv = ROPE_THETA ** (-np.arange(n_freq, dtype=np.float64) / n_freq)
    ang = np.concatenate([row[:, None] * inv, col[:, None] * inv], axis=-1)
    cos, sin = np.cos(ang), np.sin(ang)
    reps = LANES // A_HEAD_DIM
    return (jnp.asarray(np.tile(np.concatenate([cos, cos], axis=-1), (1, reps)), F32),
            jnp.asarray(np.tile(np.concatenate([-sin, sin], axis=-1), (1, reps)), F32))


def kernel(x, c, ctx, c_ctx, norm_gain, ada_w, ada_b, even_w_in, even_w_out, attn_qk_gain,
           attn_lambda, attn_subln_gain, hgrn_lb_logits, hgrn_norm_gain, odd_w_in, odd_w_out,
           gmlp_v_gain, gmlp_w_s, gmlp_b_s, conv_w):
    b, n, d = x.shape
    assert b + 1 <= COND_ROWS and n % 512 == 0 and ctx.shape[1] % HGRN_CHUNK == 0
    assert norm_gain.shape[0] == 2, "two-layer block: one even layer then one odd layer"

    cond = jnp.concatenate([c, c_ctx[None, :], jnp.zeros((COND_ROWS - b - 1, d), F32)], axis=0)
    mod0, mod1 = _adaln(cond, ada_w, ada_b)

    w_in0 = even_w_in[0].astype(BF16)
    gain0 = norm_gain[0].reshape(1, d)
    pl_x = _inproj(x, mod0, None, gain0, w_in0, TOKEN_BLOCK)
    w_ctx = jnp.concatenate([w_in0[:, g * A_WIDTH:(g + 1) * A_WIDTH] for g in CTX_GROUPS], axis=1)
    pl_c = _inproj(ctx, mod0, b, gain0, w_ctx, ctx.shape[1])

    cos, sin_signed = _rope_tables(n)
    qk_gain = jnp.tile(attn_qk_gain[0], (1, LANES // A_HEAD_DIM))
    blk = np.arange(LANES) // A_HEAD_DIM
    bd = jnp.asarray(np.where(blk[:, None] == blk[None, :], 1.0 / A_HEAD_DIM, 0.0), BF16)
    lam_init = 0.8 - 0.6 * math.exp(-0.3 * 0)
    ya = _attention(pl_x, pl_c, cos, sin_signed, qk_gain, bd,
                    attn_subln_gain[0].reshape(1, LANES), attn_lambda[0], lam_init, ATTN_Q_BLOCK)
    yb = _hgrn(pl_x, pl_c, hgrn_lb_logits, hgrn_norm_gain[0].reshape(1, LANES), HGRN_HEADS_PER_STEP)

    gain1 = norm_gain[1].reshape(1, d)
    w_out0, w_in1 = even_w_out[0].astype(BF16), odd_w_in[0].astype(BF16)
    ze = _edge_z(ya, yb, x, mod0, mod1, gain1, w_out0, w_in1, TOKEN_BLOCK)
    b_s = jnp.broadcast_to(gmlp_b_s[0][:, :, None], (C_GROUPS, C_CHUNK, LANES))
    return _layer1(ya, yb, x, ze, mod0, mod1, gain1, w_out0, w_in1,
                   gmlp_v_gain[0].reshape(1, C_WIDTH), gmlp_w_s[0].astype(BF16), b_s, conv_w[0],
                   odd_w_out[0].astype(BF16), TOKEN_BLOCK)
```

```python
import functools
import math

import jax
import jax.numpy as jnp
import numpy as np
from jax import lax
from jax.experimental import pallas as pl
from jax.experimental.pallas import tpu as pltpu

F32 = jnp.float32
BF16 = jnp.bfloat16

EPS = 1e-6
GRID_W = 64
ROPE_THETA = 10000.0
A_HEADS = 4
A_HEAD_DIM = 64
A_WIDTH = 2 * A_HEADS * A_HEAD_DIM
B_HEADS = 4
B_DIM = 128
B_WIDTH = B_HEADS * B_DIM
C_GROUPS = 4
C_CHUNK = 128
C_WIDTH = 512
D_WIDTH = 512
EVEN_IN = 4 * A_WIDTH + 5 * B_WIDTH
ODD_IN = 3 * C_WIDTH + 4 * D_WIDTH

CTX_GROUPS = (1, 2, 5, 6, 7)
REST_COL_GATE_A, REST_COL_Q, REST_COL_I, REST_COL_FF, REST_COL_FB, REST_COL_GATE_B = range(6)
CTX_COL_I, CTX_COL_FF, CTX_COL_FB = range(3)
QK_SLAB = 256

LANES = 128
HGRN_CHUNK = 128
HGRN_HEADS_PER_STEP = 4
HGRN_DIAG = 8
SCORE_BOUND = 100.0
EXP2_CLAMP = 115.0
LOG2E = math.log2(math.e)
Q_SCALE = A_HEAD_DIM ** -0.5 * LOG2E
TOKEN_BLOCK = 512
ATTN_Q_BLOCK = 256
SUBLANES = 8
BF16_ROWS_PER_VREG = 16
L1_SUB_BLOCKS = 2
COND_ROWS = 16
ADALN_COL_BLOCK = 512
VMEM_LIMIT = 56 * 1024 * 1024


def _cparams(*sem):
    return pltpu.CompilerParams(dimension_semantics=sem, vmem_limit_bytes=VMEM_LIMIT)


def _const_spec(shape):
    nd = len(shape)
    return pl.BlockSpec(shape, lambda *_: (0,) * nd, pipeline_mode=pl.Buffered(1))


def _silu(t):
    return t * jax.nn.sigmoid(t)


def _gelu(t):
    return 0.5 * t * (1.0 + lax.erf(t * (1.0 / math.sqrt(2.0))))


def _rms(t, gain):
    ms = jnp.mean(t * t, axis=-1, keepdims=True)
    return t * lax.rsqrt(ms + EPS) * gain


def _adaln_kernel(cond_ref, w_ref, b_ref, *o_refs):
    a = _silu(cond_ref[...])
    for layer, o_ref in enumerate(o_refs):
        o_ref[...] = jnp.dot(a, w_ref[layer], preferred_element_type=F32) + b_ref[layer]


def _adaln(cond, ada_w, ada_b):
    depth, d, n3 = ada_w.shape
    tn = ADALN_COL_BLOCK
    out = pl.BlockSpec((COND_ROWS, tn), lambda j: (0, j))
    return pl.pallas_call(
        _adaln_kernel,
        grid=(n3 // tn,),
        in_specs=[
            pl.BlockSpec((COND_ROWS, d), lambda j: (0, 0)),
            pl.BlockSpec((depth, d, tn), lambda j: (0, 0, j)),
            pl.BlockSpec((depth, 1, tn), lambda j: (0, 0, j)),
        ],
        out_specs=[out] * depth,
        out_shape=[jax.ShapeDtypeStruct((COND_ROWS, n3), F32)] * depth,
        compiler_params=_cparams("arbitrary"),
        name="adaln",
    )(cond, ada_w, ada_b.reshape(depth, 1, n3))


def _modulate(x, gain, mod, d):
    shift = mod[:, 0:d]
    scale = mod[:, d:2 * d]
    return _rms(x, gain) * (1.0 + scale) + shift


def _rope(t, cos, sin_signed):
    lanes = t.shape[1]
    lane = lax.broadcasted_iota(jnp.int32, t.shape, 1)
    first = (lane % A_HEAD_DIM) < (A_HEAD_DIM // 2)
    partner = jnp.where(first,
                        pltpu.roll(t, lanes - A_HEAD_DIM // 2, 1),
                        pltpu.roll(t, A_HEAD_DIM // 2, 1))
    return t * cos + partner * sin_signed


def _inproj_kernel(x_ref, mod_ref, g_ref, wa_ref, wr1_ref, wr2_ref, cos_ref, sin_ref, qkg_ref, bd_ref,
                   *out_refs,
                   mod_row, has_q, rope):
    d = x_ref.shape[-1]
    row = pl.program_id(0) if mod_row is None else mod_row
    xm = _modulate(x_ref[0], g_ref[...], mod_ref[pl.ds(row, 1), :], d).astype(BF16)
    n_attn = (3 if has_q else 2) * A_WIDTH
    rest_ref = out_refs[-1]
    n_rest = rest_ref.shape[-1]
    slabs = [slice(half * QK_SLAB, (half + 1) * QK_SLAB) for half in range(A_WIDTH // QK_SLAB)]

    attn = jnp.dot(xm, wa_ref[...], preferred_element_type=F32)
    rest_ref[0, :, 0:n_rest // 2] = jnp.dot(
        xm, wr1_ref[...], preferred_element_type=F32).astype(rest_ref.dtype)
    groups = [attn[:, g * A_WIDTH:(g + 1) * A_WIDTH] for g in range(n_attn // A_WIDTH)]
    v = groups.pop()
    mean_sq = [[jnp.dot((t[:, sl] * t[:, sl]).astype(BF16), bd_ref[...], preferred_element_type=F32)
                for sl in slabs] for t in groups]
    pad = BF16_ROWS_PER_VREG
    anchor = jnp.concatenate([mean_sq[-1][-1][0:pad, :] * 0.0] * (d // QK_SLAB), axis=1)
    xm_late = jnp.concatenate([(xm[0:pad, :].astype(F32) + anchor).astype(BF16), xm[pad:, :]], axis=0)
    rest_ref[0, :, n_rest // 2:] = jnp.dot(
        xm_late, wr2_ref[...], preferred_element_type=F32).astype(rest_ref.dtype)

    qk_refs = out_refs[:-2]
    vt_ref = out_refs[-2]
    first_gain = 0 if has_q else 1
    for gi, (t, o_ref) in enumerate(zip(groups, qk_refs)):
        gain = qkg_ref[first_gain + gi:first_gain + gi + 1, :]
        scale = Q_SCALE if (has_q and gi == 0) else 1.0
        for sl, ms in zip(slabs, mean_sq[gi]):
            tn = t[:, sl] * lax.rsqrt(ms + EPS) * gain
            if rope:
                tn = _rope(tn, cos_ref[...], sin_ref[...])
            o_ref[0, :, sl] = (tn * scale).astype(o_ref.dtype)
    for h in range(A_HEADS):
        sl = slice(h * LANES, (h + 1) * LANES)
        vt_ref[0, sl, :] = v[:, sl].T.astype(vt_ref.dtype)


def _inproj(x, mod, mod_row, gain, w, cos, sin_signed, qk_gain, bd, tb, has_q, rope):
    b, n, d = x.shape
    n_attn = (3 if has_q else 2) * A_WIDTH
    n_rest = w.shape[1] - n_attn
    cut = n_attn + n_rest // 2
    w_parts = (w[:, :n_attn], w[:, n_attn:cut], w[:, cut:])
    tok = lambda width: pl.BlockSpec((1, tb, width), lambda i, j: (i, j, 0))
    qk_out = [tok(A_WIDTH)] * (2 if has_q else 1)
    qk_shape = [jax.ShapeDtypeStruct((b, n, A_WIDTH), BF16)] * (2 if has_q else 1)
    return pl.pallas_call(
        functools.partial(_inproj_kernel, mod_row=mod_row, has_q=has_q, rope=rope),
        grid=(b, n // tb),
        in_specs=[
            tok(d),
            _const_spec(mod.shape),
            _const_spec((1, d)),
            *[_const_spec(part.shape) for part in w_parts],
            pl.BlockSpec((tb, QK_SLAB), lambda i, j: (j, 0)),
            pl.BlockSpec((tb, QK_SLAB), lambda i, j: (j, 0)),
            _const_spec(qk_gain.shape),
            _const_spec(bd.shape),
        ],
        out_specs=qk_out + [pl.BlockSpec((1, A_WIDTH, tb), lambda i, j: (i, 0, j)), tok(n_rest)],
        out_shape=qk_shape + [jax.ShapeDtypeStruct((b, A_WIDTH, n), BF16),
                              jax.ShapeDtypeStruct((b, n, n_rest), BF16)],
        compiler_params=_cparams("arbitrary", "arbitrary"),
        name="inproj_even",
    )(x, mod, gain, *w_parts, cos, sin_signed, qk_gain, bd)


def _attn_kernel(q_ref, kl_ref, kc_ref, vtl_ref, vtc_ref, g_ref, qkg_ref, subg_ref, lamp_ref,
                 o_ref, *, lam_init):
    heads = [slice(h * LANES, (h + 1) * LANES) for h in range(A_HEADS)]
    lane = lax.broadcasted_iota(jnp.int32, (1, LANES), 1)
    nt = (((1,), (1,)), ((), ()))

    lp = lamp_ref[...]
    lam = (jnp.exp(jnp.sum(lp[0:1] * lp[1:2], axis=-1, keepdims=True))
           - jnp.exp(jnp.sum(lp[2:3] * lp[3:4], axis=-1, keepdims=True)) + lam_init)
    score_bound = ((A_HEAD_DIM * Q_SCALE) * jnp.max(jnp.abs(qkg_ref[0:1, :]))
                   * jnp.max(jnp.abs(qkg_ref[1:2, :])))

    def scores(h):
        out = []
        for m in range(2):
            qm = jnp.where((lane // A_HEAD_DIM) == m, q_ref[0, :, heads[h]], 0).astype(BF16)
            out.append([lax.dot_general(k_ref[0, :, heads[h]], qm, nt, preferred_element_type=F32)
                        for k_ref in (kc_ref, kl_ref)])
        return out

    def run_heads(shift):
        s_next = scores(0)
        for h, sl in enumerate(heads):
            s_maps = s_next
            if h + 1 < A_HEADS:
                s_next = scores(h + 1)
            probs = []
            for s_parts in s_maps:
                if shift:
                    top = functools.reduce(jnp.maximum,
                                           [jnp.max(s, axis=0, keepdims=True) for s in s_parts])
                    s_parts = [s - top for s in s_parts]
                p_parts = [jnp.exp2(s) for s in s_parts]
                probs.append((p_parts, sum(jnp.sum(p, axis=0, keepdims=True) for p in p_parts)))
            (p0, l0), (p1, l1) = probs
            a0, a1 = 1.0 / l0, lam / l1
            ot = sum(jnp.dot(vt_ref[0, sl, :], (pa * a0 - pb * a1).astype(BF16),
                             preferred_element_type=F32)
                     for vt_ref, pa, pb in zip((vtc_ref, vtl_ref), p0, p1))
            ms = jnp.mean(ot * ot, axis=0, keepdims=True)
            on = (ot * lax.rsqrt(ms + EPS)).T * (subg_ref[...] * (1.0 - lam_init))
            o_ref[0, :, sl] = (on * _silu(g_ref[0, :, sl].astype(F32))).astype(o_ref.dtype)

    no_shift_ok = score_bound <= SCORE_BOUND
    pl.when(no_shift_ok)(functools.partial(run_heads, False))
    pl.when(jnp.logical_not(no_shift_ok))(functools.partial(run_heads, True))


def _attention(q, k_lat, k_ctx, vt_lat, vt_ctx, rest, qk_gain, subln_g, lam_p, lam_init, tq):
    b, n, w = q.shape
    n_ctx = k_ctx.shape[1]
    return pl.pallas_call(
        functools.partial(_attn_kernel, lam_init=lam_init),
        grid=(b, n // tq),
        in_specs=[
            pl.BlockSpec((1, tq, w), lambda i, j: (i, j, 0)),
            pl.BlockSpec((1, n, w), lambda i, j: (i, 0, 0)),
            pl.BlockSpec((1, n_ctx, w), lambda i, j: (i, 0, 0)),
            pl.BlockSpec((1, w, n), lambda i, j: (i, 0, 0)),
            pl.BlockSpec((1, w, n_ctx), lambda i, j: (i, 0, 0)),
            pl.BlockSpec((1, tq, w), lambda i, j: (i, j, REST_COL_GATE_A)),
            _const_spec(qk_gain.shape),
            _const_spec((1, LANES)),
            _const_spec((4, A_HEAD_DIM)),
        ],
        out_specs=pl.BlockSpec((1, tq, w), lambda i, j: (i, j, 0)),
        out_shape=jax.ShapeDtypeStruct((b, n, w), BF16),
        compiler_params=_cparams("arbitrary", "arbitrary"),
        name="diff_attn",
    )(q, k_lat, k_ctx, vt_lat, vt_ctx, rest, qk_gain, subln_g, lam_p)


def _split_bf16(t):
    hi = t.astype(BF16)
    return hi, (t - hi.astype(F32)).astype(BF16)


def _block_ref(g, block, row):
    c, w = g.shape
    g3 = g.reshape(c // block, block, w)
    return jnp.broadcast_to(g3[:, row:row + 1, :], g3.shape).reshape(c, w)


def _hgrn_tables(c):
    t = np.arange(c)[:, None]
    s = np.arange(c)[None, :]
    lvl = np.zeros((c, c), np.int32)
    lvl[(t // HGRN_DIAG == s // HGRN_DIAG) & (s <= t)] = 1
    b, k = HGRN_DIAG, 2
    while b < c:
        lvl[(t // b == s // b + 1) & ((s // b) % 2 == 0)] = k
        b, k = 2 * b, k + 1
    tri = (s <= t).astype(np.float32)
    return jnp.asarray(np.stack([tri, tri.T]), BF16), jnp.asarray(np.stack([lvl, lvl.T]))


def _hgrn_chunks(chains, tri_ref, lvl_ref, want_out):
    nt = (((1,), (1,)), ((), ()))
    tn = (((0,), (0,)), ((), ()))
    n = len(chains)
    c = chains[0][2].shape[0]

    kk, parts = [], []
    for (_, _, f_raw, lb, _, _) in chains:
        f = lb + (1.0 - lb) * jax.nn.sigmoid(f_raw)
        kk.append(1.0 - f)
        parts.append(_split_bf16(jnp.log(f) * LOG2E))
    cum = [sum(jnp.dot(tri_ref[ch[5]], p, preferred_element_type=F32) for p in parts[i])
           for i, ch in enumerate(chains)]
    edge = [cum[i][0:1, :] if ch[5] else cum[i][c - 1:c, :] for i, ch in enumerate(chains)]

    outs = [None] * n
    if want_out:
        a = []
        for i, (q, _, _, _, _, d) in enumerate(chains):
            ref = _block_ref(cum[i], HGRN_DIAG, HGRN_DIAG // 2)
            qd = (q * jnp.exp2(jnp.minimum(cum[i] - ref, EXP2_CLAMP))).astype(BF16)
            kd = (kk[i] * jnp.exp2(jnp.minimum(ref - cum[i], EXP2_CLAMP))).astype(BF16)
            a.append(jnp.where(lvl_ref[d] == 1,
                               lax.dot_general(qd, kd, nt, preferred_element_type=F32), 0.0))
        b, k = HGRN_DIAG, 2
        while b < c:
            for i, (q, _, _, _, _, d) in enumerate(chains):
                ref = _block_ref(cum[i], 2 * b, b if d else b - 1)
                decay = jnp.exp2(cum[i] - ref)
                ql = (q * decay).astype(BF16)
                kl = (kk[i] * (1.0 / decay)).astype(BF16)
                a[i] = jnp.where(lvl_ref[d] == k,
                                 lax.dot_general(ql, kl, nt, preferred_element_type=F32), a[i])
            b, k = 2 * b, k + 1
        for i, (q, v, _, _, st, _) in enumerate(chains):
            o = jnp.dot(a[i].astype(BF16), v, preferred_element_type=F32)
            outs[i] = o + lax.dot_general((q * jnp.exp2(cum[i])).astype(BF16), st.astype(BF16), nt,
                                          preferred_element_type=F32)

    sts = []
    for i, (_, v, _, _, st, _) in enumerate(chains):
        kg = (kk[i] * jnp.exp2(edge[i] - cum[i])).astype(BF16)
        upd = lax.dot_general(v, kg, tn, preferred_element_type=F32)
        sts.append(st * jnp.exp2(edge[i]) + upd)
    return outs, sts


def _hgrn_kernel(q_ref, i_ref, ff_ref, fb_ref, g_ref, ic_ref, ffc_ref, fbc_ref,
                 lbl_ref, ng_ref, tri_ref, lvl_ref, o_ref, acc_scr):
    c = HGRN_CHUNK
    heads = q_ref.shape[2] // LANES
    nc_lat = q_ref.shape[1] // c
    nc_ctx = ic_ref.shape[1] // c
    f_lat = (ff_ref, fb_ref)
    f_ctx = (ffc_ref, fbc_ref)

    def lower_bound(direction, sl):
        logits = [lbl_ref[direction, l, :, sl] for l in range(lbl_ref.shape[1])]
        top = functools.reduce(jnp.maximum, logits)
        e = [jnp.exp(t - top) for t in logits]
        return e[0] / sum(e)

    lanes = [slice(h * LANES, (h + 1) * LANES) for h in range(heads)]
    lbs = [[lower_bound(d, sl) for sl in lanes] for d in (0, 1)]

    def rows(i):
        return pl.ds(pl.multiple_of(i * c, c), c)

    def ctx_step(j, sts):
        chains = []
        for d in (0, 1):
            r = rows(nc_ctx - 1 - j if d else j)
            for h, sl in enumerate(lanes):
                chains.append((None, ic_ref[0, r, sl], f_ctx[d][0, r, sl].astype(F32),
                               lbs[d][h], sts[d * heads + h], d))
        return tuple(_hgrn_chunks(chains, tri_ref, lvl_ref, False)[1])

    def lat_step(j, sts, second_visit):
        chains, where = [], []
        for d in (0, 1):
            r = rows(nc_lat - 1 - j if d else j)
            for h, sl in enumerate(lanes):
                chains.append((q_ref[0, r, sl].astype(F32), i_ref[0, r, sl],
                               f_lat[d][0, r, sl].astype(F32), lbs[d][h], sts[d * heads + h], d))
                where.append((r, sl))
        outs, new = _hgrn_chunks(chains, tri_ref, lvl_ref, True)
        for o, (r, sl) in zip(outs, where):
            if second_visit:
                y = _rms(acc_scr[r, sl] + o, ng_ref[...]) * _silu(g_ref[0, r, sl].astype(F32))
                o_ref[0, r, sl] = y.astype(o_ref.dtype)
            else:
                acc_scr[r, sl] = o
        return tuple(new)

    sts = tuple(jnp.zeros((B_DIM, B_DIM), F32) for _ in range(2 * heads))
    sts = lax.fori_loop(0, nc_ctx, ctx_step, sts)
    sts = lax.fori_loop(0, nc_lat // 2, functools.partial(lat_step, second_visit=False), sts,
                        unroll=2)
    lax.fori_loop(nc_lat // 2, nc_lat, functools.partial(lat_step, second_visit=True), sts,
                  unroll=2)


def _hgrn(rest_x, rest_c, lb_logits, norm_g, heads_per_step):
    b, n, _ = rest_x.shape
    n_ctx = rest_c.shape[1]
    assert (n // HGRN_CHUNK) % 2 == 0 and B_HEADS % heads_per_step == 0
    w = heads_per_step * LANES
    steps = B_HEADS // heads_per_step

    def xs(group):
        return pl.BlockSpec((1, n, w), lambda i, h: (i, 0, group * steps + h))

    def cs(group):
        return pl.BlockSpec((1, n_ctx, w), lambda i, h: (i, 0, group * steps + h))

    n_layers = lb_logits.shape[1]
    tri, lvl = _hgrn_tables(HGRN_CHUNK)
    return pl.pallas_call(
        _hgrn_kernel,
        grid=(b, steps),
        in_specs=[
            xs(REST_COL_Q), xs(REST_COL_I), xs(REST_COL_FF), xs(REST_COL_FB), xs(REST_COL_GATE_B),
            cs(CTX_COL_I), cs(CTX_COL_FF), cs(CTX_COL_FB),
            pl.BlockSpec((2, n_layers, 1, w), lambda i, h: (0, 0, 0, h)),
            _const_spec((1, LANES)),
            _const_spec(tri.shape),
            _const_spec(lvl.shape),
        ],
        out_specs=pl.BlockSpec((1, n, w), lambda i, h: (i, 0, h)),
        out_shape=jax.ShapeDtypeStruct((b, n, B_WIDTH), BF16),
        scratch_shapes=[pltpu.VMEM((n, w), F32)],
        compiler_params=_cparams("arbitrary", "arbitrary"),
        name="hgrn2",
    )(rest_x, rest_x, rest_x, rest_x, rest_x, rest_c, rest_c, rest_c,
      lb_logits.reshape(2, n_layers, 1, B_WIDTH), norm_g, tri, lvl)


def _layer1_input(ya, yb, x, gate0, mod1, gain1, wo0_ref):
    d = x.shape[-1]
    half = ya.shape[-1]
    upd = (jnp.dot(ya, wo0_ref[0:half, :], preferred_element_type=F32)
           + jnp.dot(yb, wo0_ref[half:, :], preferred_element_type=F32))
    x1 = x + gate0 * upd
    return x1, _modulate(x1, gain1, mod1, d).astype(BF16)


def _edge_kernel(ya_ref, yb_ref, x_ref, mod0_ref, mod1_ref, g1_ref, wo0_ref, wi_ref, z_ref,
                 *, rows_per_batch):
    d = x_ref.shape[-1]

    def per_row(ref, lo, hi):
        return jnp.concatenate([jnp.broadcast_to(ref[i:i + 1, lo:hi], (rows_per_batch, hi - lo))
                                for i in range(x_ref.shape[0] // rows_per_batch)], axis=0)

    _, xm = _layer1_input(ya_ref[...], yb_ref[...], x_ref[...], per_row(mod0_ref, 2 * d, 3 * d),
                          per_row(mod1_ref, 0, 2 * d), g1_ref[...], wo0_ref)
    p = jnp.dot(xm, wi_ref[...], preferred_element_type=F32)
    z_ref[...] = p[:, :D_WIDTH] * p[:, D_WIDTH:]


def _block_edges(t, tb):
    b, n, w = t.shape
    te = t.reshape(b, n // tb, tb, w)
    return jnp.concatenate([te[:, :, :SUBLANES], te[:, :, tb - SUBLANES:]], axis=2).reshape(-1, w)


def _edge_z(ya, yb, x, mod0, mod1, gain1, w_out0, w_in1, tb):
    b, n, d = x.shape
    rows = (n // tb) * 2 * SUBLANES
    cg_start = 3 * C_WIDTH + D_WIDTH
    assert cg_start % (2 * D_WIDTH) == 0
    cg_blk = cg_start // (2 * D_WIDTH)
    full = lambda arr: _const_spec(arr.shape)
    xe, yae, ybe = _block_edges(x, tb), _block_edges(ya, tb), _block_edges(yb, tb)
    return pl.pallas_call(
        functools.partial(_edge_kernel, rows_per_batch=rows),
        grid=(1,),
        in_specs=[full(yae), full(ybe), full(xe), full(mod0), full(mod1), _const_spec((1, d)),
                  full(w_out0),
                  pl.BlockSpec((d, 2 * D_WIDTH), lambda i: (0, cg_blk), pipeline_mode=pl.Buffered(1))],
        out_specs=pl.BlockSpec((b * rows, D_WIDTH), lambda i: (0, 0)),
        out_shape=jax.ShapeDtypeStruct((b * rows, D_WIDTH), F32),
        compiler_params=_cparams("arbitrary"),
        name="conv_edge_rows",
    )(yae, ybe, xe, mod0, mod1, gain1, w_out0, w_in1).reshape(b, rows, D_WIDTH)


def _layer1_kernel(ya_ref, yb_ref, x_ref, ze_ref, mod0_ref, mod1_ref, g1_ref, wo0_ref, wi_ref,
                   vg_ref, ws_ref, bs_ref, cw_ref, wo1_ref, o_ref):
    d = x_ref.shape[-1]
    tb = x_ref.shape[1]
    j = pl.program_id(1)
    last_j = pl.num_programs(1) - 1
    mod1 = mod1_ref[pl.ds(pl.program_id(0), 1), :]
    gate0 = mod0_ref[pl.ds(pl.program_id(0), 1), 2 * d:]
    gate1 = mod1[:, 2 * d:]
    sub = tb // L1_SUB_BLOCKS
    subs = [slice(s * sub, (s + 1) * sub) for s in range(L1_SUB_BLOCKS)]
    col = lambda p, k: p[:, k * C_WIDTH:(k + 1) * C_WIDTH]
    n_gmlp = 3 * C_WIDTH

    x1s, xms = [], []
    for r in subs:
        x1, xm = _layer1_input(ya_ref[0, r, :], yb_ref[0, r, :], x_ref[0, r, :], gate0,
                               mod1, g1_ref[...], wo0_ref)
        x1s.append(x1)
        xms.append(xm)
    pgs = [jnp.dot(xm, wi_ref[:, 0:n_gmlp], preferred_element_type=F32) for xm in xms]
    pcs = [jnp.dot(xm, wi_ref[:, n_gmlp:], preferred_element_type=F32) for xm in xms]

    upd_c = []
    for pg in pgs:
        u = _gelu(col(pg, 0))
        vn = _rms(_gelu(col(pg, 1)), vg_ref[...]).astype(BF16)
        chunks = []
        for ci in range(sub // C_CHUNK):
            cr = slice(ci * C_CHUNK, (ci + 1) * C_CHUNK)
            groups = []
            for g in range(C_GROUPS):
                gl = slice(g * LANES, (g + 1) * LANES)
                groups.append(jnp.dot(ws_ref[g], vn[cr, gl], preferred_element_type=F32) + bs_ref[g])
            chunks.append(jnp.concatenate(groups, axis=1))
        o_c = u * jnp.concatenate(chunks, axis=0) * _silu(col(pg, 2))
        upd_c.append(jnp.dot(o_c.astype(BF16), wo1_ref[0:C_WIDTH, :], preferred_element_type=F32))

    z = jnp.concatenate([col(pc, 1) * col(pc, 2) for pc in pcs], axis=0)
    grp = 2 * SUBLANES
    prev_grp = ze_ref[0, pl.ds(pl.multiple_of(jnp.maximum(j - 1, 0) * grp + SUBLANES, SUBLANES),
                               SUBLANES), :]
    next_grp = ze_ref[0, pl.ds(pl.multiple_of(jnp.minimum(j + 1, last_j) * grp, SUBLANES),
                               SUBLANES), :]
    z_prev_row = jnp.where(j == 0, 0.0, prev_grp[SUBLANES - 1:, :])
    z_next_row = jnp.where(j == last_j, 0.0, next_grp[0:1, :])
    rowi = lax.broadcasted_iota(jnp.int32, z.shape, 0)
    z_prev = jnp.where(rowi == 0, z_prev_row, pltpu.roll(z, 1, 0))
    z_next = jnp.where(rowi == tb - 1, z_next_row, pltpu.roll(z, tb - 1, 0))
    conv = cw_ref[0:1, :] * z_prev + cw_ref[1:2, :] * z + cw_ref[2:3, :] * z_next

    for r, x1, pc, uc in zip(subs, x1s, pcs, upd_c):
        o_d = col(pc, 0) * conv[r, :] * _silu(col(pc, 3))
        upd = uc + jnp.dot(o_d.astype(BF16), wo1_ref[C_WIDTH:, :], preferred_element_type=F32)
        o_ref[0, r, :] = x1 + gate1 * upd


def _layer1(ya, yb, x, ze, mod0, mod1, gain1, w_out0, w_in1, v_gain, w_s, b_s, conv_w, w_out1, tb):
    b, n, d = x.shape
    half = ya.shape[-1]
    tok = lambda width: pl.BlockSpec((1, tb, width), lambda i, j: (i, j, 0))
    modspec = _const_spec(mod0.shape)
    return pl.pallas_call(
        _layer1_kernel,
        grid=(b, n // tb),
        in_specs=[tok(half), tok(half), tok(d),
                  pl.BlockSpec((1,) + ze.shape[1:], lambda i, j: (i, 0, 0)),
                  modspec, modspec, _const_spec((1, d)),
                  _const_spec(w_out0.shape), _const_spec(w_in1.shape),
                  _const_spec((1, C_WIDTH)),
                  _const_spec((C_GROUPS, C_CHUNK, C_CHUNK)),
                  _const_spec((C_GROUPS, C_CHUNK, LANES)),
                  _const_spec((3, D_WIDTH)),
                  _const_spec(w_out1.shape)],
        out_specs=tok(d),
        out_shape=jax.ShapeDtypeStruct((b, n, d), F32),
        compiler_params=_cparams("arbitrary", "arbitrary"),
        name="outproj_even_layer_odd",
    )(ya, yb, x, ze, mod0, mod1, gain1, w_out0, w_in1, v_gain, w_s, b_s, conv_w, w_out1)


def _rope_tables(n):
    rows_ = n // GRID_W
    row = np.repeat(np.arange(rows_, dtype=np.float64), GRID_W)
    col = np.tile(np.arange(GRID_W, dtype=np.float64), rows_)
    n_freq = A_HEAD_DIM // 4
    inv = ROPE_THETA ** (-np.arange(n_freq, dtype=np.float64) / n_freq)
    ang = np.concatenate([row[:, None] * inv, col[:, None] * inv], axis=-1)
    cos, sin = np.cos(ang), np.sin(ang)
    reps = QK_SLAB // A_HEAD_DIM
    return (jnp.asarray(np.tile(np.concatenate([cos, cos], axis=-1), (1, reps)), F32),
            jnp.asarray(np.tile(np.concatenate([-sin, sin], axis=-1), (1, reps)), F32))


def kernel(x, c, ctx, c_ctx, norm_gain, ada_w, ada_b, even_w_in, even_w_out, attn_qk_gain,
           attn_lambda, attn_subln_gain, hgrn_lb_logits, hgrn_norm_gain, odd_w_in, odd_w_out,
           gmlp_v_gain, gmlp_w_s, gmlp_b_s, conv_w):
    b, n, d = x.shape
    assert b + 1 <= COND_ROWS and n % 512 == 0 and ctx.shape[1] % HGRN_CHUNK == 0
    assert norm_gain.shape[0] == 2, "two-layer block: one even layer then one odd layer"

    cond = jnp.concatenate([c, c_ctx[None, :], jnp.zeros((COND_ROWS - b - 1, d), F32)], axis=0)
    mod0, mod1 = _adaln(cond, ada_w, ada_b)

    w_in0 = even_w_in[0].astype(BF16)
    gain0 = norm_gain[0].reshape(1, d)
    cos, sin_signed = _rope_tables(n)
    qk_gain = jnp.tile(attn_qk_gain[0], (1, QK_SLAB // A_HEAD_DIM))
    blk = np.arange(QK_SLAB) // A_HEAD_DIM
    bd = jnp.asarray(np.where(blk[:, None] == blk[None, :], 1.0 / A_HEAD_DIM, 0.0), BF16)
    q, k_lat, vt_lat, rest_x = _inproj(x, mod0, None, gain0, w_in0, cos, sin_signed, qk_gain, bd,
                                       TOKEN_BLOCK, True, True)
    w_ctx = jnp.concatenate([w_in0[:, g * A_WIDTH:(g + 1) * A_WIDTH] for g in CTX_GROUPS], axis=1)
    n_ctx = ctx.shape[1]
    k_ctx, vt_ctx, rest_c = _inproj(ctx, mod0, b, gain0, w_ctx, cos[:n_ctx], sin_signed[:n_ctx],
                                    qk_gain, bd, n_ctx, False, False)

    lam_init = 0.8 - 0.6 * math.exp(-0.3 * 0)
    ya = _attention(q, k_lat, k_ctx, vt_lat, vt_ctx, rest_x, qk_gain,
                    attn_subln_gain[0].reshape(1, LANES), attn_lambda[0], lam_init, ATTN_Q_BLOCK)
    yb = _hgrn(rest_x, rest_c, hgrn_lb_logits, hgrn_norm_gain[0].reshape(1, LANES),
               HGRN_HEADS_PER_STEP)

    gain1 = norm_gain[1].reshape(1, d)
    w_out0, w_in1 = even_w_out[0].astype(BF16), odd_w_in[0].astype(BF16)
    ze = _edge_z(ya, yb, x, mod0, mod1, gain1, w_out0, w_in1, TOKEN_BLOCK)
    b_s = jnp.broadcast_to(gmlp_b_s[0][:, :, None], (C_GROUPS, C_CHUNK, LANES))
    return _layer1(ya, yb, x, ze, mod0, mod1, gain1, w_out0, w_in1,
                   gmlp_v_gain[0].reshape(1, C_WIDTH), gmlp_w_s[0].astype(BF16), b_s, conv_w[0],
                   odd_w_out[0].astype(BF16), TOKEN_BLOCK)
```

```python
import functools
import math

import jax
import jax.numpy as jnp
import numpy as np
from jax import lax
from jax.experimental import pallas as pl
from jax.experimental.pallas import tpu as pltpu

F32 = jnp.float32
BF16 = jnp.bfloat16

EPS = 1e-6
GRID_W = 64
ROPE_THETA = 10000.0
A_HEADS = 4
A_HEAD_DIM = 64
A_WIDTH = 2 * A_HEADS * A_HEAD_DIM
B_HEADS = 4
B_DIM = 128
B_WIDTH = B_HEADS * B_DIM
C_GROUPS = 4
C_CHUNK = 128
C_WIDTH = 512
D_WIDTH = 512
EVEN_IN = 4 * A_WIDTH + 5 * B_WIDTH
ODD_IN = 3 * C_WIDTH + 4 * D_WIDTH

CTX_GROUPS = (1, 2, 5, 6, 7)
REST_COL_GATE_A, REST_COL_Q, REST_COL_I, REST_COL_FF, REST_COL_FB, REST_COL_GATE_B = range(6)
CTX_COL_I, CTX_COL_FF, CTX_COL_FB = range(3)
QK_SLAB = 256

LANES = 128
HGRN_CHUNK = 128
HGRN_HEADS_PER_STEP = 4
HGRN_DIAG = 8
SCORE_BOUND = 100.0
EXP2_CLAMP = 115.0
LOG2E = math.log2(math.e)
Q_SCALE = A_HEAD_DIM ** -0.5 * LOG2E
TOKEN_BLOCK = 512
ATTN_Q_BLOCK = 256
SUBLANES = 8
BF16_ROWS_PER_VREG = 16
L1_SUB_BLOCKS = 2
COND_ROWS = 16
ADALN_COL_BLOCK = 512
VMEM_LIMIT = 56 * 1024 * 1024


def _cparams(*sem):
    return pltpu.CompilerParams(dimension_semantics=sem, vmem_limit_bytes=VMEM_LIMIT)


def _const_spec(shape):
    nd = len(shape)
    return pl.BlockSpec(shape, lambda *_: (0,) * nd, pipeline_mode=pl.Buffered(1))


def _silu(t):
    return t * jax.nn.sigmoid(t)


def _gelu(t):
    return 0.5 * t * (1.0 + lax.erf(t * (1.0 / math.sqrt(2.0))))


def _rms(t, gain):
    ms = jnp.mean(t * t, axis=-1, keepdims=True)
    return t * lax.rsqrt(ms + EPS) * gain


def _adaln_kernel(cond_ref, w_ref, b_ref, *o_refs):
    a = _silu(cond_ref[...])
    for layer, o_ref in enumerate(o_refs):
        o_ref[...] = jnp.dot(a, w_ref[layer], preferred_element_type=F32) + b_ref[layer]


def _adaln(cond, ada_w, ada_b):
    depth, d, n3 = ada_w.shape
    tn = ADALN_COL_BLOCK
    out = pl.BlockSpec((COND_ROWS, tn), lambda j: (0, j))
    return pl.pallas_call(
        _adaln_kernel,
        grid=(n3 // tn,),
        in_specs=[
            pl.BlockSpec((COND_ROWS, d), lambda j: (0, 0)),
            pl.BlockSpec((depth, d, tn), lambda j: (0, 0, j)),
            pl.BlockSpec((depth, 1, tn), lambda j: (0, 0, j)),
        ],
        out_specs=[out] * depth,
        out_shape=[jax.ShapeDtypeStruct((COND_ROWS, n3), F32)] * depth,
        compiler_params=_cparams("arbitrary"),
        name="adaln",
    )(cond, ada_w, ada_b.reshape(depth, 1, n3))


def _modulate(x, gain, mod, d):
    shift = mod[:, 0:d]
    scale = mod[:, d:2 * d]
    return _rms(x, gain) * (1.0 + scale) + shift


def _rope(t, cos, sin_signed):
    lanes = t.shape[1]
    lane = lax.broadcasted_iota(jnp.int32, t.shape, 1)
    first = (lane % A_HEAD_DIM) < (A_HEAD_DIM // 2)
    partner = jnp.where(first,
                        pltpu.roll(t, lanes - A_HEAD_DIM // 2, 1),
                        pltpu.roll(t, A_HEAD_DIM // 2, 1))
    return t * cos + partner * sin_signed


def _inproj_kernel(x_ref, mod_ref, g_ref, wa_ref, wr1_ref, wr2_ref, cos_ref, sin_ref, qkg_ref, bd_ref,
                   *out_refs,
                   mod_row, has_q, rope):
    d = x_ref.shape[-1]
    row = pl.program_id(0) if mod_row is None else mod_row
    xm = _modulate(x_ref[0], g_ref[...], mod_ref[pl.ds(row, 1), :], d).astype(BF16)
    n_attn = (3 if has_q else 2) * A_WIDTH
    rest_ref = out_refs[-1]
    n_rest = rest_ref.shape[-1]
    slabs = [slice(half * QK_SLAB, (half + 1) * QK_SLAB) for half in range(A_WIDTH // QK_SLAB)]

    attn = jnp.dot(xm, wa_ref[...], preferred_element_type=F32)
    rest_ref[0, :, 0:n_rest // 2] = jnp.dot(
        xm, wr1_ref[...], preferred_element_type=F32).astype(rest_ref.dtype)
    groups = [attn[:, g * A_WIDTH:(g + 1) * A_WIDTH] for g in range(n_attn // A_WIDTH)]
    v = groups.pop()
    mean_sq = [[jnp.dot((t[:, sl] * t[:, sl]).astype(BF16), bd_ref[...], preferred_element_type=F32)
                for sl in slabs] for t in groups]
    pad = BF16_ROWS_PER_VREG
    ms_rows = sum(ms[0:pad, :] for per_group in mean_sq for ms in per_group)
    anchor = jnp.concatenate([ms_rows * 0.0] * (d // QK_SLAB), axis=1)
    xm_late = jnp.concatenate([(xm[0:pad, :].astype(F32) + anchor).astype(BF16), xm[pad:, :]], axis=0)
    rest_ref[0, :, n_rest // 2:] = jnp.dot(
        xm_late, wr2_ref[...], preferred_element_type=F32).astype(rest_ref.dtype)

    qk_refs = out_refs[:-2]
    vt_ref = out_refs[-2]
    first_gain = 0 if has_q else 1
    for gi, (t, o_ref) in enumerate(zip(groups, qk_refs)):
        gain = qkg_ref[first_gain + gi:first_gain + gi + 1, :]
        scale = Q_SCALE if (has_q and gi == 0) else 1.0
        for sl, ms in zip(slabs, mean_sq[gi]):
            tn = t[:, sl] * lax.rsqrt(ms + EPS) * gain
            if rope:
                tn = _rope(tn, cos_ref[...], sin_ref[...])
            o_ref[0, :, sl] = (tn * scale).astype(o_ref.dtype)
    for h in range(A_HEADS):
        sl = slice(h * LANES, (h + 1) * LANES)
        vt_ref[0, sl, :] = v[:, sl].T.astype(vt_ref.dtype)


def _inproj(x, mod, mod_row, gain, w, cos, sin_signed, qk_gain, bd, tb, has_q, rope):
    b, n, d = x.shape
    n_attn = (3 if has_q else 2) * A_WIDTH
    n_rest = w.shape[1] - n_attn
    cut = n_attn + n_rest // 2
    w_parts = (w[:, :n_attn], w[:, n_attn:cut], w[:, cut:])
    tok = lambda width: pl.BlockSpec((1, tb, width), lambda i, j: (i, j, 0))
    qk_out = [tok(A_WIDTH)] * (2 if has_q else 1)
    qk_shape = [jax.ShapeDtypeStruct((b, n, A_WIDTH), BF16)] * (2 if has_q else 1)
    return pl.pallas_call(
        functools.partial(_inproj_kernel, mod_row=mod_row, has_q=has_q, rope=rope),
        grid=(b, n // tb),
        in_specs=[
            tok(d),
            _const_spec(mod.shape),
            _const_spec((1, d)),
            *[_const_spec(part.shape) for part in w_parts],
            pl.BlockSpec((tb, QK_SLAB), lambda i, j: (j, 0)),
            pl.BlockSpec((tb, QK_SLAB), lambda i, j: (j, 0)),
            _const_spec(qk_gain.shape),
            _const_spec(bd.shape),
        ],
        out_specs=qk_out + [pl.BlockSpec((1, A_WIDTH, tb), lambda i, j: (i, 0, j)), tok(n_rest)],
        out_shape=qk_shape + [jax.ShapeDtypeStruct((b, A_WIDTH, n), BF16),
                              jax.ShapeDtypeStruct((b, n, n_rest), BF16)],
        compiler_params=_cparams("arbitrary", "arbitrary"),
        name="inproj_even",
    )(x, mod, gain, *w_parts, cos, sin_signed, qk_gain, bd)


def _attn_kernel(q_ref, kl_ref, kc_ref, vtl_ref, vtc_ref, g_ref, qkg_ref, subg_ref, lamp_ref,
                 o_ref, *, lam_init):
    heads = [slice(h * LANES, (h + 1) * LANES) for h in range(A_HEADS)]
    lane = lax.broadcasted_iota(jnp.int32, (1, LANES), 1)
    nt = (((1,), (1,)), ((), ()))

    lp = lamp_ref[...]
    lam = (jnp.exp(jnp.sum(lp[0:1] * lp[1:2], axis=-1, keepdims=True))
           - jnp.exp(jnp.sum(lp[2:3] * lp[3:4], axis=-1, keepdims=True)) + lam_init)
    score_bound = ((A_HEAD_DIM * Q_SCALE) * jnp.max(jnp.abs(qkg_ref[0:1, :]))
                   * jnp.max(jnp.abs(qkg_ref[1:2, :])))

    def scores(h):
        out = []
        for m in range(2):
            qm = jnp.where((lane // A_HEAD_DIM) == m, q_ref[0, :, heads[h]], 0).astype(BF16)
            out.append([lax.dot_general(k_ref[0, :, heads[h]], qm, nt, preferred_element_type=F32)
                        for k_ref in (kc_ref, kl_ref)])
        return out

    def run_heads(shift):
        s_next = scores(0)
        for h, sl in enumerate(heads):
            s_maps = s_next
            if h + 1 < A_HEADS:
                s_next = scores(h + 1)
            probs = []
            for s_parts in s_maps:
                if shift:
                    top = functools.reduce(jnp.maximum,
                                           [jnp.max(s, axis=0, keepdims=True) for s in s_parts])
                    s_parts = [s - top for s in s_parts]
                p_parts = [jnp.exp2(s) for s in s_parts]
                probs.append((p_parts, sum(jnp.sum(p, axis=0, keepdims=True) for p in p_parts)))
            (p0, l0), (p1, l1) = probs
            a0, a1 = 1.0 / l0, lam / l1
            ot = sum(jnp.dot(vt_ref[0, sl, :], (pa * a0 - pb * a1).astype(BF16),
                             preferred_element_type=F32)
                     for vt_ref, pa, pb in zip((vtc_ref, vtl_ref), p0, p1))
            ms = jnp.mean(ot * ot, axis=0, keepdims=True)
            on = (ot * lax.rsqrt(ms + EPS)).T * (subg_ref[...] * (1.0 - lam_init))
            o_ref[0, :, sl] = (on * _silu(g_ref[0, :, sl].astype(F32))).astype(o_ref.dtype)

    no_shift_ok = score_bound <= SCORE_BOUND
    pl.when(no_shift_ok)(functools.partial(run_heads, False))
    pl.when(jnp.logical_not(no_shift_ok))(functools.partial(run_heads, True))


def _attention(q, k_lat, k_ctx, vt_lat, vt_ctx, rest, qk_gain, subln_g, lam_p, lam_init, tq):
    b, n, w = q.shape
    n_ctx = k_ctx.shape[1]
    return pl.pallas_call(
        functools.partial(_attn_kernel, lam_init=lam_init),
        grid=(b, n // tq),
        in_specs=[
            pl.BlockSpec((1, tq, w), lambda i, j: (i, j, 0)),
            pl.BlockSpec((1, n, w), lambda i, j: (i, 0, 0)),
            pl.BlockSpec((1, n_ctx, w), lambda i, j: (i, 0, 0)),
            pl.BlockSpec((1, w, n), lambda i, j: (i, 0, 0)),
            pl.BlockSpec((1, w, n_ctx), lambda i, j: (i, 0, 0)),
            pl.BlockSpec((1, tq, w), lambda i, j: (i, j, REST_COL_GATE_A)),
            _const_spec(qk_gain.shape),
            _const_spec((1, LANES)),
            _const_spec((4, A_HEAD_DIM)),
        ],
        out_specs=pl.BlockSpec((1, tq, w), lambda i, j: (i, j, 0)),
        out_shape=jax.ShapeDtypeStruct((b, n, w), BF16),
        compiler_params=_cparams("arbitrary", "arbitrary"),
        name="diff_attn",
    )(q, k_lat, k_ctx, vt_lat, vt_ctx, rest, qk_gain, subln_g, lam_p)


def _split_bf16(t):
    hi = t.astype(BF16)
    return hi, (t - hi.astype(F32)).astype(BF16)


def _block_ref(g, block, row):
    c, w = g.shape
    g3 = g.reshape(c // block, block, w)
    return jnp.broadcast_to(g3[:, row:row + 1, :], g3.shape).reshape(c, w)


def _hgrn_tables(c):
    t = np.arange(c)[:, None]
    s = np.arange(c)[None, :]
    lvl = np.zeros((c, c), np.int32)
    lvl[(t // HGRN_DIAG == s // HGRN_DIAG) & (s <= t)] = 1
    b, k = HGRN_DIAG, 2
    while b < c:
        lvl[(t // b == s // b + 1) & ((s // b) % 2 == 0)] = k
        b, k = 2 * b, k + 1
    tri = (s <= t).astype(np.float32)
    return jnp.asarray(np.stack([tri, tri.T]), BF16), jnp.asarray(np.stack([lvl, lvl.T]))


def _hgrn_chunks(chains, tri_ref, lvl_ref, want_out):
    nt = (((1,), (1,)), ((), ()))
    tn = (((0,), (0,)), ((), ()))
    n = len(chains)
    c = chains[0][2].shape[0]

    kk, parts = [], []
    for (_, _, f_raw, lb, _, _) in chains:
        f = lb + (1.0 - lb) * jax.nn.sigmoid(f_raw)
        kk.append(1.0 - f)
        parts.append(_split_bf16(jnp.log(f) * LOG2E))
    cum = [sum(jnp.dot(tri_ref[ch[5]], p, preferred_element_type=F32) for p in parts[i])
           for i, ch in enumerate(chains)]
    edge = [cum[i][0:1, :] if ch[5] else cum[i][c - 1:c, :] for i, ch in enumerate(chains)]

    outs = [None] * n
    if want_out:
        a = []
        for i, (q, _, _, _, _, d) in enumerate(chains):
            ref = _block_ref(cum[i], HGRN_DIAG, HGRN_DIAG // 2)
            qd = (q * jnp.exp2(jnp.minimum(cum[i] - ref, EXP2_CLAMP))).astype(BF16)
            kd = (kk[i] * jnp.exp2(jnp.minimum(ref - cum[i], EXP2_CLAMP))).astype(BF16)
            a.append(jnp.where(lvl_ref[d] == 1,
                               lax.dot_general(qd, kd, nt, preferred_element_type=F32), 0.0))
        b, k = HGRN_DIAG, 2
        while b < c:
            for i, (q, _, _, _, _, d) in enumerate(chains):
                ref = _block_ref(cum[i], 2 * b, b if d else b - 1)
                decay = jnp.exp2(cum[i] - ref)
                ql = (q * decay).astype(BF16)
                kl = (kk[i] * (1.0 / decay)).astype(BF16)
                a[i] = jnp.where(lvl_ref[d] == k,
                                 lax.dot_general(ql, kl, nt, preferred_element_type=F32), a[i])
            b, k = 2 * b, k + 1
        for i, (q, v, _, _, st, _) in enumerate(chains):
            o = jnp.dot(a[i].astype(BF16), v, preferred_element_type=F32)
            outs[i] = o + lax.dot_general((q * jnp.exp2(cum[i])).astype(BF16), st.astype(BF16), nt,
                                          preferred_element_type=F32)

    sts = []
    for i, (_, v, _, _, st, _) in enumerate(chains):
        kg = (kk[i] * jnp.exp2(edge[i] - cum[i])).astype(BF16)
        upd = lax.dot_general(v, kg, tn, preferred_element_type=F32)
        sts.append(st * jnp.exp2(edge[i]) + upd)
    return outs, sts


def _hgrn_kernel(q_ref, i_ref, ff_ref, fb_ref, g_ref, ic_ref, ffc_ref, fbc_ref,
                 lbl_ref, ng_ref, tri_ref, lvl_ref, o_ref, acc_scr):
    c = HGRN_CHUNK
    heads = q_ref.shape[2] // LANES
    nc_lat = q_ref.shape[1] // c
    nc_ctx = ic_ref.shape[1] // c
    f_lat = (ff_ref, fb_ref)
    f_ctx = (ffc_ref, fbc_ref)

    def lower_bound(direction, sl):
        logits = [lbl_ref[direction, l, :, sl] for l in range(lbl_ref.shape[1])]
        top = functools.reduce(jnp.maximum, logits)
        e = [jnp.exp(t - top) for t in logits]
        return e[0] / sum(e)

    lanes = [slice(h * LANES, (h + 1) * LANES) for h in range(heads)]
    lbs = [[lower_bound(d, sl) for sl in lanes] for d in (0, 1)]

    def rows(i):
        return pl.ds(pl.multiple_of(i * c, c), c)

    def ctx_step(j, sts):
        chains = []
        for d in (0, 1):
            r = rows(nc_ctx - 1 - j if d else j)
            for h, sl in enumerate(lanes):
                chains.append((None, ic_ref[0, r, sl], f_ctx[d][0, r, sl].astype(F32),
                               lbs[d][h], sts[d * heads + h], d))
        return tuple(_hgrn_chunks(chains, tri_ref, lvl_ref, False)[1])

    def lat_step(j, sts, second_visit):
        chains, where = [], []
        for d in (0, 1):
            r = rows(nc_lat - 1 - j if d else j)
            for h, sl in enumerate(lanes):
                chains.append((q_ref[0, r, sl].astype(F32), i_ref[0, r, sl],
                               f_lat[d][0, r, sl].astype(F32), lbs[d][h], sts[d * heads + h], d))
                where.append((r, sl))
        outs, new = _hgrn_chunks(chains, tri_ref, lvl_ref, True)
        for o, (r, sl) in zip(outs, where):
            if second_visit:
                y = _rms(acc_scr[r, sl] + o, ng_ref[...]) * _silu(g_ref[0, r, sl].astype(F32))
                o_ref[0, r, sl] = y.astype(o_ref.dtype)
            else:
                acc_scr[r, sl] = o
        return tuple(new)

    sts = tuple(jnp.zeros((B_DIM, B_DIM), F32) for _ in range(2 * heads))
    sts = lax.fori_loop(0, nc_ctx, ctx_step, sts)
    sts = lax.fori_loop(0, nc_lat // 2, functools.partial(lat_step, second_visit=False), sts,
                        unroll=2)
    lax.fori_loop(nc_lat // 2, nc_lat, functools.partial(lat_step, second_visit=True), sts,
                  unroll=2)


def _hgrn(rest_x, rest_c, lb_logits, norm_g, heads_per_step):
    b, n, _ = rest_x.shape
    n_ctx = rest_c.shape[1]
    assert (n // HGRN_CHUNK) % 2 == 0 and B_HEADS % heads_per_step == 0
    w = heads_per_step * LANES
    steps = B_HEADS // heads_per_step

    def xs(group):
        return pl.BlockSpec((1, n, w), lambda i, h: (i, 0, group * steps + h))

    def cs(group):
        return pl.BlockSpec((1, n_ctx, w), lambda i, h: (i, 0, group * steps + h))

    n_layers = lb_logits.shape[1]
    tri, lvl = _hgrn_tables(HGRN_CHUNK)
    return pl.pallas_call(
        _hgrn_kernel,
        grid=(b, steps),
        in_specs=[
            xs(REST_COL_Q), xs(REST_COL_I), xs(REST_COL_FF), xs(REST_COL_FB), xs(REST_COL_GATE_B),
            cs(CTX_COL_I), cs(CTX_COL_FF), cs(CTX_COL_FB),
            pl.BlockSpec((2, n_layers, 1, w), lambda i, h: (0, 0, 0, h)),
            _const_spec((1, LANES)),
            _const_spec(tri.shape),
            _const_spec(lvl.shape),
        ],
        out_specs=pl.BlockSpec((1, n, w), lambda i, h: (i, 0, h)),
        out_shape=jax.ShapeDtypeStruct((b, n, B_WIDTH), BF16),
        scratch_shapes=[pltpu.VMEM((n, w), F32)],
        compiler_params=_cparams("arbitrary", "arbitrary"),
        name="hgrn2",
    )(rest_x, rest_x, rest_x, rest_x, rest_x, rest_c, rest_c, rest_c,
      lb_logits.reshape(2, n_layers, 1, B_WIDTH), norm_g, tri, lvl)


def _layer1_input(ya, yb, x, gate0, mod1, gain1, wo0_ref):
    d = x.shape[-1]
    half = ya.shape[-1]
    upd = (jnp.dot(ya, wo0_ref[0:half, :], preferred_element_type=F32)
           + jnp.dot(yb, wo0_ref[half:, :], preferred_element_type=F32))
    x1 = x + gate0 * upd
    return x1, _modulate(x1, gain1, mod1, d).astype(BF16)


def _edge_kernel(ya_ref, yb_ref, x_ref, mod0_ref, mod1_ref, g1_ref, wo0_ref, wi_ref, z_ref,
                 *, rows_per_batch):
    d = x_ref.shape[-1]

    def per_row(ref, lo, hi):
        return jnp.concatenate([jnp.broadcast_to(ref[i:i + 1, lo:hi], (rows_per_batch, hi - lo))
                                for i in range(x_ref.shape[0] // rows_per_batch)], axis=0)

    _, xm = _layer1_input(ya_ref[...], yb_ref[...], x_ref[...], per_row(mod0_ref, 2 * d, 3 * d),
                          per_row(mod1_ref, 0, 2 * d), g1_ref[...], wo0_ref)
    p = jnp.dot(xm, wi_ref[...], preferred_element_type=F32)
    z_ref[...] = p[:, :D_WIDTH] * p[:, D_WIDTH:]


def _block_edges(t, tb):
    b, n, w = t.shape
    te = t.reshape(b, n // tb, tb, w)
    return jnp.concatenate([te[:, :, :SUBLANES], te[:, :, tb - SUBLANES:]], axis=2).reshape(-1, w)


def _edge_z(ya, yb, x, mod0, mod1, gain1, w_out0, w_in1, tb):
    b, n, d = x.shape
    rows = (n // tb) * 2 * SUBLANES
    cg_start = 3 * C_WIDTH + D_WIDTH
    assert cg_start % (2 * D_WIDTH) == 0
    cg_blk = cg_start // (2 * D_WIDTH)
    full = lambda arr: _const_spec(arr.shape)
    xe, yae, ybe = _block_edges(x, tb), _block_edges(ya, tb), _block_edges(yb, tb)
    return pl.pallas_call(
        functools.partial(_edge_kernel, rows_per_batch=rows),
        grid=(1,),
        in_specs=[full(yae), full(ybe), full(xe), full(mod0), full(mod1), _const_spec((1, d)),
                  full(w_out0),
                  pl.BlockSpec((d, 2 * D_WIDTH), lambda i: (0, cg_blk), pipeline_mode=pl.Buffered(1))],
        out_specs=pl.BlockSpec((b * rows, D_WIDTH), lambda i: (0, 0)),
        out_shape=jax.ShapeDtypeStruct((b * rows, D_WIDTH), F32),
        compiler_params=_cparams("arbitrary"),
        name="conv_edge_rows",
    )(yae, ybe, xe, mod0, mod1, gain1, w_out0, w_in1).reshape(b, rows, D_WIDTH)


def _layer1_kernel(ya_ref, yb_ref, x_ref, ze_ref, mod0_ref, mod1_ref, g1_ref, wo0_ref, wi_ref,
                   vg_ref, ws_ref, bs_ref, cw_ref, wo1_ref, o_ref):
    d = x_ref.shape[-1]
    tb = x_ref.shape[1]
    j = pl.program_id(1)
    last_j = pl.num_programs(1) - 1
    mod1 = mod1_ref[pl.ds(pl.program_id(0), 1), :]
    gate0 = mod0_ref[pl.ds(pl.program_id(0), 1), 2 * d:]
    gate1 = mod1[:, 2 * d:]
    sub = tb // L1_SUB_BLOCKS
    subs = [slice(s * sub, (s + 1) * sub) for s in range(L1_SUB_BLOCKS)]
    col = lambda p, k: p[:, k * C_WIDTH:(k + 1) * C_WIDTH]
    n_gmlp = 3 * C_WIDTH

    x1s, xms = [], []
    for r in subs:
        x1, xm = _layer1_input(ya_ref[0, r, :], yb_ref[0, r, :], x_ref[0, r, :], gate0,
                               mod1, g1_ref[...], wo0_ref)
        x1s.append(x1)
        xms.append(xm)
    pgs = [jnp.dot(xm, wi_ref[:, 0:n_gmlp], preferred_element_type=F32) for xm in xms]
    pcs = [jnp.dot(xm, wi_ref[:, n_gmlp:], preferred_element_type=F32) for xm in xms]

    upd_c = []
    for pg in pgs:
        u = _gelu(col(pg, 0))
        vn = _rms(_gelu(col(pg, 1)), vg_ref[...]).astype(BF16)
        chunks = []
        for ci in range(sub // C_CHUNK):
            cr = slice(ci * C_CHUNK, (ci + 1) * C_CHUNK)
            groups = []
            for g in range(C_GROUPS):
                gl = slice(g * LANES, (g + 1) * LANES)
                groups.append(jnp.dot(ws_ref[g], vn[cr, gl], preferred_element_type=F32) + bs_ref[g])
            chunks.append(jnp.concatenate(groups, axis=1))
        o_c = u * jnp.concatenate(chunks, axis=0) * _silu(col(pg, 2))
        upd_c.append(jnp.dot(o_c.astype(BF16), wo1_ref[0:C_WIDTH, :], preferred_element_type=F32))

    z = jnp.concatenate([col(pc, 1) * col(pc, 2) for pc in pcs], axis=0)
    grp = 2 * SUBLANES
    prev_grp = ze_ref[0, pl.ds(pl.multiple_of(jnp.maximum(j - 1, 0) * grp + SUBLANES, SUBLANES),
                               SUBLANES), :]
    next_grp = ze_ref[0, pl.ds(pl.multiple_of(jnp.minimum(j + 1, last_j) * grp, SUBLANES),
                               SUBLANES), :]
    z_prev_row = jnp.where(j == 0, 0.0, prev_grp[SUBLANES - 1:, :])
    z_next_row = jnp.where(j == last_j, 0.0, next_grp[0:1, :])
    rowi = lax.broadcasted_iota(jnp.int32, z.shape, 0)
    z_prev = jnp.where(rowi == 0, z_prev_row, pltpu.roll(z, 1, 0))
    z_next = jnp.where(rowi == tb - 1, z_next_row, pltpu.roll(z, tb - 1, 0))
    conv = cw_ref[0:1, :] * z_prev + cw_ref[1:2, :] * z + cw_ref[2:3, :] * z_next

    for r, x1, pc, uc in zip(subs, x1s, pcs, upd_c):
        o_d = col(pc, 0) * conv[r, :] * _silu(col(pc, 3))
        upd = uc + jnp.dot(o_d.astype(BF16), wo1_ref[C_WIDTH:, :], preferred_element_type=F32)
        o_ref[0, r, :] = x1 + gate1 * upd


def _layer1(ya, yb, x, ze, mod0, mod1, gain1, w_out0, w_in1, v_gain, w_s, b_s, conv_w, w_out1, tb):
    b, n, d = x.shape
    half = ya.shape[-1]
    tok = lambda width: pl.BlockSpec((1, tb, width), lambda i, j: (i, j, 0))
    modspec = _const_spec(mod0.shape)
    return pl.pallas_call(
        _layer1_kernel,
        grid=(b, n // tb),
        in_specs=[tok(half), tok(half), tok(d),
                  pl.BlockSpec((1,) + ze.shape[1:], lambda i, j: (i, 0, 0)),
                  modspec, modspec, _const_spec((1, d)),
                  _const_spec(w_out0.shape), _const_spec(w_in1.shape),
                  _const_spec((1, C_WIDTH)),
                  _const_spec((C_GROUPS, C_CHUNK, C_CHUNK)),
                  _const_spec((C_GROUPS, C_CHUNK, LANES)),
                  _const_spec((3, D_WIDTH)),
                  _const_spec(w_out1.shape)],
        out_specs=tok(d),
        out_shape=jax.ShapeDtypeStruct((b, n, d), F32),
        compiler_params=_cparams("arbitrary", "arbitrary"),
        name="outproj_even_layer_odd",
    )(ya, yb, x, ze, mod0, mod1, gain1, w_out0, w_in1, v_gain, w_s, b_s, conv_w, w_out1)


def _rope_tables(n):
    rows_ = n // GRID_W
    row = np.repeat(np.arange(rows_, dtype=np.float64), GRID_W)
    col = np.tile(np.arange(GRID_W, dtype=np.float64), rows_)
    n_freq = A_HEAD_DIM // 4
    inv = ROPE_THETA ** (-np.arange(n_freq, dtype=np.float64) / n_freq)
    ang = np.concatenate([row[:, None] * inv, col[:, None] * inv], axis=-1)
    cos, sin = np.cos(ang), np.sin(ang)
    reps = QK_SLAB // A_HEAD_DIM
    return (jnp.asarray(np.tile(np.concatenate([cos, cos], axis=-1), (1, reps)), F32),
            jnp.asarray(np.tile(np.concatenate([-sin, sin], axis=-1), (1, reps)), F32))


def kernel(x, c, ctx, c_ctx, norm_gain, ada_w, ada_b, even_w_in, even_w_out, attn_qk_gain,
           attn_lambda, attn_subln_gain, hgrn_lb_logits, hgrn_norm_gain, odd_w_in, odd_w_out,
           gmlp_v_gain, gmlp_w_s, gmlp_b_s, conv_w):
    b, n, d = x.shape
    assert b + 1 <= COND_ROWS and n % 512 == 0 and ctx.shape[1] % HGRN_CHUNK == 0
    assert norm_gain.shape[0] == 2, "two-layer block: one even layer then one odd layer"

    cond = jnp.concatenate([c, c_ctx[None, :], jnp.zeros((COND_ROWS - b - 1, d), F32)], axis=0)
    mod0, mod1 = _adaln(cond, ada_w, ada_b)

    w_in0 = even_w_in[0].astype(BF16)
    gain0 = norm_gain[0].reshape(1, d)
    cos, sin_signed = _rope_tables(n)
    qk_gain = jnp.tile(attn_qk_gain[0], (1, QK_SLAB // A_HEAD_DIM))
    blk = np.arange(QK_SLAB) // A_HEAD_DIM
    bd = jnp.asarray(np.where(blk[:, None] == blk[None, :], 1.0 / A_HEAD_DIM, 0.0), BF16)
    q, k_lat, vt_lat, rest_x = _inproj(x, mod0, None, gain0, w_in0, cos, sin_signed, qk_gain, bd,
                                       TOKEN_BLOCK, True, True)
    w_ctx = jnp.concatenate([w_in0[:, g * A_WIDTH:(g + 1) * A_WIDTH] for g in CTX_GROUPS], axis=1)
    n_ctx = ctx.shape[1]
    k_ctx, vt_ctx, rest_c = _inproj(ctx, mod0, b, gain0, w_ctx, cos[:n_ctx], sin_signed[:n_ctx],
                                    qk_gain, bd, n_ctx, False, False)

    lam_init = 0.8 - 0.6 * math.exp(-0.3 * 0)
    ya = _attention(q, k_lat, k_ctx, vt_lat, vt_ctx, rest_x, qk_gain,
                    attn_subln_gain[0].reshape(1, LANES), attn_lambda[0], lam_init, ATTN_Q_BLOCK)
    yb = _hgrn(rest_x, rest_c, hgrn_lb_logits, hgrn_norm_gain[0].reshape(1, LANES),
               HGRN_HEADS_PER_STEP)

    gain1 = norm_gain[1].reshape(1, d)
    w_out0, w_in1 = even_w_out[0].astype(BF16), odd_w_in[0].astype(BF16)
    ze = _edge_z(ya, yb, x, mod0, mod1, gain1, w_out0, w_in1, TOKEN_BLOCK)
    b_s = jnp.broadcast_to(gmlp_b_s[0][:, :, None], (C_GROUPS, C_CHUNK, LANES))
    return _layer1(ya, yb, x, ze, mod0, mod1, gain1, w_out0, w_in1,
                   gmlp_v_gain[0].reshape(1, C_WIDTH), gmlp_w_s[0].astype(BF16), b_s, conv_w[0],
                   odd_w_out[0].astype(BF16), TOKEN_BLOCK)
```

```python
import functools
import math

import jax
import jax.numpy as jnp
import numpy as np
from jax import lax
from jax.experimental import pallas as pl
from jax.experimental.pallas import tpu as pltpu

F32 = jnp.float32
BF16 = jnp.bfloat16

EPS = 1e-6
GRID_W = 64
ROPE_THETA = 10000.0
A_HEADS = 4
A_HEAD_DIM = 64
A_WIDTH = 2 * A_HEADS * A_HEAD_DIM
B_HEADS = 4
B_DIM = 128
B_WIDTH = B_HEADS * B_DIM
C_GROUPS = 4
C_CHUNK = 128
C_WIDTH = 512
D_WIDTH = 512
EVEN_IN = 4 * A_WIDTH + 5 * B_WIDTH
ODD_IN = 3 * C_WIDTH + 4 * D_WIDTH

CTX_GROUPS = (1, 2, 5, 6, 7)
REST_COL_GATE_A, REST_COL_Q, REST_COL_I, REST_COL_FF, REST_COL_FB, REST_COL_GATE_B = range(6)
CTX_COL_I, CTX_COL_FF, CTX_COL_FB = range(3)
QK_SLAB = 256

LANES = 128
HGRN_CHUNK = 128
HGRN_HEADS_PER_STEP = 4
HGRN_UNROLL = 4
HGRN_DIAG = 8
SCORE_BOUND = 100.0
EXP2_CLAMP = 115.0
LOG2E = math.log2(math.e)
Q_SCALE = A_HEAD_DIM ** -0.5 * LOG2E
TOKEN_BLOCK = 512
ATTN_Q_BLOCK = 256
SUBLANES = 8
BF16_ROWS_PER_VREG = 16
L1_SUB_BLOCKS = 2
COND_ROWS = 16
ADALN_COL_BLOCK = 512
VMEM_LIMIT = 56 * 1024 * 1024


def _cparams(*sem):
    return pltpu.CompilerParams(dimension_semantics=sem, vmem_limit_bytes=VMEM_LIMIT)


def _const_spec(shape):
    nd = len(shape)
    return pl.BlockSpec(shape, lambda *_: (0,) * nd, pipeline_mode=pl.Buffered(1))


def _silu(t):
    return t * jax.nn.sigmoid(t)


def _gelu(t):
    return 0.5 * t * (1.0 + lax.erf(t * (1.0 / math.sqrt(2.0))))


def _rms(t, gain):
    ms = jnp.mean(t * t, axis=-1, keepdims=True)
    return t * lax.rsqrt(ms + EPS) * gain


def _adaln_kernel(cond_ref, w_ref, b_ref, *o_refs):
    a = _silu(cond_ref[...])
    for layer, o_ref in enumerate(o_refs):
        o_ref[...] = jnp.dot(a, w_ref[layer], preferred_element_type=F32) + b_ref[layer]


def _adaln(cond, ada_w, ada_b):
    depth, d, n3 = ada_w.shape
    tn = ADALN_COL_BLOCK
    out = pl.BlockSpec((COND_ROWS, tn), lambda j: (0, j))
    return pl.pallas_call(
        _adaln_kernel,
        grid=(n3 // tn,),
        in_specs=[
            pl.BlockSpec((COND_ROWS, d), lambda j: (0, 0)),
            pl.BlockSpec((depth, d, tn), lambda j: (0, 0, j)),
            pl.BlockSpec((depth, 1, tn), lambda j: (0, 0, j)),
        ],
        out_specs=[out] * depth,
        out_shape=[jax.ShapeDtypeStruct((COND_ROWS, n3), F32)] * depth,
        compiler_params=_cparams("arbitrary"),
        name="adaln",
    )(cond, ada_w, ada_b.reshape(depth, 1, n3))


def _modulate(x, gain, mod, d):
    shift = mod[:, 0:d]
    scale = mod[:, d:2 * d]
    return _rms(x, gain) * (1.0 + scale) + shift


def _rope(t, cos, sin_signed):
    lanes = t.shape[1]
    lane = lax.broadcasted_iota(jnp.int32, t.shape, 1)
    first = (lane % A_HEAD_DIM) < (A_HEAD_DIM // 2)
    partner = jnp.where(first,
                        pltpu.roll(t, lanes - A_HEAD_DIM // 2, 1),
                        pltpu.roll(t, A_HEAD_DIM // 2, 1))
    return t * cos + partner * sin_signed


def _inproj_kernel(x_ref, mod_ref, g_ref, wa_ref, wr1_ref, wr2_ref, cos_ref, sin_ref, qkg_ref, bd_ref,
                   *out_refs,
                   mod_row, has_q, rope):
    d = x_ref.shape[-1]
    row = pl.program_id(0) if mod_row is None else mod_row
    xm = _modulate(x_ref[0], g_ref[...], mod_ref[pl.ds(row, 1), :], d).astype(BF16)
    n_attn = (3 if has_q else 2) * A_WIDTH
    rest_ref = out_refs[-1]
    n_rest = rest_ref.shape[-1]
    slabs = [slice(half * QK_SLAB, (half + 1) * QK_SLAB) for half in range(A_WIDTH // QK_SLAB)]

    attn = jnp.dot(xm, wa_ref[...], preferred_element_type=F32)
    rest_ref[0, :, 0:n_rest // 2] = jnp.dot(
        xm, wr1_ref[...], preferred_element_type=F32).astype(rest_ref.dtype)
    groups = [attn[:, g * A_WIDTH:(g + 1) * A_WIDTH] for g in range(n_attn // A_WIDTH)]
    v = groups.pop()
    mean_sq = [[jnp.dot((t[:, sl] * t[:, sl]).astype(BF16), bd_ref[...], preferred_element_type=F32)
                for sl in slabs] for t in groups]
    pad = BF16_ROWS_PER_VREG
    ms_rows = sum(ms[0:pad, :] for per_group in mean_sq for ms in per_group)
    anchor = jnp.concatenate([ms_rows * 0.0] * (d // QK_SLAB), axis=1)
    xm_late = jnp.concatenate([(xm[0:pad, :].astype(F32) + anchor).astype(BF16), xm[pad:, :]], axis=0)
    rest_ref[0, :, n_rest // 2:] = jnp.dot(
        xm_late, wr2_ref[...], preferred_element_type=F32).astype(rest_ref.dtype)

    qk_refs = out_refs[:-2]
    vt_ref = out_refs[-2]
    first_gain = 0 if has_q else 1
    for gi, (t, o_ref) in enumerate(zip(groups, qk_refs)):
        gain = qkg_ref[first_gain + gi:first_gain + gi + 1, :]
        scale = Q_SCALE if (has_q and gi == 0) else 1.0
        for sl, ms in zip(slabs, mean_sq[gi]):
            tn = t[:, sl] * lax.rsqrt(ms + EPS) * gain
            if rope:
                tn = _rope(tn, cos_ref[...], sin_ref[...])
            o_ref[0, :, sl] = (tn * scale).astype(o_ref.dtype)
    for h in range(A_HEADS):
        sl = slice(h * LANES, (h + 1) * LANES)
        vt_ref[0, sl, :] = v[:, sl].T.astype(vt_ref.dtype)


def _inproj(x, mod, mod_row, gain, w, cos, sin_signed, qk_gain, bd, tb, has_q, rope):
    b, n, d = x.shape
    n_attn = (3 if has_q else 2) * A_WIDTH
    n_rest = w.shape[1] - n_attn
    cut = n_attn + n_rest // 2
    w_parts = (w[:, :n_attn], w[:, n_attn:cut], w[:, cut:])
    tok = lambda width: pl.BlockSpec((1, tb, width), lambda i, j: (i, j, 0))
    qk_out = [tok(A_WIDTH)] * (2 if has_q else 1)
    qk_shape = [jax.ShapeDtypeStruct((b, n, A_WIDTH), BF16)] * (2 if has_q else 1)
    return pl.pallas_call(
        functools.partial(_inproj_kernel, mod_row=mod_row, has_q=has_q, rope=rope),
        grid=(b, n // tb),
        in_specs=[
            tok(d),
            _const_spec(mod.shape),
            _const_spec((1, d)),
            *[_const_spec(part.shape) for part in w_parts],
            pl.BlockSpec((tb, QK_SLAB), lambda i, j: (j, 0)),
            pl.BlockSpec((tb, QK_SLAB), lambda i, j: (j, 0)),
            _const_spec(qk_gain.shape),
            _const_spec(bd.shape),
        ],
        out_specs=qk_out + [pl.BlockSpec((1, A_WIDTH, tb), lambda i, j: (i, 0, j)), tok(n_rest)],
        out_shape=qk_shape + [jax.ShapeDtypeStruct((b, A_WIDTH, n), BF16),
                              jax.ShapeDtypeStruct((b, n, n_rest), BF16)],
        compiler_params=_cparams("arbitrary", "arbitrary"),
        name="inproj_even",
    )(x, mod, gain, *w_parts, cos, sin_signed, qk_gain, bd)


def _attn_kernel(q_ref, kl_ref, kc_ref, vtl_ref, vtc_ref, g_ref, qkg_ref, subg_ref, lamp_ref,
                 o_ref, *, lam_init):
    heads = [slice(h * LANES, (h + 1) * LANES) for h in range(A_HEADS)]
    lane = lax.broadcasted_iota(jnp.int32, (1, LANES), 1)
    nt = (((1,), (1,)), ((), ()))

    lp = lamp_ref[...]
    lam = (jnp.exp(jnp.sum(lp[0:1] * lp[1:2], axis=-1, keepdims=True))
           - jnp.exp(jnp.sum(lp[2:3] * lp[3:4], axis=-1, keepdims=True)) + lam_init)
    score_bound = ((A_HEAD_DIM * Q_SCALE) * jnp.max(jnp.abs(qkg_ref[0:1, :]))
                   * jnp.max(jnp.abs(qkg_ref[1:2, :])))

    def scores(h):
        out = []
        for m in range(2):
            qm = jnp.where((lane // A_HEAD_DIM) == m, q_ref[0, :, heads[h]], 0).astype(BF16)
            out.append([lax.dot_general(k_ref[0, :, heads[h]], qm, nt, preferred_element_type=F32)
                        for k_ref in (kc_ref, kl_ref)])
        return out

    def run_heads(shift):
        s_next = scores(0)
        for h, sl in enumerate(heads):
            s_maps = s_next
            if h + 1 < A_HEADS:
                s_next = scores(h + 1)
            probs = []
            for s_parts in s_maps:
                if shift:
                    top = functools.reduce(jnp.maximum,
                                           [jnp.max(s, axis=0, keepdims=True) for s in s_parts])
                    s_parts = [s - top for s in s_parts]
                p_parts = [jnp.exp2(s) for s in s_parts]
                probs.append((p_parts, sum(jnp.sum(p, axis=0, keepdims=True) for p in p_parts)))
            (p0, l0), (p1, l1) = probs
            a0, a1 = 1.0 / l0, lam / l1
            ot = sum(jnp.dot(vt_ref[0, sl, :], (pa * a0 - pb * a1).astype(BF16),
                             preferred_element_type=F32)
                     for vt_ref, pa, pb in zip((vtc_ref, vtl_ref), p0, p1))
            ms = jnp.mean(ot * ot, axis=0, keepdims=True)
            on = (ot * lax.rsqrt(ms + EPS)).T * (subg_ref[...] * (1.0 - lam_init))
            o_ref[0, :, sl] = (on * _silu(g_ref[0, :, sl].astype(F32))).astype(o_ref.dtype)

    no_shift_ok = score_bound <= SCORE_BOUND
    pl.when(no_shift_ok)(functools.partial(run_heads, False))
    pl.when(jnp.logical_not(no_shift_ok))(functools.partial(run_heads, True))


def _attention(q, k_lat, k_ctx, vt_lat, vt_ctx, rest, qk_gain, subln_g, lam_p, lam_init, tq):
    b, n, w = q.shape
    n_ctx = k_ctx.shape[1]
    return pl.pallas_call(
        functools.partial(_attn_kernel, lam_init=lam_init),
        grid=(b, n // tq),
        in_specs=[
            pl.BlockSpec((1, tq, w), lambda i, j: (i, j, 0)),
            pl.BlockSpec((1, n, w), lambda i, j: (i, 0, 0)),
            pl.BlockSpec((1, n_ctx, w), lambda i, j: (i, 0, 0)),
            pl.BlockSpec((1, w, n), lambda i, j: (i, 0, 0)),
            pl.BlockSpec((1, w, n_ctx), lambda i, j: (i, 0, 0)),
            pl.BlockSpec((1, tq, w), lambda i, j: (i, j, REST_COL_GATE_A)),
            _const_spec(qk_gain.shape),
            _const_spec((1, LANES)),
            _const_spec((4, A_HEAD_DIM)),
        ],
        out_specs=pl.BlockSpec((1, tq, w), lambda i, j: (i, j, 0)),
        out_shape=jax.ShapeDtypeStruct((b, n, w), BF16),
        compiler_params=_cparams("arbitrary", "arbitrary"),
        name="diff_attn",
    )(q, k_lat, k_ctx, vt_lat, vt_ctx, rest, qk_gain, subln_g, lam_p)


def _split_bf16(t):
    hi = t.astype(BF16)
    return hi, (t - hi.astype(F32)).astype(BF16)


def _block_ref(g, block, row):
    c, w = g.shape
    g3 = g.reshape(c // block, block, w)
    return jnp.broadcast_to(g3[:, row:row + 1, :], g3.shape).reshape(c, w)


def _hgrn_tables(c):
    t = np.arange(c)[:, None]
    s = np.arange(c)[None, :]
    lvl = np.zeros((c, c), np.int32)
    lvl[(t // HGRN_DIAG == s // HGRN_DIAG) & (s <= t)] = 1
    b, k = HGRN_DIAG, 2
    while b < c:
        lvl[(t // b == s // b + 1) & ((s // b) % 2 == 0)] = k
        b, k = 2 * b, k + 1
    tri = (s <= t).astype(np.float32)
    return jnp.asarray(np.stack([tri, tri.T]), BF16), jnp.asarray(np.stack([lvl, lvl.T]))


def _hgrn_chunks(chains, tri_ref, lvl_ref, want_out):
    nt = (((1,), (1,)), ((), ()))
    tn = (((0,), (0,)), ((), ()))
    n = len(chains)
    c = chains[0][2].shape[0]

    kk, parts = [], []
    for (_, _, f_raw, lb, _, _) in chains:
        f = lb + (1.0 - lb) * jax.nn.sigmoid(f_raw)
        kk.append(1.0 - f)
        parts.append(_split_bf16(jnp.log(f) * LOG2E))
    cum = [sum(jnp.dot(tri_ref[ch[5]], p, preferred_element_type=F32) for p in parts[i])
           for i, ch in enumerate(chains)]
    edge = [cum[i][0:1, :] if ch[5] else cum[i][c - 1:c, :] for i, ch in enumerate(chains)]

    outs = [None] * n
    if want_out:
        a = []
        for i, (q, _, _, _, _, d) in enumerate(chains):
            ref = _block_ref(cum[i], HGRN_DIAG, HGRN_DIAG // 2)
            qd = (q * jnp.exp2(jnp.minimum(cum[i] - ref, EXP2_CLAMP))).astype(BF16)
            kd = (kk[i] * jnp.exp2(jnp.minimum(ref - cum[i], EXP2_CLAMP))).astype(BF16)
            a.append(jnp.where(lvl_ref[d] == 1,
                               lax.dot_general(qd, kd, nt, preferred_element_type=F32), 0.0))
        b, k = HGRN_DIAG, 2
        while b < c:
            for i, (q, _, _, _, _, d) in enumerate(chains):
                ref = _block_ref(cum[i], 2 * b, b if d else b - 1)
                decay = jnp.exp2(cum[i] - ref)
                ql = (q * decay).astype(BF16)
                kl = (kk[i] * (1.0 / decay)).astype(BF16)
                a[i] = jnp.where(lvl_ref[d] == k,
                                 lax.dot_general(ql, kl, nt, preferred_element_type=F32), a[i])
            b, k = 2 * b, k + 1
        for i, (q, v, _, _, st, _) in enumerate(chains):
            o = jnp.dot(a[i].astype(BF16), v, preferred_element_type=F32)
            outs[i] = o + lax.dot_general((q * jnp.exp2(cum[i])).astype(BF16), st.astype(BF16), nt,
                                          preferred_element_type=F32)

    sts = []
    for i, (_, v, _, _, st, _) in enumerate(chains):
        kg = (kk[i] * jnp.exp2(edge[i] - cum[i])).astype(BF16)
        upd = lax.dot_general(v, kg, tn, preferred_element_type=F32)
        sts.append(st * jnp.exp2(edge[i]) + upd)
    return outs, sts


def _hgrn_kernel(q_ref, i_ref, ff_ref, fb_ref, g_ref, ic_ref, ffc_ref, fbc_ref,
                 lbl_ref, ng_ref, tri_ref, lvl_ref, o_ref, acc_scr):
    c = HGRN_CHUNK
    heads = q_ref.shape[2] // LANES
    nc_lat = q_ref.shape[1] // c
    nc_ctx = ic_ref.shape[1] // c
    f_lat = (ff_ref, fb_ref)
    f_ctx = (ffc_ref, fbc_ref)

    def lower_bound(direction, sl):
        logits = [lbl_ref[direction, l, :, sl] for l in range(lbl_ref.shape[1])]
        top = functools.reduce(jnp.maximum, logits)
        e = [jnp.exp(t - top) for t in logits]
        return e[0] / sum(e)

    lanes = [slice(h * LANES, (h + 1) * LANES) for h in range(heads)]
    lbs = [[lower_bound(d, sl) for sl in lanes] for d in (0, 1)]

    def rows(i):
        return pl.ds(pl.multiple_of(i * c, c), c)

    def ctx_step(j, sts):
        chains = []
        for d in (0, 1):
            r = rows(nc_ctx - 1 - j if d else j)
            for h, sl in enumerate(lanes):
                chains.append((None, ic_ref[0, r, sl], f_ctx[d][0, r, sl].astype(F32),
                               lbs[d][h], sts[d * heads + h], d))
        return tuple(_hgrn_chunks(chains, tri_ref, lvl_ref, False)[1])

    def lat_step(j, sts, second_visit):
        chains, where = [], []
        for d in (0, 1):
            r = rows(nc_lat - 1 - j if d else j)
            for h, sl in enumerate(lanes):
                chains.append((q_ref[0, r, sl].astype(F32), i_ref[0, r, sl],
                               f_lat[d][0, r, sl].astype(F32), lbs[d][h], sts[d * heads + h], d))
                where.append((r, sl))
        outs, new = _hgrn_chunks(chains, tri_ref, lvl_ref, True)
        for o, (r, sl) in zip(outs, where):
            if second_visit:
                y = _rms(acc_scr[r, sl] + o, ng_ref[...]) * _silu(g_ref[0, r, sl].astype(F32))
                o_ref[0, r, sl] = y.astype(o_ref.dtype)
            else:
                acc_scr[r, sl] = o
        return tuple(new)

    sts = tuple(jnp.zeros((B_DIM, B_DIM), F32) for _ in range(2 * heads))
    sts = lax.fori_loop(0, nc_ctx, ctx_step, sts, unroll=True)
    sts = lax.fori_loop(0, nc_lat // 2, functools.partial(lat_step, second_visit=False), sts,
                        unroll=HGRN_UNROLL)
    lax.fori_loop(nc_lat // 2, nc_lat, functools.partial(lat_step, second_visit=True), sts,
                  unroll=HGRN_UNROLL)


def _hgrn(rest_x, rest_c, lb_logits, norm_g, heads_per_step):
    b, n, _ = rest_x.shape
    n_ctx = rest_c.shape[1]
    assert (n // HGRN_CHUNK) % 2 == 0 and B_HEADS % heads_per_step == 0
    w = heads_per_step * LANES
    steps = B_HEADS // heads_per_step

    def xs(group):
        return pl.BlockSpec((1, n, w), lambda i, h: (i, 0, group * steps + h))

    def cs(group):
        return pl.BlockSpec((1, n_ctx, w), lambda i, h: (i, 0, group * steps + h))

    n_layers = lb_logits.shape[1]
    tri, lvl = _hgrn_tables(HGRN_CHUNK)
    return pl.pallas_call(
        _hgrn_kernel,
        grid=(b, steps),
        in_specs=[
            xs(REST_COL_Q), xs(REST_COL_I), xs(REST_COL_FF), xs(REST_COL_FB), xs(REST_COL_GATE_B),
            cs(CTX_COL_I), cs(CTX_COL_FF), cs(CTX_COL_FB),
            pl.BlockSpec((2, n_layers, 1, w), lambda i, h: (0, 0, 0, h)),
            _const_spec((1, LANES)),
            _const_spec(tri.shape),
            _const_spec(lvl.shape),
        ],
        out_specs=pl.BlockSpec((1, n, w), lambda i, h: (i, 0, h)),
        out_shape=jax.ShapeDtypeStruct((b, n, B_WIDTH), BF16),
        scratch_shapes=[pltpu.VMEM((n, w), F32)],
        compiler_params=_cparams("arbitrary", "arbitrary"),
        name="hgrn2",
    )(rest_x, rest_x, rest_x, rest_x, rest_x, rest_c, rest_c, rest_c,
      lb_logits.reshape(2, n_layers, 1, B_WIDTH), norm_g, tri, lvl)


def _layer1_input(ya, yb, x, gate0, mod1, gain1, wo0_ref):
    d = x.shape[-1]
    half = ya.shape[-1]
    upd = (jnp.dot(ya, wo0_ref[0:half, :], preferred_element_type=F32)
           + jnp.dot(yb, wo0_ref[half:, :], preferred_element_type=F32))
    x1 = x + gate0 * upd
    return x1, _modulate(x1, gain1, mod1, d).astype(BF16)


def _edge_kernel(ya_ref, yb_ref, x_ref, mod0_ref, mod1_ref, g1_ref, wo0_ref, wi_ref, z_ref,
                 *, rows_per_batch):
    d = x_ref.shape[-1]

    def per_row(ref, lo, hi):
        return jnp.concatenate([jnp.broadcast_to(ref[i:i + 1, lo:hi], (rows_per_batch, hi - lo))
                                for i in range(x_ref.shape[0] // rows_per_batch)], axis=0)

    _, xm = _layer1_input(ya_ref[...], yb_ref[...], x_ref[...], per_row(mod0_ref, 2 * d, 3 * d),
                          per_row(mod1_ref, 0, 2 * d), g1_ref[...], wo0_ref)
    p = jnp.dot(xm, wi_ref[...], preferred_element_type=F32)
    z_ref[...] = p[:, :D_WIDTH] * p[:, D_WIDTH:]


def _block_edges(t, tb):
    b, n, w = t.shape
    te = t.reshape(b, n // tb, tb, w)
    return jnp.concatenate([te[:, :, :SUBLANES], te[:, :, tb - SUBLANES:]], axis=2).reshape(-1, w)


def _edge_z(ya, yb, x, mod0, mod1, gain1, w_out0, w_in1, tb):
    b, n, d = x.shape
    rows = (n // tb) * 2 * SUBLANES
    cg_start = 3 * C_WIDTH + D_WIDTH
    assert cg_start % (2 * D_WIDTH) == 0
    cg_blk = cg_start // (2 * D_WIDTH)
    full = lambda arr: _const_spec(arr.shape)
    xe, yae, ybe = _block_edges(x, tb), _block_edges(ya, tb), _block_edges(yb, tb)
    return pl.pallas_call(
        functools.partial(_edge_kernel, rows_per_batch=rows),
        grid=(1,),
        in_specs=[full(yae), full(ybe), full(xe), full(mod0), full(mod1), _const_spec((1, d)),
                  full(w_out0),
                  pl.BlockSpec((d, 2 * D_WIDTH), lambda i: (0, cg_blk), pipeline_mode=pl.Buffered(1))],
        out_specs=pl.BlockSpec((b * rows, D_WIDTH), lambda i: (0, 0)),
        out_shape=jax.ShapeDtypeStruct((b * rows, D_WIDTH), F32),
        compiler_params=_cparams("arbitrary"),
        name="conv_edge_rows",
    )(yae, ybe, xe, mod0, mod1, gain1, w_out0, w_in1).reshape(b, rows, D_WIDTH)


def _layer1_kernel(ya_ref, yb_ref, x_ref, ze_ref, mod0_ref, mod1_ref, g1_ref, wo0_ref, wi_ref,
                   vg_ref, ws_ref, bs_ref, cw_ref, wo1_ref, o_ref):
    d = x_ref.shape[-1]
    tb = x_ref.shape[1]
    j = pl.program_id(1)
    last_j = pl.num_programs(1) - 1
    mod1 = mod1_ref[pl.ds(pl.program_id(0), 1), :]
    gate0 = mod0_ref[pl.ds(pl.program_id(0), 1), 2 * d:]
    gate1 = mod1[:, 2 * d:]
    sub = tb // L1_SUB_BLOCKS
    subs = [slice(s * sub, (s + 1) * sub) for s in range(L1_SUB_BLOCKS)]
    col = lambda p, k: p[:, k * C_WIDTH:(k + 1) * C_WIDTH]
    n_gmlp = 3 * C_WIDTH

    x1s, xms = [], []
    for r in subs:
        x1, xm = _layer1_input(ya_ref[0, r, :], yb_ref[0, r, :], x_ref[0, r, :], gate0,
                               mod1, g1_ref[...], wo0_ref)
        x1s.append(x1)
        xms.append(xm)
    pgs = [jnp.dot(xm, wi_ref[:, 0:n_gmlp], preferred_element_type=F32) for xm in xms]
    pcs = [jnp.dot(xm, wi_ref[:, n_gmlp:], preferred_element_type=F32) for xm in xms]

    upd_c = []
    for pg in pgs:
        u = _gelu(col(pg, 0))
        vn = _rms(_gelu(col(pg, 1)), vg_ref[...]).astype(BF16)
        chunks = []
        for ci in range(sub // C_CHUNK):
            cr = slice(ci * C_CHUNK, (ci + 1) * C_CHUNK)
            groups = []
            for g in range(C_GROUPS):
                gl = slice(g * LANES, (g + 1) * LANES)
                groups.append(jnp.dot(ws_ref[g], vn[cr, gl], preferred_element_type=F32) + bs_ref[g])
            chunks.append(jnp.concatenate(groups, axis=1))
        o_c = u * jnp.concatenate(chunks, axis=0) * _silu(col(pg, 2))
        upd_c.append(jnp.dot(o_c.astype(BF16), wo1_ref[0:C_WIDTH, :], preferred_element_type=F32))

    z = jnp.concatenate([col(pc, 1) * col(pc, 2) for pc in pcs], axis=0)
    grp = 2 * SUBLANES
    prev_grp = ze_ref[0, pl.ds(pl.multiple_of(jnp.maximum(j - 1, 0) * grp + SUBLANES, SUBLANES),
                               SUBLANES), :]
    next_grp = ze_ref[0, pl.ds(pl.multiple_of(jnp.minimum(j + 1, last_j) * grp, SUBLANES),
                               SUBLANES), :]
    z_prev_row = jnp.where(j == 0, 0.0, prev_grp[SUBLANES - 1:, :])
    z_next_row = jnp.where(j == last_j, 0.0, next_grp[0:1, :])
    rowi = lax.broadcasted_iota(jnp.int32, z.shape, 0)
    z_prev = jnp.where(rowi == 0, z_prev_row, pltpu.roll(z, 1, 0))
    z_next = jnp.where(rowi == tb - 1, z_next_row, pltpu.roll(z, tb - 1, 0))
    conv = cw_ref[0:1, :] * z_prev + cw_ref[1:2, :] * z + cw_ref[2:3, :] * z_next

    for r, x1, pc, uc in zip(subs, x1s, pcs, upd_c):
        o_d = col(pc, 0) * conv[r, :] * _silu(col(pc, 3))
        upd = uc + jnp.dot(o_d.astype(BF16), wo1_ref[C_WIDTH:, :], preferred_element_type=F32)
        o_ref[0, r, :] = x1 + gate1 * upd


def _layer1(ya, yb, x, ze, mod0, mod1, gain1, w_out0, w_in1, v_gain, w_s, b_s, conv_w, w_out1, tb):
    b, n, d = x.shape
    half = ya.shape[-1]
    tok = lambda width: pl.BlockSpec((1, tb, width), lambda i, j: (i, j, 0))
    modspec = _const_spec(mod0.shape)
    return pl.pallas_call(
        _layer1_kernel,
        grid=(b, n // tb),
        in_specs=[tok(half), tok(half), tok(d),
                  pl.BlockSpec((1,) + ze.shape[1:], lambda i, j: (i, 0, 0)),
                  modspec, modspec, _const_spec((1, d)),
                  _const_spec(w_out0.shape), _const_spec(w_in1.shape),
                  _const_spec((1, C_WIDTH)),
                  _const_spec((C_GROUPS, C_CHUNK, C_CHUNK)),
                  _const_spec((C_GROUPS, C_CHUNK, LANES)),
                  _const_spec((3, D_WIDTH)),
                  _const_spec(w_out1.shape)],
        out_specs=tok(d),
        out_shape=jax.ShapeDtypeStruct((b, n, d), F32),
        compiler_params=_cparams("arbitrary", "arbitrary"),
        name="outproj_even_layer_odd",
    )(ya, yb, x, ze, mod0, mod1, gain1, w_out0, w_in1, v_gain, w_s, b_s, conv_w, w_out1)


def _rope_tables(n):
    rows_ = n // GRID_W
    row = np.repeat(np.arange(rows_, dtype=np.float64), GRID_W)
    col = np.tile(np.arange(GRID_W, dtype=np.float64), rows_)
    n_freq = A_HEAD_DIM // 4
    inv = ROPE_THETA ** (-np.arange(n_freq, dtype=np.float64) / n_freq)
    ang = np.concatenate([row[:, None] * inv, col[:, None] * inv], axis=-1)
    cos, sin = np.cos(ang), np.sin(ang)
    reps = QK_SLAB // A_HEAD_DIM
    return (jnp.asarray(np.tile(np.concatenate([cos, cos], axis=-1), (1, reps)), F32),
            jnp.asarray(np.tile(np.concatenate([-sin, sin], axis=-1), (1, reps)), F32))


def kernel(x, c, ctx, c_ctx, norm_gain, ada_w, ada_b, even_w_in, even_w_out, attn_qk_gain,
           attn_lambda, attn_subln_gain, hgrn_lb_logits, hgrn_norm_gain, odd_w_in, odd_w_out,
           gmlp_v_gain, gmlp_w_s, gmlp_b_s, conv_w):
    b, n, d = x.shape
    assert b + 1 <= COND_ROWS and n % 512 == 0 and ctx.shape[1] % HGRN_CHUNK == 0
    assert norm_gain.shape[0] == 2, "two-layer block: one even layer then one odd layer"

    cond = jnp.concatenate([c, c_ctx[None, :], jnp.zeros((COND_ROWS - b - 1, d), F32)], axis=0)
    mod0, mod1 = _adaln(cond, ada_w, ada_b)

    w_in0 = even_w_in[0].astype(BF16)
    gain0 = norm_gain[0].reshape(1, d)
    cos, sin_signed = _rope_tables(n)
    qk_gain = jnp.tile(attn_qk_gain[0], (1, QK_SLAB // A_HEAD_DIM))
    blk = np.arange(QK_SLAB) // A_HEAD_DIM
    bd = jnp.asarray(np.where(blk[:, None] == blk[None, :], 1.0 / A_HEAD_DIM, 0.0), BF16)
    q, k_lat, vt_lat, rest_x = _inproj(x, mod0, None, gain0, w_in0, cos, sin_signed, qk_gain, bd,
                                       TOKEN_BLOCK, True, True)
    w_ctx = jnp.concatenate([w_in0[:, g * A_WIDTH:(g + 1) * A_WIDTH] for g in CTX_GROUPS], axis=1)
    n_ctx = ctx.shape[1]
    k_ctx, vt_ctx, rest_c = _inproj(ctx, mod0, b, gain0, w_ctx, cos[:n_ctx], sin_signed[:n_ctx],
                                    qk_gain, bd, n_ctx, False, False)

    lam_init = 0.8 - 0.6 * math.exp(-0.3 * 0)
    ya = _attention(q, k_lat, k_ctx, vt_lat, vt_ctx, rest_x, qk_gain,
                    attn_subln_gain[0].reshape(1, LANES), attn_lambda[0], lam_init, ATTN_Q_BLOCK)
    yb = _hgrn(rest_x, rest_c, hgrn_lb_logits, hgrn_norm_gain[0].reshape(1, LANES),
               HGRN_HEADS_PER_STEP)

    gain1 = norm_gain[1].reshape(1, d)
    w_out0, w_in1 = even_w_out[0].astype(BF16), odd_w_in[0].astype(BF16)
    ze = _edge_z(ya, yb, x, mod0, mod1, gain1, w_out0, w_in1, TOKEN_BLOCK)
    b_s = jnp.broadcast_to(gmlp_b_s[0][:, :, None], (C_GROUPS, C_CHUNK, LANES))
    return _layer1(ya, yb, x, ze, mod0, mod1, gain1, w_out0, w_in1,
                   gmlp_v_gain[0].reshape(1, C_WIDTH), gmlp_w_s[0].astype(BF16), b_s, conv_w[0],
                   odd_w_out[0].astype(BF16), TOKEN_BLOCK)
```

```python
import functools
import math

import jax
import jax.numpy as jnp
import numpy as np
from jax import lax
from jax.experimental import pallas as pl
from jax.experimental.pallas import tpu as pltpu

F32 = jnp.float32
BF16 = jnp.bfloat16

EPS = 1e-6
GRID_W = 64
ROPE_THETA = 10000.0
A_HEADS = 4
A_HEAD_DIM = 64
A_WIDTH = 2 * A_HEADS * A_HEAD_DIM
B_HEADS = 4
B_DIM = 128
B_WIDTH = B_HEADS * B_DIM
C_GROUPS = 4
C_CHUNK = 128
C_WIDTH = 512
D_WIDTH = 512
EVEN_IN = 4 * A_WIDTH + 5 * B_WIDTH
ODD_IN = 3 * C_WIDTH + 4 * D_WIDTH

CTX_GROUPS = (1, 2, 5, 6, 7)
REST_COL_GATE_A, REST_COL_Q, REST_COL_I, REST_COL_FF, REST_COL_FB, REST_COL_GATE_B = range(6)
CTX_COL_I, CTX_COL_FF, CTX_COL_FB = range(3)
QK_SLAB = 256

LANES = 128
HGRN_CHUNK = 128
HGRN_HEADS_PER_STEP = 2
HGRN_UNROLL = 4
HGRN_DIAG = 8
SCORE_BOUND = 100.0
EXP2_CLAMP = 115.0
LOG2E = math.log2(math.e)
Q_SCALE = A_HEAD_DIM ** -0.5 * LOG2E
TOKEN_BLOCK = 512
ATTN_Q_BLOCK = 256
SUBLANES = 8
BF16_ROWS_PER_VREG = 16
L1_SUB_BLOCKS = 2
COND_ROWS = 16
ADALN_COL_BLOCK = 512
VMEM_LIMIT = 56 * 1024 * 1024


def _cparams(*sem):
    return pltpu.CompilerParams(dimension_semantics=sem, vmem_limit_bytes=VMEM_LIMIT)


def _const_spec(shape):
    nd = len(shape)
    return pl.BlockSpec(shape, lambda *_: (0,) * nd, pipeline_mode=pl.Buffered(1))


def _silu(t):
    return t * jax.nn.sigmoid(t)


def _gelu(t):
    return 0.5 * t * (1.0 + lax.erf(t * (1.0 / math.sqrt(2.0))))


def _rms(t, gain):
    ms = jnp.mean(t * t, axis=-1, keepdims=True)
    return t * lax.rsqrt(ms + EPS) * gain


def _adaln_kernel(cond_ref, w_ref, b_ref, *o_refs):
    a = _silu(cond_ref[...])
    for layer, o_ref in enumerate(o_refs):
        o_ref[...] = jnp.dot(a, w_ref[layer], preferred_element_type=F32) + b_ref[layer]


def _adaln(cond, ada_w, ada_b):
    depth, d, n3 = ada_w.shape
    tn = ADALN_COL_BLOCK
    out = pl.BlockSpec((COND_ROWS, tn), lambda j: (0, j))
    return pl.pallas_call(
        _adaln_kernel,
        grid=(n3 // tn,),
        in_specs=[
            pl.BlockSpec((COND_ROWS, d), lambda j: (0, 0)),
            pl.BlockSpec((depth, d, tn), lambda j: (0, 0, j)),
            pl.BlockSpec((depth, 1, tn), lambda j: (0, 0, j)),
        ],
        out_specs=[out] * depth,
        out_shape=[jax.ShapeDtypeStruct((COND_ROWS, n3), F32)] * depth,
        compiler_params=_cparams("arbitrary"),
        name="adaln",
    )(cond, ada_w, ada_b.reshape(depth, 1, n3))


def _modulate(x, gain, mod, d):
    shift = mod[:, 0:d]
    scale = mod[:, d:2 * d]
    return _rms(x, gain) * (1.0 + scale) + shift


def _rope(t, cos, sin_signed):
    lanes = t.shape[1]
    lane = lax.broadcasted_iota(jnp.int32, t.shape, 1)
    first = (lane % A_HEAD_DIM) < (A_HEAD_DIM // 2)
    partner = jnp.where(first,
                        pltpu.roll(t, lanes - A_HEAD_DIM // 2, 1),
                        pltpu.roll(t, A_HEAD_DIM // 2, 1))
    return t * cos + partner * sin_signed


def _inproj_kernel(x_ref, mod_ref, g_ref, wa_ref, wr1_ref, wr2_ref, cos_ref, sin_ref, qkg_ref, bd_ref,
                   *out_refs,
                   mod_row, has_q, rope):
    d = x_ref.shape[-1]
    row = pl.program_id(0) if mod_row is None else mod_row
    xm = _modulate(x_ref[0], g_ref[...], mod_ref[pl.ds(row, 1), :], d).astype(BF16)
    n_attn = (3 if has_q else 2) * A_WIDTH
    rest_ref = out_refs[-1]
    n_rest = rest_ref.shape[-1]
    slabs = [slice(half * QK_SLAB, (half + 1) * QK_SLAB) for half in range(A_WIDTH // QK_SLAB)]

    attn = jnp.dot(xm, wa_ref[...], preferred_element_type=F32)
    rest_ref[0, :, 0:n_rest // 2] = jnp.dot(
        xm, wr1_ref[...], preferred_element_type=F32).astype(rest_ref.dtype)
    groups = [attn[:, g * A_WIDTH:(g + 1) * A_WIDTH] for g in range(n_attn // A_WIDTH)]
    v = groups.pop()
    mean_sq = [[jnp.dot((t[:, sl] * t[:, sl]).astype(BF16), bd_ref[...], preferred_element_type=F32)
                for sl in slabs] for t in groups]
    pad = BF16_ROWS_PER_VREG
    ms_rows = sum(ms[0:pad, :] for per_group in mean_sq for ms in per_group)
    anchor = jnp.concatenate([ms_rows * 0.0] * (d // QK_SLAB), axis=1)
    xm_late = jnp.concatenate([(xm[0:pad, :].astype(F32) + anchor).astype(BF16), xm[pad:, :]], axis=0)
    rest_ref[0, :, n_rest // 2:] = jnp.dot(
        xm_late, wr2_ref[...], preferred_element_type=F32).astype(rest_ref.dtype)

    qk_refs = out_refs[:-2]
    vt_ref = out_refs[-2]
    first_gain = 0 if has_q else 1
    for gi, (t, o_ref) in enumerate(zip(groups, qk_refs)):
        gain = qkg_ref[first_gain + gi:first_gain + gi + 1, :]
        scale = Q_SCALE if (has_q and gi == 0) else 1.0
        for sl, ms in zip(slabs, mean_sq[gi]):
            tn = t[:, sl] * lax.rsqrt(ms + EPS) * gain
            if rope:
                tn = _rope(tn, cos_ref[...], sin_ref[...])
            o_ref[0, :, sl] = (tn * scale).astype(o_ref.dtype)
    for h in range(A_HEADS):
        sl = slice(h * LANES, (h + 1) * LANES)
        vt_ref[0, sl, :] = v[:, sl].T.astype(vt_ref.dtype)


def _inproj(x, mod, mod_row, gain, w, cos, sin_signed, qk_gain, bd, tb, has_q, rope):
    b, n, d = x.shape
    n_attn = (3 if has_q else 2) * A_WIDTH
    n_rest = w.shape[1] - n_attn
    cut = n_attn + n_rest // 2
    w_parts = (w[:, :n_attn], w[:, n_attn:cut], w[:, cut:])
    tok = lambda width: pl.BlockSpec((1, tb, width), lambda i, j: (i, j, 0))
    qk_out = [tok(A_WIDTH)] * (2 if has_q else 1)
    qk_shape = [jax.ShapeDtypeStruct((b, n, A_WIDTH), BF16)] * (2 if has_q else 1)
    return pl.pallas_call(
        functools.partial(_inproj_kernel, mod_row=mod_row, has_q=has_q, rope=rope),
        grid=(b, n // tb),
        in_specs=[
            tok(d),
            _const_spec(mod.shape),
            _const_spec((1, d)),
            *[_const_spec(part.shape) for part in w_parts],
            pl.BlockSpec((tb, QK_SLAB), lambda i, j: (j, 0)),
            pl.BlockSpec((tb, QK_SLAB), lambda i, j: (j, 0)),
            _const_spec(qk_gain.shape),
            _const_spec(bd.shape),
        ],
        out_specs=qk_out + [pl.BlockSpec((1, A_WIDTH, tb), lambda i, j: (i, 0, j)), tok(n_rest)],
        out_shape=qk_shape + [jax.ShapeDtypeStruct((b, A_WIDTH, n), BF16),
                              jax.ShapeDtypeStruct((b, n, n_rest), BF16)],
        compiler_params=_cparams("arbitrary", "arbitrary"),
        name="inproj_even",
    )(x, mod, gain, *w_parts, cos, sin_signed, qk_gain, bd)


def _attn_kernel(q_ref, kl_ref, kc_ref, vtl_ref, vtc_ref, g_ref, qkg_ref, subg_ref, lamp_ref,
                 o_ref, *, lam_init):
    heads = [slice(h * LANES, (h + 1) * LANES) for h in range(A_HEADS)]
    lane = lax.broadcasted_iota(jnp.int32, (1, LANES), 1)
    nt = (((1,), (1,)), ((), ()))

    lp = lamp_ref[...]
    lam = (jnp.exp(jnp.sum(lp[0:1] * lp[1:2], axis=-1, keepdims=True))
           - jnp.exp(jnp.sum(lp[2:3] * lp[3:4], axis=-1, keepdims=True)) + lam_init)
    score_bound = ((A_HEAD_DIM * Q_SCALE) * jnp.max(jnp.abs(qkg_ref[0:1, :]))
                   * jnp.max(jnp.abs(qkg_ref[1:2, :])))

    def scores(h):
        out = []
        for m in range(2):
            qm = jnp.where((lane // A_HEAD_DIM) == m, q_ref[0, :, heads[h]], 0).astype(BF16)
            out.append([lax.dot_general(k_ref[0, :, heads[h]], qm, nt, preferred_element_type=F32)
                        for k_ref in (kc_ref, kl_ref)])
        return out

    def run_heads(shift):
        s_next = scores(0)
        for h, sl in enumerate(heads):
            s_maps = s_next
            if h + 1 < A_HEADS:
                s_next = scores(h + 1)
            probs = []
            for s_parts in s_maps:
                if shift:
                    top = functools.reduce(jnp.maximum,
                                           [jnp.max(s, axis=0, keepdims=True) for s in s_parts])
                    s_parts = [s - top for s in s_parts]
                p_parts = [jnp.exp2(s) for s in s_parts]
                probs.append((p_parts, sum(jnp.sum(p, axis=0, keepdims=True) for p in p_parts)))
            (p0, l0), (p1, l1) = probs
            a0, a1 = 1.0 / l0, lam / l1
            ot = sum(jnp.dot(vt_ref[0, sl, :], (pa * a0 - pb * a1).astype(BF16),
                             preferred_element_type=F32)
                     for vt_ref, pa, pb in zip((vtc_ref, vtl_ref), p0, p1))
            ms = jnp.mean(ot * ot, axis=0, keepdims=True)
            on = (ot * lax.rsqrt(ms + EPS)).T * (subg_ref[...] * (1.0 - lam_init))
            o_ref[0, :, sl] = (on * _silu(g_ref[0, :, sl].astype(F32))).astype(o_ref.dtype)

    no_shift_ok = score_bound <= SCORE_BOUND
    pl.when(no_shift_ok)(functools.partial(run_heads, False))
    pl.when(jnp.logical_not(no_shift_ok))(functools.partial(run_heads, True))


def _attention(q, k_lat, k_ctx, vt_lat, vt_ctx, rest, qk_gain, subln_g, lam_p, lam_init, tq):
    b, n, w = q.shape
    n_ctx = k_ctx.shape[1]
    return pl.pallas_call(
        functools.partial(_attn_kernel, lam_init=lam_init),
        grid=(b, n // tq),
        in_specs=[
            pl.BlockSpec((1, tq, w), lambda i, j: (i, j, 0)),
            pl.BlockSpec((1, n, w), lambda i, j: (i, 0, 0)),
            pl.BlockSpec((1, n_ctx, w), lambda i, j: (i, 0, 0)),
            pl.BlockSpec((1, w, n), lambda i, j: (i, 0, 0)),
            pl.BlockSpec((1, w, n_ctx), lambda i, j: (i, 0, 0)),
            pl.BlockSpec((1, tq, w), lambda i, j: (i, j, REST_COL_GATE_A)),
            _const_spec(qk_gain.shape),
            _const_spec((1, LANES)),
            _const_spec((4, A_HEAD_DIM)),
        ],
        out_specs=pl.BlockSpec((1, tq, w), lambda i, j: (i, j, 0)),
        out_shape=jax.ShapeDtypeStruct((b, n, w), BF16),
        compiler_params=_cparams("arbitrary", "arbitrary"),
        name="diff_attn",
    )(q, k_lat, k_ctx, vt_lat, vt_ctx, rest, qk_gain, subln_g, lam_p)


def _split_bf16(t):
    hi = t.astype(BF16)
    return hi, (t - hi.astype(F32)).astype(BF16)


def _block_ref(g, block, row):
    c, w = g.shape
    g3 = g.reshape(c // block, block, w)
    return jnp.broadcast_to(g3[:, row:row + 1, :], g3.shape).reshape(c, w)


def _hgrn_tables(c):
    t = np.arange(c)[:, None]
    s = np.arange(c)[None, :]
    lvl = np.zeros((c, c), np.int32)
    lvl[(t // HGRN_DIAG == s // HGRN_DIAG) & (s <= t)] = 1
    b, k = HGRN_DIAG, 2
    while b < c:
        lvl[(t // b == s // b + 1) & ((s // b) % 2 == 0)] = k
        b, k = 2 * b, k + 1
    tri = (s <= t).astype(np.float32)
    return jnp.asarray(np.stack([tri, tri.T]), BF16), jnp.asarray(np.stack([lvl, lvl.T]))


def _hgrn_chunks(chains, tri_ref, lvl_ref, want_out):
    nt = (((1,), (1,)), ((), ()))
    tn = (((0,), (0,)), ((), ()))
    n = len(chains)
    c = chains[0][2].shape[0]

    kk, parts = [], []
    for (_, _, f_raw, lb, _, _) in chains:
        f = lb + (1.0 - lb) * jax.nn.sigmoid(f_raw)
        kk.append(1.0 - f)
        parts.append(_split_bf16(jnp.log(f) * LOG2E))
    cum = [sum(jnp.dot(tri_ref[ch[5]], p, preferred_element_type=F32) for p in parts[i])
           for i, ch in enumerate(chains)]
    edge = [cum[i][0:1, :] if ch[5] else cum[i][c - 1:c, :] for i, ch in enumerate(chains)]

    outs = [None] * n
    if want_out:
        a = []
        for i, (q, _, _, _, _, d) in enumerate(chains):
            ref = _block_ref(cum[i], HGRN_DIAG, HGRN_DIAG // 2)
            qd = (q * jnp.exp2(jnp.minimum(cum[i] - ref, EXP2_CLAMP))).astype(BF16)
            kd = (kk[i] * jnp.exp2(jnp.minimum(ref - cum[i], EXP2_CLAMP))).astype(BF16)
            a.append(jnp.where(lvl_ref[d] == 1,
                               lax.dot_general(qd, kd, nt, preferred_element_type=F32), 0.0))
        b, k = HGRN_DIAG, 2
        while b < c:
            for i, (q, _, _, _, _, d) in enumerate(chains):
                ref = _block_ref(cum[i], 2 * b, b if d else b - 1)
                decay = jnp.exp2(cum[i] - ref)
                ql = (q * decay).astype(BF16)
                kl = (kk[i] * (1.0 / decay)).astype(BF16)
                a[i] = jnp.where(lvl_ref[d] == k,
                                 lax.dot_general(ql, kl, nt, preferred_element_type=F32), a[i])
            b, k = 2 * b, k + 1
        for i, (q, v, _, _, st, _) in enumerate(chains):
            o = jnp.dot(a[i].astype(BF16), v, preferred_element_type=F32)
            outs[i] = o + lax.dot_general((q * jnp.exp2(cum[i])).astype(BF16), st.astype(BF16), nt,
                                          preferred_element_type=F32)

    sts = []
    for i, (_, v, _, _, st, _) in enumerate(chains):
        kg = (kk[i] * jnp.exp2(edge[i] - cum[i])).astype(BF16)
        upd = lax.dot_general(v, kg, tn, preferred_element_type=F32)
        sts.append(st * jnp.exp2(edge[i]) + upd)
    return outs, sts


def _hgrn_kernel(q_ref, i_ref, ff_ref, fb_ref, g_ref, ic_ref, ffc_ref, fbc_ref,
                 lbl_ref, ng_ref, tri_ref, lvl_ref, o_ref, acc_scr):
    c = HGRN_CHUNK
    heads = q_ref.shape[2] // LANES
    nc_lat = q_ref.shape[1] // c
    nc_ctx = ic_ref.shape[1] // c
    f_lat = (ff_ref, fb_ref)
    f_ctx = (ffc_ref, fbc_ref)

    def lower_bound(direction, sl):
        logits = [lbl_ref[direction, l, :, sl] for l in range(lbl_ref.shape[1])]
        top = functools.reduce(jnp.maximum, logits)
        e = [jnp.exp(t - top) for t in logits]
        return e[0] / sum(e)

    lanes = [slice(h * LANES, (h + 1) * LANES) for h in range(heads)]
    lbs = [[lower_bound(d, sl) for sl in lanes] for d in (0, 1)]

    def rows(i):
        return pl.ds(pl.multiple_of(i * c, c), c)

    def ctx_step(j, sts):
        chains = []
        for d in (0, 1):
            r = rows(nc_ctx - 1 - j if d else j)
            for h, sl in enumerate(lanes):
                chains.append((None, ic_ref[0, r, sl], f_ctx[d][0, r, sl].astype(F32),
                               lbs[d][h], sts[d * heads + h], d))
        return tuple(_hgrn_chunks(chains, tri_ref, lvl_ref, False)[1])

    def lat_step(j, sts, second_visit):
        chains, where = [], []
        for d in (0, 1):
            r = rows(nc_lat - 1 - j if d else j)
            for h, sl in enumerate(lanes):
                chains.append((q_ref[0, r, sl].astype(F32), i_ref[0, r, sl],
                               f_lat[d][0, r, sl].astype(F32), lbs[d][h], sts[d * heads + h], d))
                where.append((r, sl))
        outs, new = _hgrn_chunks(chains, tri_ref, lvl_ref, True)
        for o, (r, sl) in zip(outs, where):
            if second_visit:
                y = _rms(acc_scr[r, sl] + o, ng_ref[...]) * _silu(g_ref[0, r, sl].astype(F32))
                o_ref[0, r, sl] = y.astype(o_ref.dtype)
            else:
                acc_scr[r, sl] = o
        return tuple(new)

    sts = tuple(jnp.zeros((B_DIM, B_DIM), F32) for _ in range(2 * heads))
    sts = lax.fori_loop(0, nc_ctx, ctx_step, sts, unroll=True)
    sts = lax.fori_loop(0, nc_lat // 2, functools.partial(lat_step, second_visit=False), sts,
                        unroll=HGRN_UNROLL)
    lax.fori_loop(nc_lat // 2, nc_lat, functools.partial(lat_step, second_visit=True), sts,
                  unroll=HGRN_UNROLL)


def _hgrn(rest_x, rest_c, lb_logits, norm_g, heads_per_step):
    b, n, _ = rest_x.shape
    n_ctx = rest_c.shape[1]
    assert (n // HGRN_CHUNK) % 2 == 0 and B_HEADS % heads_per_step == 0
    w = heads_per_step * LANES
    steps = B_HEADS // heads_per_step

    def xs(group):
        return pl.BlockSpec((1, n, w), lambda i, h: (i, 0, group * steps + h))

    def cs(group):
        return pl.BlockSpec((1, n_ctx, w), lambda i, h: (i, 0, group * steps + h))

    n_layers = lb_logits.shape[1]
    tri, lvl = _hgrn_tables(HGRN_CHUNK)
    return pl.pallas_call(
        _hgrn_kernel,
        grid=(b, steps),
        in_specs=[
            xs(REST_COL_Q), xs(REST_COL_I), xs(REST_COL_FF), xs(REST_COL_FB), xs(REST_COL_GATE_B),
            cs(CTX_COL_I), cs(CTX_COL_FF), cs(CTX_COL_FB),
            pl.BlockSpec((2, n_layers, 1, w), lambda i, h: (0, 0, 0, h)),
            _const_spec((1, LANES)),
            _const_spec(tri.shape),
            _const_spec(lvl.shape),
        ],
        out_specs=pl.BlockSpec((1, n, w), lambda i, h: (i, 0, h)),
        out_shape=jax.ShapeDtypeStruct((b, n, B_WIDTH), BF16),
        scratch_shapes=[pltpu.VMEM((n, w), F32)],
        compiler_params=_cparams("arbitrary", "arbitrary"),
        name="hgrn2",
    )(rest_x, rest_x, rest_x, rest_x, rest_x, rest_c, rest_c, rest_c,
      lb_logits.reshape(2, n_layers, 1, B_WIDTH), norm_g, tri, lvl)


def _layer1_input(ya, yb, x, gate0, mod1, gain1, wo0_ref):
    d = x.shape[-1]
    half = ya.shape[-1]
    upd = (jnp.dot(ya, wo0_ref[0:half, :], preferred_element_type=F32)
           + jnp.dot(yb, wo0_ref[half:, :], preferred_element_type=F32))
    x1 = x + gate0 * upd
    return x1, _modulate(x1, gain1, mod1, d).astype(BF16)


def _edge_kernel(ya_ref, yb_ref, x_ref, mod0_ref, mod1_ref, g1_ref, wo0_ref, wi_ref, z_ref,
                 *, rows_per_batch):
    d = x_ref.shape[-1]

    def per_row(ref, lo, hi):
        return jnp.concatenate([jnp.broadcast_to(ref[i:i + 1, lo:hi], (rows_per_batch, hi - lo))
                                for i in range(x_ref.shape[0] // rows_per_batch)], axis=0)

    _, xm = _layer1_input(ya_ref[...], yb_ref[...], x_ref[...], per_row(mod0_ref, 2 * d, 3 * d),
                          per_row(mod1_ref, 0, 2 * d), g1_ref[...], wo0_ref)
    p = jnp.dot(xm, wi_ref[...], preferred_element_type=F32)
    z_ref[...] = p[:, :D_WIDTH] * p[:, D_WIDTH:]


def _block_edges(t, tb):
    b, n, w = t.shape
    te = t.reshape(b, n // tb, tb, w)
    return jnp.concatenate([te[:, :, :SUBLANES], te[:, :, tb - SUBLANES:]], axis=2).reshape(-1, w)


def _edge_z(ya, yb, x, mod0, mod1, gain1, w_out0, w_in1, tb):
    b, n, d = x.shape
    rows = (n // tb) * 2 * SUBLANES
    cg_start = 3 * C_WIDTH + D_WIDTH
    assert cg_start % (2 * D_WIDTH) == 0
    cg_blk = cg_start // (2 * D_WIDTH)
    full = lambda arr: _const_spec(arr.shape)
    xe, yae, ybe = _block_edges(x, tb), _block_edges(ya, tb), _block_edges(yb, tb)
    return pl.pallas_call(
        functools.partial(_edge_kernel, rows_per_batch=rows),
        grid=(1,),
        in_specs=[full(yae), full(ybe), full(xe), full(mod0), full(mod1), _const_spec((1, d)),
                  full(w_out0),
                  pl.BlockSpec((d, 2 * D_WIDTH), lambda i: (0, cg_blk), pipeline_mode=pl.Buffered(1))],
        out_specs=pl.BlockSpec((b * rows, D_WIDTH), lambda i: (0, 0)),
        out_shape=jax.ShapeDtypeStruct((b * rows, D_WIDTH), F32),
        compiler_params=_cparams("arbitrary"),
        name="conv_edge_rows",
    )(yae, ybe, xe, mod0, mod1, gain1, w_out0, w_in1).reshape(b, rows, D_WIDTH)


def _layer1_kernel(ya_ref, yb_ref, x_ref, ze_ref, mod0_ref, mod1_ref, g1_ref, wo0_ref, wi_ref,
                   vg_ref, ws_ref, bs_ref, cw_ref, wo1_ref, o_ref):
    d = x_ref.shape[-1]
    tb = x_ref.shape[1]
    j = pl.program_id(1)
    last_j = pl.num_programs(1) - 1
    mod1 = mod1_ref[pl.ds(pl.program_id(0), 1), :]
    gate0 = mod0_ref[pl.ds(pl.program_id(0), 1), 2 * d:]
    gate1 = mod1[:, 2 * d:]
    sub = tb // L1_SUB_BLOCKS
    subs = [slice(s * sub, (s + 1) * sub) for s in range(L1_SUB_BLOCKS)]
    col = lambda p, k: p[:, k * C_WIDTH:(k + 1) * C_WIDTH]
    n_gmlp = 3 * C_WIDTH

    x1s, xms = [], []
    for r in subs:
        x1, xm = _layer1_input(ya_ref[0, r, :], yb_ref[0, r, :], x_ref[0, r, :], gate0,
                               mod1, g1_ref[...], wo0_ref)
        x1s.append(x1)
        xms.append(xm)
    pgs = [jnp.dot(xm, wi_ref[:, 0:n_gmlp], preferred_element_type=F32) for xm in xms]
    pcs = [jnp.dot(xm, wi_ref[:, n_gmlp:], preferred_element_type=F32) for xm in xms]

    upd_c = []
    for pg in pgs:
        u = _gelu(col(pg, 0))
        vn = _rms(_gelu(col(pg, 1)), vg_ref[...]).astype(BF16)
        chunks = []
        for ci in range(sub // C_CHUNK):
            cr = slice(ci * C_CHUNK, (ci + 1) * C_CHUNK)
            groups = []
            for g in range(C_GROUPS):
                gl = slice(g * LANES, (g + 1) * LANES)
                groups.append(jnp.dot(ws_ref[g], vn[cr, gl], preferred_element_type=F32) + bs_ref[g])
            chunks.append(jnp.concatenate(groups, axis=1))
        o_c = u * jnp.concatenate(chunks, axis=0) * _silu(col(pg, 2))
        upd_c.append(jnp.dot(o_c.astype(BF16), wo1_ref[0:C_WIDTH, :], preferred_element_type=F32))

    z = jnp.concatenate([col(pc, 1) * col(pc, 2) for pc in pcs], axis=0)
    grp = 2 * SUBLANES
    prev_grp = ze_ref[0, pl.ds(pl.multiple_of(jnp.maximum(j - 1, 0) * grp + SUBLANES, SUBLANES),
                               SUBLANES), :]
    next_grp = ze_ref[0, pl.ds(pl.multiple_of(jnp.minimum(j + 1, last_j) * grp, SUBLANES),
                               SUBLANES), :]
    z_prev_row = jnp.where(j == 0, 0.0, prev_grp[SUBLANES - 1:, :])
    z_next_row = jnp.where(j == last_j, 0.0, next_grp[0:1, :])
    rowi = lax.broadcasted_iota(jnp.int32, z.shape, 0)
    z_prev = jnp.where(rowi == 0, z_prev_row, pltpu.roll(z, 1, 0))
    z_next = jnp.where(rowi == tb - 1, z_next_row, pltpu.roll(z, tb - 1, 0))
    conv = cw_ref[0:1, :] * z_prev + cw_ref[1:2, :] * z + cw_ref[2:3, :] * z_next

    for r, x1, pc, uc in zip(subs, x1s, pcs, upd_c):
        o_d = col(pc, 0) * conv[r, :] * _silu(col(pc, 3))
        upd = uc + jnp.dot(o_d.astype(BF16), wo1_ref[C_WIDTH:, :], preferred_element_type=F32)
        o_ref[0, r, :] = x1 + gate1 * upd


def _layer1(ya, yb, x, ze, mod0, mod1, gain1, w_out0, w_in1, v_gain, w_s, b_s, conv_w, w_out1, tb):
    b, n, d = x.shape
    half = ya.shape[-1]
    tok = lambda width: pl.BlockSpec((1, tb, width), lambda i, j: (i, j, 0))
    modspec = _const_spec(mod0.shape)
    return pl.pallas_call(
        _layer1_kernel,
        grid=(b, n // tb),
        in_specs=[tok(half), tok(half), tok(d),
                  pl.BlockSpec((1,) + ze.shape[1:], lambda i, j: (i, 0, 0)),
                  modspec, modspec, _const_spec((1, d)),
                  _const_spec(w_out0.shape), _const_spec(w_in1.shape),
                  _const_spec((1, C_WIDTH)),
                  _const_spec((C_GROUPS, C_CHUNK, C_CHUNK)),
                  _const_spec((C_GROUPS, C_CHUNK, LANES)),
                  _const_spec((3, D_WIDTH)),
                  _const_spec(w_out1.shape)],
        out_specs=tok(d),
        out_shape=jax.ShapeDtypeStruct((b, n, d), F32),
        compiler_params=_cparams("arbitrary", "arbitrary"),
        name="outproj_even_layer_odd",
    )(ya, yb, x, ze, mod0, mod1, gain1, w_out0, w_in1, v_gain, w_s, b_s, conv_w, w_out1)


def _rope_tables(n):
    rows_ = n // GRID_W
    row = np.repeat(np.arange(rows_, dtype=np.float64), GRID_W)
    col = np.tile(np.arange(GRID_W, dtype=np.float64), rows_)
    n_freq = A_HEAD_DIM // 4
    inv = ROPE_THETA ** (-np.arange(n_freq, dtype=np.float64) / n_freq)
    ang = np.concatenate([row[:, None] * inv, col[:, None] * inv], axis=-1)
    cos, sin = np.cos(ang), np.sin(ang)
    reps = QK_SLAB // A_HEAD_DIM
    return (jnp.asarray(np.tile(np.concatenate([cos, cos], axis=-1), (1, reps)), F32),
            jnp.asarray(np.tile(np.concatenate([-sin, sin], axis=-1), (1, reps)), F32))


def kernel(x, c, ctx, c_ctx, norm_gain, ada_w, ada_b, even_w_in, even_w_out, attn_qk_gain,
           attn_lambda, attn_subln_gain, hgrn_lb_logits, hgrn_norm_gain, odd_w_in, odd_w_out,
           gmlp_v_gain, gmlp_w_s, gmlp_b_s, conv_w):
    b, n, d = x.shape
    assert b + 1 <= COND_ROWS and n % 512 == 0 and ctx.shape[1] % HGRN_CHUNK == 0
    assert norm_gain.shape[0] == 2, "two-layer block: one even layer then one odd layer"

    cond = jnp.concatenate([c, c_ctx[None, :], jnp.zeros((COND_ROWS - b - 1, d), F32)], axis=0)
    mod0, mod1 = _adaln(cond, ada_w, ada_b)

    w_in0 = even_w_in[0].astype(BF16)
    gain0 = norm_gain[0].reshape(1, d)
    cos, sin_signed = _rope_tables(n)
    qk_gain = jnp.tile(attn_qk_gain[0], (1, QK_SLAB // A_HEAD_DIM))
    blk = np.arange(QK_SLAB) // A_HEAD_DIM
    bd = jnp.asarray(np.where(blk[:, None] == blk[None, :], 1.0 / A_HEAD_DIM, 0.0), BF16)
    q, k_lat, vt_lat, rest_x = _inproj(x, mod0, None, gain0, w_in0, cos, sin_signed, qk_gain, bd,
                                       TOKEN_BLOCK, True, True)
    w_ctx = jnp.concatenate([w_in0[:, g * A_WIDTH:(g + 1) * A_WIDTH] for g in CTX_GROUPS], axis=1)
    n_ctx = ctx.shape[1]
    k_ctx, vt_ctx, rest_c = _inproj(ctx, mod0, b, gain0, w_ctx, cos[:n_ctx], sin_signed[:n_ctx],
                                    qk_gain, bd, n_ctx, False, False)

    lam_init = 0.8 - 0.6 * math.exp(-0.3 * 0)
    ya = _attention(q, k_lat, k_ctx, vt_lat, vt_ctx, rest_x, qk_gain,
                    attn_subln_gain[0].reshape(1, LANES), attn_lambda[0], lam_init, ATTN_Q_BLOCK)
    yb = _hgrn(rest_x, rest_c, hgrn_lb_logits, hgrn_norm_gain[0].reshape(1, LANES),
               HGRN_HEADS_PER_STEP)

    gain1 = norm_gain[1].reshape(1, d)
    w_out0, w_in1 = even_w_out[0].astype(BF16), odd_w_in[0].astype(BF16)
    ze = _edge_z(ya, yb, x, mod0, mod1, gain1, w_out0, w_in1, TOKEN_BLOCK)
    b_s = jnp.broadcast_to(gmlp_b_s[0][:, :, None], (C_GROUPS, C_CHUNK, LANES))
    return _layer1(ya, yb, x, ze, mod0, mod1, gain1, w_out0, w_in1,
                   gmlp_v_gain[0].reshape(1, C_WIDTH), gmlp_w_s[0].astype(BF16), b_s, conv_w[0],
                   odd_w_out[0].astype(BF16), TOKEN_BLOCK)
```

```python
import functools
import math

import jax
import jax.numpy as jnp
import numpy as np
from jax import lax
from jax.experimental import pallas as pl
from jax.experimental.pallas import tpu as pltpu

F32 = jnp.float32
BF16 = jnp.bfloat16

EPS = 1e-6
GRID_W = 64
ROPE_THETA = 10000.0
A_HEADS = 4
A_HEAD_DIM = 64
A_WIDTH = 2 * A_HEADS * A_HEAD_DIM
B_HEADS = 4
B_DIM = 128
B_WIDTH = B_HEADS * B_DIM
C_GROUPS = 4
C_CHUNK = 128
C_WIDTH = 512
D_WIDTH = 512
EVEN_IN = 4 * A_WIDTH + 5 * B_WIDTH
ODD_IN = 3 * C_WIDTH + 4 * D_WIDTH

CTX_GROUPS = (1, 2, 5, 6, 7)
REST_COL_GATE_A, REST_COL_Q, REST_COL_I, REST_COL_FF, REST_COL_FB, REST_COL_GATE_B = range(6)
CTX_COL_I, CTX_COL_FF, CTX_COL_FB = range(3)
QK_SLAB = 256

LANES = 128
HGRN_CHUNK = 128
HGRN_HEADS_PER_STEP = 4
HGRN_UNROLL = 4
HGRN_DIAG = 8
SCORE_BOUND = 100.0
EXP2_CLAMP = 115.0
LOG2E = math.log2(math.e)
Q_SCALE = A_HEAD_DIM ** -0.5 * LOG2E
TOKEN_BLOCK = 512
ATTN_Q_BLOCK = 256
SUBLANES = 8
BF16_ROWS_PER_VREG = 16
L1_SUB_BLOCKS = 2
COND_ROWS = 16
ADALN_COL_BLOCK = 512
VMEM_LIMIT = 56 * 1024 * 1024


def _cparams(*sem):
    return pltpu.CompilerParams(dimension_semantics=sem, vmem_limit_bytes=VMEM_LIMIT)


def _const_spec(shape):
    nd = len(shape)
    return pl.BlockSpec(shape, lambda *_: (0,) * nd, pipeline_mode=pl.Buffered(1))


def _sigmoid(t):
    return 0.5 + 0.5 * jnp.tanh(0.5 * t)


def _silu(t):
    return t * _sigmoid(t)


def _gelu(t):
    return 0.5 * t * (1.0 + lax.erf(t * (1.0 / math.sqrt(2.0))))


def _rms(t, gain):
    ms = jnp.mean(t * t, axis=-1, keepdims=True)
    return t * lax.rsqrt(ms + EPS) * gain


def _adaln_kernel(cond_ref, w_ref, b_ref, *o_refs):
    a = _silu(cond_ref[...])
    for layer, o_ref in enumerate(o_refs):
        o_ref[...] = jnp.dot(a, w_ref[layer], preferred_element_type=F32) + b_ref[layer]


def _adaln(cond, ada_w, ada_b):
    depth, d, n3 = ada_w.shape
    tn = ADALN_COL_BLOCK
    out = pl.BlockSpec((COND_ROWS, tn), lambda j: (0, j))
    return pl.pallas_call(
        _adaln_kernel,
        grid=(n3 // tn,),
        in_specs=[
            pl.BlockSpec((COND_ROWS, d), lambda j: (0, 0)),
            pl.BlockSpec((depth, d, tn), lambda j: (0, 0, j)),
            pl.BlockSpec((depth, 1, tn), lambda j: (0, 0, j)),
        ],
        out_specs=[out] * depth,
        out_shape=[jax.ShapeDtypeStruct((COND_ROWS, n3), F32)] * depth,
        compiler_params=_cparams("arbitrary"),
        name="adaln",
    )(cond, ada_w, ada_b.reshape(depth, 1, n3))


def _modulate(x, gain, mod, d):
    shift = mod[:, 0:d]
    scale = mod[:, d:2 * d]
    return _rms(x, gain) * (1.0 + scale) + shift


def _rope(t, cos, sin_signed):
    lanes = t.shape[1]
    lane = lax.broadcasted_iota(jnp.int32, t.shape, 1)
    first = (lane % A_HEAD_DIM) < (A_HEAD_DIM // 2)
    partner = jnp.where(first,
                        pltpu.roll(t, lanes - A_HEAD_DIM // 2, 1),
                        pltpu.roll(t, A_HEAD_DIM // 2, 1))
    return t * cos + partner * sin_signed


def _inproj_kernel(x_ref, mod_ref, g_ref, wa_ref, wr1_ref, wr2_ref, cos_ref, sin_ref, qkg_ref, bd_ref,
                   *out_refs,
                   mod_row, has_q, rope):
    d = x_ref.shape[-1]
    row = pl.program_id(0) if mod_row is None else mod_row
    xm = _modulate(x_ref[0], g_ref[...], mod_ref[pl.ds(row, 1), :], d).astype(BF16)
    n_attn = (3 if has_q else 2) * A_WIDTH
    rest_ref = out_refs[-1]
    n_rest = rest_ref.shape[-1]
    slabs = [slice(half * QK_SLAB, (half + 1) * QK_SLAB) for half in range(A_WIDTH // QK_SLAB)]

    attn = jnp.dot(xm, wa_ref[...], preferred_element_type=F32)
    rest_ref[0, :, 0:n_rest // 2] = jnp.dot(
        xm, wr1_ref[...], preferred_element_type=F32).astype(rest_ref.dtype)
    groups = [attn[:, g * A_WIDTH:(g + 1) * A_WIDTH] for g in range(n_attn // A_WIDTH)]
    v = groups.pop()
    mean_sq = [[jnp.dot((t[:, sl] * t[:, sl]).astype(BF16), bd_ref[...], preferred_element_type=F32)
                for sl in slabs] for t in groups]
    pad = BF16_ROWS_PER_VREG
    ms_rows = sum(ms[0:pad, :] for per_group in mean_sq for ms in per_group)
    anchor = jnp.concatenate([ms_rows * 0.0] * (d // QK_SLAB), axis=1)
    xm_late = jnp.concatenate([(xm[0:pad, :].astype(F32) + anchor).astype(BF16), xm[pad:, :]], axis=0)
    rest_ref[0, :, n_rest // 2:] = jnp.dot(
        xm_late, wr2_ref[...], preferred_element_type=F32).astype(rest_ref.dtype)

    qk_refs = out_refs[:-2]
    vt_ref = out_refs[-2]
    first_gain = 0 if has_q else 1
    for gi, (t, o_ref) in enumerate(zip(groups, qk_refs)):
        gain = qkg_ref[first_gain + gi:first_gain + gi + 1, :]
        scale = Q_SCALE if (has_q and gi == 0) else 1.0
        for sl, ms in zip(slabs, mean_sq[gi]):
            tn = t[:, sl] * lax.rsqrt(ms + EPS) * gain
            if rope:
                tn = _rope(tn, cos_ref[...], sin_ref[...])
            o_ref[0, :, sl] = (tn * scale).astype(o_ref.dtype)
    for h in range(A_HEADS):
        sl = slice(h * LANES, (h + 1) * LANES)
        vt_ref[0, sl, :] = v[:, sl].T.astype(vt_ref.dtype)


def _inproj(x, mod, mod_row, gain, w, cos, sin_signed, qk_gain, bd, tb, has_q, rope):
    b, n, d = x.shape
    n_attn = (3 if has_q else 2) * A_WIDTH
    n_rest = w.shape[1] - n_attn
    cut = n_attn + n_rest // 2
    w_parts = (w[:, :n_attn], w[:, n_attn:cut], w[:, cut:])
    tok = lambda width: pl.BlockSpec((1, tb, width), lambda i, j: (i, j, 0))
    qk_out = [tok(A_WIDTH)] * (2 if has_q else 1)
    qk_shape = [jax.ShapeDtypeStruct((b, n, A_WIDTH), BF16)] * (2 if has_q else 1)
    return pl.pallas_call(
        functools.partial(_inproj_kernel, mod_row=mod_row, has_q=has_q, rope=rope),
        grid=(b, n // tb),
        in_specs=[
            tok(d),
            _const_spec(mod.shape),
            _const_spec((1, d)),
            *[_const_spec(part.shape) for part in w_parts],
            pl.BlockSpec((tb, QK_SLAB), lambda i, j: (j, 0)),
            pl.BlockSpec((tb, QK_SLAB), lambda i, j: (j, 0)),
            _const_spec(qk_gain.shape),
            _const_spec(bd.shape),
        ],
        out_specs=qk_out + [pl.BlockSpec((1, A_WIDTH, tb), lambda i, j: (i, 0, j)), tok(n_rest)],
        out_shape=qk_shape + [jax.ShapeDtypeStruct((b, A_WIDTH, n), BF16),
                              jax.ShapeDtypeStruct((b, n, n_rest), BF16)],
        compiler_params=_cparams("arbitrary", "arbitrary"),
        name="inproj_even",
    )(x, mod, gain, *w_parts, cos, sin_signed, qk_gain, bd)


def _attn_kernel(q_ref, kl_ref, kc_ref, vtl_ref, vtc_ref, g_ref, qkg_ref, subg_ref, lamp_ref,
                 o_ref, *, lam_init):
    heads = [slice(h * LANES, (h + 1) * LANES) for h in range(A_HEADS)]
    lane = lax.broadcasted_iota(jnp.int32, (1, LANES), 1)
    nt = (((1,), (1,)), ((), ()))

    lp = lamp_ref[...]
    lam = (jnp.exp(jnp.sum(lp[0:1] * lp[1:2], axis=-1, keepdims=True))
           - jnp.exp(jnp.sum(lp[2:3] * lp[3:4], axis=-1, keepdims=True)) + lam_init)
    score_bound = ((A_HEAD_DIM * Q_SCALE) * jnp.max(jnp.abs(qkg_ref[0:1, :]))
                   * jnp.max(jnp.abs(qkg_ref[1:2, :])))

    def scores(h):
        out = []
        for m in range(2):
            qm = jnp.where((lane // A_HEAD_DIM) == m, q_ref[0, :, heads[h]], 0).astype(BF16)
            out.append([lax.dot_general(k_ref[0, :, heads[h]], qm, nt, preferred_element_type=F32)
                        for k_ref in (kc_ref, kl_ref)])
        return out

    def run_heads(shift):
        s_next = scores(0)
        for h, sl in enumerate(heads):
            s_maps = s_next
            if h + 1 < A_HEADS:
                s_next = scores(h + 1)
            probs = []
            for s_parts in s_maps:
                if shift:
                    top = functools.reduce(jnp.maximum,
                                           [jnp.max(s, axis=0, keepdims=True) for s in s_parts])
                    s_parts = [s - top for s in s_parts]
                p_parts = [jnp.exp2(s) for s in s_parts]
                probs.append((p_parts, sum(jnp.sum(p, axis=0, keepdims=True) for p in p_parts)))
            (p0, l0), (p1, l1) = probs
            a0, a1 = 1.0 / l0, lam / l1
            ot = sum(jnp.dot(vt_ref[0, sl, :], (pa * a0 - pb * a1).astype(BF16),
                             preferred_element_type=F32)
                     for vt_ref, pa, pb in zip((vtc_ref, vtl_ref), p0, p1))
            ms = jnp.mean(ot * ot, axis=0, keepdims=True)
            on = (ot * lax.rsqrt(ms + EPS)).T * (subg_ref[...] * (1.0 - lam_init))
            o_ref[0, :, sl] = (on * _silu(g_ref[0, :, sl].astype(F32))).astype(o_ref.dtype)

    no_shift_ok = score_bound <= SCORE_BOUND
    pl.when(no_shift_ok)(functools.partial(run_heads, False))
    pl.when(jnp.logical_not(no_shift_ok))(functools.partial(run_heads, True))


def _attention(q, k_lat, k_ctx, vt_lat, vt_ctx, rest, qk_gain, subln_g, lam_p, lam_init, tq):
    b, n, w = q.shape
    n_ctx = k_ctx.shape[1]
    return pl.pallas_call(
        functools.partial(_attn_kernel, lam_init=lam_init),
        grid=(b, n // tq),
        in_specs=[
            pl.BlockSpec((1, tq, w), lambda i, j: (i, j, 0)),
            pl.BlockSpec((1, n, w), lambda i, j: (i, 0, 0)),
            pl.BlockSpec((1, n_ctx, w), lambda i, j: (i, 0, 0)),
            pl.BlockSpec((1, w, n), lambda i, j: (i, 0, 0)),
            pl.BlockSpec((1, w, n_ctx), lambda i, j: (i, 0, 0)),
            pl.BlockSpec((1, tq, w), lambda i, j: (i, j, REST_COL_GATE_A)),
            _const_spec(qk_gain.shape),
            _const_spec((1, LANES)),
            _const_spec((4, A_HEAD_DIM)),
        ],
        out_specs=pl.BlockSpec((1, tq, w), lambda i, j: (i, j, 0)),
        out_shape=jax.ShapeDtypeStruct((b, n, w), BF16),
        compiler_params=_cparams("arbitrary", "arbitrary"),
        name="diff_attn",
    )(q, k_lat, k_ctx, vt_lat, vt_ctx, rest, qk_gain, subln_g, lam_p)


def _split_bf16(t):
    hi = t.astype(BF16)
    return hi, (t - hi.astype(F32)).astype(BF16)


def _block_ref(g, block, row):
    c, w = g.shape
    g3 = g.reshape(c // block, block, w)
    return jnp.broadcast_to(g3[:, row:row + 1, :], g3.shape).reshape(c, w)


def _hgrn_tables(c):
    t = np.arange(c)[:, None]
    s = np.arange(c)[None, :]
    lvl = np.zeros((c, c), np.int32)
    lvl[(t // HGRN_DIAG == s // HGRN_DIAG) & (s <= t)] = 1
    b, k = HGRN_DIAG, 2
    while b < c:
        lvl[(t // b == s // b + 1) & ((s // b) % 2 == 0)] = k
        b, k = 2 * b, k + 1
    tri = (s <= t).astype(np.float32)
    return jnp.asarray(np.stack([tri, tri.T]), BF16), jnp.asarray(np.stack([lvl, lvl.T]))


def _hgrn_chunks(chains, tri_ref, lvl_ref, want_out):
    nt = (((1,), (1,)), ((), ()))
    tn = (((0,), (0,)), ((), ()))
    n = len(chains)
    c = chains[0][2].shape[0]

    kk, parts = [], []
    for (_, _, f_raw, lb, _, _) in chains:
        f = lb + (1.0 - lb) * _sigmoid(f_raw)
        kk.append(1.0 - f)
        parts.append(_split_bf16(jnp.log(f) * LOG2E))
    cum = [sum(jnp.dot(tri_ref[ch[5]], p, preferred_element_type=F32) for p in parts[i])
           for i, ch in enumerate(chains)]
    edge = [cum[i][0:1, :] if ch[5] else cum[i][c - 1:c, :] for i, ch in enumerate(chains)]

    outs = [None] * n
    if want_out:
        a = []
        for i, (q, _, _, _, _, d) in enumerate(chains):
            ref = _block_ref(cum[i], HGRN_DIAG, HGRN_DIAG // 2)
            qd = (q * jnp.exp2(jnp.minimum(cum[i] - ref, EXP2_CLAMP))).astype(BF16)
            kd = (kk[i] * jnp.exp2(jnp.minimum(ref - cum[i], EXP2_CLAMP))).astype(BF16)
            a.append(jnp.where(lvl_ref[d] == 1,
                               lax.dot_general(qd, kd, nt, preferred_element_type=F32), 0.0))
        b, k = HGRN_DIAG, 2
        while b < c:
            for i, (q, _, _, _, _, d) in enumerate(chains):
                ref = _block_ref(cum[i], 2 * b, b if d else b - 1)
                decay = jnp.exp2(cum[i] - ref)
                ql = (q * decay).astype(BF16)
                kl = (kk[i] * jnp.exp2(ref - cum[i])).astype(BF16)
                a[i] = jnp.where(lvl_ref[d] == k,
                                 lax.dot_general(ql, kl, nt, preferred_element_type=F32), a[i])
            b, k = 2 * b, k + 1
        for i, (q, v, _, _, st, _) in enumerate(chains):
            o = jnp.dot(a[i].astype(BF16), v, preferred_element_type=F32)
            outs[i] = o + lax.dot_general((q * jnp.exp2(cum[i])).astype(BF16), st.astype(BF16), nt,
                                          preferred_element_type=F32)

    sts = []
    for i, (_, v, _, _, st, _) in enumerate(chains):
        kg = (kk[i] * jnp.exp2(edge[i] - cum[i])).astype(BF16)
        upd = lax.dot_general(v, kg, tn, preferred_element_type=F32)
        sts.append(st * jnp.exp2(edge[i]) + upd)
    return outs, sts


def _hgrn_kernel(q_ref, i_ref, ff_ref, fb_ref, g_ref, ic_ref, ffc_ref, fbc_ref,
                 lbl_ref, ng_ref, tri_ref, lvl_ref, o_ref, acc_scr):
    c = HGRN_CHUNK
    heads = q_ref.shape[2] // LANES
    nc_lat = q_ref.shape[1] // c
    nc_ctx = ic_ref.shape[1] // c
    f_lat = (ff_ref, fb_ref)
    f_ctx = (ffc_ref, fbc_ref)

    def lower_bound(direction, sl):
        logits = [lbl_ref[direction, l, :, sl] for l in range(lbl_ref.shape[1])]
        top = functools.reduce(jnp.maximum, logits)
        e = [jnp.exp(t - top) for t in logits]
        return e[0] / sum(e)

    lanes = [slice(h * LANES, (h + 1) * LANES) for h in range(heads)]
    lbs = [[lower_bound(d, sl) for sl in lanes] for d in (0, 1)]

    def rows(i):
        return pl.ds(pl.multiple_of(i * c, c), c)

    def ctx_step(j, sts):
        chains = []
        for d in (0, 1):
            r = rows(nc_ctx - 1 - j if d else j)
            for h, sl in enumerate(lanes):
                chains.append((None, ic_ref[0, r, sl], f_ctx[d][0, r, sl].astype(F32),
                               lbs[d][h], sts[d * heads + h], d))
        return tuple(_hgrn_chunks(chains, tri_ref, lvl_ref, False)[1])

    def lat_step(j, sts, second_visit):
        chains, where = [], []
        for d in (0, 1):
            r = rows(nc_lat - 1 - j if d else j)
            for h, sl in enumerate(lanes):
                chains.append((q_ref[0, r, sl].astype(F32), i_ref[0, r, sl],
                               f_lat[d][0, r, sl].astype(F32), lbs[d][h], sts[d * heads + h], d))
                where.append((r, sl))
        outs, new = _hgrn_chunks(chains, tri_ref, lvl_ref, True)
        for o, (r, sl) in zip(outs, where):
            if second_visit:
                y = _rms(acc_scr[r, sl] + o, ng_ref[...]) * _silu(g_ref[0, r, sl].astype(F32))
                o_ref[0, r, sl] = y.astype(o_ref.dtype)
            else:
                acc_scr[r, sl] = o
        return tuple(new)

    sts = tuple(jnp.zeros((B_DIM, B_DIM), F32) for _ in range(2 * heads))
    sts = lax.fori_loop(0, nc_ctx, ctx_step, sts, unroll=True)
    sts = lax.fori_loop(0, nc_lat // 2, functools.partial(lat_step, second_visit=False), sts,
                        unroll=HGRN_UNROLL)
    lax.fori_loop(nc_lat // 2, nc_lat, functools.partial(lat_step, second_visit=True), sts,
                  unroll=HGRN_UNROLL)


def _hgrn(rest_x, rest_c, lb_logits, norm_g, heads_per_step):
    b, n, _ = rest_x.shape
    n_ctx = rest_c.shape[1]
    assert (n // HGRN_CHUNK) % 2 == 0 and B_HEADS % heads_per_step == 0
    w = heads_per_step * LANES
    steps = B_HEADS // heads_per_step

    def xs(group):
        return pl.BlockSpec((1, n, w), lambda i, h: (i, 0, group * steps + h))

    def cs(group):
        return pl.BlockSpec((1, n_ctx, w), lambda i, h: (i, 0, group * steps + h))

    n_layers = lb_logits.shape[1]
    tri, lvl = _hgrn_tables(HGRN_CHUNK)
    return pl.pallas_call(
        _hgrn_kernel,
        grid=(b, steps),
        in_specs=[
            xs(REST_COL_Q), xs(REST_COL_I), xs(REST_COL_FF), xs(REST_COL_FB), xs(REST_COL_GATE_B),
            cs(CTX_COL_I), cs(CTX_COL_FF), cs(CTX_COL_FB),
            pl.BlockSpec((2, n_layers, 1, w), lambda i, h: (0, 0, 0, h)),
            _const_spec((1, LANES)),
            _const_spec(tri.shape),
            _const_spec(lvl.shape),
        ],
        out_specs=pl.BlockSpec((1, n, w), lambda i, h: (i, 0, h)),
        out_shape=jax.ShapeDtypeStruct((b, n, B_WIDTH), BF16),
        scratch_shapes=[pltpu.VMEM((n, w), F32)],
        compiler_params=_cparams("arbitrary", "arbitrary"),
        name="hgrn2",
    )(rest_x, rest_x, rest_x, rest_x, rest_x, rest_c, rest_c, rest_c,
      lb_logits.reshape(2, n_layers, 1, B_WIDTH), norm_g, tri, lvl)


def _layer1_input(ya, yb, x, gate0, mod1, gain1, wo0_ref):
    d = x.shape[-1]
    half = ya.shape[-1]
    upd = (jnp.dot(ya, wo0_ref[0:half, :], preferred_element_type=F32)
           + jnp.dot(yb, wo0_ref[half:, :], preferred_element_type=F32))
    x1 = x + gate0 * upd
    return x1, _modulate(x1, gain1, mod1, d).astype(BF16)


def _edge_kernel(ya_ref, yb_ref, x_ref, mod0_ref, mod1_ref, g1_ref, wo0_ref, wi_ref, z_ref,
                 *, rows_per_batch):
    d = x_ref.shape[-1]

    def per_row(ref, lo, hi):
        return jnp.concatenate([jnp.broadcast_to(ref[i:i + 1, lo:hi], (rows_per_batch, hi - lo))
                                for i in range(x_ref.shape[0] // rows_per_batch)], axis=0)

    _, xm = _layer1_input(ya_ref[...], yb_ref[...], x_ref[...], per_row(mod0_ref, 2 * d, 3 * d),
                          per_row(mod1_ref, 0, 2 * d), g1_ref[...], wo0_ref)
    p = jnp.dot(xm, wi_ref[...], preferred_element_type=F32)
    z_ref[...] = p[:, :D_WIDTH] * p[:, D_WIDTH:]


def _block_edges(t, tb):
    b, n, w = t.shape
    te = t.reshape(b, n // tb, tb, w)
    return jnp.concatenate([te[:, :, :SUBLANES], te[:, :, tb - SUBLANES:]], axis=2).reshape(-1, w)


def _edge_z(ya, yb, x, mod0, mod1, gain1, w_out0, w_in1, tb):
    b, n, d = x.shape
    rows = (n // tb) * 2 * SUBLANES
    cg_start = 3 * C_WIDTH + D_WIDTH
    assert cg_start % (2 * D_WIDTH) == 0
    cg_blk = cg_start // (2 * D_WIDTH)
    full = lambda arr: _const_spec(arr.shape)
    xe, yae, ybe = _block_edges(x, tb), _block_edges(ya, tb), _block_edges(yb, tb)
    return pl.pallas_call(
        functools.partial(_edge_kernel, rows_per_batch=rows),
        grid=(1,),
        in_specs=[full(yae), full(ybe), full(xe), full(mod0), full(mod1), _const_spec((1, d)),
                  full(w_out0),
                  pl.BlockSpec((d, 2 * D_WIDTH), lambda i: (0, cg_blk), pipeline_mode=pl.Buffered(1))],
        out_specs=pl.BlockSpec((b * rows, D_WIDTH), lambda i: (0, 0)),
        out_shape=jax.ShapeDtypeStruct((b * rows, D_WIDTH), F32),
        compiler_params=_cparams("arbitrary"),
        name="conv_edge_rows",
    )(yae, ybe, xe, mod0, mod1, gain1, w_out0, w_in1).reshape(b, rows, D_WIDTH)


def _layer1_kernel(ya_ref, yb_ref, x_ref, ze_ref, mod0_ref, mod1_ref, g1_ref, wo0_ref, wi_ref,
                   vg_ref, ws_ref, bs_ref, cw_ref, wo1_ref, o_ref):
    d = x_ref.shape[-1]
    tb = x_ref.shape[1]
    j = pl.program_id(1)
    last_j = pl.num_programs(1) - 1
    mod1 = mod1_ref[pl.ds(pl.program_id(0), 1), :]
    gate0 = mod0_ref[pl.ds(pl.program_id(0), 1), 2 * d:]
    gate1 = mod1[:, 2 * d:]
    sub = tb // L1_SUB_BLOCKS
    subs = [slice(s * sub, (s + 1) * sub) for s in range(L1_SUB_BLOCKS)]
    col = lambda p, k: p[:, k * C_WIDTH:(k + 1) * C_WIDTH]
    n_gmlp = 3 * C_WIDTH

    x1s, xms = [], []
    for r in subs:
        x1, xm = _layer1_input(ya_ref[0, r, :], yb_ref[0, r, :], x_ref[0, r, :], gate0,
                               mod1, g1_ref[...], wo0_ref)
        x1s.append(x1)
        xms.append(xm)
    pgs = [jnp.dot(xm, wi_ref[:, 0:n_gmlp], preferred_element_type=F32) for xm in xms]
    pcs = [jnp.dot(xm, wi_ref[:, n_gmlp:], preferred_element_type=F32) for xm in xms]

    upd_c = []
    for pg in pgs:
        u = _gelu(col(pg, 0))
        vn = _rms(_gelu(col(pg, 1)), vg_ref[...]).astype(BF16)
        chunks = []
        for ci in range(sub // C_CHUNK):
            cr = slice(ci * C_CHUNK, (ci + 1) * C_CHUNK)
            groups = []
            for g in range(C_GROUPS):
                gl = slice(g * LANES, (g + 1) * LANES)
                groups.append(jnp.dot(ws_ref[g], vn[cr, gl], preferred_element_type=F32) + bs_ref[g])
            chunks.append(jnp.concatenate(groups, axis=1))
        o_c = u * jnp.concatenate(chunks, axis=0) * _silu(col(pg, 2))
        upd_c.append(jnp.dot(o_c.astype(BF16), wo1_ref[0:C_WIDTH, :], preferred_element_type=F32))

    z = jnp.concatenate([col(pc, 1) * col(pc, 2) for pc in pcs], axis=0)
    grp = 2 * SUBLANES
    prev_grp = ze_ref[0, pl.ds(pl.multiple_of(jnp.maximum(j - 1, 0) * grp + SUBLANES, SUBLANES),
                               SUBLANES), :]
    next_grp = ze_ref[0, pl.ds(pl.multiple_of(jnp.minimum(j + 1, last_j) * grp, SUBLANES),
                               SUBLANES), :]
    z_prev_row = jnp.where(j == 0, 0.0, prev_grp[SUBLANES - 1:, :])
    z_next_row = jnp.where(j == last_j, 0.0, next_grp[0:1, :])
    rowi = lax.broadcasted_iota(jnp.int32, z.shape, 0)
    z_prev = jnp.where(rowi == 0, z_prev_row, pltpu.roll(z, 1, 0))
    z_next = jnp.where(rowi == tb - 1, z_next_row, pltpu.roll(z, tb - 1, 0))
    conv = cw_ref[0:1, :] * z_prev + cw_ref[1:2, :] * z + cw_ref[2:3, :] * z_next

    for r, x1, pc, uc in zip(subs, x1s, pcs, upd_c):
        o_d = col(pc, 0) * conv[r, :] * _silu(col(pc, 3))
        upd = uc + jnp.dot(o_d.astype(BF16), wo1_ref[C_WIDTH:, :], preferred_element_type=F32)
        o_ref[0, r, :] = x1 + gate1 * upd


def _layer1(ya, yb, x, ze, mod0, mod1, gain1, w_out0, w_in1, v_gain, w_s, b_s, conv_w, w_out1, tb):
    b, n, d = x.shape
    half = ya.shape[-1]
    tok = lambda width: pl.BlockSpec((1, tb, width), lambda i, j: (i, j, 0))
    modspec = _const_spec(mod0.shape)
    return pl.pallas_call(
        _layer1_kernel,
        grid=(b, n // tb),
        in_specs=[tok(half), tok(half), tok(d),
                  pl.BlockSpec((1,) + ze.shape[1:], lambda i, j: (i, 0, 0)),
                  modspec, modspec, _const_spec((1, d)),
                  _const_spec(w_out0.shape), _const_spec(w_in1.shape),
                  _const_spec((1, C_WIDTH)),
                  _const_spec((C_GROUPS, C_CHUNK, C_CHUNK)),
                  _const_spec((C_GROUPS, C_CHUNK, LANES)),
                  _const_spec((3, D_WIDTH)),
                  _const_spec(w_out1.shape)],
        out_specs=tok(d),
        out_shape=jax.ShapeDtypeStruct((b, n, d), F32),
        compiler_params=_cparams("arbitrary", "arbitrary"),
        name="outproj_even_layer_odd",
    )(ya, yb, x, ze, mod0, mod1, gain1, w_out0, w_in1, v_gain, w_s, b_s, conv_w, w_out1)


def _rope_tables(n):
    rows_ = n // GRID_W
    row = np.repeat(np.arange(rows_, dtype=np.float64), GRID_W)
    col = np.tile(np.arange(GRID_W, dtype=np.float64), rows_)
    n_freq = A_HEAD_DIM // 4
    inv = ROPE_THETA ** (-np.arange(n_freq, dtype=np.float64) / n_freq)
    ang = np.concatenate([row[:, None] * inv, col[:, None] * inv], axis=-1)
    cos, sin = np.cos(ang), np.sin(ang)
    reps = QK_SLAB // A_HEAD_DIM
    return (jnp.asarray(np.tile(np.concatenate([cos, cos], axis=-1), (1, reps)), F32),
            jnp.asarray(np.tile(np.concatenate([-sin, sin], axis=-1), (1, reps)), F32))


def kernel(x, c, ctx, c_ctx, norm_gain, ada_w, ada_b, even_w_in, even_w_out, attn_qk_gain,
           attn_lambda, attn_subln_gain, hgrn_lb_logits, hgrn_norm_gain, odd_w_in, odd_w_out,
           gmlp_v_gain, gmlp_w_s, gmlp_b_s, conv_w):
    b, n, d = x.shape
    assert b + 1 <= COND_ROWS and n % 512 == 0 and ctx.shape[1] % HGRN_CHUNK == 0
    assert norm_gain.shape[0] == 2, "two-layer block: one even layer then one odd layer"

    cond = jnp.concatenate([c, c_ctx[None, :], jnp.zeros((COND_ROWS - b - 1, d), F32)], axis=0)
    mod0, mod1 = _adaln(cond, ada_w, ada_b)

    w_in0 = even_w_in[0].astype(BF16)
    gain0 = norm_gain[0].reshape(1, d)
    cos, sin_signed = _rope_tables(n)
    qk_gain = jnp.tile(attn_qk_gain[0], (1, QK_SLAB // A_HEAD_DIM))
    blk = np.arange(QK_SLAB) // A_HEAD_DIM
    bd = jnp.asarray(np.where(blk[:, None] == blk[None, :], 1.0 / A_HEAD_DIM, 0.0), BF16)
    q, k_lat, vt_lat, rest_x = _inproj(x, mod0, None, gain0, w_in0, cos, sin_signed, qk_gain, bd,
                                       TOKEN_BLOCK, True, True)
    w_ctx = jnp.concatenate([w_in0[:, g * A_WIDTH:(g + 1) * A_WIDTH] for g in CTX_GROUPS], axis=1)
    n_ctx = ctx.shape[1]
    k_ctx, vt_ctx, rest_c = _inproj(ctx, mod0, b, gain0, w_ctx, cos[:n_ctx], sin_signed[:n_ctx],
                                    qk_gain, bd, n_ctx, False, False)

    lam_init = 0.8 - 0.6 * math.exp(-0.3 * 0)
    ya = _attention(q, k_lat, k_ctx, vt_lat, vt_ctx, rest_x, qk_gain,
                    attn_subln_gain[0].reshape(1, LANES), attn_lambda[0], lam_init, ATTN_Q_BLOCK)
    yb = _hgrn(rest_x, rest_c, hgrn_lb_logits, hgrn_norm_gain[0].reshape(1, LANES),
               HGRN_HEADS_PER_STEP)

    gain1 = norm_gain[1].reshape(1, d)
    w_out0, w_in1 = even_w_out[0].astype(BF16), odd_w_in[0].astype(BF16)
    ze = _edge_z(ya, yb, x, mod0, mod1, gain1, w_out0, w_in1, TOKEN_BLOCK)
    b_s = jnp.broadcast_to(gmlp_b_s[0][:, :, None], (C_GROUPS, C_CHUNK, LANES))
    return _layer1(ya, yb, x, ze, mod0, mod1, gain1, w_out0, w_in1,
                   gmlp_v_gain[0].reshape(1, C_WIDTH), gmlp_w_s[0].astype(BF16), b_s, conv_w[0],
                   odd_w_out[0].astype(BF16), TOKEN_BLOCK)
```

```python
import functools
import math

import jax
import jax.numpy as jnp
import numpy as np
from jax import lax
from jax.experimental import pallas as pl
from jax.experimental.pallas import tpu as pltpu

F32 = jnp.float32
BF16 = jnp.bfloat16

EPS = 1e-6
GRID_W = 64
ROPE_THETA = 10000.0
A_HEADS = 4
A_HEAD_DIM = 64
A_WIDTH = 2 * A_HEADS * A_HEAD_DIM
B_HEADS = 4
B_DIM = 128
B_WIDTH = B_HEADS * B_DIM
C_GROUPS = 4
C_CHUNK = 128
C_WIDTH = 512
D_WIDTH = 512
EVEN_IN = 4 * A_WIDTH + 5 * B_WIDTH
ODD_IN = 3 * C_WIDTH + 4 * D_WIDTH

CTX_GROUPS = (1, 2, 5, 6, 7)
REST_COL_GATE_A, REST_COL_Q, REST_COL_I, REST_COL_FF, REST_COL_FB, REST_COL_GATE_B = range(6)
CTX_COL_I, CTX_COL_FF, CTX_COL_FB = range(3)
QK_SLAB = 256

LANES = 128
HGRN_CHUNK = 128
HGRN_HEADS_PER_STEP = 4
HGRN_UNROLL = 4
HGRN_DIAG = 8
SCORE_BOUND = 100.0
EXP2_CLAMP = 115.0
LOG2E = math.log2(math.e)
Q_SCALE = A_HEAD_DIM ** -0.5 * LOG2E
TOKEN_BLOCK = 512
ATTN_Q_TILE = 256
ATTN_Q_BLOCK = 512
SUBLANES = 8
BF16_ROWS_PER_VREG = 16
L1_SUB_BLOCKS = 2
COND_ROWS = 16
ADALN_COL_BLOCK = 512
VMEM_LIMIT = 56 * 1024 * 1024


def _cparams(*sem):
    return pltpu.CompilerParams(dimension_semantics=sem, vmem_limit_bytes=VMEM_LIMIT)


def _const_spec(shape):
    nd = len(shape)
    return pl.BlockSpec(shape, lambda *_: (0,) * nd, pipeline_mode=pl.Buffered(1))


def _sigmoid(t):
    return 0.5 + 0.5 * jnp.tanh(0.5 * t)


def _silu(t):
    return t * _sigmoid(t)


def _gelu(t):
    return 0.5 * t * (1.0 + lax.erf(t * (1.0 / math.sqrt(2.0))))


def _rms(t, gain):
    ms = jnp.mean(t * t, axis=-1, keepdims=True)
    return t * lax.rsqrt(ms + EPS) * gain


def _adaln_kernel(cond_ref, w_ref, b_ref, *o_refs):
    a = _silu(cond_ref[...])
    for layer, o_ref in enumerate(o_refs):
        o_ref[...] = jnp.dot(a, w_ref[layer], preferred_element_type=F32) + b_ref[layer]


def _adaln(cond, ada_w, ada_b):
    depth, d, n3 = ada_w.shape
    tn = ADALN_COL_BLOCK
    out = pl.BlockSpec((COND_ROWS, tn), lambda j: (0, j))
    return pl.pallas_call(
        _adaln_kernel,
        grid=(n3 // tn,),
        in_specs=[
            pl.BlockSpec((COND_ROWS, d), lambda j: (0, 0)),
            pl.BlockSpec((depth, d, tn), lambda j: (0, 0, j)),
            pl.BlockSpec((depth, 1, tn), lambda j: (0, 0, j)),
        ],
        out_specs=[out] * depth,
        out_shape=[jax.ShapeDtypeStruct((COND_ROWS, n3), F32)] * depth,
        compiler_params=_cparams("arbitrary"),
        name="adaln",
    )(cond, ada_w, ada_b.reshape(depth, 1, n3))


def _modulate(x, gain, mod, d):
    shift = mod[:, 0:d]
    scale = mod[:, d:2 * d]
    return _rms(x, gain) * (1.0 + scale) + shift


def _rope(t, cos, sin_signed):
    lanes = t.shape[1]
    lane = lax.broadcasted_iota(jnp.int32, t.shape, 1)
    first = (lane % A_HEAD_DIM) < (A_HEAD_DIM // 2)
    partner = jnp.where(first,
                        pltpu.roll(t, lanes - A_HEAD_DIM // 2, 1),
                        pltpu.roll(t, A_HEAD_DIM // 2, 1))
    return t * cos + partner * sin_signed


def _inproj_kernel(x_ref, mod_ref, g_ref, wa_ref, wr1_ref, wr2_ref, cos_ref, sin_ref, qkg_ref, bd_ref,
                   *out_refs,
                   mod_row, has_q, rope):
    d = x_ref.shape[-1]
    row = pl.program_id(0) if mod_row is None else mod_row
    xm = _modulate(x_ref[0], g_ref[...], mod_ref[pl.ds(row, 1), :], d).astype(BF16)
    n_attn = (3 if has_q else 2) * A_WIDTH
    rest_ref = out_refs[-1]
    n_rest = rest_ref.shape[-1]
    slabs = [slice(half * QK_SLAB, (half + 1) * QK_SLAB) for half in range(A_WIDTH // QK_SLAB)]

    attn = jnp.dot(xm, wa_ref[...], preferred_element_type=F32)
    rest_ref[0, :, 0:n_rest // 2] = jnp.dot(
        xm, wr1_ref[...], preferred_element_type=F32).astype(rest_ref.dtype)
    groups = [attn[:, g * A_WIDTH:(g + 1) * A_WIDTH] for g in range(n_attn // A_WIDTH)]
    v = groups.pop()
    mean_sq = [[jnp.dot((t[:, sl] * t[:, sl]).astype(BF16), bd_ref[...], preferred_element_type=F32)
                for sl in slabs] for t in groups]
    pad = BF16_ROWS_PER_VREG
    ms_rows = sum(ms[0:pad, :] for per_group in mean_sq for ms in per_group)
    anchor = jnp.concatenate([ms_rows * 0.0] * (d // QK_SLAB), axis=1)
    xm_late = jnp.concatenate([(xm[0:pad, :].astype(F32) + anchor).astype(BF16), xm[pad:, :]], axis=0)
    rest_ref[0, :, n_rest // 2:] = jnp.dot(
        xm_late, wr2_ref[...], preferred_element_type=F32).astype(rest_ref.dtype)

    qk_refs = out_refs[:-2]
    vt_ref = out_refs[-2]
    first_gain = 0 if has_q else 1
    for gi, (t, o_ref) in enumerate(zip(groups, qk_refs)):
        gain = qkg_ref[first_gain + gi:first_gain + gi + 1, :]
        scale = Q_SCALE if (has_q and gi == 0) else 1.0
        for sl, ms in zip(slabs, mean_sq[gi]):
            tn = t[:, sl] * lax.rsqrt(ms + EPS) * gain
            if rope:
                tn = _rope(tn, cos_ref[...], sin_ref[...])
            o_ref[0, :, sl] = (tn * scale).astype(o_ref.dtype)
    for h in range(A_HEADS):
        sl = slice(h * LANES, (h + 1) * LANES)
        vt_ref[0, sl, :] = v[:, sl].T.astype(vt_ref.dtype)


def _inproj(x, mod, mod_row, gain, w, cos, sin_signed, qk_gain, bd, tb, has_q, rope):
    b, n, d = x.shape
    n_attn = (3 if has_q else 2) * A_WIDTH
    n_rest = w.shape[1] - n_attn
    cut = n_attn + n_rest // 2
    w_parts = (w[:, :n_attn], w[:, n_attn:cut], w[:, cut:])
    tok = lambda width: pl.BlockSpec((1, tb, width), lambda i, j: (i, j, 0))
    qk_out = [tok(A_WIDTH)] * (2 if has_q else 1)
    qk_shape = [jax.ShapeDtypeStruct((b, n, A_WIDTH), BF16)] * (2 if has_q else 1)
    return pl.pallas_call(
        functools.partial(_inproj_kernel, mod_row=mod_row, has_q=has_q, rope=rope),
        grid=(b, n // tb),
        in_specs=[
            tok(d),
            _const_spec(mod.shape),
            _const_spec((1, d)),
            *[_const_spec(part.shape) for part in w_parts],
            pl.BlockSpec((tb, QK_SLAB), lambda i, j: (j, 0)),
            pl.BlockSpec((tb, QK_SLAB), lambda i, j: (j, 0)),
            _const_spec(qk_gain.shape),
            _const_spec(bd.shape),
        ],
        out_specs=qk_out + [pl.BlockSpec((1, A_WIDTH, tb), lambda i, j: (i, 0, j)), tok(n_rest)],
        out_shape=qk_shape + [jax.ShapeDtypeStruct((b, A_WIDTH, n), BF16),
                              jax.ShapeDtypeStruct((b, n, n_rest), BF16)],
        compiler_params=_cparams("arbitrary", "arbitrary"),
        name="inproj_even",
    )(x, mod, gain, *w_parts, cos, sin_signed, qk_gain, bd)


def _attn_kernel(q_ref, kl_ref, kc_ref, vtl_ref, vtc_ref, g_ref, qkg_ref, subg_ref, lamp_ref,
                 o_ref, *, lam_init):
    heads = [slice(h * LANES, (h + 1) * LANES) for h in range(A_HEADS)]
    lane = lax.broadcasted_iota(jnp.int32, (1, LANES), 1)
    nt = (((1,), (1,)), ((), ()))

    lp = lamp_ref[...]
    lam = (jnp.exp(jnp.sum(lp[0:1] * lp[1:2], axis=-1, keepdims=True))
           - jnp.exp(jnp.sum(lp[2:3] * lp[3:4], axis=-1, keepdims=True)) + lam_init)
    score_bound = ((A_HEAD_DIM * Q_SCALE) * jnp.max(jnp.abs(qkg_ref[0:1, :]))
                   * jnp.max(jnp.abs(qkg_ref[1:2, :])))

    n_sub = q_ref.shape[1] // ATTN_Q_TILE
    items = [(slice(qi * ATTN_Q_TILE, (qi + 1) * ATTN_Q_TILE), h)
             for qi in range(n_sub) for h in range(A_HEADS)]

    def scores(item):
        rows, h = item
        out = []
        for m in range(2):
            qm = jnp.where((lane // A_HEAD_DIM) == m, q_ref[0, rows, heads[h]], 0).astype(BF16)
            out.append([lax.dot_general(k_ref[0, :, heads[h]], qm, nt, preferred_element_type=F32)
                        for k_ref in (kc_ref, kl_ref)])
        return out

    def run_heads(shift):
        s_next = scores(items[0])
        for idx, (rows, h) in enumerate(items):
            sl = heads[h]
            s_maps = s_next
            if idx + 1 < len(items):
                s_next = scores(items[idx + 1])
            probs = []
            for s_parts in s_maps:
                if shift:
                    top = functools.reduce(jnp.maximum,
                                           [jnp.max(s, axis=0, keepdims=True) for s in s_parts])
                    s_parts = [s - top for s in s_parts]
                p_parts = [jnp.exp2(s) for s in s_parts]
                probs.append((p_parts, sum(jnp.sum(p, axis=0, keepdims=True) for p in p_parts)))
            (p0, l0), (p1, l1) = probs
            a0, a1 = 1.0 / l0, lam / l1
            ot = sum(jnp.dot(vt_ref[0, sl, :], (pa * a0 - pb * a1).astype(BF16),
                             preferred_element_type=F32)
                     for vt_ref, pa, pb in zip((vtc_ref, vtl_ref), p0, p1))
            ms = jnp.mean(ot * ot, axis=0, keepdims=True)
            on = (ot * lax.rsqrt(ms + EPS)).T * (subg_ref[...] * (1.0 - lam_init))
            o_ref[0, rows, sl] = (on * _silu(g_ref[0, rows, sl].astype(F32))).astype(o_ref.dtype)

    no_shift_ok = score_bound <= SCORE_BOUND
    pl.when(no_shift_ok)(functools.partial(run_heads, False))
    pl.when(jnp.logical_not(no_shift_ok))(functools.partial(run_heads, True))


def _attention(q, k_lat, k_ctx, vt_lat, vt_ctx, rest, qk_gain, subln_g, lam_p, lam_init, tq):
    b, n, w = q.shape
    n_ctx = k_ctx.shape[1]
    return pl.pallas_call(
        functools.partial(_attn_kernel, lam_init=lam_init),
        grid=(b, n // tq),
        in_specs=[
            pl.BlockSpec((1, tq, w), lambda i, j: (i, j, 0)),
            pl.BlockSpec((1, n, w), lambda i, j: (i, 0, 0)),
            pl.BlockSpec((1, n_ctx, w), lambda i, j: (i, 0, 0)),
            pl.BlockSpec((1, w, n), lambda i, j: (i, 0, 0)),
            pl.BlockSpec((1, w, n_ctx), lambda i, j: (i, 0, 0)),
            pl.BlockSpec((1, tq, w), lambda i, j: (i, j, REST_COL_GATE_A)),
            _const_spec(qk_gain.shape),
            _const_spec((1, LANES)),
            _const_spec((4, A_HEAD_DIM)),
        ],
        out_specs=pl.BlockSpec((1, tq, w), lambda i, j: (i, j, 0)),
        out_shape=jax.ShapeDtypeStruct((b, n, w), BF16),
        compiler_params=_cparams("arbitrary", "arbitrary"),
        name="diff_attn",
    )(q, k_lat, k_ctx, vt_lat, vt_ctx, rest, qk_gain, subln_g, lam_p)


def _split_bf16(t):
    hi = t.astype(BF16)
    return hi, (t - hi.astype(F32)).astype(BF16)


def _block_ref(g, block, row):
    c, w = g.shape
    g3 = g.reshape(c // block, block, w)
    return jnp.broadcast_to(g3[:, row:row + 1, :], g3.shape).reshape(c, w)


def _hgrn_tables(c):
    t = np.arange(c)[:, None]
    s = np.arange(c)[None, :]
    lvl = np.zeros((c, c), np.int32)
    lvl[(t // HGRN_DIAG == s // HGRN_DIAG) & (s <= t)] = 1
    b, k = HGRN_DIAG, 2
    while b < c:
        lvl[(t // b == s // b + 1) & ((s // b) % 2 == 0)] = k
        b, k = 2 * b, k + 1
    tri = (s <= t).astype(np.float32)
    return jnp.asarray(np.stack([tri, tri.T]), BF16), jnp.asarray(np.stack([lvl, lvl.T]))


def _hgrn_chunks(chains, tri_ref, lvl_ref, want_out):
    nt = (((1,), (1,)), ((), ()))
    tn = (((0,), (0,)), ((), ()))
    n = len(chains)
    c = chains[0][2].shape[0]

    kk, parts = [], []
    for (_, _, f_raw, lb, _, _) in chains:
        f = lb + (1.0 - lb) * _sigmoid(f_raw)
        kk.append(1.0 - f)
        parts.append(_split_bf16(jnp.log(f) * LOG2E))
    cum = [sum(jnp.dot(tri_ref[ch[5]], p, preferred_element_type=F32) for p in parts[i])
           for i, ch in enumerate(chains)]
    edge = [cum[i][0:1, :] if ch[5] else cum[i][c - 1:c, :] for i, ch in enumerate(chains)]

    outs = [None] * n
    if want_out:
        a = []
        for i, (q, _, _, _, _, d) in enumerate(chains):
            ref = _block_ref(cum[i], HGRN_DIAG, HGRN_DIAG // 2)
            qd = (q * jnp.exp2(jnp.minimum(cum[i] - ref, EXP2_CLAMP))).astype(BF16)
            kd = (kk[i] * jnp.exp2(jnp.minimum(ref - cum[i], EXP2_CLAMP))).astype(BF16)
            a.append(jnp.where(lvl_ref[d] == 1,
                               lax.dot_general(qd, kd, nt, preferred_element_type=F32), 0.0))
        b, k = HGRN_DIAG, 2
        while b < c:
            for i, (q, _, _, _, _, d) in enumerate(chains):
                ref = _block_ref(cum[i], 2 * b, b if d else b - 1)
                decay = jnp.exp2(cum[i] - ref)
                ql = (q * decay).astype(BF16)
                kl = (kk[i] * jnp.exp2(ref - cum[i])).astype(BF16)
                a[i] = jnp.where(lvl_ref[d] == k,
                                 lax.dot_general(ql, kl, nt, preferred_element_type=F32), a[i])
            b, k = 2 * b, k + 1
        for i, (q, v, _, _, st, _) in enumerate(chains):
            o = jnp.dot(a[i].astype(BF16), v, preferred_element_type=F32)
            outs[i] = o + lax.dot_general((q * jnp.exp2(cum[i])).astype(BF16), st.astype(BF16), nt,
                                          preferred_element_type=F32)

    sts = []
    for i, (_, v, _, _, st, _) in enumerate(chains):
        kg = (kk[i] * jnp.exp2(edge[i] - cum[i])).astype(BF16)
        upd = lax.dot_general(v, kg, tn, preferred_element_type=F32)
        sts.append(st * jnp.exp2(edge[i]) + upd)
    return outs, sts


def _hgrn_kernel(q_ref, i_ref, ff_ref, fb_ref, g_ref, ic_ref, ffc_ref, fbc_ref,
                 lbl_ref, ng_ref, tri_ref, lvl_ref, o_ref, acc_scr):
    c = HGRN_CHUNK
    heads = q_ref.shape[2] // LANES
    nc_lat = q_ref.shape[1] // c
    nc_ctx = ic_ref.shape[1] // c
    f_lat = (ff_ref, fb_ref)
    f_ctx = (ffc_ref, fbc_ref)

    def lower_bound(direction, sl):
        logits = [lbl_ref[direction, l, :, sl] for l in range(lbl_ref.shape[1])]
        top = functools.reduce(jnp.maximum, logits)
        e = [jnp.exp(t - top) for t in logits]
        return e[0] / sum(e)

    lanes = [slice(h * LANES, (h + 1) * LANES) for h in range(heads)]
    lbs = [[lower_bound(d, sl) for sl in lanes] for d in (0, 1)]

    def rows(i):
        return pl.ds(pl.multiple_of(i * c, c), c)

    def ctx_step(j, sts):
        chains = []
        for d in (0, 1):
            r = rows(nc_ctx - 1 - j if d else j)
            for h, sl in enumerate(lanes):
                chains.append((None, ic_ref[0, r, sl], f_ctx[d][0, r, sl].astype(F32),
                               lbs[d][h], sts[d * heads + h], d))
        return tuple(_hgrn_chunks(chains, tri_ref, lvl_ref, False)[1])

    def lat_step(j, sts, second_visit):
        chains, where = [], []
        for d in (0, 1):
            r = rows(nc_lat - 1 - j if d else j)
            for h, sl in enumerate(lanes):
                chains.append((q_ref[0, r, sl].astype(F32), i_ref[0, r, sl],
                               f_lat[d][0, r, sl].astype(F32), lbs[d][h], sts[d * heads + h], d))
                where.append((r, sl))
        outs, new = _hgrn_chunks(chains, tri_ref, lvl_ref, True)
        for o, (r, sl) in zip(outs, where):
            if second_visit:
                y = _rms(acc_scr[r, sl] + o, ng_ref[...]) * _silu(g_ref[0, r, sl].astype(F32))
                o_ref[0, r, sl] = y.astype(o_ref.dtype)
            else:
                acc_scr[r, sl] = o
        return tuple(new)

    sts = tuple(jnp.zeros((B_DIM, B_DIM), F32) for _ in range(2 * heads))
    sts = lax.fori_loop(0, nc_ctx, ctx_step, sts, unroll=True)
    sts = lax.fori_loop(0, nc_lat // 2, functools.partial(lat_step, second_visit=False), sts,
                        unroll=HGRN_UNROLL)
    lax.fori_loop(nc_lat // 2, nc_lat, functools.partial(lat_step, second_visit=True), sts,
                  unroll=HGRN_UNROLL)


def _hgrn(rest_x, rest_c, lb_logits, norm_g, heads_per_step):
    b, n, _ = rest_x.shape
    n_ctx = rest_c.shape[1]
    assert (n // HGRN_CHUNK) % 2 == 0 and B_HEADS % heads_per_step == 0
    w = heads_per_step * LANES
    steps = B_HEADS // heads_per_step

    def xs(group):
        return pl.BlockSpec((1, n, w), lambda i, h: (i, 0, group * steps + h))

    def cs(group):
        return pl.BlockSpec((1, n_ctx, w), lambda i, h: (i, 0, group * steps + h))

    n_layers = lb_logits.shape[1]
    tri, lvl = _hgrn_tables(HGRN_CHUNK)
    return pl.pallas_call(
        _hgrn_kernel,
        grid=(b, steps),
        in_specs=[
            xs(REST_COL_Q), xs(REST_COL_I), xs(REST_COL_FF), xs(REST_COL_FB), xs(REST_COL_GATE_B),
            cs(CTX_COL_I), cs(CTX_COL_FF), cs(CTX_COL_FB),
            pl.BlockSpec((2, n_layers, 1, w), lambda i, h: (0, 0, 0, h)),
            _const_spec((1, LANES)),
            _const_spec(tri.shape),
            _const_spec(lvl.shape),
        ],
        out_specs=pl.BlockSpec((1, n, w), lambda i, h: (i, 0, h)),
        out_shape=jax.ShapeDtypeStruct((b, n, B_WIDTH), BF16),
        scratch_shapes=[pltpu.VMEM((n, w), F32)],
        compiler_params=_cparams("arbitrary", "arbitrary"),
        name="hgrn2",
    )(rest_x, rest_x, rest_x, rest_x, rest_x, rest_c, rest_c, rest_c,
      lb_logits.reshape(2, n_layers, 1, B_WIDTH), norm_g, tri, lvl)


def _layer1_input(ya, yb, x, gate0, mod1, gain1, wo0_ref):
    d = x.shape[-1]
    half = ya.shape[-1]
    upd = (jnp.dot(ya, wo0_ref[0:half, :], preferred_element_type=F32)
           + jnp.dot(yb, wo0_ref[half:, :], preferred_element_type=F32))
    x1 = x + gate0 * upd
    return x1, _modulate(x1, gain1, mod1, d).astype(BF16)


def _edge_kernel(ya_ref, yb_ref, x_ref, mod0_ref, mod1_ref, g1_ref, wo0_ref, wi_ref, z_ref,
                 *, rows_per_batch):
    d = x_ref.shape[-1]

    def per_row(ref, lo, hi):
        return jnp.concatenate([jnp.broadcast_to(ref[i:i + 1, lo:hi], (rows_per_batch, hi - lo))
                                for i in range(x_ref.shape[0] // rows_per_batch)], axis=0)

    _, xm = _layer1_input(ya_ref[...], yb_ref[...], x_ref[...], per_row(mod0_ref, 2 * d, 3 * d),
                          per_row(mod1_ref, 0, 2 * d), g1_ref[...], wo0_ref)
    p = jnp.dot(xm, wi_ref[...], preferred_element_type=F32)
    z_ref[...] = p[:, :D_WIDTH] * p[:, D_WIDTH:]


def _block_edges(t, tb):
    b, n, w = t.shape
    te = t.reshape(b, n // tb, tb, w)
    return jnp.concatenate([te[:, :, :SUBLANES], te[:, :, tb - SUBLANES:]], axis=2).reshape(-1, w)


def _edge_z(ya, yb, x, mod0, mod1, gain1, w_out0, w_in1, tb):
    b, n, d = x.shape
    rows = (n // tb) * 2 * SUBLANES
    cg_start = 3 * C_WIDTH + D_WIDTH
    assert cg_start % (2 * D_WIDTH) == 0
    cg_blk = cg_start // (2 * D_WIDTH)
    full = lambda arr: _const_spec(arr.shape)
    xe, yae, ybe = _block_edges(x, tb), _block_edges(ya, tb), _block_edges(yb, tb)
    return pl.pallas_call(
        functools.partial(_edge_kernel, rows_per_batch=rows),
        grid=(1,),
        in_specs=[full(yae), full(ybe), full(xe), full(mod0), full(mod1), _const_spec((1, d)),
                  full(w_out0),
                  pl.BlockSpec((d, 2 * D_WIDTH), lambda i: (0, cg_blk), pipeline_mode=pl.Buffered(1))],
        out_specs=pl.BlockSpec((b * rows, D_WIDTH), lambda i: (0, 0)),
        out_shape=jax.ShapeDtypeStruct((b * rows, D_WIDTH), F32),
        compiler_params=_cparams("arbitrary"),
        name="conv_edge_rows",
    )(yae, ybe, xe, mod0, mod1, gain1, w_out0, w_in1).reshape(b, rows, D_WIDTH)


def _layer1_kernel(ya_ref, yb_ref, x_ref, ze_ref, mod0_ref, mod1_ref, g1_ref, wo0_ref, wi_ref,
                   vg_ref, ws_ref, bs_ref, cw_ref, wo1_ref, o_ref):
    d = x_ref.shape[-1]
    tb = x_ref.shape[1]
    j = pl.program_id(1)
    last_j = pl.num_programs(1) - 1
    mod1 = mod1_ref[pl.ds(pl.program_id(0), 1), :]
    gate0 = mod0_ref[pl.ds(pl.program_id(0), 1), 2 * d:]
    gate1 = mod1[:, 2 * d:]
    sub = tb // L1_SUB_BLOCKS
    subs = [slice(s * sub, (s + 1) * sub) for s in range(L1_SUB_BLOCKS)]
    col = lambda p, k: p[:, k * C_WIDTH:(k + 1) * C_WIDTH]
    n_gmlp = 3 * C_WIDTH

    x1s, xms = [], []
    for r in subs:
        x1, xm = _layer1_input(ya_ref[0, r, :], yb_ref[0, r, :], x_ref[0, r, :], gate0,
                               mod1, g1_ref[...], wo0_ref)
        x1s.append(x1)
        xms.append(xm)
    pgs = [jnp.dot(xm, wi_ref[:, 0:n_gmlp], preferred_element_type=F32) for xm in xms]
    pcs = [jnp.dot(xm, wi_ref[:, n_gmlp:], preferred_element_type=F32) for xm in xms]

    upd_c = []
    for pg in pgs:
        u = _gelu(col(pg, 0))
        vn = _rms(_gelu(col(pg, 1)), vg_ref[...]).astype(BF16)
        chunks = []
        for ci in range(sub // C_CHUNK):
            cr = slice(ci * C_CHUNK, (ci + 1) * C_CHUNK)
            groups = []
            for g in range(C_GROUPS):
                gl = slice(g * LANES, (g + 1) * LANES)
                groups.append(jnp.dot(ws_ref[g], vn[cr, gl], preferred_element_type=F32) + bs_ref[g])
            chunks.append(jnp.concatenate(groups, axis=1))
        o_c = u * jnp.concatenate(chunks, axis=0) * _silu(col(pg, 2))
        upd_c.append(jnp.dot(o_c.astype(BF16), wo1_ref[0:C_WIDTH, :], preferred_element_type=F32))

    z = jnp.concatenate([col(pc, 1) * col(pc, 2) for pc in pcs], axis=0)
    grp = 2 * SUBLANES
    prev_grp = ze_ref[0, pl.ds(pl.multiple_of(jnp.maximum(j - 1, 0) * grp + SUBLANES, SUBLANES),
                               SUBLANES), :]
    next_grp = ze_ref[0, pl.ds(pl.multiple_of(jnp.minimum(j + 1, last_j) * grp, SUBLANES),
                               SUBLANES), :]
    z_prev_row = jnp.where(j == 0, 0.0, prev_grp[SUBLANES - 1:, :])
    z_next_row = jnp.where(j == last_j, 0.0, next_grp[0:1, :])
    rowi = lax.broadcasted_iota(jnp.int32, z.shape, 0)
    z_prev = jnp.where(rowi == 0, z_prev_row, pltpu.roll(z, 1, 0))
    z_next = jnp.where(rowi == tb - 1, z_next_row, pltpu.roll(z, tb - 1, 0))
    conv = cw_ref[0:1, :] * z_prev + cw_ref[1:2, :] * z + cw_ref[2:3, :] * z_next

    for r, x1, pc, uc in zip(subs, x1s, pcs, upd_c):
        o_d = col(pc, 0) * conv[r, :] * _silu(col(pc, 3))
        upd = uc + jnp.dot(o_d.astype(BF16), wo1_ref[C_WIDTH:, :], preferred_element_type=F32)
        o_ref[0, r, :] = x1 + gate1 * upd


def _layer1(ya, yb, x, ze, mod0, mod1, gain1, w_out0, w_in1, v_gain, w_s, b_s, conv_w, w_out1, tb):
    b, n, d = x.shape
    half = ya.shape[-1]
    tok = lambda width: pl.BlockSpec((1, tb, width), lambda i, j: (i, j, 0))
    modspec = _const_spec(mod0.shape)
    return pl.pallas_call(
        _layer1_kernel,
        grid=(b, n // tb),
        in_specs=[tok(half), tok(half), tok(d),
                  pl.BlockSpec((1,) + ze.shape[1:], lambda i, j: (i, 0, 0)),
                  modspec, modspec, _const_spec((1, d)),
                  _const_spec(w_out0.shape), _const_spec(w_in1.shape),
                  _const_spec((1, C_WIDTH)),
                  _const_spec((C_GROUPS, C_CHUNK, C_CHUNK)),
                  _const_spec((C_GROUPS, C_CHUNK, LANES)),
                  _const_spec((3, D_WIDTH)),
                  _const_spec(w_out1.shape)],
        out_specs=tok(d),
        out_shape=jax.ShapeDtypeStruct((b, n, d), F32),
        compiler_params=_cparams("arbitrary", "arbitrary"),
        name="outproj_even_layer_odd",
    )(ya, yb, x, ze, mod0, mod1, gain1, w_out0, w_in1, v_gain, w_s, b_s, conv_w, w_out1)


def _rope_tables(n):
    rows_ = n // GRID_W
    row = np.repeat(np.arange(rows_, dtype=np.float64), GRID_W)
    col = np.tile(np.arange(GRID_W, dtype=np.float64), rows_)
    n_freq = A_HEAD_DIM // 4
    inv = ROPE_THETA ** (-np.arange(n_freq, dtype=np.float64) / n_freq)
    ang = np.concatenate([row[:, None] * inv, col[:, None] * inv], axis=-1)
    cos, sin = np.cos(ang), np.sin(ang)
    reps = QK_SLAB // A_HEAD_DIM
    return (jnp.asarray(np.tile(np.concatenate([cos, cos], axis=-1), (1, reps)), F32),
            jnp.asarray(np.tile(np.concatenate([-sin, sin], axis=-1), (1, reps)), F32))


def kernel(x, c, ctx, c_ctx, norm_gain, ada_w, ada_b, even_w_in, even_w_out, attn_qk_gain,
           attn_lambda, attn_subln_gain, hgrn_lb_logits, hgrn_norm_gain, odd_w_in, odd_w_out,
           gmlp_v_gain, gmlp_w_s, gmlp_b_s, conv_w):
    b, n, d = x.shape
    assert b + 1 <= COND_ROWS and n % 512 == 0 and ctx.shape[1] % HGRN_CHUNK == 0
    assert norm_gain.shape[0] == 2, "two-layer block: one even layer then one odd layer"

    cond = jnp.concatenate([c, c_ctx[None, :], jnp.zeros((COND_ROWS - b - 1, d), F32)], axis=0)
    mod0, mod1 = _adaln(cond, ada_w, ada_b)

    w_in0 = even_w_in[0].astype(BF16)
    gain0 = norm_gain[0].reshape(1, d)
    cos, sin_signed = _rope_tables(n)
    qk_gain = jnp.tile(attn_qk_gain[0], (1, QK_SLAB // A_HEAD_DIM))
    blk = np.arange(QK_SLAB) // A_HEAD_DIM
    bd = jnp.asarray(np.where(blk[:, None] == blk[None, :], 1.0 / A_HEAD_DIM, 0.0), BF16)
    q, k_lat, vt_lat, rest_x = _inproj(x, mod0, None, gain0, w_in0, cos, sin_signed, qk_gain, bd,
                                       TOKEN_BLOCK, True, True)
    w_ctx = jnp.concatenate([w_in0[:, g * A_WIDTH:(g + 1) * A_WIDTH] for g in CTX_GROUPS], axis=1)
    n_ctx = ctx.shape[1]
    k_ctx, vt_ctx, rest_c = _inproj(ctx, mod0, b, gain0, w_ctx, cos[:n_ctx], sin_signed[:n_ctx],
                                    qk_gain, bd, n_ctx, False, False)

    lam_init = 0.8 - 0.6 * math.exp(-0.3 * 0)
    ya = _attention(q, k_lat, k_ctx, vt_lat, vt_ctx, rest_x, qk_gain,
                    attn_subln_gain[0].reshape(1, LANES), attn_lambda[0], lam_init, ATTN_Q_BLOCK)
    yb = _hgrn(rest_x, rest_c, hgrn_lb_logits, hgrn_norm_gain[0].reshape(1, LANES),
               HGRN_HEADS_PER_STEP)

    gain1 = norm_gain[1].reshape(1, d)
    w_out0, w_in1 = even_w_out[0].astype(BF16), odd_w_in[0].astype(BF16)
    ze = _edge_z(ya, yb, x, mod0, mod1, gain1, w_out0, w_in1, TOKEN_BLOCK)
    b_s = jnp.broadcast_to(gmlp_b_s[0][:, :, None], (C_GROUPS, C_CHUNK, LANES))
    return _layer1(ya, yb, x, ze, mod0, mod1, gain1, w_out0, w_in1,
                   gmlp_v_gain[0].reshape(1, C_WIDTH), gmlp_w_s[0].astype(BF16), b_s, conv_w[0],
                   odd_w_out[0].astype(BF16), TOKEN_BLOCK)
```

```python
import functools
import math

import jax
import jax.numpy as jnp
import numpy as np
from jax import lax
from jax.experimental import pallas as pl
from jax.experimental.pallas import tpu as pltpu

F32 = jnp.float32
BF16 = jnp.bfloat16

EPS = 1e-6
GRID_W = 64
ROPE_THETA = 10000.0
A_HEADS = 4
A_HEAD_DIM = 64
A_WIDTH = 2 * A_HEADS * A_HEAD_DIM
B_HEADS = 4
B_DIM = 128
B_WIDTH = B_HEADS * B_DIM
C_GROUPS = 4
C_CHUNK = 128
C_WIDTH = 512
D_WIDTH = 512
EVEN_IN = 4 * A_WIDTH + 5 * B_WIDTH
ODD_IN = 3 * C_WIDTH + 4 * D_WIDTH

CTX_GROUPS = (1, 2, 5, 6, 7)
REST_COL_GATE_A, REST_COL_Q, REST_COL_I, REST_COL_FF, REST_COL_FB, REST_COL_GATE_B = range(6)
CTX_COL_I, CTX_COL_FF, CTX_COL_FB = range(3)
QK_SLAB = 256

LANES = 128
HGRN_CHUNK = 128
HGRN_HEADS_PER_STEP = 4
HGRN_UNROLL = 4
HGRN_DIAG = 8
SCORE_BOUND = 100.0
EXP2_CLAMP = 115.0
LOG2E = math.log2(math.e)
Q_SCALE = A_HEAD_DIM ** -0.5 * LOG2E
TOKEN_BLOCK = 512
L1_TOKEN_BLOCK = 1024
ATTN_Q_TILE = 256
ATTN_Q_BLOCK = 512
SUBLANES = 8
BF16_ROWS_PER_VREG = 16
L1_SUB_BLOCKS = 4
COND_ROWS = 16
ADALN_COL_BLOCK = 512
VMEM_LIMIT = 56 * 1024 * 1024


def _cparams(*sem):
    return pltpu.CompilerParams(dimension_semantics=sem, vmem_limit_bytes=VMEM_LIMIT)


def _const_spec(shape):
    nd = len(shape)
    return pl.BlockSpec(shape, lambda *_: (0,) * nd, pipeline_mode=pl.Buffered(1))


def _sigmoid(t):
    return 0.5 + 0.5 * jnp.tanh(0.5 * t)


def _silu(t):
    return t * _sigmoid(t)


def _gelu(t):
    return 0.5 * t * (1.0 + lax.erf(t * (1.0 / math.sqrt(2.0))))


def _rms(t, gain):
    ms = jnp.mean(t * t, axis=-1, keepdims=True)
    return t * lax.rsqrt(ms + EPS) * gain


def _adaln_kernel(cond_ref, w_ref, b_ref, *o_refs):
    a = _silu(cond_ref[...])
    for layer, o_ref in enumerate(o_refs):
        o_ref[...] = jnp.dot(a, w_ref[layer], preferred_element_type=F32) + b_ref[layer]


def _adaln(cond, ada_w, ada_b):
    depth, d, n3 = ada_w.shape
    tn = ADALN_COL_BLOCK
    out = pl.BlockSpec((COND_ROWS, tn), lambda j: (0, j))
    return pl.pallas_call(
        _adaln_kernel,
        grid=(n3 // tn,),
        in_specs=[
            pl.BlockSpec((COND_ROWS, d), lambda j: (0, 0)),
            pl.BlockSpec((depth, d, tn), lambda j: (0, 0, j)),
            pl.BlockSpec((depth, 1, tn), lambda j: (0, 0, j)),
        ],
        out_specs=[out] * depth,
        out_shape=[jax.ShapeDtypeStruct((COND_ROWS, n3), F32)] * depth,
        compiler_params=_cparams("arbitrary"),
        name="adaln",
    )(cond, ada_w, ada_b.reshape(depth, 1, n3))


def _modulate(x, gain, mod, d):
    shift = mod[:, 0:d]
    scale = mod[:, d:2 * d]
    return _rms(x, gain) * (1.0 + scale) + shift


def _rope(t, cos, sin_signed):
    lanes = t.shape[1]
    lane = lax.broadcasted_iota(jnp.int32, t.shape, 1)
    first = (lane % A_HEAD_DIM) < (A_HEAD_DIM // 2)
    partner = jnp.where(first,
                        pltpu.roll(t, lanes - A_HEAD_DIM // 2, 1),
                        pltpu.roll(t, A_HEAD_DIM // 2, 1))
    return t * cos + partner * sin_signed


def _inproj_kernel(x_ref, mod_ref, g_ref, wa_ref, wr1_ref, wr2_ref, cos_ref, sin_ref, qkg_ref, bd_ref,
                   *out_refs,
                   mod_row, has_q, rope):
    d = x_ref.shape[-1]
    row = pl.program_id(0) if mod_row is None else mod_row
    xm = _modulate(x_ref[0], g_ref[...], mod_ref[pl.ds(row, 1), :], d).astype(BF16)
    n_attn = (3 if has_q else 2) * A_WIDTH
    rest_ref = out_refs[-1]
    n_rest = rest_ref.shape[-1]
    slabs = [slice(half * QK_SLAB, (half + 1) * QK_SLAB) for half in range(A_WIDTH // QK_SLAB)]

    attn = jnp.dot(xm, wa_ref[...], preferred_element_type=F32)
    rest_ref[0, :, 0:n_rest // 2] = jnp.dot(
        xm, wr1_ref[...], preferred_element_type=F32).astype(rest_ref.dtype)
    groups = [attn[:, g * A_WIDTH:(g + 1) * A_WIDTH] for g in range(n_attn // A_WIDTH)]
    v = groups.pop()
    mean_sq = [[jnp.dot((t[:, sl] * t[:, sl]).astype(BF16), bd_ref[...], preferred_element_type=F32)
                for sl in slabs] for t in groups]
    pad = BF16_ROWS_PER_VREG
    ms_rows = sum(ms[0:pad, :] for per_group in mean_sq for ms in per_group)
    anchor = jnp.concatenate([ms_rows * 0.0] * (d // QK_SLAB), axis=1)
    xm_late = jnp.concatenate([(xm[0:pad, :].astype(F32) + anchor).astype(BF16), xm[pad:, :]], axis=0)
    rest_ref[0, :, n_rest // 2:] = jnp.dot(
        xm_late, wr2_ref[...], preferred_element_type=F32).astype(rest_ref.dtype)

    qk_refs = out_refs[:-2]
    vt_ref = out_refs[-2]
    first_gain = 0 if has_q else 1
    for gi, (t, o_ref) in enumerate(zip(groups, qk_refs)):
        gain = qkg_ref[first_gain + gi:first_gain + gi + 1, :]
        scale = Q_SCALE if (has_q and gi == 0) else 1.0
        for sl, ms in zip(slabs, mean_sq[gi]):
            tn = t[:, sl] * lax.rsqrt(ms + EPS) * gain
            if rope:
                tn = _rope(tn, cos_ref[...], sin_ref[...])
            o_ref[0, :, sl] = (tn * scale).astype(o_ref.dtype)
    for h in range(A_HEADS):
        sl = slice(h * LANES, (h + 1) * LANES)
        vt_ref[0, sl, :] = v[:, sl].T.astype(vt_ref.dtype)


def _inproj(x, mod, mod_row, gain, w, cos, sin_signed, qk_gain, bd, tb, has_q, rope):
    b, n, d = x.shape
    n_attn = (3 if has_q else 2) * A_WIDTH
    n_rest = w.shape[1] - n_attn
    bounds = (0, n_attn, n_attn + n_rest // 2, w.shape[1])
    w_parts, w_specs = [], []
    for lo, hi in zip(bounds[:-1], bounds[1:]):
        if lo % (hi - lo) == 0:
            w_parts.append(w)
            w_specs.append(pl.BlockSpec((d, hi - lo), lambda i, j, blk=lo // (hi - lo): (0, blk),
                                        pipeline_mode=pl.Buffered(1)))
        else:
            w_parts.append(w[:, lo:hi])
            w_specs.append(_const_spec((d, hi - lo)))
    tok = lambda width: pl.BlockSpec((1, tb, width), lambda i, j: (i, j, 0))
    qk_out = [tok(A_WIDTH)] * (2 if has_q else 1)
    qk_shape = [jax.ShapeDtypeStruct((b, n, A_WIDTH), BF16)] * (2 if has_q else 1)
    return pl.pallas_call(
        functools.partial(_inproj_kernel, mod_row=mod_row, has_q=has_q, rope=rope),
        grid=(b, n // tb),
        in_specs=[
            tok(d),
            _const_spec(mod.shape),
            _const_spec((1, d)),
            *w_specs,
            pl.BlockSpec((tb, QK_SLAB), lambda i, j: (j, 0)),
            pl.BlockSpec((tb, QK_SLAB), lambda i, j: (j, 0)),
            _const_spec(qk_gain.shape),
            _const_spec(bd.shape),
        ],
        out_specs=qk_out + [pl.BlockSpec((1, A_WIDTH, tb), lambda i, j: (i, 0, j)), tok(n_rest)],
        out_shape=qk_shape + [jax.ShapeDtypeStruct((b, A_WIDTH, n), BF16),
                              jax.ShapeDtypeStruct((b, n, n_rest), BF16)],
        compiler_params=_cparams("arbitrary", "arbitrary"),
        name="inproj_even",
    )(x, mod, gain, *w_parts, cos, sin_signed, qk_gain, bd)


def _attn_kernel(q_ref, kl_ref, kc_ref, vtl_ref, vtc_ref, g_ref, qkg_ref, subg_ref, lamp_ref,
                 o_ref, *, lam_init):
    heads = [slice(h * LANES, (h + 1) * LANES) for h in range(A_HEADS)]
    lane = lax.broadcasted_iota(jnp.int32, (1, LANES), 1)
    nt = (((1,), (1,)), ((), ()))

    lp = lamp_ref[...]
    lam = (jnp.exp(jnp.sum(lp[0:1] * lp[1:2], axis=-1, keepdims=True))
           - jnp.exp(jnp.sum(lp[2:3] * lp[3:4], axis=-1, keepdims=True)) + lam_init)
    score_bound = ((A_HEAD_DIM * Q_SCALE) * jnp.max(jnp.abs(qkg_ref[0:1, :]))
                   * jnp.max(jnp.abs(qkg_ref[1:2, :])))

    n_sub = q_ref.shape[1] // ATTN_Q_TILE
    items = [(slice(qi * ATTN_Q_TILE, (qi + 1) * ATTN_Q_TILE), h)
             for qi in range(n_sub) for h in range(A_HEADS)]

    def scores(item):
        rows, h = item
        out = []
        for m in range(2):
            qm = jnp.where((lane // A_HEAD_DIM) == m, q_ref[0, rows, heads[h]], 0).astype(BF16)
            out.append([lax.dot_general(k_ref[0, :, heads[h]], qm, nt, preferred_element_type=F32)
                        for k_ref in (kc_ref, kl_ref)])
        return out

    def run_heads(shift):
        s_next = scores(items[0])
        for idx, (rows, h) in enumerate(items):
            sl = heads[h]
            s_maps = s_next
            if idx + 1 < len(items):
                s_next = scores(items[idx + 1])
            probs = []
            for s_parts in s_maps:
                if shift:
                    top = functools.reduce(jnp.maximum,
                                           [jnp.max(s, axis=0, keepdims=True) for s in s_parts])
                    s_parts = [s - top for s in s_parts]
                p_parts = [jnp.exp2(s) for s in s_parts]
                probs.append((p_parts, sum(jnp.sum(p, axis=0, keepdims=True) for p in p_parts)))
            (p0, l0), (p1, l1) = probs
            a0, a1 = 1.0 / l0, lam / l1
            ot = sum(jnp.dot(vt_ref[0, sl, :], (pa * a0 - pb * a1).astype(BF16),
                             preferred_element_type=F32)
                     for vt_ref, pa, pb in zip((vtc_ref, vtl_ref), p0, p1))
            ms = jnp.mean(ot * ot, axis=0, keepdims=True)
            on = (ot * lax.rsqrt(ms + EPS)).T * (subg_ref[...] * (1.0 - lam_init))
            o_ref[0, rows, sl] = (on * _silu(g_ref[0, rows, sl].astype(F32))).astype(o_ref.dtype)

    no_shift_ok = score_bound <= SCORE_BOUND
    pl.when(no_shift_ok)(functools.partial(run_heads, False))
    pl.when(jnp.logical_not(no_shift_ok))(functools.partial(run_heads, True))


def _attention(q, k_lat, k_ctx, vt_lat, vt_ctx, rest, qk_gain, subln_g, lam_p, lam_init, tq):
    b, n, w = q.shape
    n_ctx = k_ctx.shape[1]
    return pl.pallas_call(
        functools.partial(_attn_kernel, lam_init=lam_init),
        grid=(b, n // tq),
        in_specs=[
            pl.BlockSpec((1, tq, w), lambda i, j: (i, j, 0)),
            pl.BlockSpec((1, n, w), lambda i, j: (i, 0, 0)),
            pl.BlockSpec((1, n_ctx, w), lambda i, j: (i, 0, 0)),
            pl.BlockSpec((1, w, n), lambda i, j: (i, 0, 0)),
            pl.BlockSpec((1, w, n_ctx), lambda i, j: (i, 0, 0)),
            pl.BlockSpec((1, tq, w), lambda i, j: (i, j, REST_COL_GATE_A)),
            _const_spec(qk_gain.shape),
            _const_spec((1, LANES)),
            _const_spec((4, A_HEAD_DIM)),
        ],
        out_specs=pl.BlockSpec((1, tq, w), lambda i, j: (i, j, 0)),
        out_shape=jax.ShapeDtypeStruct((b, n, w), BF16),
        compiler_params=_cparams("arbitrary", "arbitrary"),
        name="diff_attn",
    )(q, k_lat, k_ctx, vt_lat, vt_ctx, rest, qk_gain, subln_g, lam_p)


def _split_bf16(t):
    hi = t.astype(BF16)
    return hi, (t - hi.astype(F32)).astype(BF16)


def _block_ref(g, block, row):
    c, w = g.shape
    g3 = g.reshape(c // block, block, w)
    return jnp.broadcast_to(g3[:, row:row + 1, :], g3.shape).reshape(c, w)


def _hgrn_tables(c):
    t = np.arange(c)[:, None]
    s = np.arange(c)[None, :]
    lvl = np.zeros((c, c), np.int32)
    lvl[(t // HGRN_DIAG == s // HGRN_DIAG) & (s <= t)] = 1
    b, k = HGRN_DIAG, 2
    while b < c:
        lvl[(t // b == s // b + 1) & ((s // b) % 2 == 0)] = k
        b, k = 2 * b, k + 1
    tri = (s <= t).astype(np.float32)
    return jnp.asarray(np.stack([tri, tri.T]), BF16), jnp.asarray(np.stack([lvl, lvl.T]))


def _hgrn_chunks(chains, tri_ref, lvl_ref, want_out):
    nt = (((1,), (1,)), ((), ()))
    tn = (((0,), (0,)), ((), ()))
    n = len(chains)
    c = chains[0][2].shape[0]

    kk, parts = [], []
    for (_, _, f_raw, lb, _, _) in chains:
        f = lb + (1.0 - lb) * _sigmoid(f_raw)
        kk.append(1.0 - f)
        parts.append(_split_bf16(jnp.log(f) * LOG2E))
    cum = [sum(jnp.dot(tri_ref[ch[5]], p, preferred_element_type=F32) for p in parts[i])
           for i, ch in enumerate(chains)]
    edge = [cum[i][0:1, :] if ch[5] else cum[i][c - 1:c, :] for i, ch in enumerate(chains)]

    outs = [None] * n
    if want_out:
        a = []
        for i, (q, _, _, _, _, d) in enumerate(chains):
            ref = _block_ref(cum[i], HGRN_DIAG, HGRN_DIAG // 2)
            qd = (q * jnp.exp2(jnp.minimum(cum[i] - ref, EXP2_CLAMP))).astype(BF16)
            kd = (kk[i] * jnp.exp2(jnp.minimum(ref - cum[i], EXP2_CLAMP))).astype(BF16)
            a.append(jnp.where(lvl_ref[d] == 1,
                               lax.dot_general(qd, kd, nt, preferred_element_type=F32), 0.0))
        b, k = HGRN_DIAG, 2
        while b < c:
            for i, (q, _, _, _, _, d) in enumerate(chains):
                ref = _block_ref(cum[i], 2 * b, b if d else b - 1)
                decay = jnp.exp2(cum[i] - ref)
                ql = (q * decay).astype(BF16)
                kl = (kk[i] * jnp.exp2(ref - cum[i])).astype(BF16)
                a[i] = jnp.where(lvl_ref[d] == k,
                                 lax.dot_general(ql, kl, nt, preferred_element_type=F32), a[i])
            b, k = 2 * b, k + 1
        for i, (q, v, _, _, st, _) in enumerate(chains):
            o = jnp.dot(a[i].astype(BF16), v, preferred_element_type=F32)
            outs[i] = o + lax.dot_general((q * jnp.exp2(cum[i])).astype(BF16), st.astype(BF16), nt,
                                          preferred_element_type=F32)

    sts = []
    for i, (_, v, _, _, st, _) in enumerate(chains):
        kg = (kk[i] * jnp.exp2(edge[i] - cum[i])).astype(BF16)
        upd = lax.dot_general(v, kg, tn, preferred_element_type=F32)
        sts.append(st * jnp.exp2(edge[i]) + upd)
    return outs, sts


def _hgrn_kernel(q_ref, i_ref, ff_ref, fb_ref, g_ref, ic_ref, ffc_ref, fbc_ref,
                 lbl_ref, ng_ref, tri_ref, lvl_ref, o_ref, acc_scr):
    c = HGRN_CHUNK
    heads = q_ref.shape[2] // LANES
    nc_lat = q_ref.shape[1] // c
    nc_ctx = ic_ref.shape[1] // c
    f_lat = (ff_ref, fb_ref)
    f_ctx = (ffc_ref, fbc_ref)

    def lower_bound(direction, sl):
        logits = [lbl_ref[direction, l, :, sl] for l in range(lbl_ref.shape[1])]
        top = functools.reduce(jnp.maximum, logits)
        e = [jnp.exp(t - top) for t in logits]
        return e[0] / sum(e)

    lanes = [slice(h * LANES, (h + 1) * LANES) for h in range(heads)]
    lbs = [[lower_bound(d, sl) for sl in lanes] for d in (0, 1)]

    def rows(i):
        return pl.ds(pl.multiple_of(i * c, c), c)

    def ctx_step(j, sts):
        chains = []
        for d in (0, 1):
            r = rows(nc_ctx - 1 - j if d else j)
            for h, sl in enumerate(lanes):
                chains.append((None, ic_ref[0, r, sl], f_ctx[d][0, r, sl].astype(F32),
                               lbs[d][h], sts[d * heads + h], d))
        return tuple(_hgrn_chunks(chains, tri_ref, lvl_ref, False)[1])

    def lat_step(j, sts, second_visit):
        chains, where = [], []
        for d in (0, 1):
            r = rows(nc_lat - 1 - j if d else j)
            for h, sl in enumerate(lanes):
                chains.append((q_ref[0, r, sl].astype(F32), i_ref[0, r, sl],
                               f_lat[d][0, r, sl].astype(F32), lbs[d][h], sts[d * heads + h], d))
                where.append((r, sl))
        outs, new = _hgrn_chunks(chains, tri_ref, lvl_ref, True)
        for o, (r, sl) in zip(outs, where):
            if second_visit:
                y = _rms(acc_scr[r, sl] + o, ng_ref[...]) * _silu(g_ref[0, r, sl].astype(F32))
                o_ref[0, r, sl] = y.astype(o_ref.dtype)
            else:
                acc_scr[r, sl] = o
        return tuple(new)

    sts = tuple(jnp.zeros((B_DIM, B_DIM), F32) for _ in range(2 * heads))
    sts = lax.fori_loop(0, nc_ctx, ctx_step, sts, unroll=True)
    sts = lax.fori_loop(0, nc_lat // 2, functools.partial(lat_step, second_visit=False), sts,
                        unroll=HGRN_UNROLL)
    lax.fori_loop(nc_lat // 2, nc_lat, functools.partial(lat_step, second_visit=True), sts,
                  unroll=HGRN_UNROLL)


def _hgrn(rest_x, rest_c, lb_logits, norm_g, heads_per_step):
    b, n, _ = rest_x.shape
    n_ctx = rest_c.shape[1]
    assert (n // HGRN_CHUNK) % 2 == 0 and B_HEADS % heads_per_step == 0
    w = heads_per_step * LANES
    steps = B_HEADS // heads_per_step

    def xs(group):
        return pl.BlockSpec((1, n, w), lambda i, h: (i, 0, group * steps + h))

    def cs(group):
        return pl.BlockSpec((1, n_ctx, w), lambda i, h: (i, 0, group * steps + h))

    n_layers = lb_logits.shape[1]
    tri, lvl = _hgrn_tables(HGRN_CHUNK)
    return pl.pallas_call(
        _hgrn_kernel,
        grid=(b, steps),
        in_specs=[
            xs(REST_COL_Q), xs(REST_COL_I), xs(REST_COL_FF), xs(REST_COL_FB), xs(REST_COL_GATE_B),
            cs(CTX_COL_I), cs(CTX_COL_FF), cs(CTX_COL_FB),
            pl.BlockSpec((2, n_layers, 1, w), lambda i, h: (0, 0, 0, h)),
            _const_spec((1, LANES)),
            _const_spec(tri.shape),
            _const_spec(lvl.shape),
        ],
        out_specs=pl.BlockSpec((1, n, w), lambda i, h: (i, 0, h)),
        out_shape=jax.ShapeDtypeStruct((b, n, B_WIDTH), BF16),
        scratch_shapes=[pltpu.VMEM((n, w), F32)],
        compiler_params=_cparams("arbitrary", "arbitrary"),
        name="hgrn2",
    )(rest_x, rest_x, rest_x, rest_x, rest_x, rest_c, rest_c, rest_c,
      lb_logits.reshape(2, n_layers, 1, B_WIDTH), norm_g, tri, lvl)


def _layer1_input(ya, yb, x, gate0, mod1, gain1, wo0_ref):
    d = x.shape[-1]
    half = ya.shape[-1]
    upd = (jnp.dot(ya, wo0_ref[0:half, :], preferred_element_type=F32)
           + jnp.dot(yb, wo0_ref[half:, :], preferred_element_type=F32))
    x1 = x + gate0 * upd
    return x1, _modulate(x1, gain1, mod1, d).astype(BF16)


def _edge_kernel(ya_ref, yb_ref, x_ref, mod0_ref, mod1_ref, g1_ref, wo0_ref, wi_ref, z_ref,
                 *, rows_per_batch):
    d = x_ref.shape[-1]

    def per_row(ref, lo, hi):
        return jnp.concatenate([jnp.broadcast_to(ref[i:i + 1, lo:hi], (rows_per_batch, hi - lo))
                                for i in range(x_ref.shape[0] // rows_per_batch)], axis=0)

    _, xm = _layer1_input(ya_ref[...], yb_ref[...], x_ref[...], per_row(mod0_ref, 2 * d, 3 * d),
                          per_row(mod1_ref, 0, 2 * d), g1_ref[...], wo0_ref)
    p = jnp.dot(xm, wi_ref[...], preferred_element_type=F32)
    z_ref[...] = p[:, :D_WIDTH] * p[:, D_WIDTH:]


def _block_edges(t, tb):
    b, n, w = t.shape
    te = t.reshape(b, n // tb, tb, w)
    return jnp.concatenate([te[:, :, :SUBLANES], te[:, :, tb - SUBLANES:]], axis=2).reshape(-1, w)


def _edge_z(ya, yb, x, mod0, mod1, gain1, w_out0, w_in1, tb):
    b, n, d = x.shape
    rows = (n // tb) * 2 * SUBLANES
    cg_start = 3 * C_WIDTH + D_WIDTH
    assert cg_start % (2 * D_WIDTH) == 0
    cg_blk = cg_start // (2 * D_WIDTH)
    full = lambda arr: _const_spec(arr.shape)
    xe, yae, ybe = _block_edges(x, tb), _block_edges(ya, tb), _block_edges(yb, tb)
    return pl.pallas_call(
        functools.partial(_edge_kernel, rows_per_batch=rows),
        grid=(1,),
        in_specs=[full(yae), full(ybe), full(xe), full(mod0), full(mod1), _const_spec((1, d)),
                  full(w_out0),
                  pl.BlockSpec((d, 2 * D_WIDTH), lambda i: (0, cg_blk), pipeline_mode=pl.Buffered(1))],
        out_specs=pl.BlockSpec((b * rows, D_WIDTH), lambda i: (0, 0)),
        out_shape=jax.ShapeDtypeStruct((b * rows, D_WIDTH), F32),
        compiler_params=_cparams("arbitrary"),
        name="conv_edge_rows",
    )(yae, ybe, xe, mod0, mod1, gain1, w_out0, w_in1).reshape(b, rows, D_WIDTH)


def _layer1_kernel(ya_ref, yb_ref, x_ref, ze_ref, mod0_ref, mod1_ref, g1_ref, wo0_ref, wi_ref,
                   vg_ref, ws_ref, bs_ref, cw_ref, wo1_ref, o_ref):
    d = x_ref.shape[-1]
    tb = x_ref.shape[1]
    j = pl.program_id(1)
    last_j = pl.num_programs(1) - 1
    mod1 = mod1_ref[pl.ds(pl.program_id(0), 1), :]
    gate0 = mod0_ref[pl.ds(pl.program_id(0), 1), 2 * d:]
    gate1 = mod1[:, 2 * d:]
    sub = tb // L1_SUB_BLOCKS
    subs = [slice(s * sub, (s + 1) * sub) for s in range(L1_SUB_BLOCKS)]
    col = lambda p, k: p[:, k * C_WIDTH:(k + 1) * C_WIDTH]
    n_gmlp = 3 * C_WIDTH

    x1s, xms = [], []
    for r in subs:
        x1, xm = _layer1_input(ya_ref[0, r, :], yb_ref[0, r, :], x_ref[0, r, :], gate0,
                               mod1, g1_ref[...], wo0_ref)
        x1s.append(x1)
        xms.append(xm)
    pgs = [jnp.dot(xm, wi_ref[:, 0:n_gmlp], preferred_element_type=F32) for xm in xms]
    pcs = [jnp.dot(xm, wi_ref[:, n_gmlp:], preferred_element_type=F32) for xm in xms]

    upd_c = []
    for pg in pgs:
        u = _gelu(col(pg, 0))
        vn = _rms(_gelu(col(pg, 1)), vg_ref[...]).astype(BF16)
        chunks = []
        for ci in range(sub // C_CHUNK):
            cr = slice(ci * C_CHUNK, (ci + 1) * C_CHUNK)
            groups = []
            for g in range(C_GROUPS):
                gl = slice(g * LANES, (g + 1) * LANES)
                groups.append(jnp.dot(ws_ref[g], vn[cr, gl], preferred_element_type=F32) + bs_ref[g])
            chunks.append(jnp.concatenate(groups, axis=1))
        o_c = u * jnp.concatenate(chunks, axis=0) * _silu(col(pg, 2))
        upd_c.append(jnp.dot(o_c.astype(BF16), wo1_ref[0:C_WIDTH, :], preferred_element_type=F32))

    z = jnp.concatenate([col(pc, 1) * col(pc, 2) for pc in pcs], axis=0)
    grp = 2 * SUBLANES
    prev_grp = ze_ref[0, pl.ds(pl.multiple_of(jnp.maximum(j - 1, 0) * grp + SUBLANES, SUBLANES),
                               SUBLANES), :]
    next_grp = ze_ref[0, pl.ds(pl.multiple_of(jnp.minimum(j + 1, last_j) * grp, SUBLANES),
                               SUBLANES), :]
    z_prev_row = jnp.where(j == 0, 0.0, prev_grp[SUBLANES - 1:, :])
    z_next_row = jnp.where(j == last_j, 0.0, next_grp[0:1, :])
    rowi = lax.broadcasted_iota(jnp.int32, z.shape, 0)
    z_prev = jnp.where(rowi == 0, z_prev_row, pltpu.roll(z, 1, 0))
    z_next = jnp.where(rowi == tb - 1, z_next_row, pltpu.roll(z, tb - 1, 0))
    conv = cw_ref[0:1, :] * z_prev + cw_ref[1:2, :] * z + cw_ref[2:3, :] * z_next

    for r, x1, pc, uc in zip(subs, x1s, pcs, upd_c):
        o_d = col(pc, 0) * conv[r, :] * _silu(col(pc, 3))
        upd = uc + jnp.dot(o_d.astype(BF16), wo1_ref[C_WIDTH:, :], preferred_element_type=F32)
        o_ref[0, r, :] = x1 + gate1 * upd


def _layer1(ya, yb, x, ze, mod0, mod1, gain1, w_out0, w_in1, v_gain, w_s, b_s, conv_w, w_out1, tb):
    b, n, d = x.shape
    half = ya.shape[-1]
    tok = lambda width: pl.BlockSpec((1, tb, width), lambda i, j: (i, j, 0))
    modspec = _const_spec(mod0.shape)
    return pl.pallas_call(
        _layer1_kernel,
        grid=(b, n // tb),
        in_specs=[tok(half), tok(half), tok(d),
                  pl.BlockSpec((1,) + ze.shape[1:], lambda i, j: (i, 0, 0)),
                  modspec, modspec, _const_spec((1, d)),
                  _const_spec(w_out0.shape), _const_spec(w_in1.shape),
                  _const_spec((1, C_WIDTH)),
                  _const_spec((C_GROUPS, C_CHUNK, C_CHUNK)),
                  _const_spec((C_GROUPS, C_CHUNK, LANES)),
                  _const_spec((3, D_WIDTH)),
                  _const_spec(w_out1.shape)],
        out_specs=tok(d),
        out_shape=jax.ShapeDtypeStruct((b, n, d), F32),
        compiler_params=_cparams("arbitrary", "arbitrary"),
        name="outproj_even_layer_odd",
    )(ya, yb, x, ze, mod0, mod1, gain1, w_out0, w_in1, v_gain, w_s, b_s, conv_w, w_out1)


def _rope_tables(n):
    rows_ = n // GRID_W
    row = np.repeat(np.arange(rows_, dtype=np.float64), GRID_W)
    col = np.tile(np.arange(GRID_W, dtype=np.float64), rows_)
    n_freq = A_HEAD_DIM // 4
    inv = ROPE_THETA ** (-np.arange(n_freq, dtype=np.float64) / n_freq)
    ang = np.concatenate([row[:, None] * inv, col[:, None] * inv], axis=-1)
    cos, sin = np.cos(ang), np.sin(ang)
    reps = QK_SLAB // A_HEAD_DIM
    return (jnp.asarray(np.tile(np.concatenate([cos, cos], axis=-1), (1, reps)), F32),
            jnp.asarray(np.tile(np.concatenate([-sin, sin], axis=-1), (1, reps)), F32))


def kernel(x, c, ctx, c_ctx, norm_gain, ada_w, ada_b, even_w_in, even_w_out, attn_qk_gain,
           attn_lambda, attn_subln_gain, hgrn_lb_logits, hgrn_norm_gain, odd_w_in, odd_w_out,
           gmlp_v_gain, gmlp_w_s, gmlp_b_s, conv_w):
    b, n, d = x.shape
    assert b + 1 <= COND_ROWS and n % 512 == 0 and ctx.shape[1] % HGRN_CHUNK == 0
    assert norm_gain.shape[0] == 2, "two-layer block: one even layer then one odd layer"

    cond = jnp.concatenate([c, c_ctx[None, :], jnp.zeros((COND_ROWS - b - 1, d), F32)], axis=0)
    mod0, mod1 = _adaln(cond, ada_w, ada_b)

    w_in0 = even_w_in[0].astype(BF16)
    gain0 = norm_gain[0].reshape(1, d)
    cos, sin_signed = _rope_tables(n)
    qk_gain = jnp.tile(attn_qk_gain[0], (1, QK_SLAB // A_HEAD_DIM))
    blk = np.arange(QK_SLAB) // A_HEAD_DIM
    bd = jnp.asarray(np.where(blk[:, None] == blk[None, :], 1.0 / A_HEAD_DIM, 0.0), BF16)
    q, k_lat, vt_lat, rest_x = _inproj(x, mod0, None, gain0, w_in0, cos, sin_signed, qk_gain, bd,
                                       TOKEN_BLOCK, True, True)
    w_ctx = jnp.concatenate([w_in0[:, g * A_WIDTH:(g + 1) * A_WIDTH] for g in CTX_GROUPS], axis=1)
    n_ctx = ctx.shape[1]
    k_ctx, vt_ctx, rest_c = _inproj(ctx, mod0, b, gain0, w_ctx, cos[:n_ctx], sin_signed[:n_ctx],
                                    qk_gain, bd, n_ctx, False, False)

    lam_init = 0.8 - 0.6 * math.exp(-0.3 * 0)
    ya = _attention(q, k_lat, k_ctx, vt_lat, vt_ctx, rest_x, qk_gain,
                    attn_subln_gain[0].reshape(1, LANES), attn_lambda[0], lam_init, ATTN_Q_BLOCK)
    yb = _hgrn(rest_x, rest_c, hgrn_lb_logits, hgrn_norm_gain[0].reshape(1, LANES),
               HGRN_HEADS_PER_STEP)

    gain1 = norm_gain[1].reshape(1, d)
    w_out0, w_in1 = even_w_out[0].astype(BF16), odd_w_in[0].astype(BF16)
    ze = _edge_z(ya, yb, x, mod0, mod1, gain1, w_out0, w_in1, L1_TOKEN_BLOCK)
    b_s = jnp.broadcast_to(gmlp_b_s[0][:, :, None], (C_GROUPS, C_CHUNK, LANES))
    return _layer1(ya, yb, x, ze, mod0, mod1, gain1, w_out0, w_in1,
                   gmlp_v_gain[0].reshape(1, C_WIDTH), gmlp_w_s[0].astype(BF16), b_s, conv_w[0],
                   odd_w_out[0].astype(BF16), L1_TOKEN_BLOCK)
```

```python
import functools
import math

import jax
import jax.numpy as jnp
import numpy as np
from jax import lax
from jax.experimental import pallas as pl
from jax.experimental.pallas import tpu as pltpu

F32 = jnp.float32
BF16 = jnp.bfloat16

EPS = 1e-6
GRID_W = 64
ROPE_THETA = 10000.0
A_HEADS = 4
A_HEAD_DIM = 64
A_WIDTH = 2 * A_HEADS * A_HEAD_DIM
B_HEADS = 4
B_DIM = 128
B_WIDTH = B_HEADS * B_DIM
C_GROUPS = 4
C_CHUNK = 128
C_WIDTH = 512
D_WIDTH = 512
EVEN_IN = 4 * A_WIDTH + 5 * B_WIDTH
ODD_IN = 3 * C_WIDTH + 4 * D_WIDTH

CTX_GROUPS = (1, 2, 5, 6, 7)
REST_COL_GATE_A, REST_COL_Q, REST_COL_I, REST_COL_FF, REST_COL_FB, REST_COL_GATE_B = range(6)
CTX_COL_I, CTX_COL_FF, CTX_COL_FB = range(3)
QK_SLAB = 256

LANES = 128
HGRN_CHUNK = 128
HGRN_HEADS_PER_STEP = 4
HGRN_UNROLL = 4
HGRN_DIAG = 8
SCORE_BOUND = 100.0
EXP2_CLAMP = 115.0
LOG2E = math.log2(math.e)
Q_SCALE = A_HEAD_DIM ** -0.5 * LOG2E
TOKEN_BLOCK = 512
INPROJ_BLOCK = 1024
L1_TOKEN_BLOCK = 1024
ATTN_Q_TILE = 256
ATTN_Q_BLOCK = 512
SUBLANES = 8
BF16_ROWS_PER_VREG = 16
L1_SUB_BLOCKS = 4
COND_ROWS = 16
ADALN_COL_BLOCK = 512
VMEM_LIMIT = 56 * 1024 * 1024


def _cparams(*sem):
    return pltpu.CompilerParams(dimension_semantics=sem, vmem_limit_bytes=VMEM_LIMIT)


def _const_spec(shape):
    nd = len(shape)
    return pl.BlockSpec(shape, lambda *_: (0,) * nd, pipeline_mode=pl.Buffered(1))


def _sigmoid(t):
    return 0.5 + 0.5 * jnp.tanh(0.5 * t)


def _silu(t):
    return t * _sigmoid(t)


def _gelu(t):
    return 0.5 * t * (1.0 + lax.erf(t * (1.0 / math.sqrt(2.0))))


def _rms(t, gain):
    ms = jnp.mean(t * t, axis=-1, keepdims=True)
    return t * lax.rsqrt(ms + EPS) * gain


def _adaln_kernel(cond_ref, w_ref, b_ref, *o_refs):
    a = _silu(cond_ref[...])
    for layer, o_ref in enumerate(o_refs):
        o_ref[...] = jnp.dot(a, w_ref[layer], preferred_element_type=F32) + b_ref[layer]


def _adaln(cond, ada_w, ada_b):
    depth, d, n3 = ada_w.shape
    tn = ADALN_COL_BLOCK
    out = pl.BlockSpec((COND_ROWS, tn), lambda j: (0, j))
    return pl.pallas_call(
        _adaln_kernel,
        grid=(n3 // tn,),
        in_specs=[
            pl.BlockSpec((COND_ROWS, d), lambda j: (0, 0)),
            pl.BlockSpec((depth, d, tn), lambda j: (0, 0, j)),
            pl.BlockSpec((depth, 1, tn), lambda j: (0, 0, j)),
        ],
        out_specs=[out] * depth,
        out_shape=[jax.ShapeDtypeStruct((COND_ROWS, n3), F32)] * depth,
        compiler_params=_cparams("arbitrary"),
        name="adaln",
    )(cond, ada_w, ada_b.reshape(depth, 1, n3))


def _modulate(x, gain, mod, d):
    shift = mod[:, 0:d]
    scale = mod[:, d:2 * d]
    return _rms(x, gain) * (1.0 + scale) + shift


def _rope(t, cos, sin_signed):
    lanes = t.shape[1]
    lane = lax.broadcasted_iota(jnp.int32, t.shape, 1)
    first = (lane % A_HEAD_DIM) < (A_HEAD_DIM // 2)
    partner = jnp.where(first,
                        pltpu.roll(t, lanes - A_HEAD_DIM // 2, 1),
                        pltpu.roll(t, A_HEAD_DIM // 2, 1))
    return t * cos + partner * sin_signed


def _inproj_kernel(x_ref, mod_ref, g_ref, wa_ref, wr1_ref, wr2_ref, cos_ref, sin_ref, qkg_ref, bd_ref,
                   *out_refs,
                   mod_row, has_q, rope):
    d = x_ref.shape[-1]
    tb = x_ref.shape[1]
    row = pl.program_id(0) if mod_row is None else mod_row
    n_attn = (3 if has_q else 2) * A_WIDTH
    rest_ref = out_refs[-1]
    n_rest = rest_ref.shape[-1]
    slabs = [slice(half * QK_SLAB, (half + 1) * QK_SLAB) for half in range(A_WIDTH // QK_SLAB)]
    sub = min(tb, TOKEN_BLOCK)

    pending = []
    for r in [slice(s * sub, (s + 1) * sub) for s in range(tb // sub)]:
        xm = _modulate(x_ref[0, r, :], g_ref[...], mod_ref[pl.ds(row, 1), :], d).astype(BF16)
        attn = jnp.dot(xm, wa_ref[...], preferred_element_type=F32)
        rest_ref[0, r, 0:n_rest // 2] = jnp.dot(
            xm, wr1_ref[...], preferred_element_type=F32).astype(rest_ref.dtype)
        groups = [attn[:, g * A_WIDTH:(g + 1) * A_WIDTH] for g in range(n_attn // A_WIDTH)]
        v = groups.pop()
        mean_sq = [[jnp.dot((t[:, sl] * t[:, sl]).astype(BF16), bd_ref[...],
                            preferred_element_type=F32) for sl in slabs] for t in groups]
        pad = BF16_ROWS_PER_VREG
        ms_rows = sum(ms[0:pad, :] for per_group in mean_sq for ms in per_group)
        anchor = jnp.concatenate([ms_rows * 0.0] * (d // QK_SLAB), axis=1)
        xm_late = jnp.concatenate([(xm[0:pad, :].astype(F32) + anchor).astype(BF16), xm[pad:, :]],
                                  axis=0)
        rest_ref[0, r, n_rest // 2:] = jnp.dot(
            xm_late, wr2_ref[...], preferred_element_type=F32).astype(rest_ref.dtype)
        pending.append((r, groups, v, mean_sq))

    qk_refs = out_refs[:-2]
    vt_ref = out_refs[-2]
    first_gain = 0 if has_q else 1
    for r, groups, v, mean_sq in pending:
        for gi, (t, o_ref) in enumerate(zip(groups, qk_refs)):
            gain = qkg_ref[first_gain + gi:first_gain + gi + 1, :]
            scale = Q_SCALE if (has_q and gi == 0) else 1.0
            for sl, ms in zip(slabs, mean_sq[gi]):
                tn = t[:, sl] * lax.rsqrt(ms + EPS) * gain
                if rope:
                    tn = _rope(tn, cos_ref[r, :], sin_ref[r, :])
                o_ref[0, r, sl] = (tn * scale).astype(o_ref.dtype)
        for h in range(A_HEADS):
            sl = slice(h * LANES, (h + 1) * LANES)
            vt_ref[0, sl, r] = v[:, sl].T.astype(vt_ref.dtype)


def _inproj(x, mod, mod_row, gain, w, cos, sin_signed, qk_gain, bd, tb, has_q, rope):
    b, n, d = x.shape
    n_attn = (3 if has_q else 2) * A_WIDTH
    n_rest = w.shape[1] - n_attn
    bounds = (0, n_attn, n_attn + n_rest // 2, w.shape[1])
    w_parts, w_specs = [], []
    for lo, hi in zip(bounds[:-1], bounds[1:]):
        if lo % (hi - lo) == 0:
            w_parts.append(w)
            w_specs.append(pl.BlockSpec((d, hi - lo), lambda i, j, blk=lo // (hi - lo): (0, blk),
                                        pipeline_mode=pl.Buffered(1)))
        else:
            w_parts.append(w[:, lo:hi])
            w_specs.append(_const_spec((d, hi - lo)))
    tok = lambda width: pl.BlockSpec((1, tb, width), lambda i, j: (i, j, 0))
    qk_out = [tok(A_WIDTH)] * (2 if has_q else 1)
    qk_shape = [jax.ShapeDtypeStruct((b, n, A_WIDTH), BF16)] * (2 if has_q else 1)
    return pl.pallas_call(
        functools.partial(_inproj_kernel, mod_row=mod_row, has_q=has_q, rope=rope),
        grid=(b, n // tb),
        in_specs=[
            tok(d),
            _const_spec(mod.shape),
            _const_spec((1, d)),
            *w_specs,
            pl.BlockSpec((tb, QK_SLAB), lambda i, j: (j, 0)),
            pl.BlockSpec((tb, QK_SLAB), lambda i, j: (j, 0)),
            _const_spec(qk_gain.shape),
            _const_spec(bd.shape),
        ],
        out_specs=qk_out + [pl.BlockSpec((1, A_WIDTH, tb), lambda i, j: (i, 0, j)), tok(n_rest)],
        out_shape=qk_shape + [jax.ShapeDtypeStruct((b, A_WIDTH, n), BF16),
                              jax.ShapeDtypeStruct((b, n, n_rest), BF16)],
        compiler_params=_cparams("arbitrary", "arbitrary"),
        name="inproj_even",
    )(x, mod, gain, *w_parts, cos, sin_signed, qk_gain, bd)


def _attn_kernel(q_ref, kl_ref, kc_ref, vtl_ref, vtc_ref, g_ref, qkg_ref, subg_ref, lamp_ref,
                 o_ref, *, lam_init):
    heads = [slice(h * LANES, (h + 1) * LANES) for h in range(A_HEADS)]
    lane = lax.broadcasted_iota(jnp.int32, (1, LANES), 1)
    nt = (((1,), (1,)), ((), ()))

    lp = lamp_ref[...]
    lam = (jnp.exp(jnp.sum(lp[0:1] * lp[1:2], axis=-1, keepdims=True))
           - jnp.exp(jnp.sum(lp[2:3] * lp[3:4], axis=-1, keepdims=True)) + lam_init)
    score_bound = ((A_HEAD_DIM * Q_SCALE) * jnp.max(jnp.abs(qkg_ref[0:1, :]))
                   * jnp.max(jnp.abs(qkg_ref[1:2, :])))

    n_sub = q_ref.shape[1] // ATTN_Q_TILE
    items = [(slice(qi * ATTN_Q_TILE, (qi + 1) * ATTN_Q_TILE), h)
             for qi in range(n_sub) for h in range(A_HEADS)]

    def scores(item):
        rows, h = item
        out = []
        for m in range(2):
            qm = jnp.where((lane // A_HEAD_DIM) == m, q_ref[0, rows, heads[h]], 0).astype(BF16)
            out.append([lax.dot_general(k_ref[0, :, heads[h]], qm, nt, preferred_element_type=F32)
                        for k_ref in (kc_ref, kl_ref)])
        return out

    def run_heads(shift):
        s_next = scores(items[0])
        for idx, (rows, h) in enumerate(items):
            sl = heads[h]
            s_maps = s_next
            if idx + 1 < len(items):
                s_next = scores(items[idx + 1])
            probs = []
            for s_parts in s_maps:
                if shift:
                    top = functools.reduce(jnp.maximum,
                                           [jnp.max(s, axis=0, keepdims=True) for s in s_parts])
                    s_parts = [s - top for s in s_parts]
                p_parts = [jnp.exp2(s) for s in s_parts]
                probs.append((p_parts, sum(jnp.sum(p, axis=0, keepdims=True) for p in p_parts)))
            (p0, l0), (p1, l1) = probs
            a0, a1 = 1.0 / l0, lam / l1
            ot = sum(jnp.dot(vt_ref[0, sl, :], (pa * a0 - pb * a1).astype(BF16),
                             preferred_element_type=F32)
                     for vt_ref, pa, pb in zip((vtc_ref, vtl_ref), p0, p1))
            ms = jnp.mean(ot * ot, axis=0, keepdims=True)
            on = (ot * lax.rsqrt(ms + EPS)).T * (subg_ref[...] * (1.0 - lam_init))
            o_ref[0, rows, sl] = (on * _silu(g_ref[0, rows, sl].astype(F32))).astype(o_ref.dtype)

    no_shift_ok = score_bound <= SCORE_BOUND
    pl.when(no_shift_ok)(functools.partial(run_heads, False))
    pl.when(jnp.logical_not(no_shift_ok))(functools.partial(run_heads, True))


def _attention(q, k_lat, k_ctx, vt_lat, vt_ctx, rest, qk_gain, subln_g, lam_p, lam_init, tq):
    b, n, w = q.shape
    n_ctx = k_ctx.shape[1]
    return pl.pallas_call(
        functools.partial(_attn_kernel, lam_init=lam_init),
        grid=(b, n // tq),
        in_specs=[
            pl.BlockSpec((1, tq, w), lambda i, j: (i, j, 0)),
            pl.BlockSpec((1, n, w), lambda i, j: (i, 0, 0)),
            pl.BlockSpec((1, n_ctx, w), lambda i, j: (i, 0, 0)),
            pl.BlockSpec((1, w, n), lambda i, j: (i, 0, 0)),
            pl.BlockSpec((1, w, n_ctx), lambda i, j: (i, 0, 0)),
            pl.BlockSpec((1, tq, w), lambda i, j: (i, j, REST_COL_GATE_A)),
            _const_spec(qk_gain.shape),
            _const_spec((1, LANES)),
            _const_spec((4, A_HEAD_DIM)),
        ],
        out_specs=pl.BlockSpec((1, tq, w), lambda i, j: (i, j, 0)),
        out_shape=jax.ShapeDtypeStruct((b, n, w), BF16),
        compiler_params=_cparams("arbitrary", "arbitrary"),
        name="diff_attn",
    )(q, k_lat, k_ctx, vt_lat, vt_ctx, rest, qk_gain, subln_g, lam_p)


def _split_bf16(t):
    hi = t.astype(BF16)
    return hi, (t - hi.astype(F32)).astype(BF16)


def _block_ref(g, block, row):
    c, w = g.shape
    g3 = g.reshape(c // block, block, w)
    return jnp.broadcast_to(g3[:, row:row + 1, :], g3.shape).reshape(c, w)


def _hgrn_tables(c):
    t = np.arange(c)[:, None]
    s = np.arange(c)[None, :]
    lvl = np.zeros((c, c), np.int32)
    lvl[(t // HGRN_DIAG == s // HGRN_DIAG) & (s <= t)] = 1
    b, k = HGRN_DIAG, 2
    while b < c:
        lvl[(t // b == s // b + 1) & ((s // b) % 2 == 0)] = k
        b, k = 2 * b, k + 1
    tri = (s <= t).astype(np.float32)
    return jnp.asarray(np.stack([tri, tri.T]), BF16), jnp.asarray(np.stack([lvl, lvl.T]))


def _hgrn_chunks(chains, tri_ref, lvl_ref, want_out):
    nt = (((1,), (1,)), ((), ()))
    tn = (((0,), (0,)), ((), ()))
    n = len(chains)
    c = chains[0][2].shape[0]

    kk, parts = [], []
    for (_, _, f_raw, lb, _, _) in chains:
        f = lb + (1.0 - lb) * _sigmoid(f_raw)
        kk.append(1.0 - f)
        parts.append(_split_bf16(jnp.log(f) * LOG2E))
    cum = [sum(jnp.dot(tri_ref[ch[5]], p, preferred_element_type=F32) for p in parts[i])
           for i, ch in enumerate(chains)]
    edge = [cum[i][0:1, :] if ch[5] else cum[i][c - 1:c, :] for i, ch in enumerate(chains)]

    outs = [None] * n
    if want_out:
        a = []
        for i, (q, _, _, _, _, d) in enumerate(chains):
            ref = _block_ref(cum[i], HGRN_DIAG, HGRN_DIAG // 2)
            qd = (q * jnp.exp2(jnp.minimum(cum[i] - ref, EXP2_CLAMP))).astype(BF16)
            kd = (kk[i] * jnp.exp2(jnp.minimum(ref - cum[i], EXP2_CLAMP))).astype(BF16)
            a.append(jnp.where(lvl_ref[d] == 1,
                               lax.dot_general(qd, kd, nt, preferred_element_type=F32), 0.0))
        b, k = HGRN_DIAG, 2
        while b < c:
            for i, (q, _, _, _, _, d) in enumerate(chains):
                ref = _block_ref(cum[i], 2 * b, b if d else b - 1)
                decay = jnp.exp2(cum[i] - ref)
                ql = (q * decay).astype(BF16)
                kl = (kk[i] * jnp.exp2(ref - cum[i])).astype(BF16)
                a[i] = jnp.where(lvl_ref[d] == k,
                                 lax.dot_general(ql, kl, nt, preferred_element_type=F32), a[i])
            b, k = 2 * b, k + 1
        for i, (q, v, _, _, st, _) in enumerate(chains):
            o = jnp.dot(a[i].astype(BF16), v, preferred_element_type=F32)
            outs[i] = o + lax.dot_general((q * jnp.exp2(cum[i])).astype(BF16), st.astype(BF16), nt,
                                          preferred_element_type=F32)

    sts = []
    for i, (_, v, _, _, st, _) in enumerate(chains):
        kg = (kk[i] * jnp.exp2(edge[i] - cum[i])).astype(BF16)
        upd = lax.dot_general(v, kg, tn, preferred_element_type=F32)
        sts.append(st * jnp.exp2(edge[i]) + upd)
    return outs, sts


def _hgrn_kernel(q_ref, i_ref, ff_ref, fb_ref, g_ref, ic_ref, ffc_ref, fbc_ref,
                 lbl_ref, ng_ref, tri_ref, lvl_ref, o_ref, acc_scr):
    c = HGRN_CHUNK
    heads = q_ref.shape[2] // LANES
    nc_lat = q_ref.shape[1] // c
    nc_ctx = ic_ref.shape[1] // c
    f_lat = (ff_ref, fb_ref)
    f_ctx = (ffc_ref, fbc_ref)

    def lower_bound(direction, sl):
        logits = [lbl_ref[direction, l, :, sl] for l in range(lbl_ref.shape[1])]
        top = functools.reduce(jnp.maximum, logits)
        e = [jnp.exp(t - top) for t in logits]
        return e[0] / sum(e)

    lanes = [slice(h * LANES, (h + 1) * LANES) for h in range(heads)]
    lbs = [[lower_bound(d, sl) for sl in lanes] for d in (0, 1)]

    def rows(i):
        return pl.ds(pl.multiple_of(i * c, c), c)

    def ctx_step(j, sts):
        chains = []
        for d in (0, 1):
            r = rows(nc_ctx - 1 - j if d else j)
            for h, sl in enumerate(lanes):
                chains.append((None, ic_ref[0, r, sl], f_ctx[d][0, r, sl].astype(F32),
                               lbs[d][h], sts[d * heads + h], d))
        return tuple(_hgrn_chunks(chains, tri_ref, lvl_ref, False)[1])

    def lat_step(j, sts, second_visit):
        chains, where = [], []
        for d in (0, 1):
            r = rows(nc_lat - 1 - j if d else j)
            for h, sl in enumerate(lanes):
                chains.append((q_ref[0, r, sl].astype(F32), i_ref[0, r, sl],
                               f_lat[d][0, r, sl].astype(F32), lbs[d][h], sts[d * heads + h], d))
                where.append((r, sl))
        outs, new = _hgrn_chunks(chains, tri_ref, lvl_ref, True)
        for o, (r, sl) in zip(outs, where):
            if second_visit:
                y = _rms(acc_scr[r, sl] + o, ng_ref[...]) * _silu(g_ref[0, r, sl].astype(F32))
                o_ref[0, r, sl] = y.astype(o_ref.dtype)
            else:
                acc_scr[r, sl] = o
        return tuple(new)

    sts = tuple(jnp.zeros((B_DIM, B_DIM), F32) for _ in range(2 * heads))
    sts = lax.fori_loop(0, nc_ctx, ctx_step, sts, unroll=True)
    sts = lax.fori_loop(0, nc_lat // 2, functools.partial(lat_step, second_visit=False), sts,
                        unroll=HGRN_UNROLL)
    lax.fori_loop(nc_lat // 2, nc_lat, functools.partial(lat_step, second_visit=True), sts,
                  unroll=HGRN_UNROLL)


def _hgrn(rest_x, rest_c, lb_logits, norm_g, heads_per_step):
    b, n, _ = rest_x.shape
    n_ctx = rest_c.shape[1]
    assert (n // HGRN_CHUNK) % 2 == 0 and B_HEADS % heads_per_step == 0
    w = heads_per_step * LANES
    steps = B_HEADS // heads_per_step

    def xs(group):
        return pl.BlockSpec((1, n, w), lambda i, h: (i, 0, group * steps + h))

    def cs(group):
        return pl.BlockSpec((1, n_ctx, w), lambda i, h: (i, 0, group * steps + h))

    n_layers = lb_logits.shape[1]
    tri, lvl = _hgrn_tables(HGRN_CHUNK)
    return pl.pallas_call(
        _hgrn_kernel,
        grid=(b, steps),
        in_specs=[
            xs(REST_COL_Q), xs(REST_COL_I), xs(REST_COL_FF), xs(REST_COL_FB), xs(REST_COL_GATE_B),
            cs(CTX_COL_I), cs(CTX_COL_FF), cs(CTX_COL_FB),
            pl.BlockSpec((2, n_layers, 1, w), lambda i, h: (0, 0, 0, h)),
            _const_spec((1, LANES)),
            _const_spec(tri.shape),
            _const_spec(lvl.shape),
        ],
        out_specs=pl.BlockSpec((1, n, w), lambda i, h: (i, 0, h)),
        out_shape=jax.ShapeDtypeStruct((b, n, B_WIDTH), BF16),
        scratch_shapes=[pltpu.VMEM((n, w), F32)],
        compiler_params=_cparams("arbitrary", "arbitrary"),
        name="hgrn2",
    )(rest_x, rest_x, rest_x, rest_x, rest_x, rest_c, rest_c, rest_c,
      lb_logits.reshape(2, n_layers, 1, B_WIDTH), norm_g, tri, lvl)


def _layer1_input(ya, yb, x, gate0, mod1, gain1, wo0_ref):
    d = x.shape[-1]
    half = ya.shape[-1]
    upd = (jnp.dot(ya, wo0_ref[0:half, :], preferred_element_type=F32)
           + jnp.dot(yb, wo0_ref[half:, :], preferred_element_type=F32))
    x1 = x + gate0 * upd
    return x1, _modulate(x1, gain1, mod1, d).astype(BF16)


def _edge_kernel(ya_ref, yb_ref, x_ref, mod0_ref, mod1_ref, g1_ref, wo0_ref, wi_ref, z_ref,
                 *, rows_per_batch):
    d = x_ref.shape[-1]

    def per_row(ref, lo, hi):
        return jnp.concatenate([jnp.broadcast_to(ref[i:i + 1, lo:hi], (rows_per_batch, hi - lo))
                                for i in range(x_ref.shape[0] // rows_per_batch)], axis=0)

    _, xm = _layer1_input(ya_ref[...], yb_ref[...], x_ref[...], per_row(mod0_ref, 2 * d, 3 * d),
                          per_row(mod1_ref, 0, 2 * d), g1_ref[...], wo0_ref)
    p = jnp.dot(xm, wi_ref[...], preferred_element_type=F32)
    z_ref[...] = p[:, :D_WIDTH] * p[:, D_WIDTH:]


def _block_edges(t, tb):
    b, n, w = t.shape
    te = t.reshape(b, n // tb, tb, w)
    return jnp.concatenate([te[:, :, :SUBLANES], te[:, :, tb - SUBLANES:]], axis=2).reshape(-1, w)


def _edge_z(ya, yb, x, mod0, mod1, gain1, w_out0, w_in1, tb):
    b, n, d = x.shape
    rows = (n // tb) * 2 * SUBLANES
    cg_start = 3 * C_WIDTH + D_WIDTH
    assert cg_start % (2 * D_WIDTH) == 0
    cg_blk = cg_start // (2 * D_WIDTH)
    full = lambda arr: _const_spec(arr.shape)
    xe, yae, ybe = _block_edges(x, tb), _block_edges(ya, tb), _block_edges(yb, tb)
    return pl.pallas_call(
        functools.partial(_edge_kernel, rows_per_batch=rows),
        grid=(1,),
        in_specs=[full(yae), full(ybe), full(xe), full(mod0), full(mod1), _const_spec((1, d)),
                  full(w_out0),
                  pl.BlockSpec((d, 2 * D_WIDTH), lambda i: (0, cg_blk), pipeline_mode=pl.Buffered(1))],
        out_specs=pl.BlockSpec((b * rows, D_WIDTH), lambda i: (0, 0)),
        out_shape=jax.ShapeDtypeStruct((b * rows, D_WIDTH), F32),
        compiler_params=_cparams("arbitrary"),
        name="conv_edge_rows",
    )(yae, ybe, xe, mod0, mod1, gain1, w_out0, w_in1).reshape(b, rows, D_WIDTH)


def _layer1_kernel(ya_ref, yb_ref, x_ref, ze_ref, mod0_ref, mod1_ref, g1_ref, wo0_ref, wi_ref,
                   vg_ref, ws_ref, bs_ref, cw_ref, wo1_ref, o_ref):
    d = x_ref.shape[-1]
    tb = x_ref.shape[1]
    j = pl.program_id(1)
    last_j = pl.num_programs(1) - 1
    mod1 = mod1_ref[pl.ds(pl.program_id(0), 1), :]
    gate0 = mod0_ref[pl.ds(pl.program_id(0), 1), 2 * d:]
    gate1 = mod1[:, 2 * d:]
    sub = tb // L1_SUB_BLOCKS
    subs = [slice(s * sub, (s + 1) * sub) for s in range(L1_SUB_BLOCKS)]
    col = lambda p, k: p[:, k * C_WIDTH:(k + 1) * C_WIDTH]
    n_gmlp = 3 * C_WIDTH

    x1s, xms = [], []
    for r in subs:
        x1, xm = _layer1_input(ya_ref[0, r, :], yb_ref[0, r, :], x_ref[0, r, :], gate0,
                               mod1, g1_ref[...], wo0_ref)
        x1s.append(x1)
        xms.append(xm)
    pgs = [jnp.dot(xm, wi_ref[:, 0:n_gmlp], preferred_element_type=F32) for xm in xms]
    pcs = [jnp.dot(xm, wi_ref[:, n_gmlp:], preferred_element_type=F32) for xm in xms]

    upd_c = []
    for pg in pgs:
        u = _gelu(col(pg, 0))
        vn = _rms(_gelu(col(pg, 1)), vg_ref[...]).astype(BF16)
        chunks = []
        for ci in range(sub // C_CHUNK):
            cr = slice(ci * C_CHUNK, (ci + 1) * C_CHUNK)
            groups = []
            for g in range(C_GROUPS):
                gl = slice(g * LANES, (g + 1) * LANES)
                groups.append(jnp.dot(ws_ref[g], vn[cr, gl], preferred_element_type=F32) + bs_ref[g])
            chunks.append(jnp.concatenate(groups, axis=1))
        o_c = u * jnp.concatenate(chunks, axis=0) * _silu(col(pg, 2))
        upd_c.append(jnp.dot(o_c.astype(BF16), wo1_ref[0:C_WIDTH, :], preferred_element_type=F32))

    z = jnp.concatenate([col(pc, 1) * col(pc, 2) for pc in pcs], axis=0)
    grp = 2 * SUBLANES
    prev_grp = ze_ref[0, pl.ds(pl.multiple_of(jnp.maximum(j - 1, 0) * grp + SUBLANES, SUBLANES),
                               SUBLANES), :]
    next_grp = ze_ref[0, pl.ds(pl.multiple_of(jnp.minimum(j + 1, last_j) * grp, SUBLANES),
                               SUBLANES), :]
    z_prev_row = jnp.where(j == 0, 0.0, prev_grp[SUBLANES - 1:, :])
    z_next_row = jnp.where(j == last_j, 0.0, next_grp[0:1, :])
    rowi = lax.broadcasted_iota(jnp.int32, z.shape, 0)
    z_prev = jnp.where(rowi == 0, z_prev_row, pltpu.roll(z, 1, 0))
    z_next = jnp.where(rowi == tb - 1, z_next_row, pltpu.roll(z, tb - 1, 0))
    conv = cw_ref[0:1, :] * z_prev + cw_ref[1:2, :] * z + cw_ref[2:3, :] * z_next

    for r, x1, pc, uc in zip(subs, x1s, pcs, upd_c):
        o_d = col(pc, 0) * conv[r, :] * _silu(col(pc, 3))
        upd = uc + jnp.dot(o_d.astype(BF16), wo1_ref[C_WIDTH:, :], preferred_element_type=F32)
        o_ref[0, r, :] = x1 + gate1 * upd


def _layer1(ya, yb, x, ze, mod0, mod1, gain1, w_out0, w_in1, v_gain, w_s, b_s, conv_w, w_out1, tb):
    b, n, d = x.shape
    half = ya.shape[-1]
    tok = lambda width: pl.BlockSpec((1, tb, width), lambda i, j: (i, j, 0))
    modspec = _const_spec(mod0.shape)
    return pl.pallas_call(
        _layer1_kernel,
        grid=(b, n // tb),
        in_specs=[tok(half), tok(half), tok(d),
                  pl.BlockSpec((1,) + ze.shape[1:], lambda i, j: (i, 0, 0)),
                  modspec, modspec, _const_spec((1, d)),
                  _const_spec(w_out0.shape), _const_spec(w_in1.shape),
                  _const_spec((1, C_WIDTH)),
                  _const_spec((C_GROUPS, C_CHUNK, C_CHUNK)),
                  _const_spec((C_GROUPS, C_CHUNK, LANES)),
                  _const_spec((3, D_WIDTH)),
                  _const_spec(w_out1.shape)],
        out_specs=tok(d),
        out_shape=jax.ShapeDtypeStruct((b, n, d), F32),
        compiler_params=_cparams("arbitrary", "arbitrary"),
        name="outproj_even_layer_odd",
    )(ya, yb, x, ze, mod0, mod1, gain1, w_out0, w_in1, v_gain, w_s, b_s, conv_w, w_out1)


def _rope_tables(n):
    rows_ = n // GRID_W
    row = np.repeat(np.arange(rows_, dtype=np.float64), GRID_W)
    col = np.tile(np.arange(GRID_W, dtype=np.float64), rows_)
    n_freq = A_HEAD_DIM // 4
    inv = ROPE_THETA ** (-np.arange(n_freq, dtype=np.float64) / n_freq)
    ang = np.concatenate([row[:, None] * inv, col[:, None] * inv], axis=-1)
    cos, sin = np.cos(ang), np.sin(ang)
    reps = QK_SLAB // A_HEAD_DIM
    return (jnp.asarray(np.tile(np.concatenate([cos, cos], axis=-1), (1, reps)), F32),
            jnp.asarray(np.tile(np.concatenate([-sin, sin], axis=-1), (1, reps)), F32))


def kernel(x, c, ctx, c_ctx, norm_gain, ada_w, ada_b, even_w_in, even_w_out, attn_qk_gain,
           attn_lambda, attn_subln_gain, hgrn_lb_logits, hgrn_norm_gain, odd_w_in, odd_w_out,
           gmlp_v_gain, gmlp_w_s, gmlp_b_s, conv_w):
    b, n, d = x.shape
    assert b + 1 <= COND_ROWS and n % 512 == 0 and ctx.shape[1] % HGRN_CHUNK == 0
    assert norm_gain.shape[0] == 2, "two-layer block: one even layer then one odd layer"

    cond = jnp.concatenate([c, c_ctx[None, :], jnp.zeros((COND_ROWS - b - 1, d), F32)], axis=0)
    mod0, mod1 = _adaln(cond, ada_w, ada_b)

    w_in0 = even_w_in[0].astype(BF16)
    gain0 = norm_gain[0].reshape(1, d)
    cos, sin_signed = _rope_tables(n)
    qk_gain = jnp.tile(attn_qk_gain[0], (1, QK_SLAB // A_HEAD_DIM))
    blk = np.arange(QK_SLAB) // A_HEAD_DIM
    bd = jnp.asarray(np.where(blk[:, None] == blk[None, :], 1.0 / A_HEAD_DIM, 0.0), BF16)
    q, k_lat, vt_lat, rest_x = _inproj(x, mod0, None, gain0, w_in0, cos, sin_signed, qk_gain, bd,
                                       INPROJ_BLOCK, True, True)
    w_ctx = jnp.concatenate([w_in0[:, g * A_WIDTH:(g + 1) * A_WIDTH] for g in CTX_GROUPS], axis=1)
    n_ctx = ctx.shape[1]
    k_ctx, vt_ctx, rest_c = _inproj(ctx, mod0, b, gain0, w_ctx, cos[:n_ctx], sin_signed[:n_ctx],
                                    qk_gain, bd, n_ctx, False, False)

    lam_init = 0.8 - 0.6 * math.exp(-0.3 * 0)
    ya = _attention(q, k_lat, k_ctx, vt_lat, vt_ctx, rest_x, qk_gain,
                    attn_subln_gain[0].reshape(1, LANES), attn_lambda[0], lam_init, ATTN_Q_BLOCK)
    yb = _hgrn(rest_x, rest_c, hgrn_lb_logits, hgrn_norm_gain[0].reshape(1, LANES),
               HGRN_HEADS_PER_STEP)

    gain1 = norm_gain[1].reshape(1, d)
    w_out0, w_in1 = even_w_out[0].astype(BF16), odd_w_in[0].astype(BF16)
    ze = _edge_z(ya, yb, x, mod0, mod1, gain1, w_out0, w_in1, L1_TOKEN_BLOCK)
    b_s = jnp.broadcast_to(gmlp_b_s[0][:, :, None], (C_GROUPS, C_CHUNK, LANES))
    return _layer1(ya, yb, x, ze, mod0, mod1, gain1, w_out0, w_in1,
                   gmlp_v_gain[0].reshape(1, C_WIDTH), gmlp_w_s[0].astype(BF16), b_s, conv_w[0],
                   odd_w_out[0].astype(BF16), L1_TOKEN_BLOCK)
```

```python
import functools
import math

import jax
import jax.numpy as jnp
import numpy as np
from jax import lax
from jax.experimental import pallas as pl
from jax.experimental.pallas import tpu as pltpu

F32 = jnp.float32
BF16 = jnp.bfloat16

EPS = 1e-6
GRID_W = 64
ROPE_THETA = 10000.0
A_HEADS = 4
A_HEAD_DIM = 64
A_WIDTH = 2 * A_HEADS * A_HEAD_DIM
B_HEADS = 4
B_DIM = 128
B_WIDTH = B_HEADS * B_DIM
C_GROUPS = 4
C_CHUNK = 128
C_WIDTH = 512
D_WIDTH = 512
EVEN_IN = 4 * A_WIDTH + 5 * B_WIDTH
ODD_IN = 3 * C_WIDTH + 4 * D_WIDTH

CTX_GROUPS = (1, 2, 5, 6, 7)
REST_COL_GATE_A, REST_COL_Q, REST_COL_I, REST_COL_FF, REST_COL_FB, REST_COL_GATE_B = range(6)
CTX_COL_I, CTX_COL_FF, CTX_COL_FB = range(3)
QK_SLAB = 256

LANES = 128
HGRN_CHUNK = 128
HGRN_HEADS_PER_STEP = 4
HGRN_UNROLL = 4
HGRN_DIAG = 8
SCORE_BOUND = 32.0
EXP2_CLAMP = 115.0
LOG2E = math.log2(math.e)
Q_SCALE = A_HEAD_DIM ** -0.5 * LOG2E
TOKEN_BLOCK = 512
INPROJ_BLOCK = 1024
L1_TOKEN_BLOCK = 1024
ATTN_Q_TILE = 256
ATTN_Q_BLOCK = 512
SUBLANES = 8
BF16_ROWS_PER_VREG = 16
L1_SUB_BLOCKS = 4
COND_ROWS = 16
ADALN_COL_BLOCK = 512
VMEM_LIMIT = 56 * 1024 * 1024


def _cparams(*sem):
    return pltpu.CompilerParams(dimension_semantics=sem, vmem_limit_bytes=VMEM_LIMIT)


def _const_spec(shape):
    nd = len(shape)
    return pl.BlockSpec(shape, lambda *_: (0,) * nd, pipeline_mode=pl.Buffered(1))


def _sigmoid(t):
    return 0.5 + 0.5 * jnp.tanh(0.5 * t)


def _silu(t):
    return t * _sigmoid(t)


def _gelu(t):
    return 0.5 * t * (1.0 + lax.erf(t * (1.0 / math.sqrt(2.0))))


def _rms(t, gain):
    ms = jnp.mean(t * t, axis=-1, keepdims=True)
    return t * lax.rsqrt(ms + EPS) * gain


def _adaln_kernel(cond_ref, w_ref, b_ref, *o_refs):
    a = _silu(cond_ref[...])
    for layer, o_ref in enumerate(o_refs):
        o_ref[...] = jnp.dot(a, w_ref[layer], preferred_element_type=F32) + b_ref[layer]


def _adaln(cond, ada_w, ada_b):
    depth, d, n3 = ada_w.shape
    tn = ADALN_COL_BLOCK
    out = pl.BlockSpec((COND_ROWS, tn), lambda j: (0, j))
    return pl.pallas_call(
        _adaln_kernel,
        grid=(n3 // tn,),
        in_specs=[
            pl.BlockSpec((COND_ROWS, d), lambda j: (0, 0)),
            pl.BlockSpec((depth, d, tn), lambda j: (0, 0, j)),
            pl.BlockSpec((depth, 1, tn), lambda j: (0, 0, j)),
        ],
        out_specs=[out] * depth,
        out_shape=[jax.ShapeDtypeStruct((COND_ROWS, n3), F32)] * depth,
        compiler_params=_cparams("arbitrary"),
        name="adaln",
    )(cond, ada_w, ada_b.reshape(depth, 1, n3))


def _modulate(x, gain, mod, d):
    shift = mod[:, 0:d]
    scale = mod[:, d:2 * d]
    return _rms(x, gain) * (1.0 + scale) + shift


def _rope(t, cos, sin_signed):
    lanes = t.shape[1]
    lane = lax.broadcasted_iota(jnp.int32, t.shape, 1)
    first = (lane % A_HEAD_DIM) < (A_HEAD_DIM // 2)
    partner = jnp.where(first,
                        pltpu.roll(t, lanes - A_HEAD_DIM // 2, 1),
                        pltpu.roll(t, A_HEAD_DIM // 2, 1))
    return t * cos + partner * sin_signed


def _inproj_kernel(x_ref, mod_ref, g_ref, wa_ref, wr1_ref, wr2_ref, cos_ref, sin_ref, qkg_ref, bd_ref,
                   *out_refs,
                   mod_row, has_q, rope):
    d = x_ref.shape[-1]
    tb = x_ref.shape[1]
    row = pl.program_id(0) if mod_row is None else mod_row
    n_attn = (3 if has_q else 2) * A_WIDTH
    rest_ref = out_refs[-1]
    n_rest = rest_ref.shape[-1]
    slabs = [slice(half * QK_SLAB, (half + 1) * QK_SLAB) for half in range(A_WIDTH // QK_SLAB)]
    sub = min(tb, TOKEN_BLOCK)

    pending = []
    for r in [slice(s * sub, (s + 1) * sub) for s in range(tb // sub)]:
        xm = _modulate(x_ref[0, r, :], g_ref[...], mod_ref[pl.ds(row, 1), :], d).astype(BF16)
        attn = jnp.dot(xm, wa_ref[...], preferred_element_type=F32)
        rest_ref[0, r, 0:n_rest // 2] = jnp.dot(
            xm, wr1_ref[...], preferred_element_type=F32).astype(rest_ref.dtype)
        groups = [attn[:, g * A_WIDTH:(g + 1) * A_WIDTH] for g in range(n_attn // A_WIDTH)]
        v = groups.pop()
        mean_sq = [[jnp.dot((t[:, sl] * t[:, sl]).astype(BF16), bd_ref[...],
                            preferred_element_type=F32) for sl in slabs] for t in groups]
        pad = BF16_ROWS_PER_VREG
        ms_rows = sum(ms[0:pad, :] for per_group in mean_sq for ms in per_group)
        anchor = jnp.concatenate([ms_rows * 0.0] * (d // QK_SLAB), axis=1)
        xm_late = jnp.concatenate([(xm[0:pad, :].astype(F32) + anchor).astype(BF16), xm[pad:, :]],
                                  axis=0)
        rest_ref[0, r, n_rest // 2:] = jnp.dot(
            xm_late, wr2_ref[...], preferred_element_type=F32).astype(rest_ref.dtype)
        pending.append((r, groups, v, mean_sq))

    qk_refs = out_refs[:-2]
    vt_ref = out_refs[-2]
    first_gain = 0 if has_q else 1
    for r, groups, v, mean_sq in pending:
        for gi, (t, o_ref) in enumerate(zip(groups, qk_refs)):
            gain = qkg_ref[first_gain + gi:first_gain + gi + 1, :]
            scale = Q_SCALE if (has_q and gi == 0) else 1.0
            for sl, ms in zip(slabs, mean_sq[gi]):
                tn = t[:, sl] * lax.rsqrt(ms + EPS) * gain
                if rope:
                    tn = _rope(tn, cos_ref[r, :], sin_ref[r, :])
                o_ref[0, r, sl] = (tn * scale).astype(o_ref.dtype)
        for h in range(A_HEADS):
            sl = slice(h * LANES, (h + 1) * LANES)
            vt_ref[0, sl, r] = v[:, sl].T.astype(vt_ref.dtype)


def _inproj(x, mod, mod_row, gain, w, cos, sin_signed, qk_gain, bd, tb, has_q, rope):
    b, n, d = x.shape
    n_attn = (3 if has_q else 2) * A_WIDTH
    n_rest = w.shape[1] - n_attn
    bounds = (0, n_attn, n_attn + n_rest // 2, w.shape[1])
    w_parts, w_specs = [], []
    for lo, hi in zip(bounds[:-1], bounds[1:]):
        if lo % (hi - lo) == 0:
            w_parts.append(w)
            w_specs.append(pl.BlockSpec((d, hi - lo), lambda i, j, blk=lo // (hi - lo): (0, blk),
                                        pipeline_mode=pl.Buffered(1)))
        else:
            w_parts.append(w[:, lo:hi])
            w_specs.append(_const_spec((d, hi - lo)))
    tok = lambda width: pl.BlockSpec((1, tb, width), lambda i, j: (i, j, 0))
    qk_out = [tok(A_WIDTH)] * (2 if has_q else 1)
    qk_shape = [jax.ShapeDtypeStruct((b, n, A_WIDTH), BF16)] * (2 if has_q else 1)
    return pl.pallas_call(
        functools.partial(_inproj_kernel, mod_row=mod_row, has_q=has_q, rope=rope),
        grid=(b, n // tb),
        in_specs=[
            tok(d),
            _const_spec(mod.shape),
            _const_spec((1, d)),
            *w_specs,
            pl.BlockSpec((tb, QK_SLAB), lambda i, j: (j, 0)),
            pl.BlockSpec((tb, QK_SLAB), lambda i, j: (j, 0)),
            _const_spec(qk_gain.shape),
            _const_spec(bd.shape),
        ],
        out_specs=qk_out + [pl.BlockSpec((1, A_WIDTH, tb), lambda i, j: (i, 0, j)), tok(n_rest)],
        out_shape=qk_shape + [jax.ShapeDtypeStruct((b, A_WIDTH, n), BF16),
                              jax.ShapeDtypeStruct((b, n, n_rest), BF16)],
        compiler_params=_cparams("arbitrary", "arbitrary"),
        name="inproj_even",
    )(x, mod, gain, *w_parts, cos, sin_signed, qk_gain, bd)


def _attn_kernel(q_ref, kl_ref, kc_ref, vtl_ref, vtc_ref, g_ref, qkg_ref, subg_ref, lamp_ref,
                 o_ref, *, lam_init):
    heads = [slice(h * LANES, (h + 1) * LANES) for h in range(A_HEADS)]
    lane = lax.broadcasted_iota(jnp.int32, (1, LANES), 1)
    nt = (((1,), (1,)), ((), ()))

    lp = lamp_ref[...]
    lam = (jnp.exp(jnp.sum(lp[0:1] * lp[1:2], axis=-1, keepdims=True))
           - jnp.exp(jnp.sum(lp[2:3] * lp[3:4], axis=-1, keepdims=True)) + lam_init)
    score_bound = ((A_HEAD_DIM * Q_SCALE) * jnp.max(jnp.abs(qkg_ref[0:1, :]))
                   * jnp.max(jnp.abs(qkg_ref[1:2, :])))

    n_sub = q_ref.shape[1] // ATTN_Q_TILE
    items = [(slice(qi * ATTN_Q_TILE, (qi + 1) * ATTN_Q_TILE), h)
             for qi in range(n_sub) for h in range(A_HEADS)]

    def scores(item):
        rows, h = item
        out = []
        for m in range(2):
            qm = jnp.where((lane // A_HEAD_DIM) == m, q_ref[0, rows, heads[h]], 0).astype(BF16)
            out.append([lax.dot_general(k_ref[0, :, heads[h]], qm, nt, preferred_element_type=F32)
                        for k_ref in (kc_ref, kl_ref)])
        return out

    def run_heads(shift):
        s_next = scores(items[0])
        for idx, (rows, h) in enumerate(items):
            sl = heads[h]
            s_maps = s_next
            if idx + 1 < len(items):
                s_next = scores(items[idx + 1])
            probs = []
            for s_parts in s_maps:
                if shift:
                    top = functools.reduce(jnp.maximum,
                                           [jnp.max(s, axis=0, keepdims=True) for s in s_parts])
                    s_parts = [s - top for s in s_parts]
                p_parts = [jnp.exp2(s) for s in s_parts]
                probs.append((p_parts, sum(jnp.sum(p, axis=0, keepdims=True) for p in p_parts)))
            (p0, l0), (p1, l1) = probs
            ratio = lam * l0 * (1.0 / l1)
            ot = sum(jnp.dot(vt_ref[0, sl, :], (pa - pb * ratio).astype(BF16),
                             preferred_element_type=F32)
                     for vt_ref, pa, pb in zip((vtc_ref, vtl_ref), p0, p1)) * (1.0 / l0)
            ms = jnp.mean(ot * ot, axis=0, keepdims=True)
            on = (ot * lax.rsqrt(ms + EPS)).T * (subg_ref[...] * (1.0 - lam_init))
            o_ref[0, rows, sl] = (on * _silu(g_ref[0, rows, sl].astype(F32))).astype(o_ref.dtype)

    no_shift_ok = score_bound <= SCORE_BOUND
    pl.when(no_shift_ok)(functools.partial(run_heads, False))
    pl.when(jnp.logical_not(no_shift_ok))(functools.partial(run_heads, True))


def _attention(q, k_lat, k_ctx, vt_lat, vt_ctx, rest, qk_gain, subln_g, lam_p, lam_init, tq):
    b, n, w = q.shape
    n_ctx = k_ctx.shape[1]
    return pl.pallas_call(
        functools.partial(_attn_kernel, lam_init=lam_init),
        grid=(b, n // tq),
        in_specs=[
            pl.BlockSpec((1, tq, w), lambda i, j: (i, j, 0)),
            pl.BlockSpec((1, n, w), lambda i, j: (i, 0, 0)),
            pl.BlockSpec((1, n_ctx, w), lambda i, j: (i, 0, 0)),
            pl.BlockSpec((1, w, n), lambda i, j: (i, 0, 0)),
            pl.BlockSpec((1, w, n_ctx), lambda i, j: (0, 0, i)),
            pl.BlockSpec((1, tq, w), lambda i, j: (i, j, REST_COL_GATE_A)),
            _const_spec(qk_gain.shape),
            _const_spec((1, LANES)),
            _const_spec((4, A_HEAD_DIM)),
        ],
        out_specs=pl.BlockSpec((1, tq, w), lambda i, j: (i, j, 0)),
        out_shape=jax.ShapeDtypeStruct((b, n, w), BF16),
        compiler_params=_cparams("arbitrary", "arbitrary"),
        name="diff_attn",
    )(q, k_lat, k_ctx, vt_lat, vt_ctx, rest, qk_gain, subln_g, lam_p)


def _split_bf16(t):
    hi = t.astype(BF16)
    return hi, (t - hi.astype(F32)).astype(BF16)


def _block_ref(g, block, row):
    c, w = g.shape
    g3 = g.reshape(c // block, block, w)
    return jnp.broadcast_to(g3[:, row:row + 1, :], g3.shape).reshape(c, w)


def _hgrn_tables(c):
    t = np.arange(c)[:, None]
    s = np.arange(c)[None, :]
    lvl = np.zeros((c, c), np.int32)
    lvl[(t // HGRN_DIAG == s // HGRN_DIAG) & (s <= t)] = 1
    b, k = HGRN_DIAG, 2
    while b < c:
        lvl[(t // b == s // b + 1) & ((s // b) % 2 == 0)] = k
        b, k = 2 * b, k + 1
    tri = (s <= t).astype(np.float32)
    return jnp.asarray(np.stack([tri, tri.T]), BF16), jnp.asarray(np.stack([lvl, lvl.T]))


def _hgrn_chunks(chains, tri_ref, lvl_ref, want_out):
    nt = (((1,), (1,)), ((), ()))
    tn = (((0,), (0,)), ((), ()))
    n = len(chains)
    c = chains[0][2].shape[0]

    kk, parts = [], []
    for (_, _, f_raw, lb, _, _) in chains:
        f = lb + (1.0 - lb) * _sigmoid(f_raw)
        kk.append(1.0 - f)
        parts.append(_split_bf16(jnp.log(f) * LOG2E))
    cum = [sum(jnp.dot(tri_ref[ch[5]], p, preferred_element_type=F32) for p in parts[i])
           for i, ch in enumerate(chains)]
    edge = [cum[i][0:1, :] if ch[5] else cum[i][c - 1:c, :] for i, ch in enumerate(chains)]

    outs = [None] * n
    if want_out:
        a = []
        for i, (q, _, _, _, _, d) in enumerate(chains):
            ref = _block_ref(cum[i], HGRN_DIAG, HGRN_DIAG // 2)
            qd = (q * jnp.exp2(jnp.minimum(cum[i] - ref, EXP2_CLAMP))).astype(BF16)
            kd = (kk[i] * jnp.exp2(jnp.minimum(ref - cum[i], EXP2_CLAMP))).astype(BF16)
            a.append(jnp.where(lvl_ref[d] == 1,
                               lax.dot_general(qd, kd, nt, preferred_element_type=F32), 0.0))
        b, k = HGRN_DIAG, 2
        while b < c:
            for i, (q, _, _, _, _, d) in enumerate(chains):
                ref = _block_ref(cum[i], 2 * b, b if d else b - 1)
                decay = jnp.exp2(cum[i] - ref)
                ql = (q * decay).astype(BF16)
                kl = (kk[i] * jnp.exp2(ref - cum[i])).astype(BF16)
                a[i] = jnp.where(lvl_ref[d] == k,
                                 lax.dot_general(ql, kl, nt, preferred_element_type=F32), a[i])
            b, k = 2 * b, k + 1
        for i, (q, v, _, _, st, _) in enumerate(chains):
            o = jnp.dot(a[i].astype(BF16), v, preferred_element_type=F32)
            outs[i] = o + lax.dot_general((q * jnp.exp2(cum[i])).astype(BF16), st.astype(BF16), nt,
                                          preferred_element_type=F32)

    sts = []
    for i, (_, v, _, _, st, _) in enumerate(chains):
        kg = (kk[i] * jnp.exp2(edge[i] - cum[i])).astype(BF16)
        upd = lax.dot_general(v, kg, tn, preferred_element_type=F32)
        sts.append(st * jnp.exp2(edge[i]) + upd)
    return outs, sts


def _hgrn_kernel(q_ref, i_ref, ff_ref, fb_ref, g_ref, ic_ref, ffc_ref, fbc_ref,
                 lbl_ref, ng_ref, tri_ref, lvl_ref, o_ref, acc_scr):
    c = HGRN_CHUNK
    heads = q_ref.shape[2] // LANES
    nc_lat = q_ref.shape[1] // c
    nc_ctx = ic_ref.shape[1] // c
    f_lat = (ff_ref, fb_ref)
    f_ctx = (ffc_ref, fbc_ref)

    def lower_bound(direction, sl):
        logits = [lbl_ref[direction, l, :, sl] for l in range(lbl_ref.shape[1])]
        top = functools.reduce(jnp.maximum, logits)
        e = [jnp.exp(t - top) for t in logits]
        return e[0] / sum(e)

    lanes = [slice(h * LANES, (h + 1) * LANES) for h in range(heads)]
    lbs = [[lower_bound(d, sl) for sl in lanes] for d in (0, 1)]

    def rows(i):
        return pl.ds(pl.multiple_of(i * c, c), c)

    def ctx_step(j, sts):
        chains = []
        for d in (0, 1):
            r = rows(nc_ctx - 1 - j if d else j)
            for h, sl in enumerate(lanes):
                chains.append((None, ic_ref[0, r, sl], f_ctx[d][0, r, sl].astype(F32),
                               lbs[d][h], sts[d * heads + h], d))
        return tuple(_hgrn_chunks(chains, tri_ref, lvl_ref, False)[1])

    def lat_step(j, sts, second_visit):
        chains, where = [], []
        for d in (0, 1):
            r = rows(nc_lat - 1 - j if d else j)
            for h, sl in enumerate(lanes):
                chains.append((q_ref[0, r, sl].astype(F32), i_ref[0, r, sl],
                               f_lat[d][0, r, sl].astype(F32), lbs[d][h], sts[d * heads + h], d))
                where.append((r, sl))
        outs, new = _hgrn_chunks(chains, tri_ref, lvl_ref, True)
        for o, (r, sl) in zip(outs, where):
            if second_visit:
                y = _rms(acc_scr[r, sl] + o, ng_ref[...]) * _silu(g_ref[0, r, sl].astype(F32))
                o_ref[0, r, sl] = y.astype(o_ref.dtype)
            else:
                acc_scr[r, sl] = o
        return tuple(new)

    sts = tuple(jnp.zeros((B_DIM, B_DIM), F32) for _ in range(2 * heads))
    sts = lax.fori_loop(0, nc_ctx, ctx_step, sts, unroll=True)
    sts = lax.fori_loop(0, nc_lat // 2, functools.partial(lat_step, second_visit=False), sts,
                        unroll=HGRN_UNROLL)
    lax.fori_loop(nc_lat // 2, nc_lat, functools.partial(lat_step, second_visit=True), sts,
                  unroll=HGRN_UNROLL)


def _hgrn(rest_x, rest_c, lb_logits, norm_g, heads_per_step):
    b, n, _ = rest_x.shape
    n_ctx = rest_c.shape[1]
    assert (n // HGRN_CHUNK) % 2 == 0 and B_HEADS % heads_per_step == 0
    w = heads_per_step * LANES
    steps = B_HEADS // heads_per_step

    def xs(group):
        return pl.BlockSpec((1, n, w), lambda i, h: (i, 0, group * steps + h))

    def cs(group):
        return pl.BlockSpec((1, n_ctx, w), lambda i, h: (i, 0, group * steps + h))

    n_layers = lb_logits.shape[1]
    tri, lvl = _hgrn_tables(HGRN_CHUNK)
    return pl.pallas_call(
        _hgrn_kernel,
        grid=(b, steps),
        in_specs=[
            xs(REST_COL_Q), xs(REST_COL_I), xs(REST_COL_FF), xs(REST_COL_FB), xs(REST_COL_GATE_B),
            cs(CTX_COL_I), cs(CTX_COL_FF), cs(CTX_COL_FB),
            pl.BlockSpec((2, n_layers, 1, w), lambda i, h: (0, 0, 0, h)),
            _const_spec((1, LANES)),
            _const_spec(tri.shape),
            _const_spec(lvl.shape),
        ],
        out_specs=pl.BlockSpec((1, n, w), lambda i, h: (i, 0, h)),
        out_shape=jax.ShapeDtypeStruct((b, n, B_WIDTH), BF16),
        scratch_shapes=[pltpu.VMEM((n, w), F32)],
        compiler_params=_cparams("arbitrary", "arbitrary"),
        name="hgrn2",
    )(rest_x, rest_x, rest_x, rest_x, rest_x, rest_c, rest_c, rest_c,
      lb_logits.reshape(2, n_layers, 1, B_WIDTH), norm_g, tri, lvl)


def _layer1_input(ya, yb, x, gate0, mod1, gain1, wo0_ref):
    d = x.shape[-1]
    half = ya.shape[-1]
    upd = (jnp.dot(ya, wo0_ref[0:half, :], preferred_element_type=F32)
           + jnp.dot(yb, wo0_ref[half:, :], preferred_element_type=F32))
    x1 = x + gate0 * upd
    return x1, _modulate(x1, gain1, mod1, d).astype(BF16)


def _edge_kernel(ya_ref, yb_ref, x_ref, mod0_ref, mod1_ref, g1_ref, wo0_ref, wi_ref, z_ref,
                 *, rows_per_batch):
    d = x_ref.shape[-1]

    def per_row(ref, lo, hi):
        return jnp.concatenate([jnp.broadcast_to(ref[i:i + 1, lo:hi], (rows_per_batch, hi - lo))
                                for i in range(x_ref.shape[0] // rows_per_batch)], axis=0)

    _, xm = _layer1_input(ya_ref[...], yb_ref[...], x_ref[...], per_row(mod0_ref, 2 * d, 3 * d),
                          per_row(mod1_ref, 0, 2 * d), g1_ref[...], wo0_ref)
    p = jnp.dot(xm, wi_ref[...], preferred_element_type=F32)
    z_ref[...] = p[:, :D_WIDTH] * p[:, D_WIDTH:]


def _block_edges(t, tb):
    b, n, w = t.shape
    te = t.reshape(b, n // tb, tb, w)
    return jnp.concatenate([te[:, :, :SUBLANES], te[:, :, tb - SUBLANES:]], axis=2).reshape(-1, w)


def _edge_z(ya, yb, x, mod0, mod1, gain1, w_out0, w_in1, tb):
    b, n, d = x.shape
    rows = (n // tb) * 2 * SUBLANES
    cg_start = 3 * C_WIDTH + D_WIDTH
    assert cg_start % (2 * D_WIDTH) == 0
    cg_blk = cg_start // (2 * D_WIDTH)
    full = lambda arr: _const_spec(arr.shape)
    xe, yae, ybe = _block_edges(x, tb), _block_edges(ya, tb), _block_edges(yb, tb)
    return pl.pallas_call(
        functools.partial(_edge_kernel, rows_per_batch=rows),
        grid=(1,),
        in_specs=[full(yae), full(ybe), full(xe), full(mod0), full(mod1), _const_spec((1, d)),
                  full(w_out0),
                  pl.BlockSpec((d, 2 * D_WIDTH), lambda i: (0, cg_blk), pipeline_mode=pl.Buffered(1))],
        out_specs=pl.BlockSpec((b * rows, D_WIDTH), lambda i: (0, 0)),
        out_shape=jax.ShapeDtypeStruct((b * rows, D_WIDTH), F32),
        compiler_params=_cparams("arbitrary"),
        name="conv_edge_rows",
    )(yae, ybe, xe, mod0, mod1, gain1, w_out0, w_in1).reshape(b, rows, D_WIDTH)


def _layer1_kernel(ya_ref, yb_ref, x_ref, ze_ref, mod0_ref, mod1_ref, g1_ref, wo0_ref, wi_ref,
                   vg_ref, ws_ref, bs_ref, cw_ref, wo1_ref, o_ref):
    d = x_ref.shape[-1]
    tb = x_ref.shape[1]
    j = pl.program_id(1)
    last_j = pl.num_programs(1) - 1
    mod1 = mod1_ref[pl.ds(pl.program_id(0), 1), :]
    gate0 = mod0_ref[pl.ds(pl.program_id(0), 1), 2 * d:]
    gate1 = mod1[:, 2 * d:]
    sub = tb // L1_SUB_BLOCKS
    subs = [slice(s * sub, (s + 1) * sub) for s in range(L1_SUB_BLOCKS)]
    col = lambda p, k: p[:, k * C_WIDTH:(k + 1) * C_WIDTH]
    n_gmlp = 3 * C_WIDTH

    x1s, xms = [], []
    for r in subs:
        x1, xm = _layer1_input(ya_ref[0, r, :], yb_ref[0, r, :], x_ref[0, r, :], gate0,
                               mod1, g1_ref[...], wo0_ref)
        x1s.append(x1)
        xms.append(xm)
    pgs = [jnp.dot(xm, wi_ref[:, 0:n_gmlp], preferred_element_type=F32) for xm in xms]
    pcs = [jnp.dot(xm, wi_ref[:, n_gmlp:], preferred_element_type=F32) for xm in xms]

    upd_c = []
    for pg in pgs:
        u = _gelu(col(pg, 0))
        vn = _rms(_gelu(col(pg, 1)), vg_ref[...]).astype(BF16)
        chunks = []
        for ci in range(sub // C_CHUNK):
            cr = slice(ci * C_CHUNK, (ci + 1) * C_CHUNK)
            groups = []
            for g in range(C_GROUPS):
                gl = slice(g * LANES, (g + 1) * LANES)
                groups.append(jnp.dot(ws_ref[g], vn[cr, gl], preferred_element_type=F32) + bs_ref[g])
            chunks.append(jnp.concatenate(groups, axis=1))
        o_c = u * jnp.concatenate(chunks, axis=0) * _silu(col(pg, 2))
        upd_c.append(jnp.dot(o_c.astype(BF16), wo1_ref[0:C_WIDTH, :], preferred_element_type=F32))

    z = jnp.concatenate([col(pc, 1) * col(pc, 2) for pc in pcs], axis=0)
    grp = 2 * SUBLANES
    prev_grp = ze_ref[0, pl.ds(pl.multiple_of(jnp.maximum(j - 1, 0) * grp + SUBLANES, SUBLANES),
                               SUBLANES), :]
    next_grp = ze_ref[0, pl.ds(pl.multiple_of(jnp.minimum(j + 1, last_j) * grp, SUBLANES),
                               SUBLANES), :]
    z_prev_row = jnp.where(j == 0, 0.0, prev_grp[SUBLANES - 1:, :])
    z_next_row = jnp.where(j == last_j, 0.0, next_grp[0:1, :])
    rowi = lax.broadcasted_iota(jnp.int32, z.shape, 0)
    z_prev = jnp.where(rowi == 0, z_prev_row, pltpu.roll(z, 1, 0))
    z_next = jnp.where(rowi == tb - 1, z_next_row, pltpu.roll(z, tb - 1, 0))
    conv = cw_ref[0:1, :] * z_prev + cw_ref[1:2, :] * z + cw_ref[2:3, :] * z_next

    for r, x1, pc, uc in zip(subs, x1s, pcs, upd_c):
        o_d = col(pc, 0) * conv[r, :] * _silu(col(pc, 3))
        upd = uc + jnp.dot(o_d.astype(BF16), wo1_ref[C_WIDTH:, :], preferred_element_type=F32)
        o_ref[0, r, :] = x1 + gate1 * upd


def _layer1(ya, yb, x, ze, mod0, mod1, gain1, w_out0, w_in1, v_gain, w_s, b_s, conv_w, w_out1, tb):
    b, n, d = x.shape
    half = ya.shape[-1]
    tok = lambda width: pl.BlockSpec((1, tb, width), lambda i, j: (i, j, 0))
    modspec = _const_spec(mod0.shape)
    return pl.pallas_call(
        _layer1_kernel,
        grid=(b, n // tb),
        in_specs=[tok(half), tok(half), tok(d),
                  pl.BlockSpec((1,) + ze.shape[1:], lambda i, j: (i, 0, 0)),
                  modspec, modspec, _const_spec((1, d)),
                  _const_spec(w_out0.shape), _const_spec(w_in1.shape),
                  _const_spec((1, C_WIDTH)),
                  _const_spec((C_GROUPS, C_CHUNK, C_CHUNK)),
                  _const_spec((C_GROUPS, C_CHUNK, LANES)),
                  _const_spec((3, D_WIDTH)),
                  _const_spec(w_out1.shape)],
        out_specs=tok(d),
        out_shape=jax.ShapeDtypeStruct((b, n, d), F32),
        compiler_params=_cparams("arbitrary", "arbitrary"),
        name="outproj_even_layer_odd",
    )(ya, yb, x, ze, mod0, mod1, gain1, w_out0, w_in1, v_gain, w_s, b_s, conv_w, w_out1)


def _rope_tables(n):
    rows_ = n // GRID_W
    row = np.repeat(np.arange(rows_, dtype=np.float64), GRID_W)
    col = np.tile(np.arange(GRID_W, dtype=np.float64), rows_)
    n_freq = A_HEAD_DIM // 4
    inv = ROPE_THETA ** (-np.arange(n_freq, dtype=np.float64) / n_freq)
    ang = np.concatenate([row[:, None] * inv, col[:, None] * inv], axis=-1)
    cos, sin = np.cos(ang), np.sin(ang)
    reps = QK_SLAB // A_HEAD_DIM
    return (jnp.asarray(np.tile(np.concatenate([cos, cos], axis=-1), (1, reps)), F32),
            jnp.asarray(np.tile(np.concatenate([-sin, sin], axis=-1), (1, reps)), F32))


def kernel(x, c, ctx, c_ctx, norm_gain, ada_w, ada_b, even_w_in, even_w_out, attn_qk_gain,
           attn_lambda, attn_subln_gain, hgrn_lb_logits, hgrn_norm_gain, odd_w_in, odd_w_out,
           gmlp_v_gain, gmlp_w_s, gmlp_b_s, conv_w):
    b, n, d = x.shape
    assert b + 1 <= COND_ROWS and n % 512 == 0 and ctx.shape[1] % HGRN_CHUNK == 0
    assert norm_gain.shape[0] == 2, "two-layer block: one even layer then one odd layer"

    cond = jnp.concatenate([c, c_ctx[None, :], jnp.zeros((COND_ROWS - b - 1, d), F32)], axis=0)
    mod0, mod1 = _adaln(cond, ada_w, ada_b)

    w_in0 = even_w_in[0].astype(BF16)
    gain0 = norm_gain[0].reshape(1, d)
    cos, sin_signed = _rope_tables(n)
    qk_gain = jnp.tile(attn_qk_gain[0], (1, QK_SLAB // A_HEAD_DIM))
    blk = np.arange(QK_SLAB) // A_HEAD_DIM
    bd = jnp.asarray(np.where(blk[:, None] == blk[None, :], 1.0 / A_HEAD_DIM, 0.0), BF16)
    q, k_lat, vt_lat, rest_x = _inproj(x, mod0, None, gain0, w_in0, cos, sin_signed, qk_gain, bd,
                                       INPROJ_BLOCK, True, True)
    w_ctx = jnp.concatenate([w_in0[:, g * A_WIDTH:(g + 1) * A_WIDTH] for g in CTX_GROUPS], axis=1)
    n_ctx = ctx.shape[1]
    k_ctx, vt_ctx, rest_c = _inproj(ctx.reshape(1, b * n_ctx, d), mod0, b, gain0, w_ctx, cos, sin_signed,
                                    qk_gain, bd, min(INPROJ_BLOCK, b * n_ctx), False, False)
    k_ctx = k_ctx.reshape(b, n_ctx, A_WIDTH)
    rest_c = rest_c.reshape(b, n_ctx, rest_c.shape[-1])

    lam_init = 0.8 - 0.6 * math.exp(-0.3 * 0)
    ya = _attention(q, k_lat, k_ctx, vt_lat, vt_ctx, rest_x, qk_gain,
                    attn_subln_gain[0].reshape(1, LANES), attn_lambda[0], lam_init, ATTN_Q_BLOCK)
    yb = _hgrn(rest_x, rest_c, hgrn_lb_logits, hgrn_norm_gain[0].reshape(1, LANES),
               HGRN_HEADS_PER_STEP)

    gain1 = norm_gain[1].reshape(1, d)
    w_out0, w_in1 = even_w_out[0].astype(BF16), odd_w_in[0].astype(BF16)
    ze = _edge_z(ya, yb, x, mod0, mod1, gain1, w_out0, w_in1, L1_TOKEN_BLOCK)
    b_s = jnp.broadcast_to(gmlp_b_s[0][:, :, None], (C_GROUPS, C_CHUNK, LANES))
    return _layer1(ya, yb, x, ze, mod0, mod1, gain1, w_out0, w_in1,
                   gmlp_v_gain[0].reshape(1, C_WIDTH), gmlp_w_s[0].astype(BF16), b_s, conv_w[0],
                   odd_w_out[0].astype(BF16), L1_TOKEN_BLOCK)
```

```python
import functools
import math

import jax
import jax.numpy as jnp
import numpy as np
from jax import lax
from jax.experimental import pallas as pl
from jax.experimental.pallas import tpu as pltpu

F32 = jnp.float32
BF16 = jnp.bfloat16

EPS = 1e-6
GRID_W = 64
ROPE_THETA = 10000.0
A_HEADS = 4
A_HEAD_DIM = 64
A_WIDTH = 2 * A_HEADS * A_HEAD_DIM
B_HEADS = 4
B_DIM = 128
B_WIDTH = B_HEADS * B_DIM
C_GROUPS = 4
C_CHUNK = 128
C_WIDTH = 512
D_WIDTH = 512
EVEN_IN = 4 * A_WIDTH + 5 * B_WIDTH
ODD_IN = 3 * C_WIDTH + 4 * D_WIDTH

CTX_GROUPS = (1, 2, 5, 6, 7)
REST_COL_GATE_A, REST_COL_Q, REST_COL_I, REST_COL_FF, REST_COL_FB, REST_COL_GATE_B = range(6)
CTX_COL_I, CTX_COL_FF, CTX_COL_FB = range(3)
QK_SLAB = 256

LANES = 128
HGRN_CHUNK = 128
HGRN_HEADS_PER_STEP = 4
HGRN_GROUP = 4
HGRN_UNROLL = 4
HGRN_DIAG = 8
SCORE_BOUND = 32.0
EXP2_CLAMP = 115.0
LOG2E = math.log2(math.e)
Q_SCALE = A_HEAD_DIM ** -0.5 * LOG2E
TOKEN_BLOCK = 512
INPROJ_BLOCK = 1024
L1_TOKEN_BLOCK = 1024
ATTN_Q_TILE = 256
ATTN_Q_BLOCK = 512
ATTN_LOOKAHEAD = 1
SUBLANES = 8
BF16_ROWS_PER_VREG = 16
L1_SUB_BLOCKS = 4
COND_ROWS = 16
ADALN_COL_BLOCK = 512
VMEM_LIMIT = 56 * 1024 * 1024


def _cparams(*sem):
    return pltpu.CompilerParams(dimension_semantics=sem, vmem_limit_bytes=VMEM_LIMIT)


def _const_spec(shape):
    nd = len(shape)
    return pl.BlockSpec(shape, lambda *_: (0,) * nd, pipeline_mode=pl.Buffered(1))


def _sigmoid(t):
    return 0.5 + 0.5 * jnp.tanh(0.5 * t)


def _silu(t):
    return t * _sigmoid(t)


def _gelu(t):
    return 0.5 * t * (1.0 + lax.erf(t * (1.0 / math.sqrt(2.0))))


def _rms(t, gain):
    ms = jnp.mean(t * t, axis=-1, keepdims=True)
    return t * lax.rsqrt(ms + EPS) * gain


def _adaln_kernel(cond_ref, w_ref, b_ref, *o_refs):
    a = _silu(cond_ref[...])
    for layer, o_ref in enumerate(o_refs):
        o_ref[...] = jnp.dot(a, w_ref[layer], preferred_element_type=F32) + b_ref[layer]


def _adaln(cond, ada_w, ada_b):
    depth, d, n3 = ada_w.shape
    tn = ADALN_COL_BLOCK
    out = pl.BlockSpec((COND_ROWS, tn), lambda j: (0, j))
    return pl.pallas_call(
        _adaln_kernel,
        grid=(n3 // tn,),
        in_specs=[
            pl.BlockSpec((COND_ROWS, d), lambda j: (0, 0)),
            pl.BlockSpec((depth, d, tn), lambda j: (0, 0, j)),
            pl.BlockSpec((depth, 1, tn), lambda j: (0, 0, j)),
        ],
        out_specs=[out] * depth,
        out_shape=[jax.ShapeDtypeStruct((COND_ROWS, n3), F32)] * depth,
        compiler_params=_cparams("arbitrary"),
        name="adaln",
    )(cond, ada_w, ada_b.reshape(depth, 1, n3))


def _modulate(x, gain, mod, d):
    shift = mod[:, 0:d]
    scale = mod[:, d:2 * d]
    return _rms(x, gain) * (1.0 + scale) + shift


def _rope(t, cos, sin_signed):
    lanes = t.shape[1]
    lane = lax.broadcasted_iota(jnp.int32, t.shape, 1)
    first = (lane % A_HEAD_DIM) < (A_HEAD_DIM // 2)
    partner = jnp.where(first,
                        pltpu.roll(t, lanes - A_HEAD_DIM // 2, 1),
                        pltpu.roll(t, A_HEAD_DIM // 2, 1))
    return t * cos + partner * sin_signed


def _inproj_kernel(x_ref, mod_ref, g_ref, wa_ref, wr1_ref, wr2_ref, cos_ref, sin_ref, qkg_ref, bd_ref,
                   *out_refs,
                   mod_row, has_q, rope):
    d = x_ref.shape[-1]
    tb = x_ref.shape[1]
    row = pl.program_id(0) if mod_row is None else mod_row
    n_attn = (3 if has_q else 2) * A_WIDTH
    rest_ref = out_refs[-1]
    n_rest = rest_ref.shape[-1]
    slabs = [slice(half * QK_SLAB, (half + 1) * QK_SLAB) for half in range(A_WIDTH // QK_SLAB)]
    sub = min(tb, TOKEN_BLOCK)

    pending = []
    for r in [slice(s * sub, (s + 1) * sub) for s in range(tb // sub)]:
        xm = _modulate(x_ref[0, r, :], g_ref[...], mod_ref[pl.ds(row, 1), :], d).astype(BF16)
        attn = jnp.dot(xm, wa_ref[...], preferred_element_type=F32)
        rest_ref[0, r, 0:n_rest // 2] = jnp.dot(
            xm, wr1_ref[...], preferred_element_type=F32).astype(rest_ref.dtype)
        groups = [attn[:, g * A_WIDTH:(g + 1) * A_WIDTH] for g in range(n_attn // A_WIDTH)]
        v = groups.pop()
        mean_sq = [[jnp.dot((t[:, sl] * t[:, sl]).astype(BF16), bd_ref[...],
                            preferred_element_type=F32) for sl in slabs] for t in groups]
        pad = BF16_ROWS_PER_VREG
        ms_rows = sum(ms[0:pad, :] for per_group in mean_sq for ms in per_group)
        anchor = jnp.concatenate([ms_rows * 0.0] * (d // QK_SLAB), axis=1)
        xm_late = jnp.concatenate([(xm[0:pad, :].astype(F32) + anchor).astype(BF16), xm[pad:, :]],
                                  axis=0)
        rest_ref[0, r, n_rest // 2:] = jnp.dot(
            xm_late, wr2_ref[...], preferred_element_type=F32).astype(rest_ref.dtype)
        pending.append((r, groups, v, mean_sq))

    qk_refs = out_refs[:-2]
    vt_ref = out_refs[-2]
    first_gain = 0 if has_q else 1
    for r, groups, v, mean_sq in pending:
        for gi, (t, o_ref) in enumerate(zip(groups, qk_refs)):
            gain = qkg_ref[first_gain + gi:first_gain + gi + 1, :]
            scale = Q_SCALE if (has_q and gi == 0) else 1.0
            for sl, ms in zip(slabs, mean_sq[gi]):
                tn = t[:, sl] * lax.rsqrt(ms + EPS) * gain
                if rope:
                    tn = _rope(tn, cos_ref[r, :], sin_ref[r, :])
                o_ref[0, r, sl] = (tn * scale).astype(o_ref.dtype)
        for h in range(A_HEADS):
            sl = slice(h * LANES, (h + 1) * LANES)
            vt_ref[0, sl, r] = v[:, sl].T.astype(vt_ref.dtype)


def _inproj(x, mod, mod_row, gain, w, cos, sin_signed, qk_gain, bd, tb, has_q, rope):
    b, n, d = x.shape
    n_attn = (3 if has_q else 2) * A_WIDTH
    n_rest = w.shape[1] - n_attn
    bounds = (0, n_attn, n_attn + n_rest // 2, w.shape[1])
    w_parts, w_specs = [], []
    for lo, hi in zip(bounds[:-1], bounds[1:]):
        if lo % (hi - lo) == 0:
            w_parts.append(w)
            w_specs.append(pl.BlockSpec((d, hi - lo), lambda i, j, blk=lo // (hi - lo): (0, blk),
                                        pipeline_mode=pl.Buffered(1)))
        else:
            w_parts.append(w[:, lo:hi])
            w_specs.append(_const_spec((d, hi - lo)))
    tok = lambda width: pl.BlockSpec((1, tb, width), lambda i, j: (i, j, 0))
    qk_out = [tok(A_WIDTH)] * (2 if has_q else 1)
    qk_shape = [jax.ShapeDtypeStruct((b, n, A_WIDTH), BF16)] * (2 if has_q else 1)
    return pl.pallas_call(
        functools.partial(_inproj_kernel, mod_row=mod_row, has_q=has_q, rope=rope),
        grid=(b, n // tb),
        in_specs=[
            tok(d),
            _const_spec(mod.shape),
            _const_spec((1, d)),
            *w_specs,
            pl.BlockSpec((tb, QK_SLAB), lambda i, j: (j, 0)),
            pl.BlockSpec((tb, QK_SLAB), lambda i, j: (j, 0)),
            _const_spec(qk_gain.shape),
            _const_spec(bd.shape),
        ],
        out_specs=qk_out + [pl.BlockSpec((1, A_WIDTH, tb), lambda i, j: (i, 0, j)), tok(n_rest)],
        out_shape=qk_shape + [jax.ShapeDtypeStruct((b, A_WIDTH, n), BF16),
                              jax.ShapeDtypeStruct((b, n, n_rest), BF16)],
        compiler_params=_cparams("arbitrary", "arbitrary"),
        name="inproj_even",
    )(x, mod, gain, *w_parts, cos, sin_signed, qk_gain, bd)


def _attn_kernel(q_ref, kl_ref, kc_ref, vtl_ref, vtc_ref, g_ref, qkg_ref, subg_ref, lamp_ref,
                 o_ref, *, lam_init):
    heads = [slice(h * LANES, (h + 1) * LANES) for h in range(A_HEADS)]
    lane = lax.broadcasted_iota(jnp.int32, (1, LANES), 1)
    nt = (((1,), (1,)), ((), ()))

    lp = lamp_ref[...]
    lam = (jnp.exp(jnp.sum(lp[0:1] * lp[1:2], axis=-1, keepdims=True))
           - jnp.exp(jnp.sum(lp[2:3] * lp[3:4], axis=-1, keepdims=True)) + lam_init)
    score_bound = ((A_HEAD_DIM * Q_SCALE) * jnp.max(jnp.abs(qkg_ref[0:1, :]))
                   * jnp.max(jnp.abs(qkg_ref[1:2, :])))

    n_sub = q_ref.shape[1] // ATTN_Q_TILE
    items = [(slice(qi * ATTN_Q_TILE, (qi + 1) * ATTN_Q_TILE), h)
             for qi in range(n_sub) for h in range(A_HEADS)]

    def scores(item):
        rows, h = item
        out = []
        for m in range(2):
            qm = jnp.where((lane // A_HEAD_DIM) == m, q_ref[0, rows, heads[h]], 0).astype(BF16)
            out.append([lax.dot_general(k_ref[0, :, heads[h]], qm, nt, preferred_element_type=F32)
                        for k_ref in (kc_ref, kl_ref)])
        return out

    def run_heads(shift):
        queued = [scores(item) for item in items[:ATTN_LOOKAHEAD]]
        for idx, (rows, h) in enumerate(items):
            sl = heads[h]
            s_maps = queued.pop(0)
            if idx + ATTN_LOOKAHEAD < len(items):
                queued.append(scores(items[idx + ATTN_LOOKAHEAD]))
            probs = []
            for s_parts in s_maps:
                if shift:
                    top = functools.reduce(jnp.maximum,
                                           [jnp.max(s, axis=0, keepdims=True) for s in s_parts])
                    s_parts = [s - top for s in s_parts]
                p_parts = [jnp.exp2(s) for s in s_parts]
                probs.append((p_parts, sum(jnp.sum(p, axis=0, keepdims=True) for p in p_parts)))
            (p0, l0), (p1, l1) = probs
            ratio = lam * l0 * (1.0 / l1)
            ot = sum(jnp.dot(vt_ref[0, sl, :], (pa - pb * ratio).astype(BF16),
                             preferred_element_type=F32)
                     for vt_ref, pa, pb in zip((vtc_ref, vtl_ref), p0, p1)) * (1.0 / l0)
            ms = jnp.mean(ot * ot, axis=0, keepdims=True)
            on = (ot * lax.rsqrt(ms + EPS)).T * (subg_ref[...] * (1.0 - lam_init))
            o_ref[0, rows, sl] = (on * _silu(g_ref[0, rows, sl].astype(F32))).astype(o_ref.dtype)

    no_shift_ok = score_bound <= SCORE_BOUND
    pl.when(no_shift_ok)(functools.partial(run_heads, False))
    pl.when(jnp.logical_not(no_shift_ok))(functools.partial(run_heads, True))


def _attention(q, k_lat, k_ctx, vt_lat, vt_ctx, rest, qk_gain, subln_g, lam_p, lam_init, tq):
    b, n, w = q.shape
    n_ctx = k_ctx.shape[1]
    return pl.pallas_call(
        functools.partial(_attn_kernel, lam_init=lam_init),
        grid=(b, n // tq),
        in_specs=[
            pl.BlockSpec((1, tq, w), lambda i, j: (i, j, 0)),
            pl.BlockSpec((1, n, w), lambda i, j: (i, 0, 0)),
            pl.BlockSpec((1, n_ctx, w), lambda i, j: (i, 0, 0)),
            pl.BlockSpec((1, w, n), lambda i, j: (i, 0, 0)),
            pl.BlockSpec((1, w, n_ctx), lambda i, j: (0, 0, i)),
            pl.BlockSpec((1, tq, w), lambda i, j: (i, j, REST_COL_GATE_A)),
            _const_spec(qk_gain.shape),
            _const_spec((1, LANES)),
            _const_spec((4, A_HEAD_DIM)),
        ],
        out_specs=pl.BlockSpec((1, tq, w), lambda i, j: (i, j, 0)),
        out_shape=jax.ShapeDtypeStruct((b, n, w), BF16),
        compiler_params=_cparams("arbitrary", "arbitrary"),
        name="diff_attn",
    )(q, k_lat, k_ctx, vt_lat, vt_ctx, rest, qk_gain, subln_g, lam_p)


def _split_bf16(t):
    hi = t.astype(BF16)
    return hi, (t - hi.astype(F32)).astype(BF16)


def _block_ref(g, block, row):
    c, w = g.shape
    g3 = g.reshape(c // block, block, w)
    return jnp.broadcast_to(g3[:, row:row + 1, :], g3.shape).reshape(c, w)


def _hgrn_tables(c):
    t = np.arange(c)[:, None]
    s = np.arange(c)[None, :]
    lvl = np.zeros((c, c), np.int32)
    lvl[(t // HGRN_DIAG == s // HGRN_DIAG) & (s <= t)] = 1
    b, k = HGRN_DIAG, 2
    while b < c:
        lvl[(t // b == s // b + 1) & ((s // b) % 2 == 0)] = k
        b, k = 2 * b, k + 1
    tri = (s <= t).astype(np.float32)
    return jnp.asarray(np.stack([tri, tri.T]), BF16), jnp.asarray(np.stack([lvl, lvl.T]))


def _hgrn_chunks(chains, tri_ref, lvl_ref, want_out):
    nt = (((1,), (1,)), ((), ()))
    tn = (((0,), (0,)), ((), ()))
    n = len(chains)
    c = chains[0][2].shape[0]

    kk, parts = [], []
    for (_, _, f_raw, lb, _, _) in chains:
        f = lb + (1.0 - lb) * _sigmoid(f_raw)
        kk.append(1.0 - f)
        parts.append(_split_bf16(jnp.log(f) * LOG2E))
    cum = [sum(jnp.dot(tri_ref[ch[5]], p, preferred_element_type=F32) for p in parts[i])
           for i, ch in enumerate(chains)]
    edge = [cum[i][0:1, :] if ch[5] else cum[i][c - 1:c, :] for i, ch in enumerate(chains)]

    outs = [None] * n
    if want_out:
        a = []
        for i, (q, _, _, _, _, d) in enumerate(chains):
            ref = _block_ref(cum[i], HGRN_DIAG, HGRN_DIAG // 2)
            qd = (q * jnp.exp2(jnp.minimum(cum[i] - ref, EXP2_CLAMP))).astype(BF16)
            kd = (kk[i] * jnp.exp2(jnp.minimum(ref - cum[i], EXP2_CLAMP))).astype(BF16)
            a.append(jnp.where(lvl_ref[d] == 1,
                               lax.dot_general(qd, kd, nt, preferred_element_type=F32), 0.0))
        b, k = HGRN_DIAG, 2
        while b < c:
            for i, (q, _, _, _, _, d) in enumerate(chains):
                ref = _block_ref(cum[i], 2 * b, b if d else b - 1)
                decay = jnp.exp2(cum[i] - ref)
                ql = (q * decay).astype(BF16)
                kl = (kk[i] * jnp.exp2(ref - cum[i])).astype(BF16)
                a[i] = jnp.where(lvl_ref[d] == k,
                                 lax.dot_general(ql, kl, nt, preferred_element_type=F32), a[i])
            b, k = 2 * b, k + 1
        for i, (q, v, _, _, st, _) in enumerate(chains):
            o = jnp.dot(a[i].astype(BF16), v, preferred_element_type=F32)
            outs[i] = o + lax.dot_general((q * jnp.exp2(cum[i])).astype(BF16), st.astype(BF16), nt,
                                          preferred_element_type=F32)

    sts = []
    for i, (_, v, _, _, st, _) in enumerate(chains):
        kg = (kk[i] * jnp.exp2(edge[i] - cum[i])).astype(BF16)
        upd = lax.dot_general(v, kg, tn, preferred_element_type=F32)
        sts.append(st * jnp.exp2(edge[i]) + upd)
    return outs, sts


def _hgrn_kernel(q_ref, i_ref, ff_ref, fb_ref, g_ref, ic_ref, ffc_ref, fbc_ref,
                 lbl_ref, ng_ref, tri_ref, lvl_ref, o_ref, acc_scr):
    c = HGRN_CHUNK
    heads = q_ref.shape[2] // LANES
    nc_lat = q_ref.shape[1] // c
    nc_ctx = ic_ref.shape[1] // c
    f_lat = (ff_ref, fb_ref)
    f_ctx = (ffc_ref, fbc_ref)

    def lower_bound(direction, sl):
        logits = [lbl_ref[direction, l, :, sl] for l in range(lbl_ref.shape[1])]
        top = functools.reduce(jnp.maximum, logits)
        e = [jnp.exp(t - top) for t in logits]
        return e[0] / sum(e)

    lanes = [slice(h * LANES, (h + 1) * LANES) for h in range(heads)]
    lbs = [[lower_bound(d, sl) for sl in lanes] for d in (0, 1)]

    def rows(i):
        return pl.ds(pl.multiple_of(i * c, c), c)

    def ctx_step(j, sts):
        chains = []
        for d in (0, 1):
            r = rows(nc_ctx - 1 - j if d else j)
            for h, sl in enumerate(lanes):
                chains.append((None, ic_ref[0, r, sl], f_ctx[d][0, r, sl].astype(F32),
                               lbs[d][h], sts[d * heads + h], d))
        return tuple(_hgrn_chunks(chains, tri_ref, lvl_ref, False)[1])

    def lat_step(j, sts, second_visit):
        chains, where = [], []
        for d in (0, 1):
            r = rows(nc_lat - 1 - j if d else j)
            for h, sl in enumerate(lanes):
                chains.append((q_ref[0, r, sl].astype(F32), i_ref[0, r, sl],
                               f_lat[d][0, r, sl].astype(F32), lbs[d][h], sts[d * heads + h], d))
                where.append((r, sl))
        outs, new = [], []
        for g0 in range(0, len(chains), HGRN_GROUP):
            o_g, s_g = _hgrn_chunks(chains[g0:g0 + HGRN_GROUP], tri_ref, lvl_ref, True)
            outs += o_g
            new += s_g
        for o, (r, sl) in zip(outs, where):
            if second_visit:
                y = _rms(acc_scr[r, sl] + o, ng_ref[...]) * _silu(g_ref[0, r, sl].astype(F32))
                o_ref[0, r, sl] = y.astype(o_ref.dtype)
            else:
                acc_scr[r, sl] = o
        return tuple(new)

    sts = tuple(jnp.zeros((B_DIM, B_DIM), F32) for _ in range(2 * heads))
    sts = lax.fori_loop(0, nc_ctx, ctx_step, sts, unroll=True)
    sts = lax.fori_loop(0, nc_lat // 2, functools.partial(lat_step, second_visit=False), sts,
                        unroll=HGRN_UNROLL)
    lax.fori_loop(nc_lat // 2, nc_lat, functools.partial(lat_step, second_visit=True), sts,
                  unroll=HGRN_UNROLL)


def _hgrn(rest_x, rest_c, lb_logits, norm_g, heads_per_step):
    b, n, _ = rest_x.shape
    n_ctx = rest_c.shape[1]
    assert (n // HGRN_CHUNK) % 2 == 0 and B_HEADS % heads_per_step == 0
    w = heads_per_step * LANES
    steps = B_HEADS // heads_per_step

    def xs(group):
        return pl.BlockSpec((1, n, w), lambda i, h: (i, 0, group * steps + h))

    def cs(group):
        return pl.BlockSpec((1, n_ctx, w), lambda i, h: (i, 0, group * steps + h))

    n_layers = lb_logits.shape[1]
    tri, lvl = _hgrn_tables(HGRN_CHUNK)
    return pl.pallas_call(
        _hgrn_kernel,
        grid=(b, steps),
        in_specs=[
            xs(REST_COL_Q), xs(REST_COL_I), xs(REST_COL_FF), xs(REST_COL_FB), xs(REST_COL_GATE_B),
            cs(CTX_COL_I), cs(CTX_COL_FF), cs(CTX_COL_FB),
            pl.BlockSpec((2, n_layers, 1, w), lambda i, h: (0, 0, 0, h)),
            _const_spec((1, LANES)),
            _const_spec(tri.shape),
            _const_spec(lvl.shape),
        ],
        out_specs=pl.BlockSpec((1, n, w), lambda i, h: (i, 0, h)),
        out_shape=jax.ShapeDtypeStruct((b, n, B_WIDTH), BF16),
        scratch_shapes=[pltpu.VMEM((n, w), F32)],
        compiler_params=_cparams("arbitrary", "arbitrary"),
        name="hgrn2",
    )(rest_x, rest_x, rest_x, rest_x, rest_x, rest_c, rest_c, rest_c,
      lb_logits.reshape(2, n_layers, 1, B_WIDTH), norm_g, tri, lvl)


def _layer1_input(ya, yb, x, gate0, mod1, gain1, wo0_ref):
    d = x.shape[-1]
    half = ya.shape[-1]
    upd = (jnp.dot(ya, wo0_ref[0:half, :], preferred_element_type=F32)
           + jnp.dot(yb, wo0_ref[half:, :], preferred_element_type=F32))
    x1 = x + gate0 * upd
    return x1, _modulate(x1, gain1, mod1, d).astype(BF16)


def _edge_kernel(ya_ref, yb_ref, x_ref, mod0_ref, mod1_ref, g1_ref, wo0_ref, wi_ref, z_ref,
                 *, rows_per_batch):
    d = x_ref.shape[-1]

    def per_row(ref, lo, hi):
        return jnp.concatenate([jnp.broadcast_to(ref[i:i + 1, lo:hi], (rows_per_batch, hi - lo))
                                for i in range(x_ref.shape[0] // rows_per_batch)], axis=0)

    _, xm = _layer1_input(ya_ref[...], yb_ref[...], x_ref[...], per_row(mod0_ref, 2 * d, 3 * d),
                          per_row(mod1_ref, 0, 2 * d), g1_ref[...], wo0_ref)
    p = jnp.dot(xm, wi_ref[...], preferred_element_type=F32)
    z_ref[...] = p[:, :D_WIDTH] * p[:, D_WIDTH:]


def _block_edges(t, tb):
    b, n, w = t.shape
    te = t.reshape(b, n // tb, tb, w)
    return jnp.concatenate([te[:, :, :SUBLANES], te[:, :, tb - SUBLANES:]], axis=2).reshape(-1, w)


def _edge_z(ya, yb, x, mod0, mod1, gain1, w_out0, w_in1, tb):
    b, n, d = x.shape
    rows = (n // tb) * 2 * SUBLANES
    cg_start = 3 * C_WIDTH + D_WIDTH
    assert cg_start % (2 * D_WIDTH) == 0
    cg_blk = cg_start // (2 * D_WIDTH)
    full = lambda arr: _const_spec(arr.shape)
    xe, yae, ybe = _block_edges(x, tb), _block_edges(ya, tb), _block_edges(yb, tb)
    return pl.pallas_call(
        functools.partial(_edge_kernel, rows_per_batch=rows),
        grid=(1,),
        in_specs=[full(yae), full(ybe), full(xe), full(mod0), full(mod1), _const_spec((1, d)),
                  full(w_out0),
                  pl.BlockSpec((d, 2 * D_WIDTH), lambda i: (0, cg_blk), pipeline_mode=pl.Buffered(1))],
        out_specs=pl.BlockSpec((b * rows, D_WIDTH), lambda i: (0, 0)),
        out_shape=jax.ShapeDtypeStruct((b * rows, D_WIDTH), F32),
        compiler_params=_cparams("arbitrary"),
        name="conv_edge_rows",
    )(yae, ybe, xe, mod0, mod1, gain1, w_out0, w_in1).reshape(b, rows, D_WIDTH)


def _layer1_kernel(ya_ref, yb_ref, x_ref, ze_ref, mod0_ref, mod1_ref, g1_ref, wo0_ref, wi_ref,
                   vg_ref, ws_ref, bs_ref, cw_ref, wo1_ref, o_ref):
    d = x_ref.shape[-1]
    tb = x_ref.shape[1]
    j = pl.program_id(1)
    last_j = pl.num_programs(1) - 1
    mod1 = mod1_ref[pl.ds(pl.program_id(0), 1), :]
    gate0 = mod0_ref[pl.ds(pl.program_id(0), 1), 2 * d:]
    gate1 = mod1[:, 2 * d:]
    sub = tb // L1_SUB_BLOCKS
    subs = [slice(s * sub, (s + 1) * sub) for s in range(L1_SUB_BLOCKS)]
    col = lambda p, k: p[:, k * C_WIDTH:(k + 1) * C_WIDTH]
    n_gmlp = 3 * C_WIDTH

    x1s, xms = [], []
    for r in subs:
        x1, xm = _layer1_input(ya_ref[0, r, :], yb_ref[0, r, :], x_ref[0, r, :], gate0,
                               mod1, g1_ref[...], wo0_ref)
        x1s.append(x1)
        xms.append(xm)
    pgs = [jnp.dot(xm, wi_ref[:, 0:n_gmlp], preferred_element_type=F32) for xm in xms]
    pcs = [jnp.dot(xm, wi_ref[:, n_gmlp:], preferred_element_type=F32) for xm in xms]

    upd_c = []
    for pg in pgs:
        u = _gelu(col(pg, 0))
        vn = _rms(_gelu(col(pg, 1)), vg_ref[...]).astype(BF16)
        chunks = []
        for ci in range(sub // C_CHUNK):
            cr = slice(ci * C_CHUNK, (ci + 1) * C_CHUNK)
            groups = []
            for g in range(C_GROUPS):
                gl = slice(g * LANES, (g + 1) * LANES)
                groups.append(jnp.dot(ws_ref[g], vn[cr, gl], preferred_element_type=F32) + bs_ref[g])
            chunks.append(jnp.concatenate(groups, axis=1))
        o_c = u * jnp.concatenate(chunks, axis=0) * _silu(col(pg, 2))
        upd_c.append(jnp.dot(o_c.astype(BF16), wo1_ref[0:C_WIDTH, :], preferred_element_type=F32))

    z = jnp.concatenate([col(pc, 1) * col(pc, 2) for pc in pcs], axis=0)
    grp = 2 * SUBLANES
    prev_grp = ze_ref[0, pl.ds(pl.multiple_of(jnp.maximum(j - 1, 0) * grp + SUBLANES, SUBLANES),
                               SUBLANES), :]
    next_grp = ze_ref[0, pl.ds(pl.multiple_of(jnp.minimum(j + 1, last_j) * grp, SUBLANES),
                               SUBLANES), :]
    z_prev_row = jnp.where(j == 0, 0.0, prev_grp[SUBLANES - 1:, :])
    z_next_row = jnp.where(j == last_j, 0.0, next_grp[0:1, :])
    rowi = lax.broadcasted_iota(jnp.int32, z.shape, 0)
    z_prev = jnp.where(rowi == 0, z_prev_row, pltpu.roll(z, 1, 0))
    z_next = jnp.where(rowi == tb - 1, z_next_row, pltpu.roll(z, tb - 1, 0))
    conv = cw_ref[0:1, :] * z_prev + cw_ref[1:2, :] * z + cw_ref[2:3, :] * z_next

    for r, x1, pc, uc in zip(subs, x1s, pcs, upd_c):
        o_d = col(pc, 0) * conv[r, :] * _silu(col(pc, 3))
        upd = uc + jnp.dot(o_d.astype(BF16), wo1_ref[C_WIDTH:, :], preferred_element_type=F32)
        o_ref[0, r, :] = x1 + gate1 * upd


def _layer1(ya, yb, x, ze, mod0, mod1, gain1, w_out0, w_in1, v_gain, w_s, b_s, conv_w, w_out1, tb):
    b, n, d = x.shape
    half = ya.shape[-1]
    tok = lambda width: pl.BlockSpec((1, tb, width), lambda i, j: (i, j, 0))
    modspec = _const_spec(mod0.shape)
    return pl.pallas_call(
        _layer1_kernel,
        grid=(b, n // tb),
        in_specs=[tok(half), tok(half), tok(d),
                  pl.BlockSpec((1,) + ze.shape[1:], lambda i, j: (i, 0, 0)),
                  modspec, modspec, _const_spec((1, d)),
                  _const_spec(w_out0.shape), _const_spec(w_in1.shape),
                  _const_spec((1, C_WIDTH)),
                  _const_spec((C_GROUPS, C_CHUNK, C_CHUNK)),
                  _const_spec((C_GROUPS, C_CHUNK, LANES)),
                  _const_spec((3, D_WIDTH)),
                  _const_spec(w_out1.shape)],
        out_specs=tok(d),
        out_shape=jax.ShapeDtypeStruct((b, n, d), F32),
        compiler_params=_cparams("arbitrary", "arbitrary"),
        name="outproj_even_layer_odd",
    )(ya, yb, x, ze, mod0, mod1, gain1, w_out0, w_in1, v_gain, w_s, b_s, conv_w, w_out1)


def _rope_tables(n):
    rows_ = n // GRID_W
    row = np.repeat(np.arange(rows_, dtype=np.float64), GRID_W)
    col = np.tile(np.arange(GRID_W, dtype=np.float64), rows_)
    n_freq = A_HEAD_DIM // 4
    inv = ROPE_THETA ** (-np.arange(n_freq, dtype=np.float64) / n_freq)
    ang = np.concatenate([row[:, None] * inv, col[:, None] * inv], axis=-1)
    cos, sin = np.cos(ang), np.sin(ang)
    reps = QK_SLAB // A_HEAD_DIM
    return (jnp.asarray(np.tile(np.concatenate([cos, cos], axis=-1), (1, reps)), F32),
            jnp.asarray(np.tile(np.concatenate([-sin, sin], axis=-1), (1, reps)), F32))


def kernel(x, c, ctx, c_ctx, norm_gain, ada_w, ada_b, even_w_in, even_w_out, attn_qk_gain,
           attn_lambda, attn_subln_gain, hgrn_lb_logits, hgrn_norm_gain, odd_w_in, odd_w_out,
           gmlp_v_gain, gmlp_w_s, gmlp_b_s, conv_w):
    b, n, d = x.shape
    assert b + 1 <= COND_ROWS and n % 512 == 0 and ctx.shape[1] % HGRN_CHUNK == 0
    assert norm_gain.shape[0] == 2, "two-layer block: one even layer then one odd layer"

    cond = jnp.concatenate([c, c_ctx[None, :], jnp.zeros((COND_ROWS - b - 1, d), F32)], axis=0)
    mod0, mod1 = _adaln(cond, ada_w, ada_b)

    w_in0 = even_w_in[0].astype(BF16)
    gain0 = norm_gain[0].reshape(1, d)
    cos, sin_signed = _rope_tables(n)
    qk_gain = jnp.tile(attn_qk_gain[0], (1, QK_SLAB // A_HEAD_DIM))
    blk = np.arange(QK_SLAB) // A_HEAD_DIM
    bd = jnp.asarray(np.where(blk[:, None] == blk[None, :], 1.0 / A_HEAD_DIM, 0.0), BF16)
    q, k_lat, vt_lat, rest_x = _inproj(x, mod0, None, gain0, w_in0, cos, sin_signed, qk_gain, bd,
                                       INPROJ_BLOCK, True, True)
    w_ctx = jnp.concatenate([w_in0[:, g * A_WIDTH:(g + 1) * A_WIDTH] for g in CTX_GROUPS], axis=1)
    n_ctx = ctx.shape[1]
    k_ctx, vt_ctx, rest_c = _inproj(ctx.reshape(1, b * n_ctx, d), mod0, b, gain0, w_ctx, cos, sin_signed,
                                    qk_gain, bd, min(TOKEN_BLOCK, b * n_ctx), False, False)
    k_ctx = k_ctx.reshape(b, n_ctx, A_WIDTH)
    rest_c = rest_c.reshape(b, n_ctx, rest_c.shape[-1])

    lam_init = 0.8 - 0.6 * math.exp(-0.3 * 0)
    ya = _attention(q, k_lat, k_ctx, vt_lat, vt_ctx, rest_x, qk_gain,
                    attn_subln_gain[0].reshape(1, LANES), attn_lambda[0], lam_init, ATTN_Q_BLOCK)
    yb = _hgrn(rest_x, rest_c, hgrn_lb_logits, hgrn_norm_gain[0].reshape(1, LANES),
               HGRN_HEADS_PER_STEP)

    gain1 = norm_gain[1].reshape(1, d)
    w_out0, w_in1 = even_w_out[0].astype(BF16), odd_w_in[0].astype(BF16)
    ze = _edge_z(ya, yb, x, mod0, mod1, gain1, w_out0, w_in1, L1_TOKEN_BLOCK)
    b_s = jnp.broadcast_to(gmlp_b_s[0][:, :, None], (C_GROUPS, C_CHUNK, LANES))
    return _layer1(ya, yb, x, ze, mod0, mod1, gain1, w_out0, w_in1,
                   gmlp_v_gain[0].reshape(1, C_WIDTH), gmlp_w_s[0].astype(BF16), b_s, conv_w[0],
                   odd_w_out[0].astype(BF16), L1_TOKEN_BLOCK)
```

```python
import functools
import math

import jax
import jax.numpy as jnp
import numpy as np
from jax import lax
from jax.experimental import pallas as pl
from jax.experimental.pallas import tpu as pltpu

F32 = jnp.float32
BF16 = jnp.bfloat16

EPS = 1e-6
GRID_W = 64
ROPE_THETA = 10000.0
A_HEADS = 4
A_HEAD_DIM = 64
A_WIDTH = 2 * A_HEADS * A_HEAD_DIM
B_HEADS = 4
B_DIM = 128
B_WIDTH = B_HEADS * B_DIM
C_GROUPS = 4
C_CHUNK = 128
C_WIDTH = 512
D_WIDTH = 512
EVEN_IN = 4 * A_WIDTH + 5 * B_WIDTH
ODD_IN = 3 * C_WIDTH + 4 * D_WIDTH

CTX_GROUPS = (1, 2, 5, 6, 7)
REST_COL_GATE_A, REST_COL_Q, REST_COL_I, REST_COL_FF, REST_COL_FB, REST_COL_GATE_B = range(6)
CTX_COL_I, CTX_COL_FF, CTX_COL_FB = range(3)
QK_SLAB = 256

LANES = 128
HGRN_CHUNK = 128
HGRN_HEADS_PER_STEP = 4
HGRN_GROUP = 8
HGRN_UNROLL = 4
HGRN_DIAG = 8
SCORE_BOUND = 32.0
EXP2_CLAMP = 115.0
LOG2E = math.log2(math.e)
Q_SCALE = A_HEAD_DIM ** -0.5 * LOG2E
TOKEN_BLOCK = 512
INPROJ_BLOCK = 1024
L1_TOKEN_BLOCK = 1024
ATTN_Q_TILE = 256
ATTN_Q_BLOCK = 512
ATTN_LOOKAHEAD = 1
SUBLANES = 8
BF16_ROWS_PER_VREG = 16
L1_SUB_BLOCKS = 4
COND_ROWS = 16
ADALN_COL_BLOCK = 512
VMEM_LIMIT = 56 * 1024 * 1024


def _cparams(*sem):
    return pltpu.CompilerParams(dimension_semantics=sem, vmem_limit_bytes=VMEM_LIMIT)


def _const_spec(shape):
    nd = len(shape)
    return pl.BlockSpec(shape, lambda *_: (0,) * nd, pipeline_mode=pl.Buffered(1))


def _sigmoid(t):
    return 0.5 + 0.5 * jnp.tanh(0.5 * t)


def _silu(t):
    return t * _sigmoid(t)


def _gelu(t):
    return 0.5 * t * (1.0 + lax.erf(t * (1.0 / math.sqrt(2.0))))


def _rms(t, gain):
    ms = jnp.mean(t * t, axis=-1, keepdims=True)
    return t * lax.rsqrt(ms + EPS) * gain


def _adaln_kernel(cond_ref, w_ref, b_ref, *o_refs):
    a = _silu(cond_ref[...])
    for layer, o_ref in enumerate(o_refs):
        o_ref[...] = jnp.dot(a, w_ref[layer], preferred_element_type=F32) + b_ref[layer]


def _adaln(cond, ada_w, ada_b):
    depth, d, n3 = ada_w.shape
    tn = ADALN_COL_BLOCK
    out = pl.BlockSpec((COND_ROWS, tn), lambda j: (0, j))
    return pl.pallas_call(
        _adaln_kernel,
        grid=(n3 // tn,),
        in_specs=[
            pl.BlockSpec((COND_ROWS, d), lambda j: (0, 0)),
            pl.BlockSpec((depth, d, tn), lambda j: (0, 0, j)),
            pl.BlockSpec((depth, 1, tn), lambda j: (0, 0, j)),
        ],
        out_specs=[out] * depth,
        out_shape=[jax.ShapeDtypeStruct((COND_ROWS, n3), F32)] * depth,
        compiler_params=_cparams("arbitrary"),
        name="adaln",
    )(cond, ada_w, ada_b.reshape(depth, 1, n3))


def _modulate(x, gain, mod, d):
    shift = mod[:, 0:d]
    scale = mod[:, d:2 * d]
    return _rms(x, gain) * (1.0 + scale) + shift


def _rope(t, cos, sin_signed):
    lanes = t.shape[1]
    lane = lax.broadcasted_iota(jnp.int32, t.shape, 1)
    first = (lane % A_HEAD_DIM) < (A_HEAD_DIM // 2)
    partner = jnp.where(first,
                        pltpu.roll(t, lanes - A_HEAD_DIM // 2, 1),
                        pltpu.roll(t, A_HEAD_DIM // 2, 1))
    return t * cos + partner * sin_signed


def _inproj_kernel(x_ref, mod_ref, g_ref, wa_ref, wr1_ref, wr2_ref, cos_ref, sin_ref, qkg_ref, bd_ref,
                   *out_refs,
                   mod_row, has_q, rope):
    d = x_ref.shape[-1]
    tb = x_ref.shape[1]
    row = pl.program_id(0) if mod_row is None else mod_row
    n_attn = (3 if has_q else 2) * A_WIDTH
    rest_ref = out_refs[-1]
    n_rest = rest_ref.shape[-1]
    slabs = [slice(half * QK_SLAB, (half + 1) * QK_SLAB) for half in range(A_WIDTH // QK_SLAB)]
    sub = min(tb, TOKEN_BLOCK)

    pending = []
    for r in [slice(s * sub, (s + 1) * sub) for s in range(tb // sub)]:
        xm = _modulate(x_ref[0, r, :], g_ref[...], mod_ref[pl.ds(row, 1), :], d).astype(BF16)
        attn = jnp.dot(xm, wa_ref[...], preferred_element_type=F32)
        rest_ref[0, r, 0:n_rest // 2] = jnp.dot(
            xm, wr1_ref[...], preferred_element_type=F32).astype(rest_ref.dtype)
        groups = [attn[:, g * A_WIDTH:(g + 1) * A_WIDTH] for g in range(n_attn // A_WIDTH)]
        v = groups.pop()
        mean_sq = [[jnp.dot((t[:, sl] * t[:, sl]).astype(BF16), bd_ref[...],
                            preferred_element_type=F32) for sl in slabs] for t in groups]
        pad = BF16_ROWS_PER_VREG
        ms_rows = sum(ms[0:pad, :] for per_group in mean_sq for ms in per_group)
        anchor = jnp.concatenate([ms_rows * 0.0] * (d // QK_SLAB), axis=1)
        xm_late = jnp.concatenate([(xm[0:pad, :].astype(F32) + anchor).astype(BF16), xm[pad:, :]],
                                  axis=0)
        rest_ref[0, r, n_rest // 2:] = jnp.dot(
            xm_late, wr2_ref[...], preferred_element_type=F32).astype(rest_ref.dtype)
        pending.append((r, groups, v, mean_sq))

    qk_refs = out_refs[:-2]
    vt_ref = out_refs[-2]
    first_gain = 0 if has_q else 1
    for r, groups, v, mean_sq in pending:
        for gi, (t, o_ref) in enumerate(zip(groups, qk_refs)):
            gain = qkg_ref[first_gain + gi:first_gain + gi + 1, :]
            scale = Q_SCALE if (has_q and gi == 0) else 1.0
            for sl, ms in zip(slabs, mean_sq[gi]):
                tn = t[:, sl] * lax.rsqrt(ms + EPS) * gain
                if rope:
                    tn = _rope(tn, cos_ref[r, :], sin_ref[r, :])
                o_ref[0, r, sl] = (tn * scale).astype(o_ref.dtype)
        for h in range(A_HEADS):
            sl = slice(h * LANES, (h + 1) * LANES)
            vt_ref[0, sl, r] = v[:, sl].T.astype(vt_ref.dtype)


def _inproj(x, mod, mod_row, gain, w, cos, sin_signed, qk_gain, bd, tb, has_q, rope):
    b, n, d = x.shape
    n_attn = (3 if has_q else 2) * A_WIDTH
    n_rest = w.shape[1] - n_attn
    bounds = (0, n_attn, n_attn + n_rest // 2, w.shape[1])
    w_parts, w_specs = [], []
    for lo, hi in zip(bounds[:-1], bounds[1:]):
        if lo % (hi - lo) == 0:
            w_parts.append(w)
            w_specs.append(pl.BlockSpec((d, hi - lo), lambda i, j, blk=lo // (hi - lo): (0, blk),
                                        pipeline_mode=pl.Buffered(1)))
        else:
            w_parts.append(w[:, lo:hi])
            w_specs.append(_const_spec((d, hi - lo)))
    tok = lambda width: pl.BlockSpec((1, tb, width), lambda i, j: (i, j, 0))
    qk_out = [tok(A_WIDTH)] * (2 if has_q else 1)
    qk_shape = [jax.ShapeDtypeStruct((b, n, A_WIDTH), BF16)] * (2 if has_q else 1)
    return pl.pallas_call(
        functools.partial(_inproj_kernel, mod_row=mod_row, has_q=has_q, rope=rope),
        grid=(b, n // tb),
        in_specs=[
            tok(d),
            _const_spec(mod.shape),
            _const_spec((1, d)),
            *w_specs,
            pl.BlockSpec((tb, QK_SLAB), lambda i, j: (j, 0)),
            pl.BlockSpec((tb, QK_SLAB), lambda i, j: (j, 0)),
            _const_spec(qk_gain.shape),
            _const_spec(bd.shape),
        ],
        out_specs=qk_out + [pl.BlockSpec((1, A_WIDTH, tb), lambda i, j: (i, 0, j)), tok(n_rest)],
        out_shape=qk_shape + [jax.ShapeDtypeStruct((b, A_WIDTH, n), BF16),
                              jax.ShapeDtypeStruct((b, n, n_rest), BF16)],
        compiler_params=_cparams("arbitrary", "arbitrary"),
        name="inproj_even",
    )(x, mod, gain, *w_parts, cos, sin_signed, qk_gain, bd)


def _attn_kernel(q_ref, kl_ref, kc_ref, vtl_ref, vtc_ref, g_ref, qkg_ref, subg_ref, lamp_ref,
                 o_ref, *, lam_init):
    heads = [slice(h * LANES, (h + 1) * LANES) for h in range(A_HEADS)]
    lane = lax.broadcasted_iota(jnp.int32, (1, LANES), 1)
    nt = (((1,), (1,)), ((), ()))

    lp = lamp_ref[...]
    lam = (jnp.exp(jnp.sum(lp[0:1] * lp[1:2], axis=-1, keepdims=True))
           - jnp.exp(jnp.sum(lp[2:3] * lp[3:4], axis=-1, keepdims=True)) + lam_init)
    score_bound = ((A_HEAD_DIM * Q_SCALE) * jnp.max(jnp.abs(qkg_ref[0:1, :]))
                   * jnp.max(jnp.abs(qkg_ref[1:2, :])))

    n_sub = q_ref.shape[1] // ATTN_Q_TILE
    items = [(slice(qi * ATTN_Q_TILE, (qi + 1) * ATTN_Q_TILE), h)
             for qi in range(n_sub) for h in range(A_HEADS)]

    def scores(item):
        rows, h = item
        out = []
        for m in range(2):
            qm = jnp.where((lane // A_HEAD_DIM) == m, q_ref[0, rows, heads[h]], 0).astype(BF16)
            out.append([lax.dot_general(k_ref[0, :, heads[h]], qm, nt, preferred_element_type=F32)
                        for k_ref in (kc_ref, kl_ref)])
        return out

    def run_heads(shift):
        queued = [scores(item) for item in items[:ATTN_LOOKAHEAD]]
        for idx, (rows, h) in enumerate(items):
            sl = heads[h]
            s_maps = queued.pop(0)
            if idx + ATTN_LOOKAHEAD < len(items):
                queued.append(scores(items[idx + ATTN_LOOKAHEAD]))
            probs = []
            for s_parts in s_maps:
                if shift:
                    top = functools.reduce(jnp.maximum,
                                           [jnp.max(s, axis=0, keepdims=True) for s in s_parts])
                    s_parts = [s - top for s in s_parts]
                p_parts = [jnp.exp2(s) for s in s_parts]
                probs.append((p_parts, sum(jnp.sum(p, axis=0, keepdims=True) for p in p_parts)))
            (p0, l0), (p1, l1) = probs
            ratio = lam * l0 * (1.0 / l1)
            ot = sum(jnp.dot(vt_ref[0, sl, :], (pa - pb * ratio).astype(BF16),
                             preferred_element_type=F32)
                     for vt_ref, pa, pb in zip((vtc_ref, vtl_ref), p0, p1)) * (1.0 / l0)
            ms = jnp.mean(ot * ot, axis=0, keepdims=True)
            on = (ot * lax.rsqrt(ms + EPS)).T * (subg_ref[...] * (1.0 - lam_init))
            o_ref[0, rows, sl] = (on * _silu(g_ref[0, rows, sl].astype(F32))).astype(o_ref.dtype)

    no_shift_ok = score_bound <= SCORE_BOUND
    pl.when(no_shift_ok)(functools.partial(run_heads, False))
    pl.when(jnp.logical_not(no_shift_ok))(functools.partial(run_heads, True))


def _attention(q, k_lat, k_ctx, vt_lat, vt_ctx, rest, qk_gain, subln_g, lam_p, lam_init, tq):
    b, n, w = q.shape
    n_ctx = k_ctx.shape[1]
    return pl.pallas_call(
        functools.partial(_attn_kernel, lam_init=lam_init),
        grid=(b, n // tq),
        in_specs=[
            pl.BlockSpec((1, tq, w), lambda i, j: (i, j, 0)),
            pl.BlockSpec((1, n, w), lambda i, j: (i, 0, 0)),
            pl.BlockSpec((1, n_ctx, w), lambda i, j: (i, 0, 0)),
            pl.BlockSpec((1, w, n), lambda i, j: (i, 0, 0)),
            pl.BlockSpec((1, w, n_ctx), lambda i, j: (0, 0, i)),
            pl.BlockSpec((1, tq, w), lambda i, j: (i, j, REST_COL_GATE_A)),
            _const_spec(qk_gain.shape),
            _const_spec((1, LANES)),
            _const_spec((4, A_HEAD_DIM)),
        ],
        out_specs=pl.BlockSpec((1, tq, w), lambda i, j: (i, j, 0)),
        out_shape=jax.ShapeDtypeStruct((b, n, w), BF16),
        compiler_params=_cparams("arbitrary", "arbitrary"),
        name="diff_attn",
    )(q, k_lat, k_ctx, vt_lat, vt_ctx, rest, qk_gain, subln_g, lam_p)


def _split_bf16(t):
    hi = t.astype(BF16)
    return hi, (t - hi.astype(F32)).astype(BF16)


def _block_ref(g, block, row):
    c, w = g.shape
    g3 = g.reshape(c // block, block, w)
    return jnp.broadcast_to(g3[:, row:row + 1, :], g3.shape).reshape(c, w)


def _hgrn_tables(c):
    t = np.arange(c)[:, None]
    s = np.arange(c)[None, :]
    lvl = np.zeros((c, c), np.int32)
    lvl[(t // HGRN_DIAG == s // HGRN_DIAG) & (s <= t)] = 1
    b, k = HGRN_DIAG, 2
    while b < c:
        lvl[(t // b == s // b + 1) & ((s // b) % 2 == 0)] = k
        b, k = 2 * b, k + 1
    tri = (s <= t).astype(np.float32)
    return jnp.asarray(np.stack([tri, tri.T]), BF16), jnp.asarray(np.stack([lvl, lvl.T]))


def _hgrn_chunks(chains, tri_ref, lvl_ref, want_out):
    nt = (((1,), (1,)), ((), ()))
    tn = (((0,), (0,)), ((), ()))
    n = len(chains)
    c = chains[0][2].shape[0]

    kk, parts = [], []
    for (_, _, f_raw, lb, _, _) in chains:
        f = lb + (1.0 - lb) * _sigmoid(f_raw)
        kk.append(1.0 - f)
        parts.append(_split_bf16(jnp.log(f) * LOG2E))
    cum = [sum(jnp.dot(tri_ref[ch[5]], p, preferred_element_type=F32) for p in parts[i])
           for i, ch in enumerate(chains)]
    edge = [cum[i][0:1, :] if ch[5] else cum[i][c - 1:c, :] for i, ch in enumerate(chains)]

    outs = [None] * n
    if want_out:
        a = []
        for i, (q, _, _, _, _, d) in enumerate(chains):
            ref = _block_ref(cum[i], HGRN_DIAG, HGRN_DIAG // 2)
            qd = (q * jnp.exp2(jnp.minimum(cum[i] - ref, EXP2_CLAMP))).astype(BF16)
            kd = (kk[i] * jnp.exp2(jnp.minimum(ref - cum[i], EXP2_CLAMP))).astype(BF16)
            a.append(jnp.where(lvl_ref[d] == 1,
                               lax.dot_general(qd, kd, nt, preferred_element_type=F32), 0.0))
        b, k = HGRN_DIAG, 2
        while b < c:
            for i, (q, _, _, _, _, d) in enumerate(chains):
                ref = _block_ref(cum[i], 2 * b, b if d else b - 1)
                decay = jnp.exp2(cum[i] - ref)
                ql = (q * decay).astype(BF16)
                kl = (kk[i] * jnp.exp2(ref - cum[i])).astype(BF16)
                a[i] = jnp.where(lvl_ref[d] == k,
                                 lax.dot_general(ql, kl, nt, preferred_element_type=F32), a[i])
            b, k = 2 * b, k + 1
        for i, (q, v, _, _, st, _) in enumerate(chains):
            o = jnp.dot(a[i].astype(BF16), v, preferred_element_type=F32)
            outs[i] = o + lax.dot_general((q * jnp.exp2(cum[i])).astype(BF16), st.astype(BF16), nt,
                                          preferred_element_type=F32)

    sts = []
    for i, (_, v, _, _, st, _) in enumerate(chains):
        kg = (kk[i] * jnp.exp2(edge[i] - cum[i])).astype(BF16)
        upd = lax.dot_general(v, kg, tn, preferred_element_type=F32)
        sts.append(st * jnp.exp2(edge[i]) + upd)
    return outs, sts


def _hgrn_kernel(q_ref, i_ref, ff_ref, fb_ref, g_ref, ic_ref, ffc_ref, fbc_ref,
                 lbl_ref, ng_ref, tri_ref, lvl_ref, o_ref, acc_scr):
    c = HGRN_CHUNK
    heads = q_ref.shape[2] // LANES
    nc_lat = q_ref.shape[1] // c
    nc_ctx = ic_ref.shape[1] // c
    f_lat = (ff_ref, fb_ref)
    f_ctx = (ffc_ref, fbc_ref)

    def lower_bound(direction, sl):
        logits = [lbl_ref[direction, l, :, sl] for l in range(lbl_ref.shape[1])]
        top = functools.reduce(jnp.maximum, logits)
        e = [jnp.exp(t - top) for t in logits]
        return e[0] / sum(e)

    lanes = [slice(h * LANES, (h + 1) * LANES) for h in range(heads)]
    lbs = [[lower_bound(d, sl) for sl in lanes] for d in (0, 1)]

    def rows(i):
        return pl.ds(pl.multiple_of(i * c, c), c)

    def ctx_step(j, sts):
        chains = []
        for d in (0, 1):
            r = rows(nc_ctx - 1 - j if d else j)
            for h, sl in enumerate(lanes):
                chains.append((None, ic_ref[0, r, sl], f_ctx[d][0, r, sl].astype(F32),
                               lbs[d][h], sts[d * heads + h], d))
        return tuple(_hgrn_chunks(chains, tri_ref, lvl_ref, False)[1])

    def lat_step(j, sts, second_visit):
        chains, where = [], []
        for d in (0, 1):
            r = rows(nc_lat - 1 - j if d else j)
            for h, sl in enumerate(lanes):
                chains.append((q_ref[0, r, sl].astype(F32), i_ref[0, r, sl],
                               f_lat[d][0, r, sl].astype(F32), lbs[d][h], sts[d * heads + h], d))
                where.append((r, sl))
        outs, new = [], []
        for g0 in range(0, len(chains), HGRN_GROUP):
            o_g, s_g = _hgrn_chunks(chains[g0:g0 + HGRN_GROUP], tri_ref, lvl_ref, True)
            outs += o_g
            new += s_g
        for o, (r, sl) in zip(outs, where):
            if second_visit:
                y = _rms(acc_scr[r, sl] + o, ng_ref[...]) * _silu(g_ref[0, r, sl].astype(F32))
                o_ref[0, r, sl] = y.astype(o_ref.dtype)
            else:
                acc_scr[r, sl] = o
        return tuple(new)

    sts = tuple(jnp.zeros((B_DIM, B_DIM), F32) for _ in range(2 * heads))
    sts = lax.fori_loop(0, nc_ctx, ctx_step, sts, unroll=True)
    sts = lax.fori_loop(0, nc_lat // 2, functools.partial(lat_step, second_visit=False), sts,
                        unroll=HGRN_UNROLL)
    lax.fori_loop(nc_lat // 2, nc_lat, functools.partial(lat_step, second_visit=True), sts,
                  unroll=HGRN_UNROLL)


def _hgrn(rest_x, rest_c, lb_logits, norm_g, heads_per_step):
    b, n, _ = rest_x.shape
    n_ctx = rest_c.shape[1]
    assert (n // HGRN_CHUNK) % 2 == 0 and B_HEADS % heads_per_step == 0
    w = heads_per_step * LANES
    steps = B_HEADS // heads_per_step

    def xs(group):
        return pl.BlockSpec((1, n, w), lambda i, h: (i, 0, group * steps + h))

    def cs(group):
        return pl.BlockSpec((1, n_ctx, w), lambda i, h: (i, 0, group * steps + h))

    n_layers = lb_logits.shape[1]
    tri, lvl = _hgrn_tables(HGRN_CHUNK)
    return pl.pallas_call(
        _hgrn_kernel,
        grid=(b, steps),
        in_specs=[
            xs(REST_COL_Q), xs(REST_COL_I), xs(REST_COL_FF), xs(REST_COL_FB), xs(REST_COL_GATE_B),
            cs(CTX_COL_I), cs(CTX_COL_FF), cs(CTX_COL_FB),
            pl.BlockSpec((2, n_layers, 1, w), lambda i, h: (0, 0, 0, h)),
            _const_spec((1, LANES)),
            _const_spec(tri.shape),
            _const_spec(lvl.shape),
        ],
        out_specs=pl.BlockSpec((1, n, w), lambda i, h: (i, 0, h)),
        out_shape=jax.ShapeDtypeStruct((b, n, B_WIDTH), BF16),
        scratch_shapes=[pltpu.VMEM((n, w), F32)],
        compiler_params=_cparams("arbitrary", "arbitrary"),
        name="hgrn2",
    )(rest_x, rest_x, rest_x, rest_x, rest_x, rest_c, rest_c, rest_c,
      lb_logits.reshape(2, n_layers, 1, B_WIDTH), norm_g, tri, lvl)


def _layer1_input(ya, yb, x, gate0, mod1, gain1, wo0_ref):
    d = x.shape[-1]
    half = ya.shape[-1]
    upd = (jnp.dot(ya, wo0_ref[0:half, :], preferred_element_type=F32)
           + jnp.dot(yb, wo0_ref[half:, :], preferred_element_type=F32))
    x1 = x + gate0 * upd
    return x1, _modulate(x1, gain1, mod1, d).astype(BF16)


def _edge_kernel(ya_ref, yb_ref, x_ref, mod0_ref, mod1_ref, g1_ref, wo0_ref, wi_ref, z_ref,
                 *, rows_per_batch):
    d = x_ref.shape[-1]

    def per_row(ref, lo, hi):
        return jnp.concatenate([jnp.broadcast_to(ref[i:i + 1, lo:hi], (rows_per_batch, hi - lo))
                                for i in range(x_ref.shape[0] // rows_per_batch)], axis=0)

    _, xm = _layer1_input(ya_ref[...], yb_ref[...], x_ref[...], per_row(mod0_ref, 2 * d, 3 * d),
                          per_row(mod1_ref, 0, 2 * d), g1_ref[...], wo0_ref)
    p = jnp.dot(xm, wi_ref[...], preferred_element_type=F32)
    z_ref[...] = p[:, :D_WIDTH] * p[:, D_WIDTH:]


def _block_edges(t, tb):
    b, n, w = t.shape
    te = t.reshape(b, n // tb, tb, w)
    return jnp.concatenate([te[:, :, :SUBLANES], te[:, :, tb - SUBLANES:]], axis=2).reshape(-1, w)


def _edge_z(ya, yb, x, mod0, mod1, gain1, w_out0, w_in1, tb):
    b, n, d = x.shape
    rows = (n // tb) * 2 * SUBLANES
    cg_start = 3 * C_WIDTH + D_WIDTH
    assert cg_start % (2 * D_WIDTH) == 0
    cg_blk = cg_start // (2 * D_WIDTH)
    full = lambda arr: _const_spec(arr.shape)
    xe, yae, ybe = _block_edges(x, tb), _block_edges(ya, tb), _block_edges(yb, tb)
    return pl.pallas_call(
        functools.partial(_edge_kernel, rows_per_batch=rows),
        grid=(1,),
        in_specs=[full(yae), full(ybe), full(xe), full(mod0), full(mod1), _const_spec((1, d)),
                  full(w_out0),
                  pl.BlockSpec((d, 2 * D_WIDTH), lambda i: (0, cg_blk), pipeline_mode=pl.Buffered(1))],
        out_specs=pl.BlockSpec((b * rows, D_WIDTH), lambda i: (0, 0)),
        out_shape=jax.ShapeDtypeStruct((b * rows, D_WIDTH), F32),
        compiler_params=_cparams("arbitrary"),
        name="conv_edge_rows",
    )(yae, ybe, xe, mod0, mod1, gain1, w_out0, w_in1).reshape(b, rows, D_WIDTH)


def _layer1_kernel(ya_ref, yb_ref, x_ref, ze_ref, mod0_ref, mod1_ref, g1_ref, wo0_ref, wi_ref,
                   vg_ref, ws_ref, bs_ref, cw_ref, wo1_ref, o_ref):
    d = x_ref.shape[-1]
    tb = x_ref.shape[1]
    j = pl.program_id(1)
    last_j = pl.num_programs(1) - 1
    mod1 = mod1_ref[pl.ds(pl.program_id(0), 1), :]
    gate0 = mod0_ref[pl.ds(pl.program_id(0), 1), 2 * d:]
    gate1 = mod1[:, 2 * d:]
    sub = tb // L1_SUB_BLOCKS
    subs = [slice(s * sub, (s + 1) * sub) for s in range(L1_SUB_BLOCKS)]
    col = lambda p, k: p[:, k * C_WIDTH:(k + 1) * C_WIDTH]
    n_gmlp = 3 * C_WIDTH

    x1s, xms = [], []
    for r in subs:
        x1, xm = _layer1_input(ya_ref[0, r, :], yb_ref[0, r, :], x_ref[0, r, :], gate0,
                               mod1, g1_ref[...], wo0_ref)
        x1s.append(x1)
        xms.append(xm)
    pgs = [jnp.dot(xm, wi_ref[:, 0:n_gmlp], preferred_element_type=F32) for xm in xms]
    pcs = [jnp.dot(xm, wi_ref[:, n_gmlp:], preferred_element_type=F32) for xm in xms]

    upd_c = []
    for pg in pgs:
        u = _gelu(col(pg, 0))
        vn = _rms(_gelu(col(pg, 1)), vg_ref[...]).astype(BF16)
        chunks = []
        for ci in range(sub // C_CHUNK):
            cr = slice(ci * C_CHUNK, (ci + 1) * C_CHUNK)
            groups = []
            for g in range(C_GROUPS):
                gl = slice(g * LANES, (g + 1) * LANES)
                groups.append(jnp.dot(ws_ref[g], vn[cr, gl], preferred_element_type=F32) + bs_ref[g])
            chunks.append(jnp.concatenate(groups, axis=1))
        o_c = u * jnp.concatenate(chunks, axis=0) * _silu(col(pg, 2))
        upd_c.append(jnp.dot(o_c.astype(BF16), wo1_ref[0:C_WIDTH, :], preferred_element_type=F32))

    z = jnp.concatenate([col(pc, 1) * col(pc, 2) for pc in pcs], axis=0)
    grp = 2 * SUBLANES
    prev_grp = ze_ref[0, pl.ds(pl.multiple_of(jnp.maximum(j - 1, 0) * grp + SUBLANES, SUBLANES),
                               SUBLANES), :]
    next_grp = ze_ref[0, pl.ds(pl.multiple_of(jnp.minimum(j + 1, last_j) * grp, SUBLANES),
                               SUBLANES), :]
    z_prev_row = jnp.where(j == 0, 0.0, prev_grp[SUBLANES - 1:, :])
    z_next_row = jnp.where(j == last_j, 0.0, next_grp[0:1, :])
    rowi = lax.broadcasted_iota(jnp.int32, z.shape, 0)
    z_prev = jnp.where(rowi == 0, z_prev_row, pltpu.roll(z, 1, 0))
    z_next = jnp.where(rowi == tb - 1, z_next_row, pltpu.roll(z, tb - 1, 0))
    conv = cw_ref[0:1, :] * z_prev + cw_ref[1:2, :] * z + cw_ref[2:3, :] * z_next

    for r, x1, pc, uc in zip(subs, x1s, pcs, upd_c):
        o_d = col(pc, 0) * conv[r, :] * _silu(col(pc, 3))
        upd = uc + jnp.dot(o_d.astype(BF16), wo1_ref[C_WIDTH:, :], preferred_element_type=F32)
        o_ref[0, r, :] = x1 + gate1 * upd


def _layer1(ya, yb, x, ze, mod0, mod1, gain1, w_out0, w_in1, v_gain, w_s, b_s, conv_w, w_out1, tb):
    b, n, d = x.shape
    half = ya.shape[-1]
    tok = lambda width: pl.BlockSpec((1, tb, width), lambda i, j: (i, j, 0))
    modspec = _const_spec(mod0.shape)
    return pl.pallas_call(
        _layer1_kernel,
        grid=(b, n // tb),
        in_specs=[tok(half), tok(half), tok(d),
                  pl.BlockSpec((1,) + ze.shape[1:], lambda i, j: (i, 0, 0)),
                  modspec, modspec, _const_spec((1, d)),
                  _const_spec(w_out0.shape), _const_spec(w_in1.shape),
                  _const_spec((1, C_WIDTH)),
                  _const_spec((C_GROUPS, C_CHUNK, C_CHUNK)),
                  _const_spec((C_GROUPS, C_CHUNK, LANES)),
                  _const_spec((3, D_WIDTH)),
                  _const_spec(w_out1.shape)],
        out_specs=tok(d),
        out_shape=jax.ShapeDtypeStruct((b, n, d), F32),
        compiler_params=_cparams("arbitrary", "arbitrary"),
        name="outproj_even_layer_odd",
    )(ya, yb, x, ze, mod0, mod1, gain1, w_out0, w_in1, v_gain, w_s, b_s, conv_w, w_out1)


def _rope_tables(n):
    rows_ = n // GRID_W
    row = np.repeat(np.arange(rows_, dtype=np.float64), GRID_W)
    col = np.tile(np.arange(GRID_W, dtype=np.float64), rows_)
    n_freq = A_HEAD_DIM // 4
    inv = ROPE_THETA ** (-np.arange(n_freq, dtype=np.float64) / n_freq)
    ang = np.concatenate([row[:, None] * inv, col[:, None] * inv], axis=-1)
    cos, sin = np.cos(ang), np.sin(ang)
    reps = QK_SLAB // A_HEAD_DIM
    return (jnp.asarray(np.tile(np.concatenate([cos, cos], axis=-1), (1, reps)), F32),
            jnp.asarray(np.tile(np.concatenate([-sin, sin], axis=-1), (1, reps)), F32))


def kernel(x, c, ctx, c_ctx, norm_gain, ada_w, ada_b, even_w_in, even_w_out, attn_qk_gain,
           attn_lambda, attn_subln_gain, hgrn_lb_logits, hgrn_norm_gain, odd_w_in, odd_w_out,
           gmlp_v_gain, gmlp_w_s, gmlp_b_s, conv_w):
    b, n, d = x.shape
    assert b + 1 <= COND_ROWS and n % 512 == 0 and ctx.shape[1] % HGRN_CHUNK == 0
    assert norm_gain.shape[0] == 2, "two-layer block: one even layer then one odd layer"

    cond = jnp.concatenate([c, c_ctx[None, :], jnp.zeros((COND_ROWS - b - 1, d), F32)], axis=0)
    mod0, mod1 = _adaln(cond, ada_w, ada_b)

    w_in0 = even_w_in[0].astype(BF16)
    gain0 = norm_gain[0].reshape(1, d)
    cos, sin_signed = _rope_tables(n)
    qk_gain = jnp.tile(attn_qk_gain[0], (1, QK_SLAB // A_HEAD_DIM))
    blk = np.arange(QK_SLAB) // A_HEAD_DIM
    bd = jnp.asarray(np.where(blk[:, None] == blk[None, :], 1.0 / A_HEAD_DIM, 0.0), BF16)
    q, k_lat, vt_lat, rest_x = _inproj(x, mod0, None, gain0, w_in0, cos, sin_signed, qk_gain, bd,
                                       INPROJ_BLOCK, True, True)
    w_ctx = jnp.concatenate([w_in0[:, g * A_WIDTH:(g + 1) * A_WIDTH] for g in CTX_GROUPS], axis=1)
    n_ctx = ctx.shape[1]
    k_ctx, vt_ctx, rest_c = _inproj(ctx.reshape(1, b * n_ctx, d), mod0, b, gain0, w_ctx, cos, sin_signed,
                                    qk_gain, bd, min(TOKEN_BLOCK, b * n_ctx), False, False)
    k_ctx = k_ctx.reshape(b, n_ctx, A_WIDTH)
    rest_c = rest_c.reshape(b, n_ctx, rest_c.shape[-1])

    lam_init = 0.8 - 0.6 * math.exp(-0.3 * 0)
    ya = _attention(q, k_lat, k_ctx, vt_lat, vt_ctx, rest_x, qk_gain,
                    attn_subln_gain[0].reshape(1, LANES), attn_lambda[0], lam_init, ATTN_Q_BLOCK)
    yb = _hgrn(rest_x, rest_c, hgrn_lb_logits, hgrn_norm_gain[0].reshape(1, LANES),
               HGRN_HEADS_PER_STEP)

    gain1 = norm_gain[1].reshape(1, d)
    w_out0, w_in1 = even_w_out[0].astype(BF16), odd_w_in[0].astype(BF16)
    ze = _edge_z(ya, yb, x, mod0, mod1, gain1, w_out0, w_in1, L1_TOKEN_BLOCK)
    b_s = jnp.broadcast_to(gmlp_b_s[0][:, :, None], (C_GROUPS, C_CHUNK, LANES))
    return _layer1(ya, yb, x, ze, mod0, mod1, gain1, w_out0, w_in1,
                   gmlp_v_gain[0].reshape(1, C_WIDTH), gmlp_w_s[0].astype(BF16), b_s, conv_w[0],
                   odd_w_out[0].astype(BF16), L1_TOKEN_BLOCK)
```

```python
import functools
import math

import jax
import jax.numpy as jnp
import numpy as np
from jax import lax
from jax.experimental import pallas as pl
from jax.experimental.pallas import tpu as pltpu

F32 = jnp.float32
BF16 = jnp.bfloat16

EPS = 1e-6
GRID_W = 64
ROPE_THETA = 10000.0
A_HEADS = 4
A_HEAD_DIM = 64
A_WIDTH = 2 * A_HEADS * A_HEAD_DIM
B_HEADS = 4
B_DIM = 128
B_WIDTH = B_HEADS * B_DIM
C_GROUPS = 4
C_CHUNK = 128
C_WIDTH = 512
D_WIDTH = 512
EVEN_IN = 4 * A_WIDTH + 5 * B_WIDTH
ODD_IN = 3 * C_WIDTH + 4 * D_WIDTH

CTX_GROUPS = (1, 2, 5, 6, 7)
REST_COL_GATE_A, REST_COL_Q, REST_COL_I, REST_COL_FF, REST_COL_FB, REST_COL_GATE_B = range(6)
CTX_COL_I, CTX_COL_FF, CTX_COL_FB = range(3)
QK_SLAB = 256

LANES = 128
HGRN_CHUNK = 128
HGRN_HEADS_PER_STEP = 4
HGRN_GROUP = 8
HGRN_UNROLL = 4
HGRN_DIAG = 8
SCORE_BOUND = 32.0
EXP2_CLAMP = 115.0
LOG2E = math.log2(math.e)
Q_SCALE = A_HEAD_DIM ** -0.5 * LOG2E
TOKEN_BLOCK = 512
INPROJ_BLOCK = 1024
L1_TOKEN_BLOCK = 1024
ATTN_Q_TILE = 256
ATTN_Q_BLOCK = 512
ATTN_LOOKAHEAD = 1
SUBLANES = 8
BF16_ROWS_PER_VREG = 16
L1_SUB_BLOCKS = 4
COND_ROWS = 16
ADALN_COL_BLOCK = 512
VMEM_LIMIT = 56 * 1024 * 1024


def _cparams(*sem):
    return pltpu.CompilerParams(dimension_semantics=sem, vmem_limit_bytes=VMEM_LIMIT)


def _const_spec(shape):
    nd = len(shape)
    return pl.BlockSpec(shape, lambda *_: (0,) * nd, pipeline_mode=pl.Buffered(1))


def _sigmoid(t):
    return 0.5 + 0.5 * jnp.tanh(0.5 * t)


def _silu(t):
    return t * _sigmoid(t)


def _gelu(t):
    return 0.5 * t * (1.0 + lax.erf(t * (1.0 / math.sqrt(2.0))))


def _rms(t, gain):
    ms = jnp.mean(t * t, axis=-1, keepdims=True)
    return t * lax.rsqrt(ms + EPS) * gain


def _adaln_kernel(cond_ref, w_ref, b_ref, *o_refs):
    a = _silu(cond_ref[...])
    for layer, o_ref in enumerate(o_refs):
        o_ref[...] = jnp.dot(a, w_ref[layer], preferred_element_type=F32) + b_ref[layer]


def _adaln(cond, ada_w, ada_b):
    depth, d, n3 = ada_w.shape
    tn = ADALN_COL_BLOCK
    out = pl.BlockSpec((COND_ROWS, tn), lambda j: (0, j))
    return pl.pallas_call(
        _adaln_kernel,
        grid=(n3 // tn,),
        in_specs=[
            pl.BlockSpec((COND_ROWS, d), lambda j: (0, 0)),
            pl.BlockSpec((depth, d, tn), lambda j: (0, 0, j)),
            pl.BlockSpec((depth, 1, tn), lambda j: (0, 0, j)),
        ],
        out_specs=[out] * depth,
        out_shape=[jax.ShapeDtypeStruct((COND_ROWS, n3), F32)] * depth,
        compiler_params=_cparams("arbitrary"),
        name="adaln",
    )(cond, ada_w, ada_b.reshape(depth, 1, n3))


def _modulate(x, gain, mod, d):
    shift = mod[:, 0:d]
    scale = mod[:, d:2 * d]
    return _rms(x, gain) * (1.0 + scale) + shift


def _rope(t, cos, sin_signed):
    lanes = t.shape[1]
    lane = lax.broadcasted_iota(jnp.int32, t.shape, 1)
    first = (lane % A_HEAD_DIM) < (A_HEAD_DIM // 2)
    partner = jnp.where(first,
                        pltpu.roll(t, lanes - A_HEAD_DIM // 2, 1),
                        pltpu.roll(t, A_HEAD_DIM // 2, 1))
    return t * cos + partner * sin_signed


def _inproj_kernel(x_ref, mod_ref, g_ref, wa_ref, wr1_ref, wr2_ref, cos_ref, sin_ref, qkg_ref, bd_ref,
                   *out_refs,
                   mod_row, has_q, rope):
    d = x_ref.shape[-1]
    tb = x_ref.shape[1]
    row = pl.program_id(0) if mod_row is None else mod_row
    n_attn = (3 if has_q else 2) * A_WIDTH
    rest_ref = out_refs[-1]
    n_rest = rest_ref.shape[-1]
    slabs = [slice(half * QK_SLAB, (half + 1) * QK_SLAB) for half in range(A_WIDTH // QK_SLAB)]
    sub = min(tb, TOKEN_BLOCK)

    pending = []
    for r in [slice(s * sub, (s + 1) * sub) for s in range(tb // sub)]:
        xm = _modulate(x_ref[0, r, :], g_ref[...], mod_ref[pl.ds(row, 1), :], d).astype(BF16)
        attn = jnp.dot(xm, wa_ref[...], preferred_element_type=F32)
        rest_ref[0, r, 0:n_rest // 2] = jnp.dot(
            xm, wr1_ref[...], preferred_element_type=F32).astype(rest_ref.dtype)
        groups = [attn[:, g * A_WIDTH:(g + 1) * A_WIDTH] for g in range(n_attn // A_WIDTH)]
        v = groups.pop()
        mean_sq = [[jnp.dot((t[:, sl] * t[:, sl]).astype(BF16), bd_ref[...],
                            preferred_element_type=F32) for sl in slabs] for t in groups]
        pad = BF16_ROWS_PER_VREG
        ms_rows = sum(ms[0:pad, :] for per_group in mean_sq for ms in per_group)
        anchor = jnp.concatenate([ms_rows * 0.0] * (d // QK_SLAB), axis=1)
        xm_late = jnp.concatenate([(xm[0:pad, :].astype(F32) + anchor).astype(BF16), xm[pad:, :]],
                                  axis=0)
        rest_ref[0, r, n_rest // 2:] = jnp.dot(
            xm_late, wr2_ref[...], preferred_element_type=F32).astype(rest_ref.dtype)
        pending.append((r, groups, v, mean_sq))

    qk_refs = out_refs[:-2]
    vt_ref = out_refs[-2]
    first_gain = 0 if has_q else 1
    for r, groups, v, mean_sq in pending:
        for gi, (t, o_ref) in enumerate(zip(groups, qk_refs)):
            gain = qkg_ref[first_gain + gi:first_gain + gi + 1, :]
            scale = Q_SCALE if (has_q and gi == 0) else 1.0
            for sl, ms in zip(slabs, mean_sq[gi]):
                tn = t[:, sl] * lax.rsqrt(ms + EPS) * gain
                if rope:
                    tn = _rope(tn, cos_ref[r, :], sin_ref[r, :])
                o_ref[0, r, sl] = (tn * scale).astype(o_ref.dtype)
        for h in range(A_HEADS):
            sl = slice(h * LANES, (h + 1) * LANES)
            vt_ref[0, sl, r] = v[:, sl].T.astype(vt_ref.dtype)


def _inproj(x, mod, mod_row, gain, w, cos, sin_signed, qk_gain, bd, tb, has_q, rope):
    b, n, d = x.shape
    n_attn = (3 if has_q else 2) * A_WIDTH
    n_rest = w.shape[1] - n_attn
    bounds = (0, n_attn, n_attn + n_rest // 2, w.shape[1])
    w_parts, w_specs = [], []
    for lo, hi in zip(bounds[:-1], bounds[1:]):
        if lo % (hi - lo) == 0:
            w_parts.append(w)
            w_specs.append(pl.BlockSpec((d, hi - lo), lambda i, j, blk=lo // (hi - lo): (0, blk),
                                        pipeline_mode=pl.Buffered(1)))
        else:
            w_parts.append(w[:, lo:hi])
            w_specs.append(_const_spec((d, hi - lo)))
    tok = lambda width: pl.BlockSpec((1, tb, width), lambda i, j: (i, j, 0))
    qk_out = [tok(A_WIDTH)] * (2 if has_q else 1)
    qk_shape = [jax.ShapeDtypeStruct((b, n, A_WIDTH), BF16)] * (2 if has_q else 1)
    return pl.pallas_call(
        functools.partial(_inproj_kernel, mod_row=mod_row, has_q=has_q, rope=rope),
        grid=(b, n // tb),
        in_specs=[
            tok(d),
            _const_spec(mod.shape),
            _const_spec((1, d)),
            *w_specs,
            pl.BlockSpec((tb, QK_SLAB), lambda i, j: (j, 0)),
            pl.BlockSpec((tb, QK_SLAB), lambda i, j: (j, 0)),
            _const_spec(qk_gain.shape),
            _const_spec(bd.shape),
        ],
        out_specs=qk_out + [pl.BlockSpec((1, A_WIDTH, tb), lambda i, j: (i, 0, j)), tok(n_rest)],
        out_shape=qk_shape + [jax.ShapeDtypeStruct((b, A_WIDTH, n), BF16),
                              jax.ShapeDtypeStruct((b, n, n_rest), BF16)],
        compiler_params=_cparams("arbitrary", "arbitrary"),
        name="inproj_even",
    )(x, mod, gain, *w_parts, cos, sin_signed, qk_gain, bd)


def _attn_kernel(q_ref, kl_ref, kc_ref, vtl_ref, vtc_ref, g_ref, qkg_ref, subg_ref, lamp_ref,
                 o_ref, *, lam_init):
    heads = [slice(h * LANES, (h + 1) * LANES) for h in range(A_HEADS)]
    lane = lax.broadcasted_iota(jnp.int32, (1, LANES), 1)
    nt = (((1,), (1,)), ((), ()))

    lp = lamp_ref[...]
    lam = (jnp.exp(jnp.sum(lp[0:1] * lp[1:2], axis=-1, keepdims=True))
           - jnp.exp(jnp.sum(lp[2:3] * lp[3:4], axis=-1, keepdims=True)) + lam_init)
    score_bound = ((A_HEAD_DIM * Q_SCALE) * jnp.max(jnp.abs(qkg_ref[0:1, :]))
                   * jnp.max(jnp.abs(qkg_ref[1:2, :])))

    n_sub = q_ref.shape[1] // ATTN_Q_TILE
    items = [(slice(qi * ATTN_Q_TILE, (qi + 1) * ATTN_Q_TILE), h)
             for qi in range(n_sub) for h in range(A_HEADS)]

    def scores(item):
        rows, h = item
        out = []
        for m in range(2):
            qm = jnp.where((lane // A_HEAD_DIM) == m, q_ref[0, rows, heads[h]], 0).astype(BF16)
            out.append([lax.dot_general(k_ref[0, :, heads[h]], qm, nt, preferred_element_type=F32)
                        for k_ref in (kc_ref, kl_ref)])
        return out

    def run_heads(shift):
        queued = [scores(item) for item in items[:ATTN_LOOKAHEAD]]
        for idx, (rows, h) in enumerate(items):
            sl = heads[h]
            s_maps = queued.pop(0)
            if idx + ATTN_LOOKAHEAD < len(items):
                queued.append(scores(items[idx + ATTN_LOOKAHEAD]))
            probs = []
            for s_parts in s_maps:
                if shift:
                    top = functools.reduce(jnp.maximum,
                                           [jnp.max(s, axis=0, keepdims=True) for s in s_parts])
                    s_parts = [s - top for s in s_parts]
                p_parts = [jnp.exp2(s) for s in s_parts]
                probs.append((p_parts, sum(jnp.sum(p, axis=0, keepdims=True) for p in p_parts)))
            (p0, l0), (p1, l1) = probs
            ratio = lam * l0 * (1.0 / l1)
            ot = sum(jnp.dot(vt_ref[0, sl, :], (pa - pb * ratio).astype(BF16),
                             preferred_element_type=F32)
                     for vt_ref, pa, pb in zip((vtc_ref, vtl_ref), p0, p1)) * (1.0 / l0)
            ms = jnp.mean(ot * ot, axis=0, keepdims=True)
            on = (ot * lax.rsqrt(ms + EPS)).T * (subg_ref[...] * (1.0 - lam_init))
            o_ref[0, rows, sl] = (on * _silu(g_ref[0, rows, sl].astype(F32))).astype(o_ref.dtype)

    no_shift_ok = score_bound <= SCORE_BOUND
    pl.when(no_shift_ok)(functools.partial(run_heads, False))
    pl.when(jnp.logical_not(no_shift_ok))(functools.partial(run_heads, True))


def _attention(q, k_lat, k_ctx, vt_lat, vt_ctx, rest, qk_gain, subln_g, lam_p, lam_init, tq):
    b, n, w = q.shape
    n_ctx = k_ctx.shape[1]
    return pl.pallas_call(
        functools.partial(_attn_kernel, lam_init=lam_init),
        grid=(b, n // tq),
        in_specs=[
            pl.BlockSpec((1, tq, w), lambda i, j: (i, j, 0)),
            pl.BlockSpec((1, n, w), lambda i, j: (i, 0, 0)),
            pl.BlockSpec((1, n_ctx, w), lambda i, j: (i, 0, 0)),
            pl.BlockSpec((1, w, n), lambda i, j: (i, 0, 0)),
            pl.BlockSpec((1, w, n_ctx), lambda i, j: (0, 0, i)),
            pl.BlockSpec((1, tq, w), lambda i, j: (i, j, REST_COL_GATE_A)),
            _const_spec(qk_gain.shape),
            _const_spec((1, LANES)),
            _const_spec((4, A_HEAD_DIM)),
        ],
        out_specs=pl.BlockSpec((1, tq, w), lambda i, j: (i, j, 0)),
        out_shape=jax.ShapeDtypeStruct((b, n, w), BF16),
        compiler_params=_cparams("arbitrary", "arbitrary"),
        name="diff_attn",
    )(q, k_lat, k_ctx, vt_lat, vt_ctx, rest, qk_gain, subln_g, lam_p)


def _split_bf16(t):
    hi = t.astype(BF16)
    return hi, (t - hi.astype(F32)).astype(BF16)


def _block_ref(g, block, row):
    c, w = g.shape
    g3 = g.reshape(c // block, block, w)
    return jnp.broadcast_to(g3[:, row:row + 1, :], g3.shape).reshape(c, w)


def _hgrn_tables(c):
    t = np.arange(c)[:, None]
    s = np.arange(c)[None, :]
    lvl = np.zeros((c, c), np.int32)
    lvl[(t // HGRN_DIAG == s // HGRN_DIAG) & (s <= t)] = 1
    b, k = HGRN_DIAG, 2
    while b < c:
        lvl[(t // b == s // b + 1) & ((s // b) % 2 == 0)] = k
        b, k = 2 * b, k + 1
    tri = (s <= t).astype(np.float32)
    return jnp.asarray(np.stack([tri, tri.T]), BF16), jnp.asarray(np.stack([lvl, lvl.T]))


def _hgrn_chunks(chains, tri_ref, lvl_ref, want_out):
    nt = (((1,), (1,)), ((), ()))
    tn = (((0,), (0,)), ((), ()))
    n = len(chains)
    c = chains[0][2].shape[0]

    kk, parts = [], []
    for (_, _, f_raw, lb, _, _) in chains:
        f = lb + (1.0 - lb) * _sigmoid(f_raw)
        kk.append(1.0 - f)
        parts.append(_split_bf16(jnp.log(f) * LOG2E))
    cum = [sum(jnp.dot(tri_ref[ch[5]], p, preferred_element_type=F32) for p in parts[i])
           for i, ch in enumerate(chains)]
    edge = [cum[i][0:1, :] if ch[5] else cum[i][c - 1:c, :] for i, ch in enumerate(chains)]

    outs = [None] * n
    if want_out:
        a = []
        for i, (q, _, _, _, _, d) in enumerate(chains):
            ref = _block_ref(cum[i], HGRN_DIAG, HGRN_DIAG // 2)
            qd = (q * jnp.exp2(jnp.minimum(cum[i] - ref, EXP2_CLAMP))).astype(BF16)
            kd = (kk[i] * jnp.exp2(jnp.minimum(ref - cum[i], EXP2_CLAMP))).astype(BF16)
            a.append(jnp.where(lvl_ref[d] == 1,
                               lax.dot_general(qd, kd, nt, preferred_element_type=F32), 0.0))
        b, k = HGRN_DIAG, 2
        while b < c:
            for i, (q, _, _, _, _, d) in enumerate(chains):
                ref = _block_ref(cum[i], 2 * b, b if d else b - 1)
                decay = jnp.exp2(cum[i] - ref)
                ql = (q * decay).astype(BF16)
                kl = (kk[i] * jnp.exp2(ref - cum[i])).astype(BF16)
                a[i] = jnp.where(lvl_ref[d] == k,
                                 lax.dot_general(ql, kl, nt, preferred_element_type=F32), a[i])
            b, k = 2 * b, k + 1
        for i, (q, v, _, _, st, _) in enumerate(chains):
            o = jnp.dot(a[i].astype(BF16), v, preferred_element_type=F32)
            outs[i] = o + lax.dot_general((q * jnp.exp2(cum[i])).astype(BF16), st.astype(BF16), nt,
                                          preferred_element_type=F32)

    sts = []
    for i, (_, v, _, _, st, _) in enumerate(chains):
        kg = (kk[i] * jnp.exp2(edge[i] - cum[i])).astype(BF16)
        upd = lax.dot_general(v, kg, tn, preferred_element_type=F32)
        sts.append(st * jnp.exp2(edge[i]) + upd)
    return outs, sts


def _hgrn_kernel(q_ref, i_ref, ff_ref, fb_ref, g_ref, ic_ref, ffc_ref, fbc_ref,
                 lbl_ref, ng_ref, tri_ref, lvl_ref, o_ref, acc_scr):
    c = HGRN_CHUNK
    heads = q_ref.shape[2] // LANES
    nc_lat = q_ref.shape[1] // c
    nc_ctx = ic_ref.shape[1] // c
    f_lat = (ff_ref, fb_ref)
    f_ctx = (ffc_ref, fbc_ref)

    def lower_bound(direction, sl):
        logits = [lbl_ref[direction, l, :, sl] for l in range(lbl_ref.shape[1])]
        top = functools.reduce(jnp.maximum, logits)
        e = [jnp.exp(t - top) for t in logits]
        return e[0] / sum(e)

    lanes = [slice(h * LANES, (h + 1) * LANES) for h in range(heads)]
    lbs = [[lower_bound(d, sl) for sl in lanes] for d in (0, 1)]

    def rows(i):
        return pl.ds(pl.multiple_of(i * c, c), c)

    def ctx_step(j, sts):
        chains = []
        for d in (0, 1):
            r = rows(nc_ctx - 1 - j if d else j)
            for h, sl in enumerate(lanes):
                chains.append((None, ic_ref[0, r, sl], f_ctx[d][0, r, sl].astype(F32),
                               lbs[d][h], sts[d * heads + h], d))
        return tuple(_hgrn_chunks(chains, tri_ref, lvl_ref, False)[1])

    def lat_step(j, sts, second_visit):
        chains, where = [], []
        for d in (0, 1):
            r = rows(nc_lat - 1 - j if d else j)
            for h, sl in enumerate(lanes):
                chains.append((q_ref[0, r, sl].astype(F32), i_ref[0, r, sl],
                               f_lat[d][0, r, sl].astype(F32), lbs[d][h], sts[d * heads + h], d))
                where.append((r, sl))
        outs, new = [], []
        for g0 in range(0, len(chains), HGRN_GROUP):
            o_g, s_g = _hgrn_chunks(chains[g0:g0 + HGRN_GROUP], tri_ref, lvl_ref, True)
            outs += o_g
            new += s_g
        for o, (r, sl) in zip(outs, where):
            if second_visit:
                y = _rms(acc_scr[r, sl] + o, ng_ref[...]) * _silu(g_ref[0, r, sl].astype(F32))
                o_ref[0, r, sl] = y.astype(o_ref.dtype)
            else:
                acc_scr[r, sl] = o
        return tuple(new)

    sts = tuple(jnp.zeros((B_DIM, B_DIM), F32) for _ in range(2 * heads))
    sts = lax.fori_loop(0, nc_ctx, ctx_step, sts, unroll=True)
    sts = lax.fori_loop(0, nc_lat // 2, functools.partial(lat_step, second_visit=False), sts,
                        unroll=HGRN_UNROLL)
    lax.fori_loop(nc_lat // 2, nc_lat, functools.partial(lat_step, second_visit=True), sts,
                  unroll=HGRN_UNROLL)


def _hgrn(rest_x, rest_c, lb_logits, norm_g, heads_per_step):
    b, n, _ = rest_x.shape
    n_ctx = rest_c.shape[1]
    assert (n // HGRN_CHUNK) % 2 == 0 and B_HEADS % heads_per_step == 0
    w = heads_per_step * LANES
    steps = B_HEADS // heads_per_step

    def xs(group):
        return pl.BlockSpec((1, n, w), lambda i, h: (i, 0, group * steps + h))

    def cs(group):
        return pl.BlockSpec((1, n_ctx, w), lambda i, h: (i, 0, group * steps + h))

    n_layers = lb_logits.shape[1]
    tri, lvl = _hgrn_tables(HGRN_CHUNK)
    return pl.pallas_call(
        _hgrn_kernel,
        grid=(b, steps),
        in_specs=[
            xs(REST_COL_Q), xs(REST_COL_I), xs(REST_COL_FF), xs(REST_COL_FB), xs(REST_COL_GATE_B),
            cs(CTX_COL_I), cs(CTX_COL_FF), cs(CTX_COL_FB),
            pl.BlockSpec((2, n_layers, 1, w), lambda i, h: (0, 0, 0, h)),
            _const_spec((1, LANES)),
            _const_spec(tri.shape),
            _const_spec(lvl.shape),
        ],
        out_specs=pl.BlockSpec((1, n, w), lambda i, h: (i, 0, h)),
        out_shape=jax.ShapeDtypeStruct((b, n, B_WIDTH), BF16),
        scratch_shapes=[pltpu.VMEM((n, w), F32)],
        compiler_params=_cparams("arbitrary", "arbitrary"),
        name="hgrn2",
    )(rest_x, rest_x, rest_x, rest_x, rest_x, rest_c, rest_c, rest_c,
      lb_logits.reshape(2, n_layers, 1, B_WIDTH), norm_g, tri, lvl)


def _layer1_input(ya, yb, x, gate0, mod1, gain1, wo0_ref):
    d = x.shape[-1]
    half = ya.shape[-1]
    upd = (jnp.dot(ya, wo0_ref[0:half, :], preferred_element_type=F32)
           + jnp.dot(yb, wo0_ref[half:, :], preferred_element_type=F32))
    x1 = x + gate0 * upd
    return x1, _modulate(x1, gain1, mod1, d).astype(BF16)


def _edge_kernel(ya_ref, yb_ref, x_ref, mod0_ref, mod1_ref, g1_ref, wo0_ref, wi_ref, z_ref,
                 *, rows_per_batch):
    d = x_ref.shape[-1]

    def per_row(ref, lo, hi):
        return jnp.concatenate([jnp.broadcast_to(ref[i:i + 1, lo:hi], (rows_per_batch, hi - lo))
                                for i in range(x_ref.shape[0] // rows_per_batch)], axis=0)

    _, xm = _layer1_input(ya_ref[...], yb_ref[...], x_ref[...], per_row(mod0_ref, 2 * d, 3 * d),
                          per_row(mod1_ref, 0, 2 * d), g1_ref[...], wo0_ref)
    p = jnp.dot(xm, wi_ref[...], preferred_element_type=F32)
    z_ref[...] = p[:, :D_WIDTH] * p[:, D_WIDTH:]


def _block_edges(t, tb):
    b, n, w = t.shape
    te = t.reshape(b, n // tb, tb, w)
    return jnp.concatenate([te[:, :, :SUBLANES], te[:, :, tb - SUBLANES:]], axis=2).reshape(-1, w)


def _edge_z(ya, yb, x, mod0, mod1, gain1, w_out0, w_in1, tb):
    b, n, d = x.shape
    rows = (n // tb) * 2 * SUBLANES
    cg_start = 3 * C_WIDTH + D_WIDTH
    assert cg_start % (2 * D_WIDTH) == 0
    cg_blk = cg_start // (2 * D_WIDTH)
    full = lambda arr: _const_spec(arr.shape)
    xe, yae, ybe = _block_edges(x, tb), _block_edges(ya, tb), _block_edges(yb, tb)
    return pl.pallas_call(
        functools.partial(_edge_kernel, rows_per_batch=rows),
        grid=(1,),
        in_specs=[full(yae), full(ybe), full(xe), full(mod0), full(mod1), _const_spec((1, d)),
                  full(w_out0),
                  pl.BlockSpec((d, 2 * D_WIDTH), lambda i: (0, cg_blk), pipeline_mode=pl.Buffered(1))],
        out_specs=pl.BlockSpec((b * rows, D_WIDTH), lambda i: (0, 0)),
        out_shape=jax.ShapeDtypeStruct((b * rows, D_WIDTH), F32),
        compiler_params=_cparams("arbitrary"),
        name="conv_edge_rows",
    )(yae, ybe, xe, mod0, mod1, gain1, w_out0, w_in1).reshape(b, rows, D_WIDTH)


def _layer1_kernel(ya_ref, yb_ref, x_ref, ze_ref, mod0_ref, mod1_ref, g1_ref, wo0_ref, wi_ref,
                   vg_ref, ws_ref, bs_ref, cw_ref, wo1_ref, o_ref):
    d = x_ref.shape[-1]
    tb = x_ref.shape[1]
    j = pl.program_id(1)
    last_j = pl.num_programs(1) - 1
    mod1 = mod1_ref[pl.ds(pl.program_id(0), 1), :]
    gate0 = mod0_ref[pl.ds(pl.program_id(0), 1), 2 * d:]
    gate1 = mod1[:, 2 * d:]
    sub = tb // L1_SUB_BLOCKS
    subs = [slice(s * sub, (s + 1) * sub) for s in range(L1_SUB_BLOCKS)]
    col = lambda p, k: p[:, k * C_WIDTH:(k + 1) * C_WIDTH]
    n_gmlp = 3 * C_WIDTH

    x1s, xms = [], []
    for r in subs:
        x1, xm = _layer1_input(ya_ref[0, r, :], yb_ref[0, r, :], x_ref[0, r, :], gate0,
                               mod1, g1_ref[...], wo0_ref)
        x1s.append(x1)
        xms.append(xm)
    pgs = [jnp.dot(xm, wi_ref[:, 0:n_gmlp], preferred_element_type=F32) for xm in xms]
    pcs = [jnp.dot(xm, wi_ref[:, n_gmlp:], preferred_element_type=F32) for xm in xms]

    upd_c = []
    for pg in pgs:
        u = _gelu(col(pg, 0))
        vn = _rms(_gelu(col(pg, 1)), vg_ref[...]).astype(BF16)
        chunks = []
        for ci in range(sub // C_CHUNK):
            cr = slice(ci * C_CHUNK, (ci + 1) * C_CHUNK)
            groups = []
            for g in range(C_GROUPS):
                gl = slice(g * LANES, (g + 1) * LANES)
                groups.append(jnp.dot(ws_ref[g], vn[cr, gl], preferred_element_type=F32) + bs_ref[g])
            chunks.append(jnp.concatenate(groups, axis=1))
        o_c = u * jnp.concatenate(chunks, axis=0) * _silu(col(pg, 2))
        upd_c.append(jnp.dot(o_c.astype(BF16), wo1_ref[0:C_WIDTH, :], preferred_element_type=F32))

    z = jnp.concatenate([col(pc, 1) * col(pc, 2) for pc in pcs], axis=0)
    grp = 2 * SUBLANES
    prev_grp = ze_ref[0, pl.ds(pl.multiple_of(jnp.maximum(j - 1, 0) * grp + SUBLANES, SUBLANES),
                               SUBLANES), :]
    next_grp = ze_ref[0, pl.ds(pl.multiple_of(jnp.minimum(j + 1, last_j) * grp, SUBLANES),
                               SUBLANES), :]
    z_prev_row = jnp.where(j == 0, 0.0, prev_grp[SUBLANES - 1:, :])
    z_next_row = jnp.where(j == last_j, 0.0, next_grp[0:1, :])
    rowi = lax.broadcasted_iota(jnp.int32, z.shape, 0)
    z_prev = jnp.where(rowi == 0, z_prev_row, pltpu.roll(z, 1, 0))
    z_next = jnp.where(rowi == tb - 1, z_next_row, pltpu.roll(z, tb - 1, 0))
    conv = cw_ref[0:1, :] * z_prev + cw_ref[1:2, :] * z + cw_ref[2:3, :] * z_next

    for r, x1, pc, uc in zip(subs, x1s, pcs, upd_c):
        o_d = col(pc, 0) * conv[r, :] * _silu(col(pc, 3))
        upd = uc + jnp.dot(o_d.astype(BF16), wo1_ref[C_WIDTH:, :], preferred_element_type=F32)
        o_ref[0, r, :] = x1 + gate1 * upd


def _layer1(ya, yb, x, ze, mod0, mod1, gain1, w_out0, w_in1, v_gain, w_s, b_s, conv_w, w_out1, tb):
    b, n, d = x.shape
    half = ya.shape[-1]
    tok = lambda width: pl.BlockSpec((1, tb, width), lambda i, j: (i, j, 0))
    modspec = _const_spec(mod0.shape)
    return pl.pallas_call(
        _layer1_kernel,
        grid=(b, n // tb),
        in_specs=[tok(half), tok(half), tok(d),
                  pl.BlockSpec((1,) + ze.shape[1:], lambda i, j: (i, 0, 0)),
                  modspec, modspec, _const_spec((1, d)),
                  _const_spec(w_out0.shape), _const_spec(w_in1.shape),
                  _const_spec((1, C_WIDTH)),
                  _const_spec((C_GROUPS, C_CHUNK, C_CHUNK)),
                  _const_spec((C_GROUPS, C_CHUNK, LANES)),
                  _const_spec((3, D_WIDTH)),
                  _const_spec(w_out1.shape)],
        out_specs=tok(d),
        out_shape=jax.ShapeDtypeStruct((b, n, d), F32),
        compiler_params=_cparams("arbitrary", "arbitrary"),
        name="outproj_even_layer_odd",
    )(ya, yb, x, ze, mod0, mod1, gain1, w_out0, w_in1, v_gain, w_s, b_s, conv_w, w_out1)


def _rope_tables(n):
    rows_ = n // GRID_W
    row = np.repeat(np.arange(rows_, dtype=np.float64), GRID_W)
    col = np.tile(np.arange(GRID_W, dtype=np.float64), rows_)
    n_freq = A_HEAD_DIM // 4
    inv = ROPE_THETA ** (-np.arange(n_freq, dtype=np.float64) / n_freq)
    ang = np.concatenate([row[:, None] * inv, col[:, None] * inv], axis=-1)
    cos, sin = np.cos(ang), np.sin(ang)
    reps = QK_SLAB // A_HEAD_DIM
    return (jnp.asarray(np.tile(np.concatenate([cos, cos], axis=-1), (1, reps)), F32),
            jnp.asarray(np.tile(np.concatenate([-sin, sin], axis=-1), (1, reps)), F32))


def kernel(x, c, ctx, c_ctx, norm_gain, ada_w, ada_b, even_w_in, even_w_out, attn_qk_gain,
           attn_lambda, attn_subln_gain, hgrn_lb_logits, hgrn_norm_gain, odd_w_in, odd_w_out,
           gmlp_v_gain, gmlp_w_s, gmlp_b_s, conv_w):
    b, n, d = x.shape
    assert b + 1 <= COND_ROWS and ctx.shape[1] % HGRN_CHUNK == 0
    assert all(n % blk == 0 for blk in (INPROJ_BLOCK, L1_TOKEN_BLOCK, ATTN_Q_BLOCK, GRID_W))
    assert norm_gain.shape[0] == 2, "two-layer block: one even layer then one odd layer"

    cond = jnp.concatenate([c, c_ctx[None, :], jnp.zeros((COND_ROWS - b - 1, d), F32)], axis=0)
    mod0, mod1 = _adaln(cond, ada_w, ada_b)

    w_in0 = even_w_in[0].astype(BF16)
    gain0 = norm_gain[0].reshape(1, d)
    cos, sin_signed = _rope_tables(n)
    qk_gain = jnp.tile(attn_qk_gain[0], (1, QK_SLAB // A_HEAD_DIM))
    blk = np.arange(QK_SLAB) // A_HEAD_DIM
    bd = jnp.asarray(np.where(blk[:, None] == blk[None, :], 1.0 / A_HEAD_DIM, 0.0), BF16)
    q, k_lat, vt_lat, rest_x = _inproj(x, mod0, None, gain0, w_in0, cos, sin_signed, qk_gain, bd,
                                       INPROJ_BLOCK, True, True)
    w_ctx = jnp.concatenate([w_in0[:, g * A_WIDTH:(g + 1) * A_WIDTH] for g in CTX_GROUPS], axis=1)
    n_ctx = ctx.shape[1]
    k_ctx, vt_ctx, rest_c = _inproj(ctx.reshape(1, b * n_ctx, d), mod0, b, gain0, w_ctx, cos, sin_signed,
                                    qk_gain, bd, min(TOKEN_BLOCK, b * n_ctx), False, False)
    k_ctx = k_ctx.reshape(b, n_ctx, A_WIDTH)
    rest_c = rest_c.reshape(b, n_ctx, rest_c.shape[-1])

    lam_init = 0.8 - 0.6 * math.exp(-0.3 * 0)
    ya = _attention(q, k_lat, k_ctx, vt_lat, vt_ctx, rest_x, qk_gain,
                    attn_subln_gain[0].reshape(1, LANES), attn_lambda[0], lam_init, ATTN_Q_BLOCK)
    yb = _hgrn(rest_x, rest_c, hgrn_lb_logits, hgrn_norm_gain[0].reshape(1, LANES),
               HGRN_HEADS_PER_STEP)

    gain1 = norm_gain[1].reshape(1, d)
    w_out0, w_in1 = even_w_out[0].astype(BF16), odd_w_in[0].astype(BF16)
    ze = _edge_z(ya, yb, x, mod0, mod1, gain1, w_out0, w_in1, L1_TOKEN_BLOCK)
    b_s = jnp.broadcast_to(gmlp_b_s[0][:, :, None], (C_GROUPS, C_CHUNK, LANES))
    return _layer1(ya, yb, x, ze, mod0, mod1, gain1, w_out0, w_in1,
                   gmlp_v_gain[0].reshape(1, C_WIDTH), gmlp_w_s[0].astype(BF16), b_s, conv_w[0],
                   odd_w_out[0].astype(BF16), L1_TOKEN_BLOCK)
```

```python
import functools
import math

import jax
import jax.numpy as jnp
import numpy as np
from jax import lax
from jax.experimental import pallas as pl
from jax.experimental.pallas import tpu as pltpu

F32 = jnp.float32
BF16 = jnp.bfloat16

EPS = 1e-6
GRID_W = 64
ROPE_THETA = 10000.0
A_HEADS = 4
A_HEAD_DIM = 64
A_WIDTH = 2 * A_HEADS * A_HEAD_DIM
B_HEADS = 4
B_DIM = 128
B_WIDTH = B_HEADS * B_DIM
C_GROUPS = 4
C_CHUNK = 128
C_WIDTH = 512
D_WIDTH = 512
EVEN_IN = 4 * A_WIDTH + 5 * B_WIDTH
ODD_IN = 3 * C_WIDTH + 4 * D_WIDTH

CTX_GROUPS = (1, 2, 5, 6, 7)
REST_COL_GATE_A, REST_COL_Q, REST_COL_I, REST_COL_FF, REST_COL_FB, REST_COL_GATE_B = range(6)
CTX_COL_I, CTX_COL_FF, CTX_COL_FB = range(3)
QK_SLAB = 256

LANES = 128
HGRN_CHUNK = 128
HGRN_HEADS_PER_STEP = 4
HGRN_GROUP = 8
HGRN_UNROLL = 4
HGRN_DIAG = 8
SCORE_BOUND = 32.0
EXP2_CLAMP = 115.0
LOG2E = math.log2(math.e)
Q_SCALE = A_HEAD_DIM ** -0.5 * LOG2E
TOKEN_BLOCK = 512
INPROJ_BLOCK = 1024
L1_TOKEN_BLOCK = 1024
ATTN_Q_TILE = 256
ATTN_Q_BLOCK = 1024
ATTN_LOOKAHEAD = 1
SUBLANES = 8
BF16_ROWS_PER_VREG = 16
L1_SUB_BLOCKS = 4
COND_ROWS = 16
ADALN_COL_BLOCK = 512
VMEM_LIMIT = 56 * 1024 * 1024


def _cparams(*sem):
    return pltpu.CompilerParams(dimension_semantics=sem, vmem_limit_bytes=VMEM_LIMIT)


def _const_spec(shape):
    nd = len(shape)
    return pl.BlockSpec(shape, lambda *_: (0,) * nd, pipeline_mode=pl.Buffered(1))


def _sigmoid(t):
    return 0.5 + 0.5 * jnp.tanh(0.5 * t)


def _silu(t):
    return t * _sigmoid(t)


def _gelu(t):
    return 0.5 * t * (1.0 + lax.erf(t * (1.0 / math.sqrt(2.0))))


def _rms(t, gain):
    ms = jnp.mean(t * t, axis=-1, keepdims=True)
    return t * lax.rsqrt(ms + EPS) * gain


def _adaln_kernel(cond_ref, w_ref, b_ref, *o_refs):
    a = _silu(cond_ref[...])
    for layer, o_ref in enumerate(o_refs):
        o_ref[...] = jnp.dot(a, w_ref[layer], preferred_element_type=F32) + b_ref[layer]


def _adaln(cond, ada_w, ada_b):
    depth, d, n3 = ada_w.shape
    tn = ADALN_COL_BLOCK
    out = pl.BlockSpec((COND_ROWS, tn), lambda j: (0, j))
    return pl.pallas_call(
        _adaln_kernel,
        grid=(n3 // tn,),
        in_specs=[
            pl.BlockSpec((COND_ROWS, d), lambda j: (0, 0)),
            pl.BlockSpec((depth, d, tn), lambda j: (0, 0, j)),
            pl.BlockSpec((depth, 1, tn), lambda j: (0, 0, j)),
        ],
        out_specs=[out] * depth,
        out_shape=[jax.ShapeDtypeStruct((COND_ROWS, n3), F32)] * depth,
        compiler_params=_cparams("arbitrary"),
        name="adaln",
    )(cond, ada_w, ada_b.reshape(depth, 1, n3))


def _modulate(x, gain, mod, d):
    shift = mod[:, 0:d]
    scale = mod[:, d:2 * d]
    return _rms(x, gain) * (1.0 + scale) + shift


def _rope(t, cos, sin_signed):
    lanes = t.shape[1]
    lane = lax.broadcasted_iota(jnp.int32, t.shape, 1)
    first = (lane % A_HEAD_DIM) < (A_HEAD_DIM // 2)
    partner = jnp.where(first,
                        pltpu.roll(t, lanes - A_HEAD_DIM // 2, 1),
                        pltpu.roll(t, A_HEAD_DIM // 2, 1))
    return t * cos + partner * sin_signed


def _inproj_kernel(x_ref, mod_ref, g_ref, wa_ref, wr1_ref, wr2_ref, cos_ref, sin_ref, qkg_ref, bd_ref,
                   *out_refs,
                   mod_row, has_q, rope):
    d = x_ref.shape[-1]
    tb = x_ref.shape[1]
    row = pl.program_id(0) if mod_row is None else mod_row
    n_attn = (3 if has_q else 2) * A_WIDTH
    rest_ref = out_refs[-1]
    n_rest = rest_ref.shape[-1]
    slabs = [slice(half * QK_SLAB, (half + 1) * QK_SLAB) for half in range(A_WIDTH // QK_SLAB)]
    sub = min(tb, TOKEN_BLOCK)

    pending = []
    for r in [slice(s * sub, (s + 1) * sub) for s in range(tb // sub)]:
        xm = _modulate(x_ref[0, r, :], g_ref[...], mod_ref[pl.ds(row, 1), :], d).astype(BF16)
        attn = jnp.dot(xm, wa_ref[...], preferred_element_type=F32)
        rest_ref[0, r, 0:n_rest // 2] = jnp.dot(
            xm, wr1_ref[...], preferred_element_type=F32).astype(rest_ref.dtype)
        groups = [attn[:, g * A_WIDTH:(g + 1) * A_WIDTH] for g in range(n_attn // A_WIDTH)]
        v = groups.pop()
        mean_sq = [[jnp.dot((t[:, sl] * t[:, sl]).astype(BF16), bd_ref[...],
                            preferred_element_type=F32) for sl in slabs] for t in groups]
        pad = BF16_ROWS_PER_VREG
        ms_rows = sum(ms[0:pad, :] for per_group in mean_sq for ms in per_group)
        anchor = jnp.concatenate([ms_rows * 0.0] * (d // QK_SLAB), axis=1)
        xm_late = jnp.concatenate([(xm[0:pad, :].astype(F32) + anchor).astype(BF16), xm[pad:, :]],
                                  axis=0)
        rest_ref[0, r, n_rest // 2:] = jnp.dot(
            xm_late, wr2_ref[...], preferred_element_type=F32).astype(rest_ref.dtype)
        pending.append((r, groups, v, mean_sq))

    qk_refs = out_refs[:-2]
    vt_ref = out_refs[-2]
    first_gain = 0 if has_q else 1
    for r, groups, v, mean_sq in pending:
        for gi, (t, o_ref) in enumerate(zip(groups, qk_refs)):
            gain = qkg_ref[first_gain + gi:first_gain + gi + 1, :]
            scale = Q_SCALE if (has_q and gi == 0) else 1.0
            for sl, ms in zip(slabs, mean_sq[gi]):
                tn = t[:, sl] * lax.rsqrt(ms + EPS) * gain
                if rope:
                    tn = _rope(tn, cos_ref[r, :], sin_ref[r, :])
                o_ref[0, r, sl] = (tn * scale).astype(o_ref.dtype)
        for h in range(A_HEADS):
            sl = slice(h * LANES, (h + 1) * LANES)
            vt_ref[0, sl, r] = v[:, sl].T.astype(vt_ref.dtype)


def _inproj(x, mod, mod_row, gain, w, cos, sin_signed, qk_gain, bd, tb, has_q, rope):
    b, n, d = x.shape
    n_attn = (3 if has_q else 2) * A_WIDTH
    n_rest = w.shape[1] - n_attn
    bounds = (0, n_attn, n_attn + n_rest // 2, w.shape[1])
    w_parts, w_specs = [], []
    for lo, hi in zip(bounds[:-1], bounds[1:]):
        if lo % (hi - lo) == 0:
            w_parts.append(w)
            w_specs.append(pl.BlockSpec((d, hi - lo), lambda i, j, blk=lo // (hi - lo): (0, blk),
                                        pipeline_mode=pl.Buffered(1)))
        else:
            w_parts.append(w[:, lo:hi])
            w_specs.append(_const_spec((d, hi - lo)))
    tok = lambda width: pl.BlockSpec((1, tb, width), lambda i, j: (i, j, 0))
    qk_out = [tok(A_WIDTH)] * (2 if has_q else 1)
    qk_shape = [jax.ShapeDtypeStruct((b, n, A_WIDTH), BF16)] * (2 if has_q else 1)
    return pl.pallas_call(
        functools.partial(_inproj_kernel, mod_row=mod_row, has_q=has_q, rope=rope),
        grid=(b, n // tb),
        in_specs=[
            tok(d),
            _const_spec(mod.shape),
            _const_spec((1, d)),
            *w_specs,
            pl.BlockSpec((tb, QK_SLAB), lambda i, j: (j, 0)),
            pl.BlockSpec((tb, QK_SLAB), lambda i, j: (j, 0)),
            _const_spec(qk_gain.shape),
            _const_spec(bd.shape),
        ],
        out_specs=qk_out + [pl.BlockSpec((1, A_WIDTH, tb), lambda i, j: (i, 0, j)), tok(n_rest)],
        out_shape=qk_shape + [jax.ShapeDtypeStruct((b, A_WIDTH, n), BF16),
                              jax.ShapeDtypeStruct((b, n, n_rest), BF16)],
        compiler_params=_cparams("arbitrary", "arbitrary"),
        name="inproj_even",
    )(x, mod, gain, *w_parts, cos, sin_signed, qk_gain, bd)


def _attn_kernel(q_ref, kl_ref, kc_ref, vtl_ref, vtc_ref, g_ref, qkg_ref, subg_ref, lamp_ref,
                 o_ref, *, lam_init):
    heads = [slice(h * LANES, (h + 1) * LANES) for h in range(A_HEADS)]
    lane = lax.broadcasted_iota(jnp.int32, (1, LANES), 1)
    nt = (((1,), (1,)), ((), ()))

    lp = lamp_ref[...]
    lam = (jnp.exp(jnp.sum(lp[0:1] * lp[1:2], axis=-1, keepdims=True))
           - jnp.exp(jnp.sum(lp[2:3] * lp[3:4], axis=-1, keepdims=True)) + lam_init)
    score_bound = ((A_HEAD_DIM * Q_SCALE) * jnp.max(jnp.abs(qkg_ref[0:1, :]))
                   * jnp.max(jnp.abs(qkg_ref[1:2, :])))

    n_sub = q_ref.shape[1] // ATTN_Q_TILE
    items = [(slice(qi * ATTN_Q_TILE, (qi + 1) * ATTN_Q_TILE), h)
             for qi in range(n_sub) for h in range(A_HEADS)]

    def scores(item):
        rows, h = item
        out = []
        for m in range(2):
            qm = jnp.where((lane // A_HEAD_DIM) == m, q_ref[0, rows, heads[h]], 0).astype(BF16)
            out.append([lax.dot_general(k_ref[0, :, heads[h]], qm, nt, preferred_element_type=F32)
                        for k_ref in (kc_ref, kl_ref)])
        return out

    def run_heads(shift):
        queued = [scores(item) for item in items[:ATTN_LOOKAHEAD]]
        for idx, (rows, h) in enumerate(items):
            sl = heads[h]
            s_maps = queued.pop(0)
            if idx + ATTN_LOOKAHEAD < len(items):
                queued.append(scores(items[idx + ATTN_LOOKAHEAD]))
            probs = []
            for s_parts in s_maps:
                if shift:
                    top = functools.reduce(jnp.maximum,
                                           [jnp.max(s, axis=0, keepdims=True) for s in s_parts])
                    s_parts = [s - top for s in s_parts]
                p_parts = [jnp.exp2(s) for s in s_parts]
                probs.append((p_parts, sum(jnp.sum(p, axis=0, keepdims=True) for p in p_parts)))
            (p0, l0), (p1, l1) = probs
            ratio = lam * l0 * (1.0 / l1)
            ot = sum(jnp.dot(vt_ref[0, sl, :], (pa - pb * ratio).astype(BF16),
                             preferred_element_type=F32)
                     for vt_ref, pa, pb in zip((vtc_ref, vtl_ref), p0, p1)) * (1.0 / l0)
            ms = jnp.mean(ot * ot, axis=0, keepdims=True)
            on = (ot * lax.rsqrt(ms + EPS)).T * (subg_ref[...] * (1.0 - lam_init))
            o_ref[0, rows, sl] = (on * _silu(g_ref[0, rows, sl].astype(F32))).astype(o_ref.dtype)

    no_shift_ok = score_bound <= SCORE_BOUND
    pl.when(no_shift_ok)(functools.partial(run_heads, False))
    pl.when(jnp.logical_not(no_shift_ok))(functools.partial(run_heads, True))


def _attention(q, k_lat, k_ctx, vt_lat, vt_ctx, rest, qk_gain, subln_g, lam_p, lam_init, tq):
    b, n, w = q.shape
    n_ctx = k_ctx.shape[1]
    return pl.pallas_call(
        functools.partial(_attn_kernel, lam_init=lam_init),
        grid=(b, n // tq),
        in_specs=[
            pl.BlockSpec((1, tq, w), lambda i, j: (i, j, 0)),
            pl.BlockSpec((1, n, w), lambda i, j: (i, 0, 0)),
            pl.BlockSpec((1, n_ctx, w), lambda i, j: (i, 0, 0)),
            pl.BlockSpec((1, w, n), lambda i, j: (i, 0, 0)),
            pl.BlockSpec((1, w, n_ctx), lambda i, j: (0, 0, i)),
            pl.BlockSpec((1, tq, w), lambda i, j: (i, j, REST_COL_GATE_A)),
            _const_spec(qk_gain.shape),
            _const_spec((1, LANES)),
            _const_spec((4, A_HEAD_DIM)),
        ],
        out_specs=pl.BlockSpec((1, tq, w), lambda i, j: (i, j, 0)),
        out_shape=jax.ShapeDtypeStruct((b, n, w), BF16),
        compiler_params=_cparams("arbitrary", "arbitrary"),
        name="diff_attn",
    )(q, k_lat, k_ctx, vt_lat, vt_ctx, rest, qk_gain, subln_g, lam_p)


def _split_bf16(t):
    hi = t.astype(BF16)
    return hi, (t - hi.astype(F32)).astype(BF16)


def _block_ref(g, block, row):
    c, w = g.shape
    g3 = g.reshape(c // block, block, w)
    return jnp.broadcast_to(g3[:, row:row + 1, :], g3.shape).reshape(c, w)


def _hgrn_tables(c):
    t = np.arange(c)[:, None]
    s = np.arange(c)[None, :]
    lvl = np.zeros((c, c), np.int32)
    lvl[(t // HGRN_DIAG == s // HGRN_DIAG) & (s <= t)] = 1
    b, k = HGRN_DIAG, 2
    while b < c:
        lvl[(t // b == s // b + 1) & ((s // b) % 2 == 0)] = k
        b, k = 2 * b, k + 1
    tri = (s <= t).astype(np.float32)
    return jnp.asarray(np.stack([tri, tri.T]), BF16), jnp.asarray(np.stack([lvl, lvl.T]))


def _hgrn_chunks(chains, tri_ref, lvl_ref, want_out):
    nt = (((1,), (1,)), ((), ()))
    tn = (((0,), (0,)), ((), ()))
    n = len(chains)
    c = chains[0][2].shape[0]

    kk, parts = [], []
    for (_, _, f_raw, lb, _, _) in chains:
        f = lb + (1.0 - lb) * _sigmoid(f_raw)
        kk.append(1.0 - f)
        parts.append(_split_bf16(jnp.log(f) * LOG2E))
    cum = [sum(jnp.dot(tri_ref[ch[5]], p, preferred_element_type=F32) for p in parts[i])
           for i, ch in enumerate(chains)]
    edge = [cum[i][0:1, :] if ch[5] else cum[i][c - 1:c, :] for i, ch in enumerate(chains)]

    outs = [None] * n
    if want_out:
        a = []
        for i, (q, _, _, _, _, d) in enumerate(chains):
            ref = _block_ref(cum[i], HGRN_DIAG, HGRN_DIAG // 2)
            qd = (q * jnp.exp2(jnp.minimum(cum[i] - ref, EXP2_CLAMP))).astype(BF16)
            kd = (kk[i] * jnp.exp2(jnp.minimum(ref - cum[i], EXP2_CLAMP))).astype(BF16)
            a.append(jnp.where(lvl_ref[d] == 1,
                               lax.dot_general(qd, kd, nt, preferred_element_type=F32), 0.0))
        b, k = HGRN_DIAG, 2
        while b < c:
            for i, (q, _, _, _, _, d) in enumerate(chains):
                ref = _block_ref(cum[i], 2 * b, b if d else b - 1)
                decay = jnp.exp2(cum[i] - ref)
                ql = (q * decay).astype(BF16)
                kl = (kk[i] * jnp.exp2(ref - cum[i])).astype(BF16)
                a[i] = jnp.where(lvl_ref[d] == k,
                                 lax.dot_general(ql, kl, nt, preferred_element_type=F32), a[i])
            b, k = 2 * b, k + 1
        for i, (q, v, _, _, st, _) in enumerate(chains):
            o = jnp.dot(a[i].astype(BF16), v, preferred_element_type=F32)
            outs[i] = o + lax.dot_general((q * jnp.exp2(cum[i])).astype(BF16), st.astype(BF16), nt,
                                          preferred_element_type=F32)

    sts = []
    for i, (_, v, _, _, st, _) in enumerate(chains):
        kg = (kk[i] * jnp.exp2(edge[i] - cum[i])).astype(BF16)
        upd = lax.dot_general(v, kg, tn, preferred_element_type=F32)
        sts.append(st * jnp.exp2(edge[i]) + upd)
    return outs, sts


def _hgrn_kernel(q_ref, i_ref, ff_ref, fb_ref, g_ref, ic_ref, ffc_ref, fbc_ref,
                 lbl_ref, ng_ref, tri_ref, lvl_ref, o_ref, acc_scr):
    c = HGRN_CHUNK
    heads = q_ref.shape[2] // LANES
    nc_lat = q_ref.shape[1] // c
    nc_ctx = ic_ref.shape[1] // c
    f_lat = (ff_ref, fb_ref)
    f_ctx = (ffc_ref, fbc_ref)

    def lower_bound(direction, sl):
        logits = [lbl_ref[direction, l, :, sl] for l in range(lbl_ref.shape[1])]
        top = functools.reduce(jnp.maximum, logits)
        e = [jnp.exp(t - top) for t in logits]
        return e[0] / sum(e)

    lanes = [slice(h * LANES, (h + 1) * LANES) for h in range(heads)]
    lbs = [[lower_bound(d, sl) for sl in lanes] for d in (0, 1)]

    def rows(i):
        return pl.ds(pl.multiple_of(i * c, c), c)

    def ctx_step(j, sts):
        chains = []
        for d in (0, 1):
            r = rows(nc_ctx - 1 - j if d else j)
            for h, sl in enumerate(lanes):
                chains.append((None, ic_ref[0, r, sl], f_ctx[d][0, r, sl].astype(F32),
                               lbs[d][h], sts[d * heads + h], d))
        return tuple(_hgrn_chunks(chains, tri_ref, lvl_ref, False)[1])

    def lat_step(j, sts, second_visit):
        chains, where = [], []
        for d in (0, 1):
            r = rows(nc_lat - 1 - j if d else j)
            for h, sl in enumerate(lanes):
                chains.append((q_ref[0, r, sl].astype(F32), i_ref[0, r, sl],
                               f_lat[d][0, r, sl].astype(F32), lbs[d][h], sts[d * heads + h], d))
                where.append((r, sl))
        outs, new = [], []
        for g0 in range(0, len(chains), HGRN_GROUP):
            o_g, s_g = _hgrn_chunks(chains[g0:g0 + HGRN_GROUP], tri_ref, lvl_ref, True)
            outs += o_g
            new += s_g
        for o, (r, sl) in zip(outs, where):
            if second_visit:
                y = _rms(acc_scr[r, sl] + o, ng_ref[...]) * _silu(g_ref[0, r, sl].astype(F32))
                o_ref[0, r, sl] = y.astype(o_ref.dtype)
            else:
                acc_scr[r, sl] = o
        return tuple(new)

    sts = tuple(jnp.zeros((B_DIM, B_DIM), F32) for _ in range(2 * heads))
    sts = lax.fori_loop(0, nc_ctx, ctx_step, sts, unroll=True)
    sts = lax.fori_loop(0, nc_lat // 2, functools.partial(lat_step, second_visit=False), sts,
                        unroll=HGRN_UNROLL)
    lax.fori_loop(nc_lat // 2, nc_lat, functools.partial(lat_step, second_visit=True), sts,
                  unroll=HGRN_UNROLL)


def _hgrn(rest_x, rest_c, lb_logits, norm_g, heads_per_step):
    b, n, _ = rest_x.shape
    n_ctx = rest_c.shape[1]
    assert (n // HGRN_CHUNK) % 2 == 0 and B_HEADS % heads_per_step == 0
    w = heads_per_step * LANES
    steps = B_HEADS // heads_per_step

    def xs(group):
        return pl.BlockSpec((1, n, w), lambda i, h: (i, 0, group * steps + h))

    def cs(group):
        return pl.BlockSpec((1, n_ctx, w), lambda i, h: (i, 0, group * steps + h))

    n_layers = lb_logits.shape[1]
    tri, lvl = _hgrn_tables(HGRN_CHUNK)
    return pl.pallas_call(
        _hgrn_kernel,
        grid=(b, steps),
        in_specs=[
            xs(REST_COL_Q), xs(REST_COL_I), xs(REST_COL_FF), xs(REST_COL_FB), xs(REST_COL_GATE_B),
            cs(CTX_COL_I), cs(CTX_COL_FF), cs(CTX_COL_FB),
            pl.BlockSpec((2, n_layers, 1, w), lambda i, h: (0, 0, 0, h)),
            _const_spec((1, LANES)),
            _const_spec(tri.shape),
            _const_spec(lvl.shape),
        ],
        out_specs=pl.BlockSpec((1, n, w), lambda i, h: (i, 0, h)),
        out_shape=jax.ShapeDtypeStruct((b, n, B_WIDTH), BF16),
        scratch_shapes=[pltpu.VMEM((n, w), F32)],
        compiler_params=_cparams("arbitrary", "arbitrary"),
        name="hgrn2",
    )(rest_x, rest_x, rest_x, rest_x, rest_x, rest_c, rest_c, rest_c,
      lb_logits.reshape(2, n_layers, 1, B_WIDTH), norm_g, tri, lvl)


def _layer1_input(ya, yb, x, gate0, mod1, gain1, wo0_ref):
    d = x.shape[-1]
    half = ya.shape[-1]
    upd = (jnp.dot(ya, wo0_ref[0:half, :], preferred_element_type=F32)
           + jnp.dot(yb, wo0_ref[half:, :], preferred_element_type=F32))
    x1 = x + gate0 * upd
    return x1, _modulate(x1, gain1, mod1, d).astype(BF16)


def _edge_kernel(ya_ref, yb_ref, x_ref, mod0_ref, mod1_ref, g1_ref, wo0_ref, wi_ref, z_ref,
                 *, rows_per_batch):
    d = x_ref.shape[-1]

    def per_row(ref, lo, hi):
        return jnp.concatenate([jnp.broadcast_to(ref[i:i + 1, lo:hi], (rows_per_batch, hi - lo))
                                for i in range(x_ref.shape[0] // rows_per_batch)], axis=0)

    _, xm = _layer1_input(ya_ref[...], yb_ref[...], x_ref[...], per_row(mod0_ref, 2 * d, 3 * d),
                          per_row(mod1_ref, 0, 2 * d), g1_ref[...], wo0_ref)
    p = jnp.dot(xm, wi_ref[...], preferred_element_type=F32)
    z_ref[...] = p[:, :D_WIDTH] * p[:, D_WIDTH:]


def _block_edges(t, tb):
    b, n, w = t.shape
    te = t.reshape(b, n // tb, tb, w)
    return jnp.concatenate([te[:, :, :SUBLANES], te[:, :, tb - SUBLANES:]], axis=2).reshape(-1, w)


def _edge_z(ya, yb, x, mod0, mod1, gain1, w_out0, w_in1, tb):
    b, n, d = x.shape
    rows = (n // tb) * 2 * SUBLANES
    cg_start = 3 * C_WIDTH + D_WIDTH
    assert cg_start % (2 * D_WIDTH) == 0
    cg_blk = cg_start // (2 * D_WIDTH)
    full = lambda arr: _const_spec(arr.shape)
    xe, yae, ybe = _block_edges(x, tb), _block_edges(ya, tb), _block_edges(yb, tb)
    return pl.pallas_call(
        functools.partial(_edge_kernel, rows_per_batch=rows),
        grid=(1,),
        in_specs=[full(yae), full(ybe), full(xe), full(mod0), full(mod1), _const_spec((1, d)),
                  full(w_out0),
                  pl.BlockSpec((d, 2 * D_WIDTH), lambda i: (0, cg_blk), pipeline_mode=pl.Buffered(1))],
        out_specs=pl.BlockSpec((b * rows, D_WIDTH), lambda i: (0, 0)),
        out_shape=jax.ShapeDtypeStruct((b * rows, D_WIDTH), F32),
        compiler_params=_cparams("arbitrary"),
        name="conv_edge_rows",
    )(yae, ybe, xe, mod0, mod1, gain1, w_out0, w_in1).reshape(b, rows, D_WIDTH)


def _layer1_kernel(ya_ref, yb_ref, x_ref, ze_ref, mod0_ref, mod1_ref, g1_ref, wo0_ref, wi_ref,
                   vg_ref, ws_ref, bs_ref, cw_ref, wo1_ref, o_ref):
    d = x_ref.shape[-1]
    tb = x_ref.shape[1]
    j = pl.program_id(1)
    last_j = pl.num_programs(1) - 1
    mod1 = mod1_ref[pl.ds(pl.program_id(0), 1), :]
    gate0 = mod0_ref[pl.ds(pl.program_id(0), 1), 2 * d:]
    gate1 = mod1[:, 2 * d:]
    sub = tb // L1_SUB_BLOCKS
    subs = [slice(s * sub, (s + 1) * sub) for s in range(L1_SUB_BLOCKS)]
    col = lambda p, k: p[:, k * C_WIDTH:(k + 1) * C_WIDTH]
    n_gmlp = 3 * C_WIDTH

    x1s, xms = [], []
    for r in subs:
        x1, xm = _layer1_input(ya_ref[0, r, :], yb_ref[0, r, :], x_ref[0, r, :], gate0,
                               mod1, g1_ref[...], wo0_ref)
        x1s.append(x1)
        xms.append(xm)
    pgs = [jnp.dot(xm, wi_ref[:, 0:n_gmlp], preferred_element_type=F32) for xm in xms]
    pcs = [jnp.dot(xm, wi_ref[:, n_gmlp:], preferred_element_type=F32) for xm in xms]

    upd_c = []
    for pg in pgs:
        u = _gelu(col(pg, 0))
        vn = _rms(_gelu(col(pg, 1)), vg_ref[...]).astype(BF16)
        chunks = []
        for ci in range(sub // C_CHUNK):
            cr = slice(ci * C_CHUNK, (ci + 1) * C_CHUNK)
            groups = []
            for g in range(C_GROUPS):
                gl = slice(g * LANES, (g + 1) * LANES)
                groups.append(jnp.dot(ws_ref[g], vn[cr, gl], preferred_element_type=F32) + bs_ref[g])
            chunks.append(jnp.concatenate(groups, axis=1))
        o_c = u * jnp.concatenate(chunks, axis=0) * _silu(col(pg, 2))
        upd_c.append(jnp.dot(o_c.astype(BF16), wo1_ref[0:C_WIDTH, :], preferred_element_type=F32))

    z = jnp.concatenate([col(pc, 1) * col(pc, 2) for pc in pcs], axis=0)
    grp = 2 * SUBLANES
    prev_grp = ze_ref[0, pl.ds(pl.multiple_of(jnp.maximum(j - 1, 0) * grp + SUBLANES, SUBLANES),
                               SUBLANES), :]
    next_grp = ze_ref[0, pl.ds(pl.multiple_of(jnp.minimum(j + 1, last_j) * grp, SUBLANES),
                               SUBLANES), :]
    z_prev_row = jnp.where(j == 0, 0.0, prev_grp[SUBLANES - 1:, :])
    z_next_row = jnp.where(j == last_j, 0.0, next_grp[0:1, :])
    rowi = lax.broadcasted_iota(jnp.int32, z.shape, 0)
    z_prev = jnp.where(rowi == 0, z_prev_row, pltpu.roll(z, 1, 0))
    z_next = jnp.where(rowi == tb - 1, z_next_row, pltpu.roll(z, tb - 1, 0))
    conv = cw_ref[0:1, :] * z_prev + cw_ref[1:2, :] * z + cw_ref[2:3, :] * z_next

    for r, x1, pc, uc in zip(subs, x1s, pcs, upd_c):
        o_d = col(pc, 0) * conv[r, :] * _silu(col(pc, 3))
        upd = uc + jnp.dot(o_d.astype(BF16), wo1_ref[C_WIDTH:, :], preferred_element_type=F32)
        o_ref[0, r, :] = x1 + gate1 * upd


def _layer1(ya, yb, x, ze, mod0, mod1, gain1, w_out0, w_in1, v_gain, w_s, b_s, conv_w, w_out1, tb):
    b, n, d = x.shape
    half = ya.shape[-1]
    tok = lambda width: pl.BlockSpec((1, tb, width), lambda i, j: (i, j, 0))
    modspec = _const_spec(mod0.shape)
    return pl.pallas_call(
        _layer1_kernel,
        grid=(b, n // tb),
        in_specs=[tok(half), tok(half), tok(d),
                  pl.BlockSpec((1,) + ze.shape[1:], lambda i, j: (i, 0, 0)),
                  modspec, modspec, _const_spec((1, d)),
                  _const_spec(w_out0.shape), _const_spec(w_in1.shape),
                  _const_spec((1, C_WIDTH)),
                  _const_spec((C_GROUPS, C_CHUNK, C_CHUNK)),
                  _const_spec((C_GROUPS, C_CHUNK, LANES)),
                  _const_spec((3, D_WIDTH)),
                  _const_spec(w_out1.shape)],
        out_specs=tok(d),
        out_shape=jax.ShapeDtypeStruct((b, n, d), F32),
        compiler_params=_cparams("arbitrary", "arbitrary"),
        name="outproj_even_layer_odd",
    )(ya, yb, x, ze, mod0, mod1, gain1, w_out0, w_in1, v_gain, w_s, b_s, conv_w, w_out1)


def _rope_tables(n):
    rows_ = n // GRID_W
    row = np.repeat(np.arange(rows_, dtype=np.float64), GRID_W)
    col = np.tile(np.arange(GRID_W, dtype=np.float64), rows_)
    n_freq = A_HEAD_DIM // 4
    inv = ROPE_THETA ** (-np.arange(n_freq, dtype=np.float64) / n_freq)
    ang = np.concatenate([row[:, None] * inv, col[:, None] * inv], axis=-1)
    cos, sin = np.cos(ang), np.sin(ang)
    reps = QK_SLAB // A_HEAD_DIM
    return (jnp.asarray(np.tile(np.concatenate([cos, cos], axis=-1), (1, reps)), F32),
            jnp.asarray(np.tile(np.concatenate([-sin, sin], axis=-1), (1, reps)), F32))


def kernel(x, c, ctx, c_ctx, norm_gain, ada_w, ada_b, even_w_in, even_w_out, attn_qk_gain,
           attn_lambda, attn_subln_gain, hgrn_lb_logits, hgrn_norm_gain, odd_w_in, odd_w_out,
           gmlp_v_gain, gmlp_w_s, gmlp_b_s, conv_w):
    b, n, d = x.shape
    assert b + 1 <= COND_ROWS and ctx.shape[1] % HGRN_CHUNK == 0
    assert all(n % blk == 0 for blk in (INPROJ_BLOCK, L1_TOKEN_BLOCK, ATTN_Q_BLOCK, GRID_W))
    assert norm_gain.shape[0] == 2, "two-layer block: one even layer then one odd layer"

    cond = jnp.concatenate([c, c_ctx[None, :], jnp.zeros((COND_ROWS - b - 1, d), F32)], axis=0)
    mod0, mod1 = _adaln(cond, ada_w, ada_b)

    w_in0 = even_w_in[0].astype(BF16)
    gain0 = norm_gain[0].reshape(1, d)
    cos, sin_signed = _rope_tables(n)
    qk_gain = jnp.tile(attn_qk_gain[0], (1, QK_SLAB // A_HEAD_DIM))
    blk = np.arange(QK_SLAB) // A_HEAD_DIM
    bd = jnp.asarray(np.where(blk[:, None] == blk[None, :], 1.0 / A_HEAD_DIM, 0.0), BF16)
    q, k_lat, vt_lat, rest_x = _inproj(x, mod0, None, gain0, w_in0, cos, sin_signed, qk_gain, bd,
                                       INPROJ_BLOCK, True, True)
    w_ctx = jnp.concatenate([w_in0[:, g * A_WIDTH:(g + 1) * A_WIDTH] for g in CTX_GROUPS], axis=1)
    n_ctx = ctx.shape[1]
    k_ctx, vt_ctx, rest_c = _inproj(ctx.reshape(1, b * n_ctx, d), mod0, b, gain0, w_ctx, cos, sin_signed,
                                    qk_gain, bd, min(TOKEN_BLOCK, b * n_ctx), False, False)
    k_ctx = k_ctx.reshape(b, n_ctx, A_WIDTH)
    rest_c = rest_c.reshape(b, n_ctx, rest_c.shape[-1])

    lam_init = 0.8 - 0.6 * math.exp(-0.3 * 0)
    ya = _attention(q, k_lat, k_ctx, vt_lat, vt_ctx, rest_x, qk_gain,
                    attn_subln_gain[0].reshape(1, LANES), attn_lambda[0], lam_init, ATTN_Q_BLOCK)
    yb = _hgrn(rest_x, rest_c, hgrn_lb_logits, hgrn_norm_gain[0].reshape(1, LANES),
               HGRN_HEADS_PER_STEP)

    gain1 = norm_gain[1].reshape(1, d)
    w_out0, w_in1 = even_w_out[0].astype(BF16), odd_w_in[0].astype(BF16)
    ze = _edge_z(ya, yb, x, mod0, mod1, gain1, w_out0, w_in1, L1_TOKEN_BLOCK)
    b_s = jnp.broadcast_to(gmlp_b_s[0][:, :, None], (C_GROUPS, C_CHUNK, LANES))
    return _layer1(ya, yb, x, ze, mod0, mod1, gain1, w_out0, w_in1,
                   gmlp_v_gain[0].reshape(1, C_WIDTH), gmlp_w_s[0].astype(BF16), b_s, conv_w[0],
                   odd_w_out[0].astype(BF16), L1_TOKEN_BLOCK)
```

```python
import functools
import math

import jax
import jax.numpy as jnp
import numpy as np
from jax import lax
from jax.experimental import pallas as pl
from jax.experimental.pallas import tpu as pltpu

F32 = jnp.float32
BF16 = jnp.bfloat16

EPS = 1e-6
GRID_W = 64
ROPE_THETA = 10000.0
A_HEADS = 4
A_HEAD_DIM = 64
A_WIDTH = 2 * A_HEADS * A_HEAD_DIM
B_HEADS = 4
B_DIM = 128
B_WIDTH = B_HEADS * B_DIM
C_GROUPS = 4
C_CHUNK = 128
C_WIDTH = 512
D_WIDTH = 512
EVEN_IN = 4 * A_WIDTH + 5 * B_WIDTH
ODD_IN = 3 * C_WIDTH + 4 * D_WIDTH

CTX_GROUPS = (1, 2, 5, 6, 7)
REST_COL_GATE_A, REST_COL_Q, REST_COL_I, REST_COL_FF, REST_COL_FB, REST_COL_GATE_B = range(6)
CTX_COL_I, CTX_COL_FF, CTX_COL_FB = range(3)
QK_SLAB = 256

LANES = 128
HGRN_CHUNK = 128
HGRN_HEADS_PER_STEP = 4
HGRN_GROUP = 8
HGRN_UNROLL = 8
HGRN_DIAG = 8
SCORE_BOUND = 32.0
EXP2_CLAMP = 115.0
LOG2E = math.log2(math.e)
Q_SCALE = A_HEAD_DIM ** -0.5 * LOG2E
TOKEN_BLOCK = 512
INPROJ_BLOCK = 1024
L1_TOKEN_BLOCK = 1024
ATTN_Q_TILE = 256
ATTN_Q_BLOCK = 512
ATTN_LOOKAHEAD = 1
SUBLANES = 8
BF16_ROWS_PER_VREG = 16
L1_SUB_BLOCKS = 4
COND_ROWS = 16
ADALN_COL_BLOCK = 512
VMEM_LIMIT = 56 * 1024 * 1024


def _cparams(*sem):
    return pltpu.CompilerParams(dimension_semantics=sem, vmem_limit_bytes=VMEM_LIMIT)


def _const_spec(shape):
    nd = len(shape)
    return pl.BlockSpec(shape, lambda *_: (0,) * nd, pipeline_mode=pl.Buffered(1))


def _sigmoid(t):
    return 0.5 + 0.5 * jnp.tanh(0.5 * t)


def _silu(t):
    return t * _sigmoid(t)


def _gelu(t):
    return 0.5 * t * (1.0 + lax.erf(t * (1.0 / math.sqrt(2.0))))


def _rms(t, gain):
    ms = jnp.mean(t * t, axis=-1, keepdims=True)
    return t * lax.rsqrt(ms + EPS) * gain


def _adaln_kernel(cond_ref, w_ref, b_ref, *o_refs):
    a = _silu(cond_ref[...])
    for layer, o_ref in enumerate(o_refs):
        o_ref[...] = jnp.dot(a, w_ref[layer], preferred_element_type=F32) + b_ref[layer]


def _adaln(cond, ada_w, ada_b):
    depth, d, n3 = ada_w.shape
    tn = ADALN_COL_BLOCK
    out = pl.BlockSpec((COND_ROWS, tn), lambda j: (0, j))
    return pl.pallas_call(
        _adaln_kernel,
        grid=(n3 // tn,),
        in_specs=[
            pl.BlockSpec((COND_ROWS, d), lambda j: (0, 0)),
            pl.BlockSpec((depth, d, tn), lambda j: (0, 0, j)),
            pl.BlockSpec((depth, 1, tn), lambda j: (0, 0, j)),
        ],
        out_specs=[out] * depth,
        out_shape=[jax.ShapeDtypeStruct((COND_ROWS, n3), F32)] * depth,
        compiler_params=_cparams("arbitrary"),
        name="adaln",
    )(cond, ada_w, ada_b.reshape(depth, 1, n3))


def _modulate(x, gain, mod, d):
    shift = mod[:, 0:d]
    scale = mod[:, d:2 * d]
    return _rms(x, gain) * (1.0 + scale) + shift


def _rope(t, cos, sin_signed):
    lanes = t.shape[1]
    lane = lax.broadcasted_iota(jnp.int32, t.shape, 1)
    first = (lane % A_HEAD_DIM) < (A_HEAD_DIM // 2)
    partner = jnp.where(first,
                        pltpu.roll(t, lanes - A_HEAD_DIM // 2, 1),
                        pltpu.roll(t, A_HEAD_DIM // 2, 1))
    return t * cos + partner * sin_signed


def _inproj_kernel(x_ref, mod_ref, g_ref, wa_ref, wr1_ref, wr2_ref, cos_ref, sin_ref, qkg_ref, bd_ref,
                   *out_refs,
                   mod_row, has_q, rope):
    d = x_ref.shape[-1]
    tb = x_ref.shape[1]
    row = pl.program_id(0) if mod_row is None else mod_row
    n_attn = (3 if has_q else 2) * A_WIDTH
    rest_ref = out_refs[-1]
    n_rest = rest_ref.shape[-1]
    slabs = [slice(half * QK_SLAB, (half + 1) * QK_SLAB) for half in range(A_WIDTH // QK_SLAB)]
    sub = min(tb, TOKEN_BLOCK)

    pending = []
    for r in [slice(s * sub, (s + 1) * sub) for s in range(tb // sub)]:
        xm = _modulate(x_ref[0, r, :], g_ref[...], mod_ref[pl.ds(row, 1), :], d).astype(BF16)
        attn = jnp.dot(xm, wa_ref[...], preferred_element_type=F32)
        rest_ref[0, r, 0:n_rest // 2] = jnp.dot(
            xm, wr1_ref[...], preferred_element_type=F32).astype(rest_ref.dtype)
        groups = [attn[:, g * A_WIDTH:(g + 1) * A_WIDTH] for g in range(n_attn // A_WIDTH)]
        v = groups.pop()
        mean_sq = [[jnp.dot((t[:, sl] * t[:, sl]).astype(BF16), bd_ref[...],
                            preferred_element_type=F32) for sl in slabs] for t in groups]
        pad = BF16_ROWS_PER_VREG
        ms_rows = sum(ms[0:pad, :] for per_group in mean_sq for ms in per_group)
        anchor = jnp.concatenate([ms_rows * 0.0] * (d // QK_SLAB), axis=1)
        xm_late = jnp.concatenate([(xm[0:pad, :].astype(F32) + anchor).astype(BF16), xm[pad:, :]],
                                  axis=0)
        rest_ref[0, r, n_rest // 2:] = jnp.dot(
            xm_late, wr2_ref[...], preferred_element_type=F32).astype(rest_ref.dtype)
        pending.append((r, groups, v, mean_sq))

    qk_refs = out_refs[:-2]
    vt_ref = out_refs[-2]
    first_gain = 0 if has_q else 1
    for r, groups, v, mean_sq in pending:
        for gi, (t, o_ref) in enumerate(zip(groups, qk_refs)):
            gain = qkg_ref[first_gain + gi:first_gain + gi + 1, :]
            scale = Q_SCALE if (has_q and gi == 0) else 1.0
            for sl, ms in zip(slabs, mean_sq[gi]):
                tn = t[:, sl] * lax.rsqrt(ms + EPS) * gain
                if rope:
                    tn = _rope(tn, cos_ref[r, :], sin_ref[r, :])
                o_ref[0, r, sl] = (tn * scale).astype(o_ref.dtype)
        for h in range(A_HEADS):
            sl = slice(h * LANES, (h + 1) * LANES)
            vt_ref[0, sl, r] = v[:, sl].T.astype(vt_ref.dtype)


def _inproj(x, mod, mod_row, gain, w, cos, sin_signed, qk_gain, bd, tb, has_q, rope):
    b, n, d = x.shape
    n_attn = (3 if has_q else 2) * A_WIDTH
    n_rest = w.shape[1] - n_attn
    bounds = (0, n_attn, n_attn + n_rest // 2, w.shape[1])
    w_parts, w_specs = [], []
    for lo, hi in zip(bounds[:-1], bounds[1:]):
        if lo % (hi - lo) == 0:
            w_parts.append(w)
            w_specs.append(pl.BlockSpec((d, hi - lo), lambda i, j, blk=lo // (hi - lo): (0, blk),
                                        pipeline_mode=pl.Buffered(1)))
        else:
            w_parts.append(w[:, lo:hi])
            w_specs.append(_const_spec((d, hi - lo)))
    tok = lambda width: pl.BlockSpec((1, tb, width), lambda i, j: (i, j, 0))
    qk_out = [tok(A_WIDTH)] * (2 if has_q else 1)
    qk_shape = [jax.ShapeDtypeStruct((b, n, A_WIDTH), BF16)] * (2 if has_q else 1)
    return pl.pallas_call(
        functools.partial(_inproj_kernel, mod_row=mod_row, has_q=has_q, rope=rope),
        grid=(b, n // tb),
        in_specs=[
            tok(d),
            _const_spec(mod.shape),
            _const_spec((1, d)),
            *w_specs,
            pl.BlockSpec((tb, QK_SLAB), lambda i, j: (j, 0)),
            pl.BlockSpec((tb, QK_SLAB), lambda i, j: (j, 0)),
            _const_spec(qk_gain.shape),
            _const_spec(bd.shape),
        ],
        out_specs=qk_out + [pl.BlockSpec((1, A_WIDTH, tb), lambda i, j: (i, 0, j)), tok(n_rest)],
        out_shape=qk_shape + [jax.ShapeDtypeStruct((b, A_WIDTH, n), BF16),
                              jax.ShapeDtypeStruct((b, n, n_rest), BF16)],
        compiler_params=_cparams("arbitrary", "arbitrary"),
        name="inproj_even",
    )(x, mod, gain, *w_parts, cos, sin_signed, qk_gain, bd)


def _attn_kernel(q_ref, kl_ref, kc_ref, vtl_ref, vtc_ref, g_ref, qkg_ref, subg_ref, lamp_ref,
                 o_ref, *, lam_init):
    heads = [slice(h * LANES, (h + 1) * LANES) for h in range(A_HEADS)]
    lane = lax.broadcasted_iota(jnp.int32, (1, LANES), 1)
    nt = (((1,), (1,)), ((), ()))

    lp = lamp_ref[...]
    lam = (jnp.exp(jnp.sum(lp[0:1] * lp[1:2], axis=-1, keepdims=True))
           - jnp.exp(jnp.sum(lp[2:3] * lp[3:4], axis=-1, keepdims=True)) + lam_init)
    score_bound = ((A_HEAD_DIM * Q_SCALE) * jnp.max(jnp.abs(qkg_ref[0:1, :]))
                   * jnp.max(jnp.abs(qkg_ref[1:2, :])))

    n_sub = q_ref.shape[1] // ATTN_Q_TILE
    items = [(slice(qi * ATTN_Q_TILE, (qi + 1) * ATTN_Q_TILE), h)
             for qi in range(n_sub) for h in range(A_HEADS)]

    def scores(item):
        rows, h = item
        out = []
        for m in range(2):
            qm = jnp.where((lane // A_HEAD_DIM) == m, q_ref[0, rows, heads[h]], 0).astype(BF16)
            out.append([lax.dot_general(k_ref[0, :, heads[h]], qm, nt, preferred_element_type=F32)
                        for k_ref in (kc_ref, kl_ref)])
        return out

    def run_heads(shift):
        queued = [scores(item) for item in items[:ATTN_LOOKAHEAD]]
        for idx, (rows, h) in enumerate(items):
            sl = heads[h]
            s_maps = queued.pop(0)
            if idx + ATTN_LOOKAHEAD < len(items):
                queued.append(scores(items[idx + ATTN_LOOKAHEAD]))
            probs = []
            for s_parts in s_maps:
                if shift:
                    top = functools.reduce(jnp.maximum,
                                           [jnp.max(s, axis=0, keepdims=True) for s in s_parts])
                    s_parts = [s - top for s in s_parts]
                p_parts = [jnp.exp2(s) for s in s_parts]
                probs.append((p_parts, sum(jnp.sum(p, axis=0, keepdims=True) for p in p_parts)))
            (p0, l0), (p1, l1) = probs
            ratio = lam * l0 * (1.0 / l1)
            ot = sum(jnp.dot(vt_ref[0, sl, :], (pa - pb * ratio).astype(BF16),
                             preferred_element_type=F32)
                     for vt_ref, pa, pb in zip((vtc_ref, vtl_ref), p0, p1)) * (1.0 / l0)
            ms = jnp.mean(ot * ot, axis=0, keepdims=True)
            on = (ot * lax.rsqrt(ms + EPS)).T * (subg_ref[...] * (1.0 - lam_init))
            o_ref[0, rows, sl] = (on * _silu(g_ref[0, rows, sl].astype(F32))).astype(o_ref.dtype)

    no_shift_ok = score_bound <= SCORE_BOUND
    pl.when(no_shift_ok)(functools.partial(run_heads, False))
    pl.when(jnp.logical_not(no_shift_ok))(functools.partial(run_heads, True))


def _attention(q, k_lat, k_ctx, vt_lat, vt_ctx, rest, qk_gain, subln_g, lam_p, lam_init, tq):
    b, n, w = q.shape
    n_ctx = k_ctx.shape[1]
    return pl.pallas_call(
        functools.partial(_attn_kernel, lam_init=lam_init),
        grid=(b, n // tq),
        in_specs=[
            pl.BlockSpec((1, tq, w), lambda i, j: (i, j, 0)),
            pl.BlockSpec((1, n, w), lambda i, j: (i, 0, 0)),
            pl.BlockSpec((1, n_ctx, w), lambda i, j: (i, 0, 0)),
            pl.BlockSpec((1, w, n), lambda i, j: (i, 0, 0)),
            pl.BlockSpec((1, w, n_ctx), lambda i, j: (0, 0, i)),
            pl.BlockSpec((1, tq, w), lambda i, j: (i, j, REST_COL_GATE_A)),
            _const_spec(qk_gain.shape),
            _const_spec((1, LANES)),
            _const_spec((4, A_HEAD_DIM)),
        ],
        out_specs=pl.BlockSpec((1, tq, w), lambda i, j: (i, j, 0)),
        out_shape=jax.ShapeDtypeStruct((b, n, w), BF16),
        compiler_params=_cparams("arbitrary", "arbitrary"),
        name="diff_attn",
    )(q, k_lat, k_ctx, vt_lat, vt_ctx, rest, qk_gain, subln_g, lam_p)


def _split_bf16(t):
    hi = t.astype(BF16)
    return hi, (t - hi.astype(F32)).astype(BF16)


def _block_ref(g, block, row):
    c, w = g.shape
    g3 = g.reshape(c // block, block, w)
    return jnp.broadcast_to(g3[:, row:row + 1, :], g3.shape).reshape(c, w)


def _hgrn_tables(c):
    t = np.arange(c)[:, None]
    s = np.arange(c)[None, :]
    lvl = np.zeros((c, c), np.int32)
    lvl[(t // HGRN_DIAG == s // HGRN_DIAG) & (s <= t)] = 1
    b, k = HGRN_DIAG, 2
    while b < c:
        lvl[(t // b == s // b + 1) & ((s // b) % 2 == 0)] = k
        b, k = 2 * b, k + 1
    tri = (s <= t).astype(np.float32)
    return jnp.asarray(np.stack([tri, tri.T]), BF16), jnp.asarray(np.stack([lvl, lvl.T]))


def _hgrn_chunks(chains, tri_ref, lvl_ref, want_out):
    nt = (((1,), (1,)), ((), ()))
    tn = (((0,), (0,)), ((), ()))
    n = len(chains)
    c = chains[0][2].shape[0]

    kk, parts = [], []
    for (_, _, f_raw, lb, _, _) in chains:
        f = lb + (1.0 - lb) * _sigmoid(f_raw)
        kk.append(1.0 - f)
        parts.append(_split_bf16(jnp.log(f) * LOG2E))
    cum = [sum(jnp.dot(tri_ref[ch[5]], p, preferred_element_type=F32) for p in parts[i])
           for i, ch in enumerate(chains)]
    edge = [cum[i][0:1, :] if ch[5] else cum[i][c - 1:c, :] for i, ch in enumerate(chains)]

    outs = [None] * n
    if want_out:
        a = []
        for i, (q, _, _, _, _, d) in enumerate(chains):
            ref = _block_ref(cum[i], HGRN_DIAG, HGRN_DIAG // 2)
            qd = (q * jnp.exp2(jnp.minimum(cum[i] - ref, EXP2_CLAMP))).astype(BF16)
            kd = (kk[i] * jnp.exp2(jnp.minimum(ref - cum[i], EXP2_CLAMP))).astype(BF16)
            a.append(jnp.where(lvl_ref[d] == 1,
                               lax.dot_general(qd, kd, nt, preferred_element_type=F32), 0.0))
        b, k = HGRN_DIAG, 2
        while b < c:
            for i, (q, _, _, _, _, d) in enumerate(chains):
                ref = _block_ref(cum[i], 2 * b, b if d else b - 1)
                decay = jnp.exp2(cum[i] - ref)
                ql = (q * decay).astype(BF16)
                kl = (kk[i] * jnp.exp2(ref - cum[i])).astype(BF16)
                a[i] = jnp.where(lvl_ref[d] == k,
                                 lax.dot_general(ql, kl, nt, preferred_element_type=F32), a[i])
            b, k = 2 * b, k + 1
        for i, (q, v, _, _, st, _) in enumerate(chains):
            o = jnp.dot(a[i].astype(BF16), v, preferred_element_type=F32)
            outs[i] = o + lax.dot_general((q * jnp.exp2(cum[i])).astype(BF16), st.astype(BF16), nt,
                                          preferred_element_type=F32)

    sts = []
    for i, (_, v, _, _, st, _) in enumerate(chains):
        kg = (kk[i] * jnp.exp2(edge[i] - cum[i])).astype(BF16)
        upd = lax.dot_general(v, kg, tn, preferred_element_type=F32)
        sts.append(st * jnp.exp2(edge[i]) + upd)
    return outs, sts


def _hgrn_kernel(q_ref, i_ref, ff_ref, fb_ref, g_ref, ic_ref, ffc_ref, fbc_ref,
                 lbl_ref, ng_ref, tri_ref, lvl_ref, o_ref, acc_scr):
    c = HGRN_CHUNK
    heads = q_ref.shape[2] // LANES
    nc_lat = q_ref.shape[1] // c
    nc_ctx = ic_ref.shape[1] // c
    f_lat = (ff_ref, fb_ref)
    f_ctx = (ffc_ref, fbc_ref)

    def lower_bound(direction, sl):
        logits = [lbl_ref[direction, l, :, sl] for l in range(lbl_ref.shape[1])]
        top = functools.reduce(jnp.maximum, logits)
        e = [jnp.exp(t - top) for t in logits]
        return e[0] / sum(e)

    lanes = [slice(h * LANES, (h + 1) * LANES) for h in range(heads)]
    lbs = [[lower_bound(d, sl) for sl in lanes] for d in (0, 1)]

    def rows(i):
        return pl.ds(pl.multiple_of(i * c, c), c)

    def ctx_step(j, sts):
        chains = []
        for d in (0, 1):
            r = rows(nc_ctx - 1 - j if d else j)
            for h, sl in enumerate(lanes):
                chains.append((None, ic_ref[0, r, sl], f_ctx[d][0, r, sl].astype(F32),
                               lbs[d][h], sts[d * heads + h], d))
        return tuple(_hgrn_chunks(chains, tri_ref, lvl_ref, False)[1])

    def lat_step(j, sts, second_visit):
        chains, where = [], []
        for d in (0, 1):
            r = rows(nc_lat - 1 - j if d else j)
            for h, sl in enumerate(lanes):
                chains.append((q_ref[0, r, sl].astype(F32), i_ref[0, r, sl],
                               f_lat[d][0, r, sl].astype(F32), lbs[d][h], sts[d * heads + h], d))
                where.append((r, sl))
        outs, new = [], []
        for g0 in range(0, len(chains), HGRN_GROUP):
            o_g, s_g = _hgrn_chunks(chains[g0:g0 + HGRN_GROUP], tri_ref, lvl_ref, True)
            outs += o_g
            new += s_g
        for o, (r, sl) in zip(outs, where):
            if second_visit:
                y = _rms(acc_scr[r, sl] + o, ng_ref[...]) * _silu(g_ref[0, r, sl].astype(F32))
                o_ref[0, r, sl] = y.astype(o_ref.dtype)
            else:
                acc_scr[r, sl] = o
        return tuple(new)

    sts = tuple(jnp.zeros((B_DIM, B_DIM), F32) for _ in range(2 * heads))
    sts = lax.fori_loop(0, nc_ctx, ctx_step, sts, unroll=True)
    sts = lax.fori_loop(0, nc_lat // 2, functools.partial(lat_step, second_visit=False), sts,
                        unroll=HGRN_UNROLL)
    lax.fori_loop(nc_lat // 2, nc_lat, functools.partial(lat_step, second_visit=True), sts,
                  unroll=HGRN_UNROLL)


def _hgrn(rest_x, rest_c, lb_logits, norm_g, heads_per_step):
    b, n, _ = rest_x.shape
    n_ctx = rest_c.shape[1]
    assert (n // HGRN_CHUNK) % 2 == 0 and B_HEADS % heads_per_step == 0
    w = heads_per_step * LANES
    steps = B_HEADS // heads_per_step

    def xs(group):
        return pl.BlockSpec((1, n, w), lambda i, h: (i, 0, group * steps + h))

    def cs(group):
        return pl.BlockSpec((1, n_ctx, w), lambda i, h: (i, 0, group * steps + h))

    n_layers = lb_logits.shape[1]
    tri, lvl = _hgrn_tables(HGRN_CHUNK)
    return pl.pallas_call(
        _hgrn_kernel,
        grid=(b, steps),
        in_specs=[
            xs(REST_COL_Q), xs(REST_COL_I), xs(REST_COL_FF), xs(REST_COL_FB), xs(REST_COL_GATE_B),
            cs(CTX_COL_I), cs(CTX_COL_FF), cs(CTX_COL_FB),
            pl.BlockSpec((2, n_layers, 1, w), lambda i, h: (0, 0, 0, h)),
            _const_spec((1, LANES)),
            _const_spec(tri.shape),
            _const_spec(lvl.shape),
        ],
        out_specs=pl.BlockSpec((1, n, w), lambda i, h: (i, 0, h)),
        out_shape=jax.ShapeDtypeStruct((b, n, B_WIDTH), BF16),
        scratch_shapes=[pltpu.VMEM((n, w), F32)],
        compiler_params=_cparams("arbitrary", "arbitrary"),
        name="hgrn2",
    )(rest_x, rest_x, rest_x, rest_x, rest_x, rest_c, rest_c, rest_c,
      lb_logits.reshape(2, n_layers, 1, B_WIDTH), norm_g, tri, lvl)


def _layer1_input(ya, yb, x, gate0, mod1, gain1, wo0_ref):
    d = x.shape[-1]
    half = ya.shape[-1]
    upd = (jnp.dot(ya, wo0_ref[0:half, :], preferred_element_type=F32)
           + jnp.dot(yb, wo0_ref[half:, :], preferred_element_type=F32))
    x1 = x + gate0 * upd
    return x1, _modulate(x1, gain1, mod1, d).astype(BF16)


def _edge_kernel(ya_ref, yb_ref, x_ref, mod0_ref, mod1_ref, g1_ref, wo0_ref, wi_ref, z_ref,
                 *, rows_per_batch):
    d = x_ref.shape[-1]

    def per_row(ref, lo, hi):
        return jnp.concatenate([jnp.broadcast_to(ref[i:i + 1, lo:hi], (rows_per_batch, hi - lo))
                                for i in range(x_ref.shape[0] // rows_per_batch)], axis=0)

    _, xm = _layer1_input(ya_ref[...], yb_ref[...], x_ref[...], per_row(mod0_ref, 2 * d, 3 * d),
                          per_row(mod1_ref, 0, 2 * d), g1_ref[...], wo0_ref)
    p = jnp.dot(xm, wi_ref[...], preferred_element_type=F32)
    z_ref[...] = p[:, :D_WIDTH] * p[:, D_WIDTH:]


def _block_edges(t, tb):
    b, n, w = t.shape
    te = t.reshape(b, n // tb, tb, w)
    return jnp.concatenate([te[:, :, :SUBLANES], te[:, :, tb - SUBLANES:]], axis=2).reshape(-1, w)


def _edge_z(ya, yb, x, mod0, mod1, gain1, w_out0, w_in1, tb):
    b, n, d = x.shape
    rows = (n // tb) * 2 * SUBLANES
    cg_start = 3 * C_WIDTH + D_WIDTH
    assert cg_start % (2 * D_WIDTH) == 0
    cg_blk = cg_start // (2 * D_WIDTH)
    full = lambda arr: _const_spec(arr.shape)
    xe, yae, ybe = _block_edges(x, tb), _block_edges(ya, tb), _block_edges(yb, tb)
    return pl.pallas_call(
        functools.partial(_edge_kernel, rows_per_batch=rows),
        grid=(1,),
        in_specs=[full(yae), full(ybe), full(xe), full(mod0), full(mod1), _const_spec((1, d)),
                  full(w_out0),
                  pl.BlockSpec((d, 2 * D_WIDTH), lambda i: (0, cg_blk), pipeline_mode=pl.Buffered(1))],
        out_specs=pl.BlockSpec((b * rows, D_WIDTH), lambda i: (0, 0)),
        out_shape=jax.ShapeDtypeStruct((b * rows, D_WIDTH), F32),
        compiler_params=_cparams("arbitrary"),
        name="conv_edge_rows",
    )(yae, ybe, xe, mod0, mod1, gain1, w_out0, w_in1).reshape(b, rows, D_WIDTH)


def _layer1_kernel(ya_ref, yb_ref, x_ref, ze_ref, mod0_ref, mod1_ref, g1_ref, wo0_ref, wi_ref,
                   vg_ref, ws_ref, bs_ref, cw_ref, wo1_ref, o_ref):
    d = x_ref.shape[-1]
    tb = x_ref.shape[1]
    j = pl.program_id(1)
    last_j = pl.num_programs(1) - 1
    mod1 = mod1_ref[pl.ds(pl.program_id(0), 1), :]
    gate0 = mod0_ref[pl.ds(pl.program_id(0), 1), 2 * d:]
    gate1 = mod1[:, 2 * d:]
    sub = tb // L1_SUB_BLOCKS
    subs = [slice(s * sub, (s + 1) * sub) for s in range(L1_SUB_BLOCKS)]
    col = lambda p, k: p[:, k * C_WIDTH:(k + 1) * C_WIDTH]
    n_gmlp = 3 * C_WIDTH

    x1s, xms = [], []
    for r in subs:
        x1, xm = _layer1_input(ya_ref[0, r, :], yb_ref[0, r, :], x_ref[0, r, :], gate0,
                               mod1, g1_ref[...], wo0_ref)
        x1s.append(x1)
        xms.append(xm)
    pgs = [jnp.dot(xm, wi_ref[:, 0:n_gmlp], preferred_element_type=F32) for xm in xms]
    pcs = [jnp.dot(xm, wi_ref[:, n_gmlp:], preferred_element_type=F32) for xm in xms]

    upd_c = []
    for pg in pgs:
        u = _gelu(col(pg, 0))
        vn = _rms(_gelu(col(pg, 1)), vg_ref[...]).astype(BF16)
        chunks = []
        for ci in range(sub // C_CHUNK):
            cr = slice(ci * C_CHUNK, (ci + 1) * C_CHUNK)
            groups = []
            for g in range(C_GROUPS):
                gl = slice(g * LANES, (g + 1) * LANES)
                groups.append(jnp.dot(ws_ref[g], vn[cr, gl], preferred_element_type=F32) + bs_ref[g])
            chunks.append(jnp.concatenate(groups, axis=1))
        o_c = u * jnp.concatenate(chunks, axis=0) * _silu(col(pg, 2))
        upd_c.append(jnp.dot(o_c.astype(BF16), wo1_ref[0:C_WIDTH, :], preferred_element_type=F32))

    z = jnp.concatenate([col(pc, 1) * col(pc, 2) for pc in pcs], axis=0)
    grp = 2 * SUBLANES
    prev_grp = ze_ref[0, pl.ds(pl.multiple_of(jnp.maximum(j - 1, 0) * grp + SUBLANES, SUBLANES),
                               SUBLANES), :]
    next_grp = ze_ref[0, pl.ds(pl.multiple_of(jnp.minimum(j + 1, last_j) * grp, SUBLANES),
                               SUBLANES), :]
    z_prev_row = jnp.where(j == 0, 0.0, prev_grp[SUBLANES - 1:, :])
    z_next_row = jnp.where(j == last_j, 0.0, next_grp[0:1, :])
    rowi = lax.broadcasted_iota(jnp.int32, z.shape, 0)
    z_prev = jnp.where(rowi == 0, z_prev_row, pltpu.roll(z, 1, 0))
    z_next = jnp.where(rowi == tb - 1, z_next_row, pltpu.roll(z, tb - 1, 0))
    conv = cw_ref[0:1, :] * z_prev + cw_ref[1:2, :] * z + cw_ref[2:3, :] * z_next

    for r, x1, pc, uc in zip(subs, x1s, pcs, upd_c):
        o_d = col(pc, 0) * conv[r, :] * _silu(col(pc, 3))
        upd = uc + jnp.dot(o_d.astype(BF16), wo1_ref[C_WIDTH:, :], preferred_element_type=F32)
        o_ref[0, r, :] = x1 + gate1 * upd


def _layer1(ya, yb, x, ze, mod0, mod1, gain1, w_out0, w_in1, v_gain, w_s, b_s, conv_w, w_out1, tb):
    b, n, d = x.shape
    half = ya.shape[-1]
    tok = lambda width: pl.BlockSpec((1, tb, width), lambda i, j: (i, j, 0))
    modspec = _const_spec(mod0.shape)
    return pl.pallas_call(
        _layer1_kernel,
        grid=(b, n // tb),
        in_specs=[tok(half), tok(half), tok(d),
                  pl.BlockSpec((1,) + ze.shape[1:], lambda i, j: (i, 0, 0)),
                  modspec, modspec, _const_spec((1, d)),
                  _const_spec(w_out0.shape), _const_spec(w_in1.shape),
                  _const_spec((1, C_WIDTH)),
                  _const_spec((C_GROUPS, C_CHUNK, C_CHUNK)),
                  _const_spec((C_GROUPS, C_CHUNK, LANES)),
                  _const_spec((3, D_WIDTH)),
                  _const_spec(w_out1.shape)],
        out_specs=tok(d),
        out_shape=jax.ShapeDtypeStruct((b, n, d), F32),
        compiler_params=_cparams("arbitrary", "arbitrary"),
        name="outproj_even_layer_odd",
    )(ya, yb, x, ze, mod0, mod1, gain1, w_out0, w_in1, v_gain, w_s, b_s, conv_w, w_out1)


def _rope_tables(n):
    rows_ = n // GRID_W
    row = np.repeat(np.arange(rows_, dtype=np.float64), GRID_W)
    col = np.tile(np.arange(GRID_W, dtype=np.float64), rows_)
    n_freq = A_HEAD_DIM // 4
    inv = ROPE_THETA ** (-np.arange(n_freq, dtype=np.float64) / n_freq)
    ang = np.concatenate([row[:, None] * inv, col[:, None] * inv], axis=-1)
    cos, sin = np.cos(ang), np.sin(ang)
    reps = QK_SLAB // A_HEAD_DIM
    return (jnp.asarray(np.tile(np.concatenate([cos, cos], axis=-1), (1, reps)), F32),
            jnp.asarray(np.tile(np.concatenate([-sin, sin], axis=-1), (1, reps)), F32))


def kernel(x, c, ctx, c_ctx, norm_gain, ada_w, ada_b, even_w_in, even_w_out, attn_qk_gain,
           attn_lambda, attn_subln_gain, hgrn_lb_logits, hgrn_norm_gain, odd_w_in, odd_w_out,
           gmlp_v_gain, gmlp_w_s, gmlp_b_s, conv_w):
    b, n, d = x.shape
    assert b + 1 <= COND_ROWS and ctx.shape[1] % HGRN_CHUNK == 0
    assert all(n % blk == 0 for blk in (INPROJ_BLOCK, L1_TOKEN_BLOCK, ATTN_Q_BLOCK, GRID_W))
    assert norm_gain.shape[0] == 2, "two-layer block: one even layer then one odd layer"

    cond = jnp.concatenate([c, c_ctx[None, :], jnp.zeros((COND_ROWS - b - 1, d), F32)], axis=0)
    mod0, mod1 = _adaln(cond, ada_w, ada_b)

    w_in0 = even_w_in[0].astype(BF16)
    gain0 = norm_gain[0].reshape(1, d)
    cos, sin_signed = _rope_tables(n)
    qk_gain = jnp.tile(attn_qk_gain[0], (1, QK_SLAB // A_HEAD_DIM))
    blk = np.arange(QK_SLAB) // A_HEAD_DIM
    bd = jnp.asarray(np.where(blk[:, None] == blk[None, :], 1.0 / A_HEAD_DIM, 0.0), BF16)
    q, k_lat, vt_lat, rest_x = _inproj(x, mod0, None, gain0, w_in0, cos, sin_signed, qk_gain, bd,
                                       INPROJ_BLOCK, True, True)
    w_ctx = jnp.concatenate([w_in0[:, g * A_WIDTH:(g + 1) * A_WIDTH] for g in CTX_GROUPS], axis=1)
    n_ctx = ctx.shape[1]
    k_ctx, vt_ctx, rest_c = _inproj(ctx.reshape(1, b * n_ctx, d), mod0, b, gain0, w_ctx, cos, sin_signed,
                                    qk_gain, bd, min(TOKEN_BLOCK, b * n_ctx), False, False)
    k_ctx = k_ctx.reshape(b, n_ctx, A_WIDTH)
    rest_c = rest_c.reshape(b, n_ctx, rest_c.shape[-1])

    lam_init = 0.8 - 0.6 * math.exp(-0.3 * 0)
    ya = _attention(q, k_lat, k_ctx, vt_lat, vt_ctx, rest_x, qk_gain,
                    attn_subln_gain[0].reshape(1, LANES), attn_lambda[0], lam_init, ATTN_Q_BLOCK)
    yb = _hgrn(rest_x, rest_c, hgrn_lb_logits, hgrn_norm_gain[0].reshape(1, LANES),
               HGRN_HEADS_PER_STEP)

    gain1 = norm_gain[1].reshape(1, d)
    w_out0, w_in1 = even_w_out[0].astype(BF16), odd_w_in[0].astype(BF16)
    ze = _edge_z(ya, yb, x, mod0, mod1, gain1, w_out0, w_in1, L1_TOKEN_BLOCK)
    b_s = jnp.broadcast_to(gmlp_b_s[0][:, :, None], (C_GROUPS, C_CHUNK, LANES))
    return _layer1(ya, yb, x, ze, mod0, mod1, gain1, w_out0, w_in1,
                   gmlp_v_gain[0].reshape(1, C_WIDTH), gmlp_w_s[0].astype(BF16), b_s, conv_w[0],
                   odd_w_out[0].astype(BF16), L1_TOKEN_BLOCK)
```

```python
import functools
import math

import jax
import jax.numpy as jnp
import numpy as np
from jax import lax
from jax.experimental import pallas as pl
from jax.experimental.pallas import tpu as pltpu

F32 = jnp.float32
BF16 = jnp.bfloat16

EPS = 1e-6
GRID_W = 64
ROPE_THETA = 10000.0
A_HEADS = 4
A_HEAD_DIM = 64
A_WIDTH = 2 * A_HEADS * A_HEAD_DIM
B_HEADS = 4
B_DIM = 128
B_WIDTH = B_HEADS * B_DIM
C_GROUPS = 4
C_CHUNK = 128
C_WIDTH = 512
D_WIDTH = 512
EVEN_IN = 4 * A_WIDTH + 5 * B_WIDTH
ODD_IN = 3 * C_WIDTH + 4 * D_WIDTH

CTX_GROUPS = (1, 2, 5, 6, 7)
REST_COL_GATE_A, REST_COL_Q, REST_COL_I, REST_COL_FF, REST_COL_FB, REST_COL_GATE_B = range(6)
CTX_COL_I, CTX_COL_FF, CTX_COL_FB = range(3)
QK_SLAB = 256

LANES = 128
HGRN_CHUNK = 128
HGRN_HEADS_PER_STEP = 4
HGRN_GROUP = 8
HGRN_UNROLL = 4
HGRN_DIAGS = (16, 8)
SCORE_BOUND = 32.0
EXP2_CLAMP = 115.0
LOG2E = math.log2(math.e)
Q_SCALE = A_HEAD_DIM ** -0.5 * LOG2E
TOKEN_BLOCK = 512
INPROJ_BLOCK = 1024
L1_TOKEN_BLOCK = 1024
ATTN_Q_TILE = 256
ATTN_Q_BLOCK = 512
ATTN_LOOKAHEAD = 1
SUBLANES = 8
BF16_ROWS_PER_VREG = 16
L1_SUB_BLOCKS = 4
COND_ROWS = 16
ADALN_COL_BLOCK = 512
VMEM_LIMIT = 56 * 1024 * 1024


def _cparams(*sem):
    return pltpu.CompilerParams(dimension_semantics=sem, vmem_limit_bytes=VMEM_LIMIT)


def _const_spec(shape):
    nd = len(shape)
    return pl.BlockSpec(shape, lambda *_: (0,) * nd, pipeline_mode=pl.Buffered(1))


def _sigmoid(t):
    return 0.5 + 0.5 * jnp.tanh(0.5 * t)


def _silu(t):
    return t * _sigmoid(t)


def _gelu(t):
    return 0.5 * t * (1.0 + lax.erf(t * (1.0 / math.sqrt(2.0))))


def _rms(t, gain):
    ms = jnp.mean(t * t, axis=-1, keepdims=True)
    return t * lax.rsqrt(ms + EPS) * gain


def _adaln_kernel(cond_ref, w_ref, b_ref, *o_refs):
    a = _silu(cond_ref[...])
    for layer, o_ref in enumerate(o_refs):
        o_ref[...] = jnp.dot(a, w_ref[layer], preferred_element_type=F32) + b_ref[layer]


def _adaln(cond, ada_w, ada_b):
    depth, d, n3 = ada_w.shape
    tn = ADALN_COL_BLOCK
    out = pl.BlockSpec((COND_ROWS, tn), lambda j: (0, j))
    return pl.pallas_call(
        _adaln_kernel,
        grid=(n3 // tn,),
        in_specs=[
            pl.BlockSpec((COND_ROWS, d), lambda j: (0, 0)),
            pl.BlockSpec((depth, d, tn), lambda j: (0, 0, j)),
            pl.BlockSpec((depth, 1, tn), lambda j: (0, 0, j)),
        ],
        out_specs=[out] * depth,
        out_shape=[jax.ShapeDtypeStruct((COND_ROWS, n3), F32)] * depth,
        compiler_params=_cparams("arbitrary"),
        name="adaln",
    )(cond, ada_w, ada_b.reshape(depth, 1, n3))


def _modulate(x, gain, mod, d):
    shift = mod[:, 0:d]
    scale = mod[:, d:2 * d]
    return _rms(x, gain) * (1.0 + scale) + shift


def _rope(t, cos, sin_signed):
    lanes = t.shape[1]
    lane = lax.broadcasted_iota(jnp.int32, t.shape, 1)
    first = (lane % A_HEAD_DIM) < (A_HEAD_DIM // 2)
    partner = jnp.where(first,
                        pltpu.roll(t, lanes - A_HEAD_DIM // 2, 1),
                        pltpu.roll(t, A_HEAD_DIM // 2, 1))
    return t * cos + partner * sin_signed


def _inproj_kernel(x_ref, mod_ref, g_ref, wa_ref, wr1_ref, wr2_ref, cos_ref, sin_ref, qkg_ref, bd_ref,
                   *out_refs,
                   mod_row, has_q, rope):
    d = x_ref.shape[-1]
    tb = x_ref.shape[1]
    row = pl.program_id(0) if mod_row is None else mod_row
    n_attn = (3 if has_q else 2) * A_WIDTH
    rest_ref = out_refs[-1]
    n_rest = rest_ref.shape[-1]
    slabs = [slice(half * QK_SLAB, (half + 1) * QK_SLAB) for half in range(A_WIDTH // QK_SLAB)]
    sub = min(tb, TOKEN_BLOCK)

    pending = []
    for r in [slice(s * sub, (s + 1) * sub) for s in range(tb // sub)]:
        xm = _modulate(x_ref[0, r, :], g_ref[...], mod_ref[pl.ds(row, 1), :], d).astype(BF16)
        attn = jnp.dot(xm, wa_ref[...], preferred_element_type=F32)
        rest_ref[0, r, 0:n_rest // 2] = jnp.dot(
            xm, wr1_ref[...], preferred_element_type=F32).astype(rest_ref.dtype)
        groups = [attn[:, g * A_WIDTH:(g + 1) * A_WIDTH] for g in range(n_attn // A_WIDTH)]
        v = groups.pop()
        mean_sq = [[jnp.dot((t[:, sl] * t[:, sl]).astype(BF16), bd_ref[...],
                            preferred_element_type=F32) for sl in slabs] for t in groups]
        pad = BF16_ROWS_PER_VREG
        ms_rows = sum(ms[0:pad, :] for per_group in mean_sq for ms in per_group)
        anchor = jnp.concatenate([ms_rows * 0.0] * (d // QK_SLAB), axis=1)
        xm_late = jnp.concatenate([(xm[0:pad, :].astype(F32) + anchor).astype(BF16), xm[pad:, :]],
                                  axis=0)
        rest_ref[0, r, n_rest // 2:] = jnp.dot(
            xm_late, wr2_ref[...], preferred_element_type=F32).astype(rest_ref.dtype)
        pending.append((r, groups, v, mean_sq))

    qk_refs = out_refs[:-2]
    vt_ref = out_refs[-2]
    first_gain = 0 if has_q else 1
    for r, groups, v, mean_sq in pending:
        for gi, (t, o_ref) in enumerate(zip(groups, qk_refs)):
            gain = qkg_ref[first_gain + gi:first_gain + gi + 1, :]
            scale = Q_SCALE if (has_q and gi == 0) else 1.0
            for sl, ms in zip(slabs, mean_sq[gi]):
                tn = t[:, sl] * lax.rsqrt(ms + EPS) * gain
                if rope:
                    tn = _rope(tn, cos_ref[r, :], sin_ref[r, :])
                o_ref[0, r, sl] = (tn * scale).astype(o_ref.dtype)
        for h in range(A_HEADS):
            sl = slice(h * LANES, (h + 1) * LANES)
            vt_ref[0, sl, r] = v[:, sl].T.astype(vt_ref.dtype)


def _inproj(x, mod, mod_row, gain, w, cos, sin_signed, qk_gain, bd, tb, has_q, rope):
    b, n, d = x.shape
    n_attn = (3 if has_q else 2) * A_WIDTH
    n_rest = w.shape[1] - n_attn
    bounds = (0, n_attn, n_attn + n_rest // 2, w.shape[1])
    w_parts, w_specs = [], []
    for lo, hi in zip(bounds[:-1], bounds[1:]):
        if lo % (hi - lo) == 0:
            w_parts.append(w)
            w_specs.append(pl.BlockSpec((d, hi - lo), lambda i, j, blk=lo // (hi - lo): (0, blk),
                                        pipeline_mode=pl.Buffered(1)))
        else:
            w_parts.append(w[:, lo:hi])
            w_specs.append(_const_spec((d, hi - lo)))
    tok = lambda width: pl.BlockSpec((1, tb, width), lambda i, j: (i, j, 0))
    qk_out = [tok(A_WIDTH)] * (2 if has_q else 1)
    qk_shape = [jax.ShapeDtypeStruct((b, n, A_WIDTH), BF16)] * (2 if has_q else 1)
    return pl.pallas_call(
        functools.partial(_inproj_kernel, mod_row=mod_row, has_q=has_q, rope=rope),
        grid=(b, n // tb),
        in_specs=[
            tok(d),
            _const_spec(mod.shape),
            _const_spec((1, d)),
            *w_specs,
            pl.BlockSpec((tb, QK_SLAB), lambda i, j: (j, 0)),
            pl.BlockSpec((tb, QK_SLAB), lambda i, j: (j, 0)),
            _const_spec(qk_gain.shape),
            _const_spec(bd.shape),
        ],
        out_specs=qk_out + [pl.BlockSpec((1, A_WIDTH, tb), lambda i, j: (i, 0, j)), tok(n_rest)],
        out_shape=qk_shape + [jax.ShapeDtypeStruct((b, A_WIDTH, n), BF16),
                              jax.ShapeDtypeStruct((b, n, n_rest), BF16)],
        compiler_params=_cparams("arbitrary", "arbitrary"),
        name="inproj_even",
    )(x, mod, gain, *w_parts, cos, sin_signed, qk_gain, bd)


def _attn_kernel(q_ref, kl_ref, kc_ref, vtl_ref, vtc_ref, g_ref, qkg_ref, subg_ref, lamp_ref,
                 o_ref, *, lam_init):
    heads = [slice(h * LANES, (h + 1) * LANES) for h in range(A_HEADS)]
    lane = lax.broadcasted_iota(jnp.int32, (1, LANES), 1)
    nt = (((1,), (1,)), ((), ()))

    lp = lamp_ref[...]
    lam = (jnp.exp(jnp.sum(lp[0:1] * lp[1:2], axis=-1, keepdims=True))
           - jnp.exp(jnp.sum(lp[2:3] * lp[3:4], axis=-1, keepdims=True)) + lam_init)
    score_bound = ((A_HEAD_DIM * Q_SCALE) * jnp.max(jnp.abs(qkg_ref[0:1, :]))
                   * jnp.max(jnp.abs(qkg_ref[1:2, :])))

    n_sub = q_ref.shape[1] // ATTN_Q_TILE
    items = [(slice(qi * ATTN_Q_TILE, (qi + 1) * ATTN_Q_TILE), h)
             for qi in range(n_sub) for h in range(A_HEADS)]

    def scores(item):
        rows, h = item
        out = []
        for m in range(2):
            qm = jnp.where((lane // A_HEAD_DIM) == m, q_ref[0, rows, heads[h]], 0).astype(BF16)
            out.append([lax.dot_general(k_ref[0, :, heads[h]], qm, nt, preferred_element_type=F32)
                        for k_ref in (kc_ref, kl_ref)])
        return out

    def run_heads(shift):
        queued = [scores(item) for item in items[:ATTN_LOOKAHEAD]]
        for idx, (rows, h) in enumerate(items):
            sl = heads[h]
            s_maps = queued.pop(0)
            if idx + ATTN_LOOKAHEAD < len(items):
                queued.append(scores(items[idx + ATTN_LOOKAHEAD]))
            probs = []
            for s_parts in s_maps:
                if shift:
                    top = functools.reduce(jnp.maximum,
                                           [jnp.max(s, axis=0, keepdims=True) for s in s_parts])
                    s_parts = [s - top for s in s_parts]
                p_parts = [jnp.exp2(s) for s in s_parts]
                probs.append((p_parts, sum(jnp.sum(p, axis=0, keepdims=True) for p in p_parts)))
            (p0, l0), (p1, l1) = probs
            ratio = lam * l0 * (1.0 / l1)
            ot = sum(jnp.dot(vt_ref[0, sl, :], (pa - pb * ratio).astype(BF16),
                             preferred_element_type=F32)
                     for vt_ref, pa, pb in zip((vtc_ref, vtl_ref), p0, p1)) * (1.0 / l0)
            ms = jnp.mean(ot * ot, axis=0, keepdims=True)
            on = (ot * lax.rsqrt(ms + EPS)).T * (subg_ref[...] * (1.0 - lam_init))
            o_ref[0, rows, sl] = (on * _silu(g_ref[0, rows, sl].astype(F32))).astype(o_ref.dtype)

    no_shift_ok = score_bound <= SCORE_BOUND
    pl.when(no_shift_ok)(functools.partial(run_heads, False))
    pl.when(jnp.logical_not(no_shift_ok))(functools.partial(run_heads, True))


def _attention(q, k_lat, k_ctx, vt_lat, vt_ctx, rest, qk_gain, subln_g, lam_p, lam_init, tq):
    b, n, w = q.shape
    n_ctx = k_ctx.shape[1]
    return pl.pallas_call(
        functools.partial(_attn_kernel, lam_init=lam_init),
        grid=(b, n // tq),
        in_specs=[
            pl.BlockSpec((1, tq, w), lambda i, j: (i, j, 0)),
            pl.BlockSpec((1, n, w), lambda i, j: (i, 0, 0)),
            pl.BlockSpec((1, n_ctx, w), lambda i, j: (i, 0, 0)),
            pl.BlockSpec((1, w, n), lambda i, j: (i, 0, 0)),
            pl.BlockSpec((1, w, n_ctx), lambda i, j: (0, 0, i)),
            pl.BlockSpec((1, tq, w), lambda i, j: (i, j, REST_COL_GATE_A)),
            _const_spec(qk_gain.shape),
            _const_spec((1, LANES)),
            _const_spec((4, A_HEAD_DIM)),
        ],
        out_specs=pl.BlockSpec((1, tq, w), lambda i, j: (i, j, 0)),
        out_shape=jax.ShapeDtypeStruct((b, n, w), BF16),
        compiler_params=_cparams("arbitrary", "arbitrary"),
        name="diff_attn",
    )(q, k_lat, k_ctx, vt_lat, vt_ctx, rest, qk_gain, subln_g, lam_p)


def _split_bf16(t):
    hi = t.astype(BF16)
    return hi, (t - hi.astype(F32)).astype(BF16)


def _block_ref(g, block, row):
    c, w = g.shape
    g3 = g.reshape(c // block, block, w)
    return jnp.broadcast_to(g3[:, row:row + 1, :], g3.shape).reshape(c, w)


def _hgrn_tables(c):
    t = np.arange(c)[:, None]
    s = np.arange(c)[None, :]
    lvls = []
    for diag in HGRN_DIAGS:
        lvl = np.zeros((c, c), np.int32)
        lvl[(t // diag == s // diag) & (s <= t)] = 1
        b, k = diag, 2
        while b < c:
            lvl[(t // b == s // b + 1) & ((s // b) % 2 == 0)] = k
            b, k = 2 * b, k + 1
        lvls.append(np.stack([lvl, lvl.T]))
    tri = (s <= t).astype(np.float32)
    return jnp.asarray(np.stack([tri, tri.T]), BF16), jnp.asarray(np.stack(lvls))


def _hgrn_scores(chains, kk, cum, lvl_ref, diag):
    nt = (((1,), (1,)), ((), ()))
    c = cum[0].shape[0]
    a = []
    for i, (q, _, _, _, _, d) in enumerate(chains):
        ref = _block_ref(cum[i], diag, diag // 2)
        qd = (q * jnp.exp2(jnp.minimum(cum[i] - ref, EXP2_CLAMP))).astype(BF16)
        kd = (kk[i] * jnp.exp2(jnp.minimum(ref - cum[i], EXP2_CLAMP))).astype(BF16)
        a.append(jnp.where(lvl_ref[d] == 1,
                           lax.dot_general(qd, kd, nt, preferred_element_type=F32), 0.0))
    b, k = diag, 2
    while b < c:
        for i, (q, _, _, _, _, d) in enumerate(chains):
            ref = _block_ref(cum[i], 2 * b, b if d else b - 1)
            decay = jnp.exp2(cum[i] - ref)
            ql = (q * decay).astype(BF16)
            kl = (kk[i] * jnp.exp2(ref - cum[i])).astype(BF16)
            a[i] = jnp.where(lvl_ref[d] == k,
                             lax.dot_general(ql, kl, nt, preferred_element_type=F32), a[i])
        b, k = 2 * b, k + 1
    return a


def _hgrn_chunks(chains, tri_ref, lvl_ref, want_out):
    nt = (((1,), (1,)), ((), ()))
    tn = (((0,), (0,)), ((), ()))
    n = len(chains)
    c = chains[0][2].shape[0]

    kk, parts = [], []
    for (_, _, f_raw, lb, _, _) in chains:
        f = lb + (1.0 - lb) * _sigmoid(f_raw)
        kk.append(1.0 - f)
        parts.append(_split_bf16(jnp.log(f) * LOG2E))
    cum = [sum(jnp.dot(tri_ref[ch[5]], p, preferred_element_type=F32) for p in parts[i])
           for i, ch in enumerate(chains)]
    edge = [cum[i][0:1, :] if ch[5] else cum[i][c - 1:c, :] for i, ch in enumerate(chains)]

    outs = [None] * n
    if want_out:
        wide, narrow = HGRN_DIAGS
        worst = functools.reduce(jnp.maximum, [
            jnp.max(jnp.abs(cum[i] - _block_ref(cum[i], wide, wide // 2)), axis=0, keepdims=True)
            for i in range(n)])
        a = lax.cond(jnp.max(worst) < EXP2_CLAMP,
                     lambda: _hgrn_scores(chains, kk, cum, lvl_ref.at[0], wide),
                     lambda: _hgrn_scores(chains, kk, cum, lvl_ref.at[1], narrow))
        for i, (q, v, _, _, st, _) in enumerate(chains):
            o = jnp.dot(a[i].astype(BF16), v, preferred_element_type=F32)
            outs[i] = o + lax.dot_general((q * jnp.exp2(cum[i])).astype(BF16), st.astype(BF16), nt,
                                          preferred_element_type=F32)

    sts = []
    for i, (_, v, _, _, st, _) in enumerate(chains):
        kg = (kk[i] * jnp.exp2(edge[i] - cum[i])).astype(BF16)
        upd = lax.dot_general(v, kg, tn, preferred_element_type=F32)
        sts.append(st * jnp.exp2(edge[i]) + upd)
    return outs, sts


def _hgrn_kernel(q_ref, i_ref, ff_ref, fb_ref, g_ref, ic_ref, ffc_ref, fbc_ref,
                 lbl_ref, ng_ref, tri_ref, lvl_ref, o_ref, acc_scr):
    c = HGRN_CHUNK
    heads = q_ref.shape[2] // LANES
    nc_lat = q_ref.shape[1] // c
    nc_ctx = ic_ref.shape[1] // c
    f_lat = (ff_ref, fb_ref)
    f_ctx = (ffc_ref, fbc_ref)

    def lower_bound(direction, sl):
        logits = [lbl_ref[direction, l, :, sl] for l in range(lbl_ref.shape[1])]
        top = functools.reduce(jnp.maximum, logits)
        e = [jnp.exp(t - top) for t in logits]
        return e[0] / sum(e)

    lanes = [slice(h * LANES, (h + 1) * LANES) for h in range(heads)]
    lbs = [[lower_bound(d, sl) for sl in lanes] for d in (0, 1)]

    def rows(i):
        return pl.ds(pl.multiple_of(i * c, c), c)

    def ctx_step(j, sts):
        chains = []
        for d in (0, 1):
            r = rows(nc_ctx - 1 - j if d else j)
            for h, sl in enumerate(lanes):
                chains.append((None, ic_ref[0, r, sl], f_ctx[d][0, r, sl].astype(F32),
                               lbs[d][h], sts[d * heads + h], d))
        return tuple(_hgrn_chunks(chains, tri_ref, lvl_ref, False)[1])

    def lat_step(j, sts, second_visit):
        chains, where = [], []
        for d in (0, 1):
            r = rows(nc_lat - 1 - j if d else j)
            for h, sl in enumerate(lanes):
                chains.append((q_ref[0, r, sl].astype(F32), i_ref[0, r, sl],
                               f_lat[d][0, r, sl].astype(F32), lbs[d][h], sts[d * heads + h], d))
                where.append((r, sl))
        outs, new = [], []
        for g0 in range(0, len(chains), HGRN_GROUP):
            o_g, s_g = _hgrn_chunks(chains[g0:g0 + HGRN_GROUP], tri_ref, lvl_ref, True)
            outs += o_g
            new += s_g
        for o, (r, sl) in zip(outs, where):
            if second_visit:
                y = _rms(acc_scr[r, sl] + o, ng_ref[...]) * _silu(g_ref[0, r, sl].astype(F32))
                o_ref[0, r, sl] = y.astype(o_ref.dtype)
            else:
                acc_scr[r, sl] = o
        return tuple(new)

    sts = tuple(jnp.zeros((B_DIM, B_DIM), F32) for _ in range(2 * heads))
    sts = lax.fori_loop(0, nc_ctx, ctx_step, sts, unroll=True)
    sts = lax.fori_loop(0, nc_lat // 2, functools.partial(lat_step, second_visit=False), sts,
                        unroll=HGRN_UNROLL)
    lax.fori_loop(nc_lat // 2, nc_lat, functools.partial(lat_step, second_visit=True), sts,
                  unroll=HGRN_UNROLL)


def _hgrn(rest_x, rest_c, lb_logits, norm_g, heads_per_step):
    b, n, _ = rest_x.shape
    n_ctx = rest_c.shape[1]
    assert (n // HGRN_CHUNK) % 2 == 0 and B_HEADS % heads_per_step == 0
    w = heads_per_step * LANES
    steps = B_HEADS // heads_per_step

    def xs(group):
        return pl.BlockSpec((1, n, w), lambda i, h: (i, 0, group * steps + h))

    def cs(group):
        return pl.BlockSpec((1, n_ctx, w), lambda i, h: (i, 0, group * steps + h))

    n_layers = lb_logits.shape[1]
    tri, lvl = _hgrn_tables(HGRN_CHUNK)
    return pl.pallas_call(
        _hgrn_kernel,
        grid=(b, steps),
        in_specs=[
            xs(REST_COL_Q), xs(REST_COL_I), xs(REST_COL_FF), xs(REST_COL_FB), xs(REST_COL_GATE_B),
            cs(CTX_COL_I), cs(CTX_COL_FF), cs(CTX_COL_FB),
            pl.BlockSpec((2, n_layers, 1, w), lambda i, h: (0, 0, 0, h)),
            _const_spec((1, LANES)),
            _const_spec(tri.shape),
            _const_spec(lvl.shape),
        ],
        out_specs=pl.BlockSpec((1, n, w), lambda i, h: (i, 0, h)),
        out_shape=jax.ShapeDtypeStruct((b, n, B_WIDTH), BF16),
        scratch_shapes=[pltpu.VMEM((n, w), F32)],
        compiler_params=_cparams("arbitrary", "arbitrary"),
        name="hgrn2",
    )(rest_x, rest_x, rest_x, rest_x, rest_x, rest_c, rest_c, rest_c,
      lb_logits.reshape(2, n_layers, 1, B_WIDTH), norm_g, tri, lvl)


def _layer1_input(ya, yb, x, gate0, mod1, gain1, wo0_ref):
    d = x.shape[-1]
    half = ya.shape[-1]
    upd = (jnp.dot(ya, wo0_ref[0:half, :], preferred_element_type=F32)
           + jnp.dot(yb, wo0_ref[half:, :], preferred_element_type=F32))
    x1 = x + gate0 * upd
    return x1, _modulate(x1, gain1, mod1, d).astype(BF16)


def _edge_kernel(ya_ref, yb_ref, x_ref, mod0_ref, mod1_ref, g1_ref, wo0_ref, wi_ref, z_ref,
                 *, rows_per_batch):
    d = x_ref.shape[-1]

    def per_row(ref, lo, hi):
        return jnp.concatenate([jnp.broadcast_to(ref[i:i + 1, lo:hi], (rows_per_batch, hi - lo))
                                for i in range(x_ref.shape[0] // rows_per_batch)], axis=0)

    _, xm = _layer1_input(ya_ref[...], yb_ref[...], x_ref[...], per_row(mod0_ref, 2 * d, 3 * d),
                          per_row(mod1_ref, 0, 2 * d), g1_ref[...], wo0_ref)
    p = jnp.dot(xm, wi_ref[...], preferred_element_type=F32)
    z_ref[...] = p[:, :D_WIDTH] * p[:, D_WIDTH:]


def _block_edges(t, tb):
    b, n, w = t.shape
    te = t.reshape(b, n // tb, tb, w)
    return jnp.concatenate([te[:, :, :SUBLANES], te[:, :, tb - SUBLANES:]], axis=2).reshape(-1, w)


def _edge_z(ya, yb, x, mod0, mod1, gain1, w_out0, w_in1, tb):
    b, n, d = x.shape
    rows = (n // tb) * 2 * SUBLANES
    cg_start = 3 * C_WIDTH + D_WIDTH
    assert cg_start % (2 * D_WIDTH) == 0
    cg_blk = cg_start // (2 * D_WIDTH)
    full = lambda arr: _const_spec(arr.shape)
    xe, yae, ybe = _block_edges(x, tb), _block_edges(ya, tb), _block_edges(yb, tb)
    return pl.pallas_call(
        functools.partial(_edge_kernel, rows_per_batch=rows),
        grid=(1,),
        in_specs=[full(yae), full(ybe), full(xe), full(mod0), full(mod1), _const_spec((1, d)),
                  full(w_out0),
                  pl.BlockSpec((d, 2 * D_WIDTH), lambda i: (0, cg_blk), pipeline_mode=pl.Buffered(1))],
        out_specs=pl.BlockSpec((b * rows, D_WIDTH), lambda i: (0, 0)),
        out_shape=jax.ShapeDtypeStruct((b * rows, D_WIDTH), F32),
        compiler_params=_cparams("arbitrary"),
        name="conv_edge_rows",
    )(yae, ybe, xe, mod0, mod1, gain1, w_out0, w_in1).reshape(b, rows, D_WIDTH)


def _layer1_kernel(ya_ref, yb_ref, x_ref, ze_ref, mod0_ref, mod1_ref, g1_ref, wo0_ref, wi_ref,
                   vg_ref, ws_ref, bs_ref, cw_ref, wo1_ref, o_ref):
    d = x_ref.shape[-1]
    tb = x_ref.shape[1]
    j = pl.program_id(1)
    last_j = pl.num_programs(1) - 1
    mod1 = mod1_ref[pl.ds(pl.program_id(0), 1), :]
    gate0 = mod0_ref[pl.ds(pl.program_id(0), 1), 2 * d:]
    gate1 = mod1[:, 2 * d:]
    sub = tb // L1_SUB_BLOCKS
    subs = [slice(s * sub, (s + 1) * sub) for s in range(L1_SUB_BLOCKS)]
    col = lambda p, k: p[:, k * C_WIDTH:(k + 1) * C_WIDTH]
    n_gmlp = 3 * C_WIDTH

    x1s, xms = [], []
    for r in subs:
        x1, xm = _layer1_input(ya_ref[0, r, :], yb_ref[0, r, :], x_ref[0, r, :], gate0,
                               mod1, g1_ref[...], wo0_ref)
        x1s.append(x1)
        xms.append(xm)
    pgs = [jnp.dot(xm, wi_ref[:, 0:n_gmlp], preferred_element_type=F32) for xm in xms]
    pcs = [jnp.dot(xm, wi_ref[:, n_gmlp:], preferred_element_type=F32) for xm in xms]

    upd_c = []
    for pg in pgs:
        u = _gelu(col(pg, 0))
        vn = _rms(_gelu(col(pg, 1)), vg_ref[...]).astype(BF16)
        chunks = []
        for ci in range(sub // C_CHUNK):
            cr = slice(ci * C_CHUNK, (ci + 1) * C_CHUNK)
            groups = []
            for g in range(C_GROUPS):
                gl = slice(g * LANES, (g + 1) * LANES)
                groups.append(jnp.dot(ws_ref[g], vn[cr, gl], preferred_element_type=F32) + bs_ref[g])
            chunks.append(jnp.concatenate(groups, axis=1))
        o_c = u * jnp.concatenate(chunks, axis=0) * _silu(col(pg, 2))
        upd_c.append(jnp.dot(o_c.astype(BF16), wo1_ref[0:C_WIDTH, :], preferred_element_type=F32))

    z = jnp.concatenate([col(pc, 1) * col(pc, 2) for pc in pcs], axis=0)
    grp = 2 * SUBLANES
    prev_grp = ze_ref[0, pl.ds(pl.multiple_of(jnp.maximum(j - 1, 0) * grp + SUBLANES, SUBLANES),
                               SUBLANES), :]
    next_grp = ze_ref[0, pl.ds(pl.multiple_of(jnp.minimum(j + 1, last_j) * grp, SUBLANES),
                               SUBLANES), :]
    z_prev_row = jnp.where(j == 0, 0.0, prev_grp[SUBLANES - 1:, :])
    z_next_row = jnp.where(j == last_j, 0.0, next_grp[0:1, :])
    rowi = lax.broadcasted_iota(jnp.int32, z.shape, 0)
    z_prev = jnp.where(rowi == 0, z_prev_row, pltpu.roll(z, 1, 0))
    z_next = jnp.where(rowi == tb - 1, z_next_row, pltpu.roll(z, tb - 1, 0))
    conv = cw_ref[0:1, :] * z_prev + cw_ref[1:2, :] * z + cw_ref[2:3, :] * z_next

    for r, x1, pc, uc in zip(subs, x1s, pcs, upd_c):
        o_d = col(pc, 0) * conv[r, :] * _silu(col(pc, 3))
        upd = uc + jnp.dot(o_d.astype(BF16), wo1_ref[C_WIDTH:, :], preferred_element_type=F32)
        o_ref[0, r, :] = x1 + gate1 * upd


def _layer1(ya, yb, x, ze, mod0, mod1, gain1, w_out0, w_in1, v_gain, w_s, b_s, conv_w, w_out1, tb):
    b, n, d = x.shape
    half = ya.shape[-1]
    tok = lambda width: pl.BlockSpec((1, tb, width), lambda i, j: (i, j, 0))
    modspec = _const_spec(mod0.shape)
    return pl.pallas_call(
        _layer1_kernel,
        grid=(b, n // tb),
        in_specs=[tok(half), tok(half), tok(d),
                  pl.BlockSpec((1,) + ze.shape[1:], lambda i, j: (i, 0, 0)),
                  modspec, modspec, _const_spec((1, d)),
                  _const_spec(w_out0.shape), _const_spec(w_in1.shape),
                  _const_spec((1, C_WIDTH)),
                  _const_spec((C_GROUPS, C_CHUNK, C_CHUNK)),
                  _const_spec((C_GROUPS, C_CHUNK, LANES)),
                  _const_spec((3, D_WIDTH)),
                  _const_spec(w_out1.shape)],
        out_specs=tok(d),
        out_shape=jax.ShapeDtypeStruct((b, n, d), F32),
        compiler_params=_cparams("arbitrary", "arbitrary"),
        name="outproj_even_layer_odd",
    )(ya, yb, x, ze, mod0, mod1, gain1, w_out0, w_in1, v_gain, w_s, b_s, conv_w, w_out1)


def _rope_tables(n):
    rows_ = n // GRID_W
    row = np.repeat(np.arange(rows_, dtype=np.float64), GRID_W)
    col = np.tile(np.arange(GRID_W, dtype=np.float64), rows_)
    n_freq = A_HEAD_DIM // 4
    inv = ROPE_THETA ** (-np.arange(n_freq, dtype=np.float64) / n_freq)
    ang = np.concatenate([row[:, None] * inv, col[:, None] * inv], axis=-1)
    cos, sin = np.cos(ang), np.sin(ang)
    reps = QK_SLAB // A_HEAD_DIM
    return (jnp.asarray(np.tile(np.concatenate([cos, cos], axis=-1), (1, reps)), F32),
            jnp.asarray(np.tile(np.concatenate([-sin, sin], axis=-1), (1, reps)), F32))


def kernel(x, c, ctx, c_ctx, norm_gain, ada_w, ada_b, even_w_in, even_w_out, attn_qk_gain,
           attn_lambda, attn_subln_gain, hgrn_lb_logits, hgrn_norm_gain, odd_w_in, odd_w_out,
           gmlp_v_gain, gmlp_w_s, gmlp_b_s, conv_w):
    b, n, d = x.shape
    assert b + 1 <= COND_ROWS and ctx.shape[1] % HGRN_CHUNK == 0
    assert all(n % blk == 0 for blk in (INPROJ_BLOCK, L1_TOKEN_BLOCK, ATTN_Q_BLOCK, GRID_W))
    assert norm_gain.shape[0] == 2, "two-layer block: one even layer then one odd layer"

    cond = jnp.concatenate([c, c_ctx[None, :], jnp.zeros((COND_ROWS - b - 1, d), F32)], axis=0)
    mod0, mod1 = _adaln(cond, ada_w, ada_b)

    w_in0 = even_w_in[0].astype(BF16)
    gain0 = norm_gain[0].reshape(1, d)
    cos, sin_signed = _rope_tables(n)
    qk_gain = jnp.tile(attn_qk_gain[0], (1, QK_SLAB // A_HEAD_DIM))
    blk = np.arange(QK_SLAB) // A_HEAD_DIM
    bd = jnp.asarray(np.where(blk[:, None] == blk[None, :], 1.0 / A_HEAD_DIM, 0.0), BF16)
    q, k_lat, vt_lat, rest_x = _inproj(x, mod0, None, gain0, w_in0, cos, sin_signed, qk_gain, bd,
                                       INPROJ_BLOCK, True, True)
    w_ctx = jnp.concatenate([w_in0[:, g * A_WIDTH:(g + 1) * A_WIDTH] for g in CTX_GROUPS], axis=1)
    n_ctx = ctx.shape[1]
    k_ctx, vt_ctx, rest_c = _inproj(ctx.reshape(1, b * n_ctx, d), mod0, b, gain0, w_ctx, cos, sin_signed,
                                    qk_gain, bd, min(TOKEN_BLOCK, b * n_ctx), False, False)
    k_ctx = k_ctx.reshape(b, n_ctx, A_WIDTH)
    rest_c = rest_c.reshape(b, n_ctx, rest_c.shape[-1])

    lam_init = 0.8 - 0.6 * math.exp(-0.3 * 0)
    ya = _attention(q, k_lat, k_ctx, vt_lat, vt_ctx, rest_x, qk_gain,
                    attn_subln_gain[0].reshape(1, LANES), attn_lambda[0], lam_init, ATTN_Q_BLOCK)
    yb = _hgrn(rest_x, rest_c, hgrn_lb_logits, hgrn_norm_gain[0].reshape(1, LANES),
               HGRN_HEADS_PER_STEP)

    gain1 = norm_gain[1].reshape(1, d)
    w_out0, w_in1 = even_w_out[0].astype(BF16), odd_w_in[0].astype(BF16)
    ze = _edge_z(ya, yb, x, mod0, mod1, gain1, w_out0, w_in1, L1_TOKEN_BLOCK)
    b_s = jnp.broadcast_to(gmlp_b_s[0][:, :, None], (C_GROUPS, C_CHUNK, LANES))
    return _layer1(ya, yb, x, ze, mod0, mod1, gain1, w_out0, w_in1,
                   gmlp_v_gain[0].reshape(1, C_WIDTH), gmlp_w_s[0].astype(BF16), b_s, conv_w[0],
                   odd_w_out[0].astype(BF16), L1_TOKEN_BLOCK)
```
